```python
import math
import jax, jax.numpy as jnp
from jax import lax
import numpy as np

D_MODEL = 1024
BATCH = 1
SEQ = 16384
DEPTH = 1
DEC_BATCH = 32
DEC_SEQ = 8
PAST_LEN = 16384
PAGE_SIZE = 128

HEAD_DIM = 64
DIL_GROUPS = ((128, 1), (512, 4), (2048, 16))
N_GROUPS = len(DIL_GROUPS)
HEADS_PER_GROUP = 4
ATTN_WIDTH = N_GROUPS * HEADS_PER_GROUP * HEAD_DIM
BAND = 128
CONV_CH = D_MODEL - ATTN_WIDTH
CONV_WIDTH = 31
IN_WIDTH = 3 * ATTN_WIDTH + 2 * CONV_CH
N_EXPERTS = 256
TOP_K = 8
N_EXPERT_GROUPS = 8
TOPK_GROUPS = 4
EXPERT_FF = 256
ROUTED_SCALE = 2.5
DN_ALPHA = (2 * DEPTH) ** 0.25
DN_BETA = (8 * DEPTH) ** -0.25
LN_EPS = 1e-5

kernel_name = 'hymba_longnet_conformer_moe_step'


def layer_norm(x, g, b):
    xf = x.astype(jnp.float32)
    mu = xf.mean(-1, keepdims=True)
    var = jnp.square(xf - mu).mean(-1, keepdims=True)
    return ((xf - mu) * lax.rsqrt(var + LN_EPS) * g + b).astype(x.dtype)


def alibi_slopes():
    n = N_GROUPS * HEADS_PER_GROUP
    h = jnp.arange(1, n + 1, dtype=jnp.float32)
    return (2.0 ** (-8.0 * h / n)).reshape(N_GROUPS, HEADS_PER_GROUP)


def project_in(x, w_in, b_in):
    z = jnp.einsum('bsd,de->bse', x, w_in) + b_in
    q, k, v, glu = jnp.split(z, [ATTN_WIDTH, 2 * ATTN_WIDTH, 3 * ATTN_WIDTH], axis=-1)
    shp = x.shape[:2] + (N_GROUPS, HEADS_PER_GROUP, HEAD_DIM)
    a, gate = jnp.split(glu, 2, axis=-1)
    return q.reshape(shp), k.reshape(shp), v.reshape(shp), a * jax.nn.sigmoid(gate)


def dilated_attention_prompt(q, k, v, slopes, window, dil):
    b, s, h, hd = q.shape
    nk = window // dil
    L = s // dil

    def to_sub(t):
        return t.reshape(b, L, dil, h, hd).transpose(0, 2, 1, 3, 4).reshape(b * dil, L, h, hd)

    qs, ks, vs = to_sub(q), to_sub(k), to_sub(v)
    nb = -(-L // BAND)
    lp = nb * BAND
    qs = jnp.pad(qs, ((0, 0), (0, lp - L), (0, 0), (0, 0))).reshape(b * dil, nb, BAND, h, hd)

    def band(t):
        t = jnp.pad(t, ((0, 0), (BAND, lp - L), (0, 0), (0, 0))).reshape(b * dil, nb + 1, BAND, h, hd)
        return jnp.concatenate([t[:, :-1], t[:, 1:]], axis=2)

    kb, vb = band(ks), band(vs)
    scores = jnp.einsum('nbqhd,nbkhd->nbhqk', qs, kb).astype(jnp.float32) * HEAD_DIM ** -0.5
    qi = jnp.arange(BAND)[:, None]
    kj = jnp.arange(2 * BAND)[None, :]
    steps = qi + BAND - kj
    kpos = jnp.arange(nb)[:, None, None] * BAND + kj[None] - BAND
    valid = (steps >= 0) & (steps <= nk) & (kpos >= 0)
    bias = -slopes[:, None, None] * (steps * dil).astype(jnp.float32)
    scores = jnp.where(valid[:, None], scores + bias, -jnp.inf)
    lse = jax.nn.logsumexp(scores, axis=-1)
    p = jnp.exp(scores - lse[..., None])
    o = jnp.einsum('nbhqk,nbkhd->nbqhd', p.astype(v.dtype), vb)
    o = o.reshape(b, dil, lp, h, hd)[:, :, :L].transpose(0, 2, 1, 3, 4).reshape(b, s, h, hd)
    lse = lse.transpose(0, 1, 3, 2).reshape(b, dil, lp, h)[:, :, :L].transpose(0, 2, 1, 3).reshape(b, s, h)
    return o, lse


def dilated_attention_sample(q, k_all, v_all, slopes, window, dil, n_buf):
    t = q.shape[1]
    nk = window // dil
    steps = jnp.arange(nk + 1)
    idx = n_buf + jnp.arange(t)[:, None] - steps[None, :] * dil
    valid = idx >= 0
    idx_c = jnp.maximum(idx, 0)
    kg = k_all[:, idx_c]
    vg = v_all[:, idx_c]
    scores = jnp.einsum('bqhd,bqkhd->bhqk', q, kg).astype(jnp.float32) * HEAD_DIM ** -0.5
    bias = -slopes[:, None, None] * (steps * dil).astype(jnp.float32)[None, None, :]
    scores = jnp.where(valid, scores + bias, -jnp.inf)
    lse = jax.nn.logsumexp(scores, axis=-1)
    p = jnp.exp(scores - lse[..., None])
    o = jnp.einsum('bhqk,bqkhd->bqhd', p.astype(vg.dtype), vg)
    return o, lse.transpose(0, 2, 1)


def conv_module(u_hist, w_dw, b_dw, ln_g, ln_b):
    y = lax.conv_general_dilated(u_hist, w_dw[:, None, :], window_strides=(1,), padding='VALID',
                                 dimension_numbers=('NWC', 'WIO', 'NWC'),
                                 feature_group_count=CONV_CH) + b_dw
    return jax.nn.silu(layer_norm(y, ln_g, ln_b))


def merge_and_project(o_groups, lse_groups, conv_out, w_o, b_o):
    b, t = o_groups.shape[:2]
    alpha = jax.nn.softmax(lse_groups, axis=2)
    attn = (o_groups * alpha[..., None].astype(o_groups.dtype)).reshape(b, t, ATTN_WIDTH)
    cat = jnp.concatenate([attn, conv_out], axis=-1)
    return jnp.einsum('bte,ed->btd', cat, w_o) + b_o


def mixers_prompt(x, w_in, b_in, w_dw, b_dw, cg, cb, w_o, b_o):
    q, k, v, u = project_in(x, w_in, b_in)
    slopes = alibi_slopes()
    outs, lses, kv_new = [], [], []
    for g, (win, dil) in enumerate(DIL_GROUPS):
        o, lse = dilated_attention_prompt(q[:, :, g], k[:, :, g], v[:, :, g], slopes[g], win, dil)
        outs.append(o)
        lses.append(lse)
        keep = min(win, x.shape[1])
        kv_new.append(jnp.stack([k[:, -keep:, g], v[:, -keep:, g]], axis=2))
    u_hist = jnp.concatenate([jnp.zeros((x.shape[0], CONV_WIDTH - 1, CONV_CH), u.dtype), u], axis=1)
    conv_out = conv_module(u_hist, w_dw, b_dw, cg, cb)
    conv_new = u_hist[:, -(CONV_WIDTH - 1):]
    mixed = merge_and_project(jnp.stack(outs, 2), jnp.stack(lses, 2), conv_out, w_o, b_o)
    return mixed, kv_new, conv_new


def mixers_sample(x, kv_caches, conv_state, w_in, b_in, w_dw, b_dw, cg, cb, w_o, b_o):
    q, k, v, u = project_in(x, w_in, b_in)
    slopes = alibi_slopes()
    outs, lses, kv_new = [], [], []
    for g, (win, dil) in enumerate(DIL_GROUPS):
        kv_g = kv_caches[g]
        k_all = jnp.concatenate([kv_g[:, :, 0], k[:, :, g]], axis=1)
        v_all = jnp.concatenate([kv_g[:, :, 1], v[:, :, g]], axis=1)
        o, lse = dilated_attention_sample(q[:, :, g], k_all, v_all, slopes[g], win, dil, kv_g.shape[1])
        outs.append(o)
        lses.append(lse)
        kv_new.append(jnp.stack([k[:, :, g], v[:, :, g]], axis=2))
    u_hist = jnp.concatenate([conv_state.astype(u.dtype), u], axis=1)
    conv_out = conv_module(u_hist, w_dw, b_dw, cg, cb)
    conv_new = u_hist[:, -(CONV_WIDTH - 1):]
    mixed = merge_and_project(jnp.stack(outs, 2), jnp.stack(lses, 2), conv_out, w_o, b_o)
    return mixed, kv_new, conv_new


def route(x2, w_router, router_bias):
    t = x2.shape[0]
    scores = jax.nn.sigmoid(jnp.einsum('td,de->te', x2, w_router).astype(jnp.float32))
    biased = (scores + router_bias.astype(jnp.float32)).reshape(t, N_EXPERT_GROUPS, -1)
    grp_score = lax.top_k(biased, 2)[0].sum(-1)
    top_grp = lax.top_k(grp_score, TOPK_GROUPS)[1]
    grp_mask = jnp.any(top_grp[..., None] == jnp.arange(N_EXPERT_GROUPS), axis=-2)
    masked = jnp.where(grp_mask[..., None], biased, -jnp.inf).reshape(t, N_EXPERTS)
    idx = lax.top_k(masked, TOP_K)[1]
    w = jnp.take_along_axis(scores, idx, axis=-1)
    w = w / w.sum(-1, keepdims=True) * ROUTED_SCALE
    return idx, w


def routed_experts(x2, idx, w, w_gate, w_up, w_down):
    t, d = x2.shape
    a = t * TOP_K
    blk = min(128, max(8, 1 << max(0, (a // N_EXPERTS).bit_length() - 1)))
    flat_e = idx.reshape(-1)
    flat_t = jnp.arange(a, dtype=jnp.int32) // TOP_K
    flat_w = w.reshape(-1)
    order = jnp.argsort(flat_e)
    se = flat_e[order]
    counts = jnp.bincount(flat_e, length=N_EXPERTS)
    starts = jnp.cumsum(counts) - counts
    padded = (counts + blk - 1) // blk * blk
    pends = jnp.cumsum(padded)
    pstarts = pends - padded
    dest = pstarts[se] + jnp.arange(a) - starts[se]
    nblk = -(-(a + N_EXPERTS * (blk - 1)) // blk)
    npad = nblk * blk
    row_tok = jnp.full((npad,), t, jnp.int32).at[dest].set(flat_t[order])
    row_w = jnp.zeros((npad,), jnp.float32).at[dest].set(flat_w[order])
    blk_e = jnp.minimum(jnp.searchsorted(pends, jnp.arange(nblk) * blk, side='right'), N_EXPERTS - 1)
    x_pad = jnp.concatenate([x2, jnp.zeros((1, d), x2.dtype)], axis=0)

    def expert_block(args):
        rows, e = args
        xb = x_pad[rows]
        hdn = jax.nn.silu(xb @ w_gate[e]) * (xb @ w_up[e])
        return hdn @ w_down[e]

    out = lax.map(expert_block, (row_tok.reshape(nblk, blk), blk_e)).reshape(npad, d)
    out = out * row_w[:, None].astype(out.dtype)
    return jnp.zeros((t + 1, d), out.dtype).at[row_tok].add(out)[:t]


def moe(x, w_router, router_bias, w_gate, w_up, w_down, ws_gate, ws_up, ws_down):
    b, s, d = x.shape
    x2 = x.reshape(b * s, d)
    idx, wts = route(x2, w_router, router_bias)
    routed = routed_experts(x2, idx, wts, w_gate, w_up, w_down)
    shared = (jax.nn.silu(x2 @ ws_gate) * (x2 @ ws_up)) @ ws_down
    return (routed + shared).reshape(b, s, d)


def setup_inputs(seed: int = 0) -> dict:
    key = jax.random.key(seed)
    ks = jax.random.split(key, 26)
    f32 = jnp.float32

    def nrm(k, shape, scale):
        return jax.random.normal(k, shape, f32) * scale

    L = DEPTH
    kv_shape = lambda win: (L, DEC_BATCH, min(win, PAST_LEN), 2, HEADS_PER_GROUP, HEAD_DIM)
    return {
        'x_prompt': nrm(ks[0], (BATCH, SEQ, D_MODEL), 1.0),
        'x_sample': nrm(ks[1], (DEC_BATCH, DEC_SEQ, D_MODEL), 1.0),
        'cache_kv_w128': nrm(ks[2], kv_shape(DIL_GROUPS[0][0]), 1.0),
        'cache_kv_w512': nrm(ks[3], kv_shape(DIL_GROUPS[1][0]), 1.0),
        'cache_kv_w2048': nrm(ks[4], kv_shape(DIL_GROUPS[2][0]), 1.0),
        'state_conv': nrm(ks[5], (L, DEC_BATCH, CONV_WIDTH - 1, CONV_CH), 0.5),
        'w_in': nrm(ks[6], (L, D_MODEL, IN_WIDTH), D_MODEL ** -0.5),
        'b_in': nrm(ks[7], (L, IN_WIDTH), 0.02),
        'w_dw': nrm(ks[8], (L, CONV_WIDTH, CONV_CH), CONV_WIDTH ** -0.5),
        'b_dw': nrm(ks[9], (L, CONV_CH), 0.02),
        'conv_ln_g': 1.0 + nrm(ks[10], (L, CONV_CH), 0.1),
        'conv_ln_b': nrm(ks[11], (L, CONV_CH), 0.02),
        'w_o': nrm(ks[12], (L, ATTN_WIDTH + CONV_CH, D_MODEL), DN_BETA * (ATTN_WIDTH + CONV_CH) ** -0.5),
        'b_o': nrm(ks[13], (L, D_MODEL), 0.02),
        'ln1_g': 1.0 + nrm(ks[14], (L, D_MODEL), 0.1),
        'ln1_b': nrm(ks[15], (L, D_MODEL), 0.02),
        'w_router': nrm(ks[16], (L, D_MODEL, N_EXPERTS), D_MODEL ** -0.5),
        'router_bias': nrm(ks[17], (L, N_EXPERTS), 0.01),
        'w_gate': nrm(ks[18], (L, N_EXPERTS, D_MODEL, EXPERT_FF), D_MODEL ** -0.5),
        'w_up': nrm(ks[19], (L, N_EXPERTS, D_MODEL, EXPERT_FF), D_MODEL ** -0.5),
        'w_down': nrm(ks[20], (L, N_EXPERTS, EXPERT_FF, D_MODEL), DN_BETA * EXPERT_FF ** -0.5),
        'ws_gate': nrm(ks[21], (L, D_MODEL, EXPERT_FF), D_MODEL ** -0.5),
        'ws_up': nrm(ks[22], (L, D_MODEL, EXPERT_FF), D_MODEL ** -0.5),
        'ws_down': nrm(ks[23], (L, EXPERT_FF, D_MODEL), DN_BETA * EXPERT_FF ** -0.5),
        'ln2_g': 1.0 + nrm(ks[24], (L, D_MODEL), 0.1),
        'ln2_b': nrm(ks[25], (L, D_MODEL), 0.02),
    }


def reference(x_prompt, x_sample, cache_kv_w128, cache_kv_w512, cache_kv_w2048, state_conv,
              w_in, b_in, w_dw, b_dw, conv_ln_g, conv_ln_b, w_o, b_o, ln1_g, ln1_b,
              w_router, router_bias, w_gate, w_up, w_down, ws_gate, ws_up, ws_down, ln2_g, ln2_b):
    kv_caches = (cache_kv_w128, cache_kv_w512, cache_kv_w2048)
    hp, hs = x_prompt, x_sample
    new_p = [[] for _ in range(N_GROUPS + 1)]
    new_s = [[] for _ in range(N_GROUPS + 1)]
    for l in range(DEPTH):
        mix_w = (w_in[l], b_in[l], w_dw[l], b_dw[l], conv_ln_g[l], conv_ln_b[l], w_o[l], b_o[l])
        ffn_w = (w_router[l], router_bias[l], w_gate[l], w_up[l], w_down[l], ws_gate[l], ws_up[l], ws_down[l])
        mp, kvp, cp = mixers_prompt(hp, *mix_w)
        ms, kvs, cs = mixers_sample(hs, [c[l] for c in kv_caches], state_conv[l], *mix_w)
        hp = layer_norm(DN_ALPHA * hp + mp, ln1_g[l], ln1_b[l])
        hs = layer_norm(DN_ALPHA * hs + ms, ln1_g[l], ln1_b[l])
        hp = layer_norm(DN_ALPHA * hp + moe(hp, *ffn_w), ln2_g[l], ln2_b[l])
        hs = layer_norm(DN_ALPHA * hs + moe(hs, *ffn_w), ln2_g[l], ln2_b[l])
        for g in range(N_GROUPS):
            new_p[g].append(kvp[g])
            new_s[g].append(kvs[g])
        new_p[N_GROUPS].append(cp)
        new_s[N_GROUPS].append(cs)
    kv128_p, kv512_p, kv2048_p, conv_p = [jnp.stack(a, 0) for a in new_p]
    kv128_s, kv512_s, kv2048_s, conv_s = [jnp.stack(a, 0) for a in new_s]
    return (hp, hs, kv128_p, kv512_p, kv2048_p, conv_p, kv128_s, kv512_s, kv2048_s, conv_s)
```

```python
import functools

import jax
import jax.numpy as jnp
import numpy as np
from jax import lax
from jax.experimental import pallas as pl
from jax.experimental.pallas import tpu as pltpu

F32 = jnp.float32
BF16 = jnp.bfloat16
I32 = jnp.int32

D_MODEL = 1024
HEAD_DIM = 64
HEADS_PER_GROUP = 4
GROUP_W = HEADS_PER_GROUP * HEAD_DIM
DIL_GROUPS = ((128, 1), (512, 4), (2048, 16))
N_GROUPS = len(DIL_GROUPS)
ATTN_W = N_GROUPS * GROUP_W
CONV_CH = D_MODEL - ATTN_W
CONV_WIDTH = 31
IN_W = 3 * ATTN_W + 2 * CONV_CH
BAND = 128
N_EXPERTS = 256
TOP_K = 8
N_EXPERT_GROUPS = 8
EXPERTS_PER_GROUP = N_EXPERTS // N_EXPERT_GROUPS
TOPK_GROUPS = 4
EXPERT_FF = 256
ROUTED_SCALE = 2.5
DEPTH = 1
DN_ALPHA = (2 * DEPTH) ** 0.25
LN_EPS = 1e-5
MASKED = -1e30

VMEM_LIMIT_BYTES = 56 * 1024 * 1024


def _params(*sem):
    return pltpu.CompilerParams(dimension_semantics=sem, vmem_limit_bytes=VMEM_LIMIT_BYTES)


def _dot(a, b):
    return jnp.dot(a, b, preferred_element_type=F32)


def _dot_nt(a, b):
    return lax.dot_general(a, b, (((1,), (1,)), ((), ())), preferred_element_type=F32)


def _layer_norm(x, g, b):
    mu = jnp.mean(x, axis=-1, keepdims=True)
    xc = x - mu
    var = jnp.mean(xc * xc, axis=-1, keepdims=True)
    return xc * lax.rsqrt(var + LN_EPS) * g + b


def _silu(x):
    return x * jax.nn.sigmoid(x)


def _alibi_slopes():
    n = N_GROUPS * HEADS_PER_GROUP
    h = jnp.arange(1, n + 1, dtype=F32)
    return (2.0 ** (-8.0 * h / n)).reshape(N_GROUPS, HEADS_PER_GROUP)


def _proj_kernel(x_ref, w_ref, b_ref, qkv_ref, u_ref, kv_ref):
    x = x_ref[...].astype(BF16)
    for c0 in range(0, 3 * ATTN_W, GROUP_W):
        z = _dot(x, w_ref[:, c0:c0 + GROUP_W]) + b_ref[:, c0:c0 + GROUP_W]
        qkv_ref[:, c0:c0 + GROUP_W] = z.astype(qkv_ref.dtype)
        if c0 >= ATTN_W:
            kv_ref[:, c0 - ATTN_W:c0 - ATTN_W + GROUP_W] = z
    c0 = 3 * ATTN_W
    a = _dot(x, w_ref[:, c0:c0 + CONV_CH]) + b_ref[:, c0:c0 + CONV_CH]
    gate = _dot(x, w_ref[:, c0 + CONV_CH:]) + b_ref[:, c0 + CONV_CH:]
    u_ref[...] = a * jax.nn.sigmoid(gate)


def _project(x, w_in, b_in, tm, keep, qkv_dtype):
    n = x.shape[0]
    nt = n // tm
    nk = keep // tm
    return pl.pallas_call(
        _proj_kernel,
        grid=(nt,),
        in_specs=[
            pl.BlockSpec((tm, D_MODEL), lambda i: (i, 0)),
            pl.BlockSpec((D_MODEL, IN_W), lambda i: (0, 0)),
            pl.BlockSpec((1, IN_W), lambda i: (0, 0)),
        ],
        out_specs=[
            pl.BlockSpec((tm, 3 * ATTN_W), lambda i: (i, 0)),
            pl.BlockSpec((tm, CONV_CH), lambda i: (i, 0)),
            pl.BlockSpec((tm, 2 * ATTN_W), lambda i: (jnp.maximum(i - (nt - nk), 0), 0)),
        ],
        out_shape=[
            jax.ShapeDtypeStruct((n, 3 * ATTN_W), qkv_dtype),
            jax.ShapeDtypeStruct((n, CONV_CH), F32),
            jax.ShapeDtypeStruct((keep, 2 * ATTN_W), F32),
        ],
        compiler_params=_params("arbitrary"),
        name="proj",
    )(x, w_in, b_in)


def _head_select(parts, rows):
    col = lax.broadcasted_iota(I32, (rows, GROUP_W), 1) // HEAD_DIM
    out = jnp.broadcast_to(parts[-1], (rows, GROUP_W))
    for h in range(HEADS_PER_GROUP - 2, -1, -1):
        out = jnp.where(col == h, parts[h], out)
    return out


def _head_rows(q, rows):
    col = lax.broadcasted_iota(I32, (rows, GROUP_W), 1) // HEAD_DIM
    return jnp.concatenate([jnp.where(col == h, q, jnp.zeros_like(q)) for h in range(HEADS_PER_GROUP)], axis=0)


def _softmax_pv(s, v, rows):
    m = jnp.max(s, axis=-1, keepdims=True)
    e = jnp.exp(s - m)
    l = jnp.sum(e, axis=-1, keepdims=True)
    pv = _dot(e.astype(BF16), v) / l
    lse = m + jnp.log(l)
    o = _head_select([pv[h * rows:(h + 1) * rows] for h in range(HEADS_PER_GROUP)], rows)
    lse_x = _head_select([lse[h * rows:(h + 1) * rows] for h in range(HEADS_PER_GROUP)], rows)
    return o, lse_x


def _attn_kernel(q_ref, kp_ref, kc_ref, vp_ref, vc_ref, bias_ref, o_ref, lse_ref):
    qm = _head_rows(q_ref[...], BAND)
    k = jnp.concatenate([kp_ref[...], kc_ref[...]], axis=0)
    v = jnp.concatenate([vp_ref[...], vc_ref[...]], axis=0)
    s = _dot_nt(qm, k) * HEAD_DIM ** -0.5 + bias_ref[0]
    o, lse_x = _softmax_pv(s, v, BAND)
    o_ref[...] = o.astype(o_ref.dtype)
    lse_ref[...] = lse_x


def _prompt_bias(g, dil):
    slopes = _alibi_slopes()[g]
    qi = jnp.arange(BAND)[:, None]
    kj = jnp.arange(2 * BAND)[None, :]
    steps = qi + BAND - kj
    nk = DIL_GROUPS[g][0] // dil
    valid = (steps >= 0) & (steps <= nk)
    bias = -slopes[:, None, None] * (steps * dil).astype(F32)
    inner = jnp.where(valid[None], bias, MASKED)
    first = jnp.where((valid & (kj >= BAND))[None], bias, MASKED)
    return jnp.stack([first, inner]).reshape(2, HEADS_PER_GROUP * BAND, 2 * BAND)


def _attention_prompt(qkv, g):
    s_len = qkv.shape[0]
    win, dil = DIL_GROUPS[g]
    assert win // dil <= BAND and s_len % (dil * BAND) == 0
    n_cls = s_len // dil
    nblk = 3 * N_GROUPS
    view = qkv.reshape(n_cls, dil * 3 * ATTN_W)
    cur = lambda part: pl.BlockSpec((BAND, GROUP_W), lambda r, b: (b, r * nblk + part * N_GROUPS + g))
    prev = lambda part: pl.BlockSpec((BAND, GROUP_W),
                                     lambda r, b: (jnp.maximum(b - 1, 0), r * nblk + part * N_GROUPS + g))
    o, lse = pl.pallas_call(
        _attn_kernel,
        grid=(dil, n_cls // BAND),
        in_specs=[cur(0), prev(1), cur(1), prev(2), cur(2),
                  pl.BlockSpec((1, HEADS_PER_GROUP * BAND, 2 * BAND), lambda r, b: (jnp.minimum(b, 1), 0, 0))],
        out_specs=[pl.BlockSpec((BAND, GROUP_W), lambda r, b: (b, r)),
                   pl.BlockSpec((BAND, GROUP_W), lambda r, b: (b, r))],
        out_shape=[jax.ShapeDtypeStruct((n_cls, dil * GROUP_W), BF16),
                   jax.ShapeDtypeStruct((n_cls, dil * GROUP_W), F32)],
        compiler_params=_params("arbitrary", "arbitrary"),
        name=f"attn_g{g}",
    )(view, view, view, view, view, _prompt_bias(g, dil))
    return o.reshape(s_len, GROUP_W), lse.reshape(s_len, GROUP_W)


def _attn_sample_kernel(q_ref, k_ref, v_ref, cache_ref, bias_ref, o_ref, lse_ref, kbuf, vbuf, *, n_buf, t):
    kbuf[0:n_buf, :] = cache_ref[0, :, 0:GROUP_W].astype(BF16)
    vbuf[0:n_buf, :] = cache_ref[0, :, GROUP_W:].astype(BF16)
    pad = jnp.zeros((BAND - t, GROUP_W), F32)
    kbuf[n_buf:, :] = jnp.concatenate([k_ref[...], pad], axis=0).astype(BF16)
    vbuf[n_buf:, :] = jnp.concatenate([v_ref[...], pad], axis=0).astype(BF16)
    qm = _head_rows(q_ref[...], t).astype(BF16)
    s = _dot_nt(qm, kbuf[...]) * HEAD_DIM ** -0.5 + bias_ref[...]
    o, lse_x = _softmax_pv(s, vbuf[...], t)
    o_ref[...] = o
    lse_ref[...] = lse_x


def _sample_bias(g, n_buf, t):
    win, dil = DIL_GROUPS[g]
    slopes = _alibi_slopes()[g]
    tq = jnp.arange(t)[:, None]
    j = jnp.arange(n_buf + BAND)[None, :]
    dist = n_buf + tq - j
    valid = (dist >= 0) & (dist % dil == 0) & (dist <= win) & (j < n_buf + t)
    bias = -slopes[:, None, None] * dist.astype(F32)
    return jnp.where(valid[None], bias, MASKED).reshape(HEADS_PER_GROUP * t, n_buf + BAND)


def _attention_sample(qkv, cache, g, t):
    nb, n_buf = cache.shape[0], cache.shape[1]
    col = lambda part: pl.BlockSpec((t, GROUP_W), lambda i: (i, part * N_GROUPS + g))
    kern = functools.partial(_attn_sample_kernel, n_buf=n_buf, t=t)
    return pl.pallas_call(
        kern,
        grid=(nb,),
        in_specs=[col(0), col(1), col(2),
                  pl.BlockSpec((1, n_buf, 2 * GROUP_W), lambda i: (i, 0, 0)),
                  pl.BlockSpec((HEADS_PER_GROUP * t, n_buf + BAND), lambda i: (0, 0))],
        out_specs=[pl.BlockSpec((t, GROUP_W), lambda i: (i, 0)),
                   pl.BlockSpec((t, GROUP_W), lambda i: (i, 0))],
        out_shape=[jax.ShapeDtypeStruct((nb * t, GROUP_W), F32),
                   jax.ShapeDtypeStruct((nb * t, GROUP_W), F32)],
        scratch_shapes=[pltpu.VMEM((n_buf + BAND, GROUP_W), BF16),
                        pltpu.VMEM((n_buf + BAND, GROUP_W), BF16)],
        compiler_params=_params("arbitrary"),
        name=f"attn_sample_g{g}",
    )(qkv, qkv, qkv, cache, _sample_bias(g, n_buf, t))


CONV_HALO = 32


def _conv_tail(acc, b_ref, g_ref, beta_ref):
    return _silu(_layer_norm(acc + b_ref[...], g_ref[...], beta_ref[...]))


def _conv_prompt_kernel(halo_ref, u_ref, w_ref, b_ref, g_ref, beta_ref, o_ref, hist, *, tm):
    i = pl.program_id(0)
    hist[0:CONV_HALO, :] = jnp.where(i == 0, 0.0, halo_ref[...])
    hist[CONV_HALO:, :] = u_ref[...]
    off = CONV_HALO - (CONV_WIDTH - 1)
    acc = jnp.zeros((tm, CONV_CH), F32)
    for j in range(CONV_WIDTH):
        acc = acc + w_ref[j:j + 1, :] * hist[off + j:off + j + tm, :]
    o_ref[...] = _conv_tail(acc, b_ref, g_ref, beta_ref).astype(o_ref.dtype)


def _conv_prompt(u, w_dw, b_dw, ln_g, ln_b, tm):
    n = u.shape[0]
    vec = pl.BlockSpec((1, CONV_CH), lambda i: (0, 0))
    return pl.pallas_call(
        functools.partial(_conv_prompt_kernel, tm=tm),
        grid=(n // tm,),
        in_specs=[pl.BlockSpec((CONV_HALO, CONV_CH), lambda i: (jnp.maximum(i * (tm // CONV_HALO) - 1, 0), 0)),
                  pl.BlockSpec((tm, CONV_CH), lambda i: (i, 0)),
                  pl.BlockSpec((CONV_WIDTH, CONV_CH), lambda i: (0, 0)), vec, vec, vec],
        out_specs=pl.BlockSpec((tm, CONV_CH), lambda i: (i, 0)),
        out_shape=jax.ShapeDtypeStruct((n, CONV_CH), BF16),
        scratch_shapes=[pltpu.VMEM((CONV_HALO + tm, CONV_CH), F32)],
        compiler_params=_params("arbitrary"),
        name="conv_prompt",
    )(u, u, w_dw, b_dw, ln_g, ln_b)


def _conv_sample_kernel(hist_ref, w_ref, b_ref, g_ref, beta_ref, o_ref, *, t):
    acc = jnp.zeros((t, CONV_CH), F32)
    for j in range(CONV_WIDTH):
        acc = acc + w_ref[j:j + 1, :] * hist_ref[0, j:j + t, :]
    o_ref[...] = _conv_tail(acc, b_ref, g_ref, beta_ref)


def _conv_sample(u_hist, w_dw, b_dw, ln_g, ln_b, t):
    nb, rows = u_hist.shape[0], u_hist.shape[1]
    vec = pl.BlockSpec((1, CONV_CH), lambda i: (0, 0))
    return pl.pallas_call(
        functools.partial(_conv_sample_kernel, t=t),
        grid=(nb,),
        in_specs=[pl.BlockSpec((1, rows, CONV_CH), lambda i: (i, 0, 0)),
                  pl.BlockSpec((CONV_WIDTH, CONV_CH), lambda i: (0, 0)), vec, vec, vec],
        out_specs=pl.BlockSpec((t, CONV_CH), lambda i: (i, 0)),
        out_shape=jax.ShapeDtypeStruct((nb * t, CONV_CH), F32),
        compiler_params=_params("arbitrary"),
        name="conv_sample",
    )(u_hist, w_dw, b_dw, ln_g, ln_b)


def _outproj_kernel(o0, o1, o2, l0, l1, l2, c_ref, x_ref, w_ref, b_ref, g_ref, beta_ref, h_ref):
    ls = [l0[...], l1[...], l2[...]]
    m = jnp.maximum(jnp.maximum(ls[0], ls[1]), ls[2])
    es = [jnp.exp(l - m) for l in ls]
    inv = 1.0 / (es[0] + es[1] + es[2])
    mixed = b_ref[...]
    for g, o_ref in enumerate((o0, o1, o2)):
        a = (o_ref[...].astype(F32) * (es[g] * inv)).astype(BF16)
        mixed = mixed + _dot(a, w_ref[g * GROUP_W:(g + 1) * GROUP_W, :])
    mixed = mixed + _dot(c_ref[...].astype(BF16), w_ref[ATTN_W:, :])
    h_ref[...] = _layer_norm(DN_ALPHA * x_ref[...] + mixed, g_ref[...], beta_ref[...])


def _outproj(os_, lses, conv, x, w_o, b_o, ln_g, ln_b, tm):
    n = x.shape[0]
    grp = pl.BlockSpec((tm, GROUP_W), lambda i: (i, 0))
    vec = pl.BlockSpec((1, D_MODEL), lambda i: (0, 0))
    return pl.pallas_call(
        _outproj_kernel,
        grid=(n // tm,),
        in_specs=[grp] * 7 + [pl.BlockSpec((tm, D_MODEL), lambda i: (i, 0)),
                              pl.BlockSpec((D_MODEL, D_MODEL), lambda i: (0, 0)), vec, vec, vec],
        out_specs=pl.BlockSpec((tm, D_MODEL), lambda i: (i, 0)),
        out_shape=jax.ShapeDtypeStruct((n, D_MODEL), F32),
        compiler_params=_params("arbitrary"),
        name="outproj",
    )(*os_, *lses, conv, x, w_o, b_o, ln_g, ln_b)


ROUTER_TM = 256


def _first_index(hit, idx, limit, axis):
    return jnp.min(jnp.where(hit, idx, limit), axis=axis, keepdims=True)


def _router_kernel(h_ref, w_ref, rb_ref, tri_ref, idx_ref, wt_ref, pos_ref, cnt_ref, run):
    i = pl.program_id(0)
    tm = ROUTER_TM

    @pl.when(i == 0)
    def _():
        run[...] = jnp.zeros_like(run)

    logits = _dot_nt(w_ref[...], h_ref[...].astype(BF16))
    scores = jax.nn.sigmoid(logits)
    biased = scores + rb_ref[...]
    groups = [biased[g * EXPERTS_PER_GROUP:(g + 1) * EXPERTS_PER_GROUP] for g in range(N_EXPERT_GROUPS)]
    ei = lax.broadcasted_iota(I32, (EXPERTS_PER_GROUP, tm), 0).astype(F32)
    gs = []
    for bg in groups:
        m1 = jnp.max(bg, axis=0, keepdims=True)
        f1 = _first_index(bg == m1, ei, float(EXPERTS_PER_GROUP), 0)
        m2 = jnp.max(jnp.where(ei == f1, -jnp.inf, bg), axis=0, keepdims=True)
        gs.append(m1 + m2)
    gs = jnp.concatenate(gs, axis=0)
    gi = lax.broadcasted_iota(I32, gs.shape, 0).astype(F32)
    keep = jnp.zeros(gs.shape, F32)
    cur = gs
    for _ in range(TOPK_GROUPS):
        m = jnp.max(cur, axis=0, keepdims=True)
        f = _first_index(cur == m, gi, float(N_EXPERT_GROUPS), 0)
        hit = gi == f
        keep = jnp.where(hit, 1.0, keep)
        cur = jnp.where(hit, -jnp.inf, cur)
    masked = jnp.concatenate([jnp.where(keep[g:g + 1] > 0.0, bg, -jnp.inf) for g, bg in enumerate(groups)], axis=0)
    xi = lax.broadcasted_iota(I32, (N_EXPERTS, tm), 0).astype(F32)
    cur = masked
    sel = jnp.zeros((N_EXPERTS, tm), F32)
    picks = []
    for _ in range(TOP_K):
        m = jnp.max(cur, axis=0, keepdims=True)
        f = _first_index(cur == m, xi, float(N_EXPERTS), 0)
        hit = xi == f
        picks.append((f, hit))
        sel = jnp.where(hit, 1.0, sel)
        cur = jnp.where(hit, -jnp.inf, cur)
    before = _dot(sel.astype(BF16), tri_ref[...]) + run[...]
    run[...] = run[...] + jnp.sum(sel, axis=1, keepdims=True)
    ws = [jnp.sum(jnp.where(hit, scores, 0.0), axis=0, keepdims=True) for _, hit in picks]
    wsum = ws[0]
    for w in ws[1:]:
        wsum = wsum + w
    for k, (f, hit) in enumerate(picks):
        idx_ref[k:k + 1, :] = f.astype(I32)
        wt_ref[k:k + 1, :] = ws[k] / wsum * ROUTED_SCALE
        pos_ref[k:k + 1, :] = jnp.sum(jnp.where(hit, before, 0.0), axis=0, keepdims=True).astype(I32)
    cnt_ref[...] = jnp.broadcast_to(run[...], cnt_ref.shape).astype(I32)


def _route(h, w_router_t, router_bias):
    t = h.shape[0]
    tm = ROUTER_TM
    tri = (jnp.arange(tm)[:, None] < jnp.arange(tm)[None, :]).astype(BF16)
    slot = pl.BlockSpec((TOP_K, tm), lambda i: (0, i))
    idx, wts, pos, cnt = pl.pallas_call(
        _router_kernel,
        grid=(t // tm,),
        in_specs=[pl.BlockSpec((tm, D_MODEL), lambda i: (i, 0)),
                  pl.BlockSpec((N_EXPERTS, D_MODEL), lambda i: (0, 0)),
                  pl.BlockSpec((N_EXPERTS, 1), lambda i: (0, 0)),
                  pl.BlockSpec((tm, tm), lambda i: (0, 0))],
        out_specs=[slot, slot, slot, pl.BlockSpec((N_EXPERTS, 128), lambda i: (0, 0))],
        out_shape=[jax.ShapeDtypeStruct((TOP_K, t), I32), jax.ShapeDtypeStruct((TOP_K, t), F32),
                   jax.ShapeDtypeStruct((TOP_K, t), I32), jax.ShapeDtypeStruct((N_EXPERTS, 128), I32)],
        scratch_shapes=[pltpu.VMEM((N_EXPERTS, 1), F32)],
        compiler_params=_params("arbitrary"),
        name="router",
    )(h, w_router_t, router_bias.reshape(N_EXPERTS, 1), tri)
    return idx, wts, pos, cnt[:, 0]


DISPATCH_TM = 256


def _row_copy(src_ref, src_row, dst_ref, dst_row, sem):
    return pltpu.make_async_copy(src_ref.at[pl.ds(src_row, 1)], dst_ref.at[pl.ds(dst_row, 1)], sem)


def _dispatch_kernel(starts_ref, idx_ref, pos_ref, h_ref, xs_ref, sem):
    def issue(j, c):
        for k in range(TOP_K):
            dst = starts_ref[idx_ref[k, j]] + pos_ref[k, j]
            _row_copy(h_ref, j, xs_ref, dst, sem).start()
        return c

    lax.fori_loop(0, DISPATCH_TM, issue, 0)

    def drain(j, c):
        for k in range(TOP_K):
            _row_copy(h_ref, 0, xs_ref, 0, sem).wait()
        return c

    lax.fori_loop(0, DISPATCH_TM, drain, 0)


def _dispatch(starts, idx, pos, h):
    t = h.shape[0]
    tm = DISPATCH_TM
    slot = pl.BlockSpec((TOP_K, tm), lambda i, s: (0, i), memory_space=pltpu.SMEM)
    return pl.pallas_call(
        _dispatch_kernel,
        grid_spec=pltpu.PrefetchScalarGridSpec(
            num_scalar_prefetch=1,
            grid=(t // tm,),
            in_specs=[slot, slot, pl.BlockSpec((tm, D_MODEL), lambda i, s: (i, 0))],
            out_specs=pl.BlockSpec(memory_space=pl.ANY),
            scratch_shapes=[pltpu.SemaphoreType.DMA],
        ),
        out_shape=jax.ShapeDtypeStruct((t * TOP_K, D_MODEL), F32),
        compiler_params=_params("arbitrary"),
        name="dispatch",
    )(starts, idx, pos, h)


EXPERT_BM = 256


def _expert_kernel(blk_ref, exp_ref, lo_ref, hi_ref, first_ref, xs_ref, wg_ref, wu_ref, wd_ref, ys_ref):
    v = pl.program_id(0)
    x = xs_ref[...].astype(BF16)
    gate = _dot(x, wg_ref[0].astype(BF16))
    up = _dot(x, wu_ref[0].astype(BF16))
    hid = (_silu(gate) * up).astype(BF16)
    y = _dot(hid, wd_ref[0].astype(BF16))
    row = lax.broadcasted_iota(I32, (EXPERT_BM, 1), 0)
    y = jnp.where((row >= lo_ref[v]) & (row < hi_ref[v]), y, 0.0)

    @pl.when(first_ref[v] == 1)
    def _():
        ys_ref[...] = y

    @pl.when(first_ref[v] == 0)
    def _():
        ys_ref[...] = ys_ref[...] + y


def _visits(counts, n_rows):
    bm = EXPERT_BM
    nblk = n_rows // bm
    nv = nblk + N_EXPERTS - 1
    ends = jnp.cumsum(counts)
    starts = ends - counts
    nb_e = jnp.where(counts > 0, (ends - 1) // bm - starts // bm + 1, 0)
    vend = jnp.cumsum(nb_e)
    vstart = vend - nb_e
    v = jnp.arange(nv, dtype=I32)
    live = v < vend[-1]
    vc = jnp.minimum(v, vend[-1] - 1)
    e = jnp.minimum(jnp.searchsorted(vend, vc, side='right'), N_EXPERTS - 1).astype(I32)
    blk = (starts[e] // bm + (vc - vstart[e])).astype(I32)
    lo = jnp.clip(starts[e] - blk * bm, 0, bm)
    hi = jnp.clip(ends[e] - blk * bm, 0, bm)
    first = jnp.concatenate([jnp.ones((1,), I32), (blk[1:] != blk[:-1]).astype(I32)])
    lo = jnp.where(live, lo, 0).astype(I32)
    hi = jnp.where(live, hi, 0).astype(I32)
    first = jnp.where(live, first, 0).astype(I32)
    return starts.astype(I32), blk, e, lo, hi, first


def _experts(visits, xs, w_gate, w_up, w_down):
    blk, e, lo, hi, first = visits
    n_rows = xs.shape[0]
    bm = EXPERT_BM
    nv = blk.shape[0]
    return pl.pallas_call(
        _expert_kernel,
        grid_spec=pltpu.PrefetchScalarGridSpec(
            num_scalar_prefetch=5,
            grid=(nv,),
            in_specs=[pl.BlockSpec((bm, D_MODEL), lambda v, b, e, *_: (b[v], 0)),
                      pl.BlockSpec((1, D_MODEL, EXPERT_FF), lambda v, b, e, *_: (e[v], 0, 0)),
                      pl.BlockSpec((1, D_MODEL, EXPERT_FF), lambda v, b, e, *_: (e[v], 0, 0)),
                      pl.BlockSpec((1, EXPERT_FF, D_MODEL), lambda v, b, e, *_: (e[v], 0, 0))],
            out_specs=pl.BlockSpec((bm, D_MODEL), lambda v, b, e, *_: (b[v], 0)),
        ),
        out_shape=jax.ShapeDtypeStruct((n_rows, D_MODEL), F32),
        compiler_params=_params("arbitrary"),
        name="experts",
    )(blk, e, lo, hi, first, xs, w_gate, w_up, w_down)


COMBINE_TM = 128


def _combine_kernel(starts_ref, idx_ref, pos_ref, wt_ref, h_ref, sg_ref, su_ref, sd_ref, g_ref, beta_ref,
                    ys_ref, o_ref, ybuf, sem):
    tm = COMBINE_TM

    def issue(j, c):
        for k in range(TOP_K):
            src = starts_ref[idx_ref[k, j]] + pos_ref[k, j]
            _row_copy(ys_ref, src, ybuf.at[k], j, sem).start()
        return c

    lax.fori_loop(0, tm, issue, 0)

    h = h_ref[...]
    hb = h.astype(BF16)
    hid = (_silu(_dot(hb, sg_ref[...])) * _dot(hb, su_ref[...])).astype(BF16)
    acc = DN_ALPHA * h + _dot(hid, sd_ref[...])

    def drain(j, c):
        for k in range(TOP_K):
            _row_copy(ys_ref, 0, ybuf.at[k], 0, sem).wait()
        return c

    lax.fori_loop(0, tm, drain, 0)

    for k in range(TOP_K):
        acc = acc + wt_ref[:, k:k + 1] * ybuf[k]
    o_ref[...] = _layer_norm(acc, g_ref[...], beta_ref[...])


def _combine(starts, idx, pos, wts_tk, h, ys, ws_gate, ws_up, ws_down, ln_g, ln_b, t0, n):
    tm = COMBINE_TM
    b0 = t0 // tm
    slot = pl.BlockSpec((TOP_K, tm), lambda i, s: (0, b0 + i), memory_space=pltpu.SMEM)
    vec = pl.BlockSpec((1, D_MODEL), lambda i, s: (0, 0))
    return pl.pallas_call(
        _combine_kernel,
        grid_spec=pltpu.PrefetchScalarGridSpec(
            num_scalar_prefetch=1,
            grid=(n // tm,),
            in_specs=[slot, slot,
                      pl.BlockSpec((tm, TOP_K), lambda i, s: (b0 + i, 0)),
                      pl.BlockSpec((tm, D_MODEL), lambda i, s: (b0 + i, 0)),
                      pl.BlockSpec((D_MODEL, EXPERT_FF), lambda i, s: (0, 0)),
                      pl.BlockSpec((D_MODEL, EXPERT_FF), lambda i, s: (0, 0)),
                      pl.BlockSpec((EXPERT_FF, D_MODEL), lambda i, s: (0, 0)),
                      vec, vec,
                      pl.BlockSpec(memory_space=pl.ANY)],
            out_specs=pl.BlockSpec((tm, D_MODEL), lambda i, s: (i, 0)),
            scratch_shapes=[pltpu.VMEM((TOP_K, tm, D_MODEL), F32), pltpu.SemaphoreType.DMA],
        ),
        out_shape=jax.ShapeDtypeStruct((n, D_MODEL), F32),
        compiler_params=_params("arbitrary"),
        name="combine",
    )(starts, idx, pos, wts_tk, h, ws_gate, ws_up, ws_down, ln_g, ln_b, ys)


def _kv_cache(kv_tail, g, keep):
    k = kv_tail[-keep:, g * GROUP_W:(g + 1) * GROUP_W]
    v = kv_tail[-keep:, ATTN_W + g * GROUP_W:ATTN_W + (g + 1) * GROUP_W]
    return jnp.stack([k, v], axis=1).reshape(keep, 2, HEADS_PER_GROUP, HEAD_DIM)


def kernel(x_prompt, x_sample, cache_kv_w128, cache_kv_w512, cache_kv_w2048, state_conv, w_in, b_in, w_dw, b_dw,
           conv_ln_g, conv_ln_b, w_o, b_o, ln1_g, ln1_b, w_router, router_bias, w_gate, w_up, w_down, ws_gate,
           ws_up, ws_down, ln2_g, ln2_b):
    assert w_in.shape[0] == DEPTH == 1
    batch, seq, _ = x_prompt.shape
    dec_batch, dec_seq, _ = x_sample.shape
    assert batch == 1
    n_s = dec_batch * dec_seq
    caches = (cache_kv_w128, cache_kv_w512, cache_kv_w2048)
    row = lambda a: a[0].reshape(1, -1)

    w_in_b = w_in[0].astype(BF16)
    b_in_r = row(b_in)
    w_o_b = w_o[0].astype(BF16)
    conv_w = (w_dw[0], row(b_dw), row(conv_ln_g), row(conv_ln_b))
    ln1 = (row(ln1_g), row(ln1_b))

    xp = x_prompt[0]
    keep_p = min(max(w for w, _ in DIL_GROUPS), seq)
    qkv_p, u_p, kv_p = _project(xp, w_in_b, b_in_r, 512, keep_p, BF16)
    attn_p = [_attention_prompt(qkv_p, g) for g in range(N_GROUPS)]
    conv_p = _conv_prompt(u_p, *conv_w, 512)
    h_p = _outproj([a[0] for a in attn_p], [a[1] for a in attn_p], conv_p, xp, w_o_b, row(b_o), *ln1, 512)

    xs = x_sample.reshape(n_s, D_MODEL)
    qkv_s, u_s, kv_s = _project(xs, w_in_b, b_in_r, n_s, n_s, F32)
    attn_s = [_attention_sample(qkv_s, caches[g][0].reshape(dec_batch, -1, 2 * GROUP_W), g, dec_seq)
              for g in range(N_GROUPS)]
    u_hist = jnp.concatenate([state_conv[0], u_s.reshape(dec_batch, dec_seq, CONV_CH)], axis=1)
    conv_s = _conv_sample(u_hist, *conv_w, dec_seq)
    h_s = _outproj([a[0] for a in attn_s], [a[1] for a in attn_s], conv_s, xs, w_o_b, row(b_o), *ln1, n_s)

    h = jnp.concatenate([h_p, h_s], axis=0)
    idx, wts, pos, counts = _route(h, w_router[0].T.astype(BF16), router_bias[0])
    starts, *visits = _visits(counts, h.shape[0] * TOP_K)
    x_sorted = _dispatch(starts, idx, pos, h)
    y_sorted = _experts(visits, x_sorted, w_gate[0], w_up[0], w_down[0])
    shared = (ws_gate[0].astype(BF16), ws_up[0].astype(BF16), ws_down[0].astype(BF16))
    comb = functools.partial(_combine, starts, idx, pos, wts.T, h, y_sorted, *shared, row(ln2_g), row(ln2_b))
    y_p = comb(0, seq)
    y_s = comb(seq, n_s)

    kv_prompt = [_kv_cache(kv_p, g, min(w, seq))[None, None] for g, (w, _) in enumerate(DIL_GROUPS)]
    assert seq >= CONV_WIDTH - 1
    conv_prompt = u_p[-(CONV_WIDTH - 1):]
    kv_s4 = kv_s.reshape(dec_batch, dec_seq, 2, N_GROUPS, HEADS_PER_GROUP, HEAD_DIM)
    kv_sample = [kv_s4[:, :, :, g][None] for g in range(N_GROUPS)]
    conv_sample = u_hist[:, -(CONV_WIDTH - 1):]
    return (y_p[None], y_s.reshape(dec_batch, dec_seq, D_MODEL), *kv_prompt, conv_prompt[None, None],
            *kv_sample, conv_sample[None])
```

```python
import functools

import jax
import jax.numpy as jnp
import numpy as np
from jax import lax
from jax.experimental import pallas as pl
from jax.experimental.pallas import tpu as pltpu

F32 = jnp.float32
BF16 = jnp.bfloat16
I32 = jnp.int32

D_MODEL = 1024
HEAD_DIM = 64
HEADS_PER_GROUP = 4
GROUP_W = HEADS_PER_GROUP * HEAD_DIM
DIL_GROUPS = ((128, 1), (512, 4), (2048, 16))
N_GROUPS = len(DIL_GROUPS)
ATTN_W = N_GROUPS * GROUP_W
CONV_CH = D_MODEL - ATTN_W
CONV_WIDTH = 31
IN_W = 3 * ATTN_W + 2 * CONV_CH
BAND = 128
N_EXPERTS = 256
TOP_K = 8
N_EXPERT_GROUPS = 8
EXPERTS_PER_GROUP = N_EXPERTS // N_EXPERT_GROUPS
TOPK_GROUPS = 4
EXPERT_FF = 256
ROUTED_SCALE = 2.5
DEPTH = 1
DN_ALPHA = (2 * DEPTH) ** 0.25
LN_EPS = 1e-5
MASKED = -1e30

VMEM_LIMIT_BYTES = 56 * 1024 * 1024


def _params(*sem):
    return pltpu.CompilerParams(dimension_semantics=sem, vmem_limit_bytes=VMEM_LIMIT_BYTES)


def _dot(a, b):
    return jnp.dot(a, b, preferred_element_type=F32)


def _dot_nt(a, b):
    return lax.dot_general(a, b, (((1,), (1,)), ((), ())), preferred_element_type=F32)


def _layer_norm(x, g, b):
    mu = jnp.mean(x, axis=-1, keepdims=True)
    xc = x - mu
    var = jnp.mean(xc * xc, axis=-1, keepdims=True)
    return xc * lax.rsqrt(var + LN_EPS) * g + b


def _silu(x):
    return x * jax.nn.sigmoid(x)


def _alibi_slopes():
    n = N_GROUPS * HEADS_PER_GROUP
    h = jnp.arange(1, n + 1, dtype=F32)
    return (2.0 ** (-8.0 * h / n)).reshape(N_GROUPS, HEADS_PER_GROUP)


LANES = 128


def _proj_kernel(x_ref, w_ref, b_ref, *refs, dils, tm):
    qkv_refs, (u_ref, kv_ref, zs) = refs[:N_GROUPS], refs[N_GROUPS:]
    x = x_ref[...].astype(BF16)
    for part in range(3):
        for g in range(N_GROUPS):
            c0 = part * ATTN_W + g * GROUP_W
            z = _dot(x, w_ref[:, c0:c0 + GROUP_W]) + b_ref[:, c0:c0 + GROUP_W]
            if part > 0:
                kv_ref[:, c0 - ATTN_W:c0 - ATTN_W + GROUP_W] = z
            out, dil = qkv_refs[g], dils[g]
            if dil == 1:
                out[:, part * GROUP_W:(part + 1) * GROUP_W] = z.astype(out.dtype)
                continue
            for half in range(GROUP_W // LANES):
                zs[half] = z[:, half * LANES:(half + 1) * LANES]
            for r in range(dil):
                for half in range(GROUP_W // LANES):
                    c = r * 3 * GROUP_W + part * GROUP_W + half * LANES
                    out[:, c:c + LANES] = zs[half, pl.ds(r, tm // dil, stride=dil), :].astype(out.dtype)
    c0 = 3 * ATTN_W
    a = _dot(x, w_ref[:, c0:c0 + CONV_CH]) + b_ref[:, c0:c0 + CONV_CH]
    gate = _dot(x, w_ref[:, c0 + CONV_CH:]) + b_ref[:, c0 + CONV_CH:]
    u_ref[...] = a * jax.nn.sigmoid(gate)


def _project(x, w_in, b_in, tm, keep, qkv_dtype, dils):
    n = x.shape[0]
    nt = n // tm
    nk = keep // tm
    return pl.pallas_call(
        functools.partial(_proj_kernel, dils=dils, tm=tm),
        grid=(nt,),
        in_specs=[
            pl.BlockSpec((tm, D_MODEL), lambda i: (i, 0)),
            pl.BlockSpec((D_MODEL, IN_W), lambda i: (0, 0)),
            pl.BlockSpec((1, IN_W), lambda i: (0, 0)),
        ],
        out_specs=[pl.BlockSpec((tm // d, d * 3 * GROUP_W), lambda i: (i, 0)) for d in dils] + [
            pl.BlockSpec((tm, CONV_CH), lambda i: (i, 0)),
            pl.BlockSpec((tm, 2 * ATTN_W), lambda i: (jnp.maximum(i - (nt - nk), 0), 0)),
        ],
        out_shape=[jax.ShapeDtypeStruct((n // d, d * 3 * GROUP_W), qkv_dtype) for d in dils] + [
            jax.ShapeDtypeStruct((n, CONV_CH), F32),
            jax.ShapeDtypeStruct((keep, 2 * ATTN_W), F32),
        ],
        scratch_shapes=[pltpu.VMEM((GROUP_W // LANES, tm, LANES), F32)],
        compiler_params=_params("arbitrary"),
        name="proj",
    )(x, w_in, b_in)


def _head_select(parts, rows):
    col = lax.broadcasted_iota(I32, (rows, GROUP_W), 1) // HEAD_DIM
    out = jnp.broadcast_to(parts[-1], (rows, GROUP_W))
    for h in range(HEADS_PER_GROUP - 2, -1, -1):
        out = jnp.where(col == h, parts[h], out)
    return out


def _head_rows(q, rows):
    col = lax.broadcasted_iota(I32, (rows, GROUP_W), 1) // HEAD_DIM
    return jnp.concatenate([jnp.where(col == h, q, jnp.zeros_like(q)) for h in range(HEADS_PER_GROUP)], axis=0)


def _softmax_pv(s, v, rows):
    m = jnp.max(s, axis=-1, keepdims=True)
    e = jnp.exp(s - m)
    l = jnp.sum(e, axis=-1, keepdims=True)
    pv = _dot(e.astype(BF16), v) / l
    lse = m + jnp.log(l)
    o = _head_select([pv[h * rows:(h + 1) * rows] for h in range(HEADS_PER_GROUP)], rows)
    lse_x = _head_select([lse[h * rows:(h + 1) * rows] for h in range(HEADS_PER_GROUP)], rows)
    return o, lse_x


def _attn_kernel(q_ref, kp_ref, kc_ref, vp_ref, vc_ref, bias_ref, o_ref, lse_ref):
    qm = _head_rows(q_ref[...], BAND)
    k = jnp.concatenate([kp_ref[...], kc_ref[...]], axis=0)
    v = jnp.concatenate([vp_ref[...], vc_ref[...]], axis=0)
    s = _dot_nt(qm, k) * HEAD_DIM ** -0.5 + bias_ref[0]
    o, lse_x = _softmax_pv(s, v, BAND)
    o_ref[...] = o.astype(o_ref.dtype)
    lse_ref[...] = lse_x


def _prompt_bias(g, dil):
    slopes = _alibi_slopes()[g]
    qi = jnp.arange(BAND)[:, None]
    kj = jnp.arange(2 * BAND)[None, :]
    steps = qi + BAND - kj
    nk = DIL_GROUPS[g][0] // dil
    valid = (steps >= 0) & (steps <= nk)
    bias = -slopes[:, None, None] * (steps * dil).astype(F32)
    inner = jnp.where(valid[None], bias, MASKED)
    first = jnp.where((valid & (kj >= BAND))[None], bias, MASKED)
    return jnp.stack([first, inner]).reshape(2, HEADS_PER_GROUP * BAND, 2 * BAND)


def _attention_prompt(view, g):
    win, dil = DIL_GROUPS[g]
    n_cls = view.shape[0]
    assert win // dil <= BAND and n_cls % BAND == 0 and view.shape[1] == dil * 3 * GROUP_W
    cur = lambda part: pl.BlockSpec((BAND, GROUP_W), lambda r, b: (b, r * 3 + part))
    prev = lambda part: pl.BlockSpec((BAND, GROUP_W), lambda r, b: (jnp.maximum(b - 1, 0), r * 3 + part))
    return pl.pallas_call(
        _attn_kernel,
        grid=(dil, n_cls // BAND),
        in_specs=[cur(0), prev(1), cur(1), prev(2), cur(2),
                  pl.BlockSpec((1, HEADS_PER_GROUP * BAND, 2 * BAND), lambda r, b: (jnp.minimum(b, 1), 0, 0))],
        out_specs=[pl.BlockSpec((BAND, GROUP_W), lambda r, b: (b, r)),
                   pl.BlockSpec((BAND, GROUP_W), lambda r, b: (b, r))],
        out_shape=[jax.ShapeDtypeStruct((n_cls, dil * GROUP_W), BF16),
                   jax.ShapeDtypeStruct((n_cls, dil * GROUP_W), F32)],
        compiler_params=_params("arbitrary", "arbitrary"),
        name=f"attn_g{g}",
    )(view, view, view, view, view, _prompt_bias(g, dil))


def _attn_sample_kernel(q_ref, k_ref, v_ref, cache_ref, bias_ref, o_ref, lse_ref, kbuf, vbuf, *, n_buf, t):
    kbuf[0:n_buf, :] = cache_ref[0, :, 0:GROUP_W].astype(BF16)
    vbuf[0:n_buf, :] = cache_ref[0, :, GROUP_W:].astype(BF16)
    pad = jnp.zeros((BAND - t, GROUP_W), F32)
    kbuf[n_buf:, :] = jnp.concatenate([k_ref[...], pad], axis=0).astype(BF16)
    vbuf[n_buf:, :] = jnp.concatenate([v_ref[...], pad], axis=0).astype(BF16)
    qm = _head_rows(q_ref[...], t).astype(BF16)
    s = _dot_nt(qm, kbuf[...]) * HEAD_DIM ** -0.5 + bias_ref[...]
    o, lse_x = _softmax_pv(s, vbuf[...], t)
    o_ref[...] = o
    lse_ref[...] = lse_x


def _sample_bias(g, n_buf, t):
    win, dil = DIL_GROUPS[g]
    slopes = _alibi_slopes()[g]
    tq = jnp.arange(t)[:, None]
    j = jnp.arange(n_buf + BAND)[None, :]
    dist = n_buf + tq - j
    valid = (dist >= 0) & (dist % dil == 0) & (dist <= win) & (j < n_buf + t)
    bias = -slopes[:, None, None] * dist.astype(F32)
    return jnp.where(valid[None], bias, MASKED).reshape(HEADS_PER_GROUP * t, n_buf + BAND)


def _attention_sample(qkv, cache, g, t):
    nb, n_buf = cache.shape[0], cache.shape[1]
    col = lambda part: pl.BlockSpec((t, GROUP_W), lambda i: (i, part))
    kern = functools.partial(_attn_sample_kernel, n_buf=n_buf, t=t)
    return pl.pallas_call(
        kern,
        grid=(nb,),
        in_specs=[col(0), col(1), col(2),
                  pl.BlockSpec((1, n_buf, 2 * GROUP_W), lambda i: (i, 0, 0)),
                  pl.BlockSpec((HEADS_PER_GROUP * t, n_buf + BAND), lambda i: (0, 0))],
        out_specs=[pl.BlockSpec((t, GROUP_W), lambda i: (i, 0)),
                   pl.BlockSpec((t, GROUP_W), lambda i: (i, 0))],
        out_shape=[jax.ShapeDtypeStruct((nb * t, GROUP_W), F32),
                   jax.ShapeDtypeStruct((nb * t, GROUP_W), F32)],
        scratch_shapes=[pltpu.VMEM((n_buf + BAND, GROUP_W), BF16),
                        pltpu.VMEM((n_buf + BAND, GROUP_W), BF16)],
        compiler_params=_params("arbitrary"),
        name=f"attn_sample_g{g}",
    )(qkv, qkv, qkv, cache, _sample_bias(g, n_buf, t))


CONV_HALO = 32


def _conv_tail(acc, b_ref, g_ref, beta_ref):
    return _silu(_layer_norm(acc + b_ref[...], g_ref[...], beta_ref[...]))


def _conv_prompt_kernel(halo_ref, u_ref, w_ref, b_ref, g_ref, beta_ref, o_ref, hist, *, tm):
    i = pl.program_id(0)
    hist[0:CONV_HALO, :] = jnp.where(i == 0, 0.0, halo_ref[...])
    hist[CONV_HALO:, :] = u_ref[...]
    off = CONV_HALO - (CONV_WIDTH - 1)
    acc = jnp.zeros((tm, CONV_CH), F32)
    for j in range(CONV_WIDTH):
        acc = acc + w_ref[j:j + 1, :] * hist[off + j:off + j + tm, :]
    o_ref[...] = _conv_tail(acc, b_ref, g_ref, beta_ref).astype(o_ref.dtype)


def _conv_prompt(u, w_dw, b_dw, ln_g, ln_b, tm):
    n = u.shape[0]
    vec = pl.BlockSpec((1, CONV_CH), lambda i: (0, 0))
    return pl.pallas_call(
        functools.partial(_conv_prompt_kernel, tm=tm),
        grid=(n // tm,),
        in_specs=[pl.BlockSpec((CONV_HALO, CONV_CH), lambda i: (jnp.maximum(i * (tm // CONV_HALO) - 1, 0), 0)),
                  pl.BlockSpec((tm, CONV_CH), lambda i: (i, 0)),
                  pl.BlockSpec((CONV_WIDTH, CONV_CH), lambda i: (0, 0)), vec, vec, vec],
        out_specs=pl.BlockSpec((tm, CONV_CH), lambda i: (i, 0)),
        out_shape=jax.ShapeDtypeStruct((n, CONV_CH), BF16),
        scratch_shapes=[pltpu.VMEM((CONV_HALO + tm, CONV_CH), F32)],
        compiler_params=_params("arbitrary"),
        name="conv_prompt",
    )(u, u, w_dw, b_dw, ln_g, ln_b)


def _conv_sample_kernel(hist_ref, w_ref, b_ref, g_ref, beta_ref, o_ref, *, t):
    acc = jnp.zeros((t, CONV_CH), F32)
    for j in range(CONV_WIDTH):
        acc = acc + w_ref[j:j + 1, :] * hist_ref[0, j:j + t, :]
    o_ref[...] = _conv_tail(acc, b_ref, g_ref, beta_ref)


def _conv_sample(u_hist, w_dw, b_dw, ln_g, ln_b, t):
    nb, rows = u_hist.shape[0], u_hist.shape[1]
    vec = pl.BlockSpec((1, CONV_CH), lambda i: (0, 0))
    return pl.pallas_call(
        functools.partial(_conv_sample_kernel, t=t),
        grid=(nb,),
        in_specs=[pl.BlockSpec((1, rows, CONV_CH), lambda i: (i, 0, 0)),
                  pl.BlockSpec((CONV_WIDTH, CONV_CH), lambda i: (0, 0)), vec, vec, vec],
        out_specs=pl.BlockSpec((t, CONV_CH), lambda i: (i, 0)),
        out_shape=jax.ShapeDtypeStruct((nb * t, CONV_CH), F32),
        compiler_params=_params("arbitrary"),
        name="conv_sample",
    )(u_hist, w_dw, b_dw, ln_g, ln_b)


ROW_SLABS = D_MODEL // LANES


def _token_order(ref, dil, scr, tm):
    if dil == 1:
        return ref[...].astype(F32)
    for r in range(dil):
        for half in range(GROUP_W // LANES):
            c = r * GROUP_W + half * LANES
            scr[half, pl.ds(r, tm // dil, stride=dil), :] = ref[:, c:c + LANES].astype(F32)
    return jnp.concatenate([scr[half] for half in range(GROUP_W // LANES)], axis=1)


def _store_rows(ref, val):
    for c in range(ROW_SLABS):
        ref[:, c, :] = val[:, c * LANES:(c + 1) * LANES]


def _load_rows(ref):
    return jnp.concatenate([ref[:, c, :] for c in range(ROW_SLABS)], axis=1)


def _outproj_kernel(o0, o1, o2, l0, l1, l2, c_ref, x_ref, w_ref, b_ref, g_ref, beta_ref, h_ref, *scr, dils, tm):
    os_ = [_token_order(r, d, scr[2 * g], tm) for g, (r, d) in enumerate(zip((o0, o1, o2), dils))]
    ls = [_token_order(r, d, scr[2 * g + 1], tm) for g, (r, d) in enumerate(zip((l0, l1, l2), dils))]
    m = jnp.maximum(jnp.maximum(ls[0], ls[1]), ls[2])
    es = [jnp.exp(l - m) for l in ls]
    inv = 1.0 / (es[0] + es[1] + es[2])
    mixed = b_ref[...]
    for g in range(N_GROUPS):
        a = (os_[g] * (es[g] * inv)).astype(BF16)
        mixed = mixed + _dot(a, w_ref[g * GROUP_W:(g + 1) * GROUP_W, :])
    mixed = mixed + _dot(c_ref[...].astype(BF16), w_ref[ATTN_W:, :])
    _store_rows(h_ref, _layer_norm(DN_ALPHA * x_ref[...] + mixed, g_ref[...], beta_ref[...]))


def _outproj(os_, lses, conv, x, w_o, b_o, ln_g, ln_b, tm, dils):
    n = x.shape[0]
    grp = [pl.BlockSpec((tm // d, d * GROUP_W), lambda i: (i, 0)) for d in dils]
    vec = pl.BlockSpec((1, D_MODEL), lambda i: (0, 0))
    return pl.pallas_call(
        functools.partial(_outproj_kernel, dils=dils, tm=tm),
        grid=(n // tm,),
        in_specs=grp + grp + [pl.BlockSpec((tm, CONV_CH), lambda i: (i, 0)),
                              pl.BlockSpec((tm, D_MODEL), lambda i: (i, 0)),
                              pl.BlockSpec((D_MODEL, D_MODEL), lambda i: (0, 0)), vec, vec, vec],
        out_specs=pl.BlockSpec((tm, ROW_SLABS, LANES), lambda i: (i, 0, 0)),
        out_shape=jax.ShapeDtypeStruct((n, ROW_SLABS, LANES), F32),
        scratch_shapes=[pltpu.VMEM((GROUP_W // LANES, tm, LANES), F32) for _ in range(2 * N_GROUPS)],
        compiler_params=_params("arbitrary"),
        name="outproj",
    )(*os_, *lses, conv, x, w_o, b_o, ln_g, ln_b)


ROUTER_TM = 256


def _first_index(hit, idx, limit, axis):
    return jnp.min(jnp.where(hit, idx, limit), axis=axis, keepdims=True)


def _router_kernel(h_ref, w_ref, rb_ref, tri_ref, idx_ref, wt_ref, pos_ref, cnt_ref, run):
    i = pl.program_id(0)
    tm = ROUTER_TM

    @pl.when(i == 0)
    def _():
        run[...] = jnp.zeros_like(run)

    logits = _dot_nt(w_ref[...], _load_rows(h_ref).astype(BF16))
    scores = jax.nn.sigmoid(logits)
    biased = scores + rb_ref[...]
    groups = [biased[g * EXPERTS_PER_GROUP:(g + 1) * EXPERTS_PER_GROUP] for g in range(N_EXPERT_GROUPS)]
    ei = lax.broadcasted_iota(I32, (EXPERTS_PER_GROUP, tm), 0).astype(F32)
    gs = []
    for bg in groups:
        m1 = jnp.max(bg, axis=0, keepdims=True)
        f1 = _first_index(bg == m1, ei, float(EXPERTS_PER_GROUP), 0)
        m2 = jnp.max(jnp.where(ei == f1, -jnp.inf, bg), axis=0, keepdims=True)
        gs.append(m1 + m2)
    gs = jnp.concatenate(gs, axis=0)
    gi = lax.broadcasted_iota(I32, gs.shape, 0).astype(F32)
    keep = jnp.zeros(gs.shape, F32)
    cur = gs
    for _ in range(TOPK_GROUPS):
        m = jnp.max(cur, axis=0, keepdims=True)
        f = _first_index(cur == m, gi, float(N_EXPERT_GROUPS), 0)
        hit = gi == f
        keep = jnp.where(hit, 1.0, keep)
        cur = jnp.where(hit, -jnp.inf, cur)
    masked = jnp.concatenate([jnp.where(keep[g:g + 1] > 0.0, bg, -jnp.inf) for g, bg in enumerate(groups)], axis=0)
    xi = lax.broadcasted_iota(I32, (N_EXPERTS, tm), 0).astype(F32)
    cur = masked
    sel = jnp.zeros((N_EXPERTS, tm), F32)
    picks = []
    for _ in range(TOP_K):
        m = jnp.max(cur, axis=0, keepdims=True)
        f = _first_index(cur == m, xi, float(N_EXPERTS), 0)
        hit = xi == f
        picks.append((f, hit))
        sel = jnp.where(hit, 1.0, sel)
        cur = jnp.where(hit, -jnp.inf, cur)
    before = _dot(sel.astype(BF16), tri_ref[...]) + run[...]
    run[...] = run[...] + jnp.sum(sel, axis=1, keepdims=True)
    ws = [jnp.sum(jnp.where(hit, scores, 0.0), axis=0, keepdims=True) for _, hit in picks]
    wsum = ws[0]
    for w in ws[1:]:
        wsum = wsum + w
    for k, (f, hit) in enumerate(picks):
        idx_ref[k:k + 1, :] = f.astype(I32)
        wt_ref[k:k + 1, :] = ws[k] / wsum * ROUTED_SCALE
        pos_ref[k:k + 1, :] = jnp.sum(jnp.where(hit, before, 0.0), axis=0, keepdims=True).astype(I32)
    cnt_ref[...] = jnp.broadcast_to(run[...], cnt_ref.shape).astype(I32)


def _route(h, w_router_t, router_bias):
    t = h.shape[0]
    tm = ROUTER_TM
    tri = (jnp.arange(tm)[:, None] < jnp.arange(tm)[None, :]).astype(BF16)
    slot = pl.BlockSpec((TOP_K, tm), lambda i: (0, i))
    idx, wts, pos, cnt = pl.pallas_call(
        _router_kernel,
        grid=(t // tm,),
        in_specs=[pl.BlockSpec((tm, ROW_SLABS, LANES), lambda i: (i, 0, 0)),
                  pl.BlockSpec((N_EXPERTS, D_MODEL), lambda i: (0, 0)),
                  pl.BlockSpec((N_EXPERTS, 1), lambda i: (0, 0)),
                  pl.BlockSpec((tm, tm), lambda i: (0, 0))],
        out_specs=[slot, slot, slot, pl.BlockSpec((N_EXPERTS, 128), lambda i: (0, 0))],
        out_shape=[jax.ShapeDtypeStruct((TOP_K, t), I32), jax.ShapeDtypeStruct((TOP_K, t), F32),
                   jax.ShapeDtypeStruct((TOP_K, t), I32), jax.ShapeDtypeStruct((N_EXPERTS, 128), I32)],
        scratch_shapes=[pltpu.VMEM((N_EXPERTS, 1), F32)],
        compiler_params=_params("arbitrary"),
        name="router",
    )(h, w_router_t, router_bias.reshape(N_EXPERTS, 1), tri)
    return idx, wts, pos, cnt[:, 0]


DISPATCH_TM = 256


def _dispatch_kernel(starts_ref, idx_ref, pos_ref, h_ref, xs_ref, sem):
    def issue(j, c):
        for k in range(TOP_K):
            dst = starts_ref[idx_ref[k, j]] + pos_ref[k, j]
            pltpu.make_async_copy(h_ref.at[j], xs_ref.at[dst], sem).start()
        return c

    lax.fori_loop(0, DISPATCH_TM, issue, 0)
    for k in range(TOP_K):
        pltpu.make_async_copy(h_ref, xs_ref.at[pl.ds(0, DISPATCH_TM)], sem).wait()


def _dispatch(starts, idx, pos, h):
    t = h.shape[0]
    tm = DISPATCH_TM
    slot = pl.BlockSpec((TOP_K, tm), lambda i, s: (0, i), memory_space=pltpu.SMEM)
    return pl.pallas_call(
        _dispatch_kernel,
        grid_spec=pltpu.PrefetchScalarGridSpec(
            num_scalar_prefetch=1,
            grid=(t // tm,),
            in_specs=[slot, slot, pl.BlockSpec((tm, ROW_SLABS, LANES), lambda i, s: (i, 0, 0))],
            out_specs=pl.BlockSpec(memory_space=pl.ANY),
            scratch_shapes=[pltpu.SemaphoreType.DMA],
        ),
        out_shape=jax.ShapeDtypeStruct((t * TOP_K, ROW_SLABS, LANES), F32),
        compiler_params=_params("arbitrary"),
        name="dispatch",
    )(starts, idx, pos, h)


EXPERT_BM = 256


def _expert_kernel(blk_ref, exp_ref, lo_ref, hi_ref, first_ref, xs_ref, wg_ref, wu_ref, wd_ref, ys_ref):
    v = pl.program_id(0)
    x = _load_rows(xs_ref).astype(BF16)
    gate = _dot(x, wg_ref[0].astype(BF16))
    up = _dot(x, wu_ref[0].astype(BF16))
    hid = (_silu(gate) * up).astype(BF16)
    y = _dot(hid, wd_ref[0].astype(BF16))
    row = lax.broadcasted_iota(I32, (EXPERT_BM, 1), 0)
    y = jnp.where((row >= lo_ref[v]) & (row < hi_ref[v]), y, 0.0)

    @pl.when(first_ref[v] == 1)
    def _():
        _store_rows(ys_ref, y)

    @pl.when(first_ref[v] == 0)
    def _():
        _store_rows(ys_ref, _load_rows(ys_ref) + y)


def _visits(counts, n_rows):
    bm = EXPERT_BM
    nblk = n_rows // bm
    nv = nblk + N_EXPERTS - 1
    ends = jnp.cumsum(counts)
    starts = ends - counts
    nb_e = jnp.where(counts > 0, (ends - 1) // bm - starts // bm + 1, 0)
    vend = jnp.cumsum(nb_e)
    vstart = vend - nb_e
    v = jnp.arange(nv, dtype=I32)
    live = v < vend[-1]
    vc = jnp.minimum(v, vend[-1] - 1)
    e = jnp.minimum(jnp.sum(vend[None, :] <= vc[:, None], axis=1), N_EXPERTS - 1).astype(I32)
    of_e = lambda a: jnp.sum(jnp.where(e[:, None] == jnp.arange(N_EXPERTS)[None, :], a[None, :], 0), axis=1)
    starts_e, ends_e = of_e(starts), of_e(ends)
    blk = (starts_e // bm + (vc - of_e(vstart))).astype(I32)
    lo = jnp.clip(starts_e - blk * bm, 0, bm)
    hi = jnp.clip(ends_e - blk * bm, 0, bm)
    first = jnp.concatenate([jnp.ones((1,), I32), (blk[1:] != blk[:-1]).astype(I32)])
    lo = jnp.where(live, lo, 0).astype(I32)
    hi = jnp.where(live, hi, 0).astype(I32)
    first = jnp.where(live, first, 0).astype(I32)
    return starts.astype(I32), blk, e, lo, hi, first


def _experts(visits, xs, w_gate, w_up, w_down):
    blk, e, lo, hi, first = visits
    n_rows = xs.shape[0]
    bm = EXPERT_BM
    nv = blk.shape[0]
    return pl.pallas_call(
        _expert_kernel,
        grid_spec=pltpu.PrefetchScalarGridSpec(
            num_scalar_prefetch=5,
            grid=(nv,),
            in_specs=[pl.BlockSpec((bm, ROW_SLABS, LANES), lambda v, b, e, *_: (b[v], 0, 0)),
                      pl.BlockSpec((1, D_MODEL, EXPERT_FF), lambda v, b, e, *_: (e[v], 0, 0)),
                      pl.BlockSpec((1, D_MODEL, EXPERT_FF), lambda v, b, e, *_: (e[v], 0, 0)),
                      pl.BlockSpec((1, EXPERT_FF, D_MODEL), lambda v, b, e, *_: (e[v], 0, 0))],
            out_specs=pl.BlockSpec((bm, ROW_SLABS, LANES), lambda v, b, e, *_: (b[v], 0, 0)),
        ),
        out_shape=jax.ShapeDtypeStruct((n_rows, ROW_SLABS, LANES), F32),
        compiler_params=_params("arbitrary"),
        name="experts",
    )(blk, e, lo, hi, first, xs, w_gate, w_up, w_down)


COMBINE_TM = 128


def _combine_kernel(starts_ref, idx_ref, pos_ref, wt_ref, h_ref, sg_ref, su_ref, sd_ref, g_ref, beta_ref,
                    ys_ref, o_ref, ybuf, sem):
    tm = COMBINE_TM

    def issue(j, c):
        for k in range(TOP_K):
            src = starts_ref[idx_ref[k, j]] + pos_ref[k, j]
            pltpu.make_async_copy(ys_ref.at[src], ybuf.at[k, j], sem).start()
        return c

    lax.fori_loop(0, tm, issue, 0)

    h = _load_rows(h_ref)
    hb = h.astype(BF16)
    hid = (_silu(_dot(hb, sg_ref[...])) * _dot(hb, su_ref[...])).astype(BF16)
    acc = DN_ALPHA * h + _dot(hid, sd_ref[...])

    for k in range(TOP_K):
        pltpu.make_async_copy(ys_ref.at[pl.ds(0, tm)], ybuf.at[k], sem).wait()

    for k in range(TOP_K):
        acc = acc + wt_ref[:, k:k + 1] * _load_rows(ybuf.at[k])
    o_ref[...] = _layer_norm(acc, g_ref[...], beta_ref[...])


def _combine(starts, idx, pos, wts_tk, h, ys, ws_gate, ws_up, ws_down, ln_g, ln_b, t0, n):
    tm = COMBINE_TM
    b0 = t0 // tm
    slot = pl.BlockSpec((TOP_K, tm), lambda i, s: (0, b0 + i), memory_space=pltpu.SMEM)
    vec = pl.BlockSpec((1, D_MODEL), lambda i, s: (0, 0))
    return pl.pallas_call(
        _combine_kernel,
        grid_spec=pltpu.PrefetchScalarGridSpec(
            num_scalar_prefetch=1,
            grid=(n // tm,),
            in_specs=[slot, slot,
                      pl.BlockSpec((tm, TOP_K), lambda i, s: (b0 + i, 0)),
                      pl.BlockSpec((tm, ROW_SLABS, LANES), lambda i, s: (b0 + i, 0, 0)),
                      pl.BlockSpec((D_MODEL, EXPERT_FF), lambda i, s: (0, 0)),
                      pl.BlockSpec((D_MODEL, EXPERT_FF), lambda i, s: (0, 0)),
                      pl.BlockSpec((EXPERT_FF, D_MODEL), lambda i, s: (0, 0)),
                      vec, vec,
                      pl.BlockSpec(memory_space=pl.ANY)],
            out_specs=pl.BlockSpec((tm, D_MODEL), lambda i, s: (i, 0)),
            scratch_shapes=[pltpu.VMEM((TOP_K, tm, ROW_SLABS, LANES), F32), pltpu.SemaphoreType.DMA],
        ),
        out_shape=jax.ShapeDtypeStruct((n, D_MODEL), F32),
        compiler_params=_params("arbitrary"),
        name="combine",
    )(starts, idx, pos, wts_tk, h, ws_gate, ws_up, ws_down, ln_g, ln_b, ys)


def _kv_cache(kv_tail, g, keep):
    k = kv_tail[-keep:, g * GROUP_W:(g + 1) * GROUP_W]
    v = kv_tail[-keep:, ATTN_W + g * GROUP_W:ATTN_W + (g + 1) * GROUP_W]
    return jnp.stack([k, v], axis=1).reshape(keep, 2, HEADS_PER_GROUP, HEAD_DIM)


def kernel(x_prompt, x_sample, cache_kv_w128, cache_kv_w512, cache_kv_w2048, state_conv, w_in, b_in, w_dw, b_dw,
           conv_ln_g, conv_ln_b, w_o, b_o, ln1_g, ln1_b, w_router, router_bias, w_gate, w_up, w_down, ws_gate,
           ws_up, ws_down, ln2_g, ln2_b):
    assert w_in.shape[0] == DEPTH == 1
    batch, seq, _ = x_prompt.shape
    dec_batch, dec_seq, _ = x_sample.shape
    assert batch == 1
    n_s = dec_batch * dec_seq
    caches = (cache_kv_w128, cache_kv_w512, cache_kv_w2048)
    row = lambda a: a[0].reshape(1, -1)

    w_in_b = w_in[0].astype(BF16)
    b_in_r = row(b_in)
    w_o_b = w_o[0].astype(BF16)
    conv_w = (w_dw[0], row(b_dw), row(conv_ln_g), row(conv_ln_b))
    ln1 = (row(ln1_g), row(ln1_b))

    xp = x_prompt[0]
    keep_p = min(max(w for w, _ in DIL_GROUPS), seq)
    dils = tuple(d for _, d in DIL_GROUPS)
    *qkv_p, u_p, kv_p = _project(xp, w_in_b, b_in_r, 512, keep_p, BF16, dils)
    attn_p = [_attention_prompt(qkv_p[g], g) for g in range(N_GROUPS)]
    conv_p = _conv_prompt(u_p, *conv_w, 512)
    h_p = _outproj([a[0] for a in attn_p], [a[1] for a in attn_p], conv_p, xp, w_o_b, row(b_o), *ln1, 512, dils)

    xs = x_sample.reshape(n_s, D_MODEL)
    ones = (1,) * N_GROUPS
    *qkv_s, u_s, kv_s = _project(xs, w_in_b, b_in_r, n_s, n_s, F32, ones)
    attn_s = [_attention_sample(qkv_s[g], caches[g][0].reshape(dec_batch, -1, 2 * GROUP_W), g, dec_seq)
              for g in range(N_GROUPS)]
    u_hist = jnp.concatenate([state_conv[0], u_s.reshape(dec_batch, dec_seq, CONV_CH)], axis=1)
    conv_s = _conv_sample(u_hist, *conv_w, dec_seq)
    h_s = _outproj([a[0] for a in attn_s], [a[1] for a in attn_s], conv_s, xs, w_o_b, row(b_o), *ln1, n_s, ones)

    h = jnp.concatenate([h_p, h_s], axis=0)
    idx, wts, pos, counts = _route(h, w_router[0].T.astype(BF16), router_bias[0])
    starts, *visits = _visits(counts, h.shape[0] * TOP_K)
    x_sorted = _dispatch(starts, idx, pos, h)
    y_sorted = _experts(visits, x_sorted, w_gate[0], w_up[0], w_down[0])
    shared = (ws_gate[0].astype(BF16), ws_up[0].astype(BF16), ws_down[0].astype(BF16))
    comb = functools.partial(_combine, starts, idx, pos, wts.T, h, y_sorted, *shared, row(ln2_g), row(ln2_b))
    y_p = comb(0, seq)
    y_s = comb(seq, n_s)

    kv_prompt = [_kv_cache(kv_p, g, min(w, seq))[None, None] for g, (w, _) in enumerate(DIL_GROUPS)]
    assert seq >= CONV_WIDTH - 1
    conv_prompt = u_p[-(CONV_WIDTH - 1):]
    kv_s4 = kv_s.reshape(dec_batch, dec_seq, 2, N_GROUPS, HEADS_PER_GROUP, HEAD_DIM)
    kv_sample = [kv_s4[:, :, :, g][None] for g in range(N_GROUPS)]
    conv_sample = u_hist[:, -(CONV_WIDTH - 1):]
    return (y_p[None], y_s.reshape(dec_batch, dec_seq, D_MODEL), *kv_prompt, conv_prompt[None, None],
            *kv_sample, conv_sample[None])
```

```python
import functools

import jax
import jax.numpy as jnp
import numpy as np
from jax import lax
from jax.experimental import pallas as pl
from jax.experimental.pallas import tpu as pltpu

F32 = jnp.float32
BF16 = jnp.bfloat16
I32 = jnp.int32

D_MODEL = 1024
HEAD_DIM = 64
HEADS_PER_GROUP = 4
GROUP_W = HEADS_PER_GROUP * HEAD_DIM
DIL_GROUPS = ((128, 1), (512, 4), (2048, 16))
N_GROUPS = len(DIL_GROUPS)
ATTN_W = N_GROUPS * GROUP_W
CONV_CH = D_MODEL - ATTN_W
CONV_WIDTH = 31
IN_W = 3 * ATTN_W + 2 * CONV_CH
BAND = 128
N_EXPERTS = 256
TOP_K = 8
N_EXPERT_GROUPS = 8
EXPERTS_PER_GROUP = N_EXPERTS // N_EXPERT_GROUPS
TOPK_GROUPS = 4
EXPERT_FF = 256
ROUTED_SCALE = 2.5
DEPTH = 1
DN_ALPHA = (2 * DEPTH) ** 0.25
LN_EPS = 1e-5
MASKED = -1e30

VMEM_LIMIT_BYTES = 56 * 1024 * 1024


def _params(*sem):
    return pltpu.CompilerParams(dimension_semantics=sem, vmem_limit_bytes=VMEM_LIMIT_BYTES)


def _dot(a, b):
    return jnp.dot(a, b, preferred_element_type=F32)


def _dot_nt(a, b):
    return lax.dot_general(a, b, (((1,), (1,)), ((), ())), preferred_element_type=F32)


def _layer_norm(x, g, b):
    mu = jnp.mean(x, axis=-1, keepdims=True)
    xc = x - mu
    var = jnp.mean(xc * xc, axis=-1, keepdims=True)
    return xc * lax.rsqrt(var + LN_EPS) * g + b


def _silu(x):
    return x * jax.nn.sigmoid(x)


def _alibi_slopes():
    n = N_GROUPS * HEADS_PER_GROUP
    h = jnp.arange(1, n + 1, dtype=F32)
    return (2.0 ** (-8.0 * h / n)).reshape(N_GROUPS, HEADS_PER_GROUP)


LANES = 128


def _proj_kernel(x_ref, w_ref, b_ref, *refs, dils, tm):
    qkv_refs, (u_ref, kv_ref, zs) = refs[:N_GROUPS], refs[N_GROUPS:]
    x = x_ref[...].astype(BF16)
    for part in range(3):
        for g in range(N_GROUPS):
            c0 = part * ATTN_W + g * GROUP_W
            z = _dot(x, w_ref[:, c0:c0 + GROUP_W]) + b_ref[:, c0:c0 + GROUP_W]
            if part > 0:
                kv_ref[:, c0 - ATTN_W:c0 - ATTN_W + GROUP_W] = z
            out, dil = qkv_refs[g], dils[g]
            if dil == 1:
                out[:, part * GROUP_W:(part + 1) * GROUP_W] = z.astype(out.dtype)
                continue
            for half in range(GROUP_W // LANES):
                zs[half] = z[:, half * LANES:(half + 1) * LANES]
            for r in range(dil):
                for half in range(GROUP_W // LANES):
                    c = r * 3 * GROUP_W + part * GROUP_W + half * LANES
                    out[:, c:c + LANES] = zs[half, pl.ds(r, tm // dil, stride=dil), :].astype(out.dtype)
    c0 = 3 * ATTN_W
    a = _dot(x, w_ref[:, c0:c0 + CONV_CH]) + b_ref[:, c0:c0 + CONV_CH]
    gate = _dot(x, w_ref[:, c0 + CONV_CH:]) + b_ref[:, c0 + CONV_CH:]
    u_ref[...] = a * jax.nn.sigmoid(gate)


def _project(x, w_in, b_in, tm, keep, qkv_dtype, dils):
    n = x.shape[0]
    nt = n // tm
    nk = keep // tm
    return pl.pallas_call(
        functools.partial(_proj_kernel, dils=dils, tm=tm),
        grid=(nt,),
        in_specs=[
            pl.BlockSpec((tm, D_MODEL), lambda i: (i, 0)),
            pl.BlockSpec((D_MODEL, IN_W), lambda i: (0, 0)),
            pl.BlockSpec((1, IN_W), lambda i: (0, 0)),
        ],
        out_specs=[pl.BlockSpec((tm // d, d * 3 * GROUP_W), lambda i: (i, 0)) for d in dils] + [
            pl.BlockSpec((tm, CONV_CH), lambda i: (i, 0)),
            pl.BlockSpec((tm, 2 * ATTN_W), lambda i: (jnp.maximum(i - (nt - nk), 0), 0)),
        ],
        out_shape=[jax.ShapeDtypeStruct((n // d, d * 3 * GROUP_W), qkv_dtype) for d in dils] + [
            jax.ShapeDtypeStruct((n, CONV_CH), F32),
            jax.ShapeDtypeStruct((keep, 2 * ATTN_W), F32),
        ],
        scratch_shapes=[pltpu.VMEM((GROUP_W // LANES, tm, LANES), F32)],
        compiler_params=_params("arbitrary"),
        name="proj",
    )(x, w_in, b_in)


def _head_select(parts, rows):
    col = lax.broadcasted_iota(I32, (rows, GROUP_W), 1) // HEAD_DIM
    out = jnp.broadcast_to(parts[-1], (rows, GROUP_W))
    for h in range(HEADS_PER_GROUP - 2, -1, -1):
        out = jnp.where(col == h, parts[h], out)
    return out


def _head_rows(q, rows):
    col = lax.broadcasted_iota(I32, (rows, GROUP_W), 1) // HEAD_DIM
    return jnp.concatenate([jnp.where(col == h, q, jnp.zeros_like(q)) for h in range(HEADS_PER_GROUP)], axis=0)


def _softmax_pv(s, v, rows):
    m = jnp.max(s, axis=-1, keepdims=True)
    e = jnp.exp(s - m)
    l = jnp.sum(e, axis=-1, keepdims=True)
    pv = _dot(e.astype(BF16), v) / l
    lse = m + jnp.log(l)
    o = _head_select([pv[h * rows:(h + 1) * rows] for h in range(HEADS_PER_GROUP)], rows)
    lse_x = _head_select([lse[h * rows:(h + 1) * rows] for h in range(HEADS_PER_GROUP)], rows)
    return o, lse_x


def _attn_kernel(q_ref, kp_ref, kc_ref, vp_ref, vc_ref, bias_ref, o_ref, lse_ref):
    qm = _head_rows(q_ref[...], BAND)
    k = jnp.concatenate([kp_ref[...], kc_ref[...]], axis=0)
    v = jnp.concatenate([vp_ref[...], vc_ref[...]], axis=0)
    s = _dot_nt(qm, k) * HEAD_DIM ** -0.5 + bias_ref[0]
    o, lse_x = _softmax_pv(s, v, BAND)
    o_ref[...] = o.astype(o_ref.dtype)
    lse_ref[...] = lse_x


def _prompt_bias(g, dil):
    slopes = _alibi_slopes()[g]
    qi = jnp.arange(BAND)[:, None]
    kj = jnp.arange(2 * BAND)[None, :]
    steps = qi + BAND - kj
    nk = DIL_GROUPS[g][0] // dil
    valid = (steps >= 0) & (steps <= nk)
    bias = -slopes[:, None, None] * (steps * dil).astype(F32)
    inner = jnp.where(valid[None], bias, MASKED)
    first = jnp.where((valid & (kj >= BAND))[None], bias, MASKED)
    return jnp.stack([first, inner]).reshape(2, HEADS_PER_GROUP * BAND, 2 * BAND)


def _attention_prompt(view, g):
    win, dil = DIL_GROUPS[g]
    n_cls = view.shape[0]
    assert win // dil <= BAND and n_cls % BAND == 0 and view.shape[1] == dil * 3 * GROUP_W
    cur = lambda part: pl.BlockSpec((BAND, GROUP_W), lambda r, b: (b, r * 3 + part))
    prev = lambda part: pl.BlockSpec((BAND, GROUP_W), lambda r, b: (jnp.maximum(b - 1, 0), r * 3 + part))
    return pl.pallas_call(
        _attn_kernel,
        grid=(dil, n_cls // BAND),
        in_specs=[cur(0), prev(1), cur(1), prev(2), cur(2),
                  pl.BlockSpec((1, HEADS_PER_GROUP * BAND, 2 * BAND), lambda r, b: (jnp.minimum(b, 1), 0, 0))],
        out_specs=[pl.BlockSpec((BAND, GROUP_W), lambda r, b: (b, r)),
                   pl.BlockSpec((BAND, GROUP_W), lambda r, b: (b, r))],
        out_shape=[jax.ShapeDtypeStruct((n_cls, dil * GROUP_W), BF16),
                   jax.ShapeDtypeStruct((n_cls, dil * GROUP_W), F32)],
        compiler_params=_params("arbitrary", "arbitrary"),
        name=f"attn_g{g}",
    )(view, view, view, view, view, _prompt_bias(g, dil))


def _attn_sample_kernel(q_ref, k_ref, v_ref, cache_ref, bias_ref, o_ref, lse_ref, kbuf, vbuf, *, n_buf, t):
    kbuf[0:n_buf, :] = cache_ref[0, :, 0:GROUP_W].astype(BF16)
    vbuf[0:n_buf, :] = cache_ref[0, :, GROUP_W:].astype(BF16)
    pad = jnp.zeros((BAND - t, GROUP_W), F32)
    kbuf[n_buf:, :] = jnp.concatenate([k_ref[...], pad], axis=0).astype(BF16)
    vbuf[n_buf:, :] = jnp.concatenate([v_ref[...], pad], axis=0).astype(BF16)
    qm = _head_rows(q_ref[...], t).astype(BF16)
    s = _dot_nt(qm, kbuf[...]) * HEAD_DIM ** -0.5 + bias_ref[...]
    o, lse_x = _softmax_pv(s, vbuf[...], t)
    o_ref[...] = o
    lse_ref[...] = lse_x


def _sample_bias(g, n_buf, t):
    win, dil = DIL_GROUPS[g]
    slopes = _alibi_slopes()[g]
    tq = jnp.arange(t)[:, None]
    j = jnp.arange(n_buf + BAND)[None, :]
    dist = n_buf + tq - j
    valid = (dist >= 0) & (dist % dil == 0) & (dist <= win) & (j < n_buf + t)
    bias = -slopes[:, None, None] * dist.astype(F32)
    return jnp.where(valid[None], bias, MASKED).reshape(HEADS_PER_GROUP * t, n_buf + BAND)


def _attention_sample(qkv, cache, g, t):
    nb, n_buf = cache.shape[0], cache.shape[1]
    col = lambda part: pl.BlockSpec((t, GROUP_W), lambda i: (i, part))
    kern = functools.partial(_attn_sample_kernel, n_buf=n_buf, t=t)
    return pl.pallas_call(
        kern,
        grid=(nb,),
        in_specs=[col(0), col(1), col(2),
                  pl.BlockSpec((1, n_buf, 2 * GROUP_W), lambda i: (i, 0, 0)),
                  pl.BlockSpec((HEADS_PER_GROUP * t, n_buf + BAND), lambda i: (0, 0))],
        out_specs=[pl.BlockSpec((t, GROUP_W), lambda i: (i, 0)),
                   pl.BlockSpec((t, GROUP_W), lambda i: (i, 0))],
        out_shape=[jax.ShapeDtypeStruct((nb * t, GROUP_W), F32),
                   jax.ShapeDtypeStruct((nb * t, GROUP_W), F32)],
        scratch_shapes=[pltpu.VMEM((n_buf + BAND, GROUP_W), BF16),
                        pltpu.VMEM((n_buf + BAND, GROUP_W), BF16)],
        compiler_params=_params("arbitrary"),
        name=f"attn_sample_g{g}",
    )(qkv, qkv, qkv, cache, _sample_bias(g, n_buf, t))


CONV_HALO = 32


def _conv_tail(acc, b_ref, g_ref, beta_ref):
    return _silu(_layer_norm(acc + b_ref[...], g_ref[...], beta_ref[...]))


def _conv_prompt_kernel(halo_ref, u_ref, w_ref, b_ref, g_ref, beta_ref, o_ref, hist, *, tm):
    i = pl.program_id(0)
    hist[0:CONV_HALO, :] = jnp.where(i == 0, 0.0, halo_ref[...])
    hist[CONV_HALO:, :] = u_ref[...]
    off = CONV_HALO - (CONV_WIDTH - 1)
    acc = jnp.zeros((tm, CONV_CH), F32)
    for j in range(CONV_WIDTH):
        acc = acc + w_ref[j:j + 1, :] * hist[off + j:off + j + tm, :]
    o_ref[...] = _conv_tail(acc, b_ref, g_ref, beta_ref).astype(o_ref.dtype)


def _conv_prompt(u, w_dw, b_dw, ln_g, ln_b, tm):
    n = u.shape[0]
    vec = pl.BlockSpec((1, CONV_CH), lambda i: (0, 0))
    return pl.pallas_call(
        functools.partial(_conv_prompt_kernel, tm=tm),
        grid=(n // tm,),
        in_specs=[pl.BlockSpec((CONV_HALO, CONV_CH), lambda i: (jnp.maximum(i * (tm // CONV_HALO) - 1, 0), 0)),
                  pl.BlockSpec((tm, CONV_CH), lambda i: (i, 0)),
                  pl.BlockSpec((CONV_WIDTH, CONV_CH), lambda i: (0, 0)), vec, vec, vec],
        out_specs=pl.BlockSpec((tm, CONV_CH), lambda i: (i, 0)),
        out_shape=jax.ShapeDtypeStruct((n, CONV_CH), BF16),
        scratch_shapes=[pltpu.VMEM((CONV_HALO + tm, CONV_CH), F32)],
        compiler_params=_params("arbitrary"),
        name="conv_prompt",
    )(u, u, w_dw, b_dw, ln_g, ln_b)


def _conv_sample_kernel(hist_ref, w_ref, b_ref, g_ref, beta_ref, o_ref, *, t):
    acc = jnp.zeros((t, CONV_CH), F32)
    for j in range(CONV_WIDTH):
        acc = acc + w_ref[j:j + 1, :] * hist_ref[0, j:j + t, :]
    o_ref[...] = _conv_tail(acc, b_ref, g_ref, beta_ref)


def _conv_sample(u_hist, w_dw, b_dw, ln_g, ln_b, t):
    nb, rows = u_hist.shape[0], u_hist.shape[1]
    vec = pl.BlockSpec((1, CONV_CH), lambda i: (0, 0))
    return pl.pallas_call(
        functools.partial(_conv_sample_kernel, t=t),
        grid=(nb,),
        in_specs=[pl.BlockSpec((1, rows, CONV_CH), lambda i: (i, 0, 0)),
                  pl.BlockSpec((CONV_WIDTH, CONV_CH), lambda i: (0, 0)), vec, vec, vec],
        out_specs=pl.BlockSpec((t, CONV_CH), lambda i: (i, 0)),
        out_shape=jax.ShapeDtypeStruct((nb * t, CONV_CH), F32),
        compiler_params=_params("arbitrary"),
        name="conv_sample",
    )(u_hist, w_dw, b_dw, ln_g, ln_b)


ROW_SLABS = D_MODEL // LANES


def _token_order(ref, dil, scr, tm):
    if dil == 1:
        return ref[...].astype(F32)
    for r in range(dil):
        for half in range(GROUP_W // LANES):
            c = r * GROUP_W + half * LANES
            scr[half, pl.ds(r, tm // dil, stride=dil), :] = ref[:, c:c + LANES].astype(F32)
    return jnp.concatenate([scr[half] for half in range(GROUP_W // LANES)], axis=1)


def _store_rows(ref, val):
    rows = val.shape[0]
    for c in range(ROW_SLABS):
        ref[pl.ds(c, rows, stride=ROW_SLABS), :] = val[:, c * LANES:(c + 1) * LANES]


def _load_rows(ref):
    rows = ref.shape[0] // ROW_SLABS
    return jnp.concatenate([ref[pl.ds(c, rows, stride=ROW_SLABS), :] for c in range(ROW_SLABS)], axis=1)


def _row(ref, i):
    return ref.at[pl.ds(pl.multiple_of(i * ROW_SLABS, ROW_SLABS), ROW_SLABS)]


def _outproj_kernel(o0, o1, o2, l0, l1, l2, c_ref, x_ref, w_ref, b_ref, g_ref, beta_ref, h_ref, *scr, dils, tm):
    os_ = [_token_order(r, d, scr[2 * g], tm) for g, (r, d) in enumerate(zip((o0, o1, o2), dils))]
    ls = [_token_order(r, d, scr[2 * g + 1], tm) for g, (r, d) in enumerate(zip((l0, l1, l2), dils))]
    m = jnp.maximum(jnp.maximum(ls[0], ls[1]), ls[2])
    es = [jnp.exp(l - m) for l in ls]
    inv = 1.0 / (es[0] + es[1] + es[2])
    mixed = b_ref[...]
    for g in range(N_GROUPS):
        a = (os_[g] * (es[g] * inv)).astype(BF16)
        mixed = mixed + _dot(a, w_ref[g * GROUP_W:(g + 1) * GROUP_W, :])
    mixed = mixed + _dot(c_ref[...].astype(BF16), w_ref[ATTN_W:, :])
    _store_rows(h_ref, _layer_norm(DN_ALPHA * x_ref[...] + mixed, g_ref[...], beta_ref[...]))


def _outproj(os_, lses, conv, x, w_o, b_o, ln_g, ln_b, tm, dils):
    n = x.shape[0]
    grp = [pl.BlockSpec((tm // d, d * GROUP_W), lambda i: (i, 0)) for d in dils]
    vec = pl.BlockSpec((1, D_MODEL), lambda i: (0, 0))
    return pl.pallas_call(
        functools.partial(_outproj_kernel, dils=dils, tm=tm),
        grid=(n // tm,),
        in_specs=grp + grp + [pl.BlockSpec((tm, CONV_CH), lambda i: (i, 0)),
                              pl.BlockSpec((tm, D_MODEL), lambda i: (i, 0)),
                              pl.BlockSpec((D_MODEL, D_MODEL), lambda i: (0, 0)), vec, vec, vec],
        out_specs=pl.BlockSpec((tm * ROW_SLABS, LANES), lambda i: (i, 0)),
        out_shape=jax.ShapeDtypeStruct((n * ROW_SLABS, LANES), F32),
        scratch_shapes=[pltpu.VMEM((GROUP_W // LANES, tm, LANES), F32) for _ in range(2 * N_GROUPS)],
        compiler_params=_params("arbitrary"),
        name="outproj",
    )(*os_, *lses, conv, x, w_o, b_o, ln_g, ln_b)


ROUTER_TM = 256


def _first_index(hit, idx, limit, axis):
    return jnp.min(jnp.where(hit, idx, limit), axis=axis, keepdims=True)


def _router_kernel(h_ref, w_ref, rb_ref, tri_ref, idx_ref, wt_ref, pos_ref, cnt_ref, run):
    i = pl.program_id(0)
    tm = ROUTER_TM

    @pl.when(i == 0)
    def _():
        run[...] = jnp.zeros_like(run)

    logits = _dot_nt(w_ref[...], _load_rows(h_ref).astype(BF16))
    scores = jax.nn.sigmoid(logits)
    biased = scores + rb_ref[...]
    groups = [biased[g * EXPERTS_PER_GROUP:(g + 1) * EXPERTS_PER_GROUP] for g in range(N_EXPERT_GROUPS)]
    ei = lax.broadcasted_iota(I32, (EXPERTS_PER_GROUP, tm), 0).astype(F32)
    gs = []
    for bg in groups:
        m1 = jnp.max(bg, axis=0, keepdims=True)
        f1 = _first_index(bg == m1, ei, float(EXPERTS_PER_GROUP), 0)
        m2 = jnp.max(jnp.where(ei == f1, -jnp.inf, bg), axis=0, keepdims=True)
        gs.append(m1 + m2)
    gs = jnp.concatenate(gs, axis=0)
    gi = lax.broadcasted_iota(I32, gs.shape, 0).astype(F32)
    keep = jnp.zeros(gs.shape, F32)
    cur = gs
    for _ in range(TOPK_GROUPS):
        m = jnp.max(cur, axis=0, keepdims=True)
        f = _first_index(cur == m, gi, float(N_EXPERT_GROUPS), 0)
        hit = gi == f
        keep = jnp.where(hit, 1.0, keep)
        cur = jnp.where(hit, -jnp.inf, cur)
    masked = jnp.concatenate([jnp.where(keep[g:g + 1] > 0.0, bg, -jnp.inf) for g, bg in enumerate(groups)], axis=0)
    xi = lax.broadcasted_iota(I32, (N_EXPERTS, tm), 0).astype(F32)
    cur = masked
    sel = jnp.zeros((N_EXPERTS, tm), F32)
    picks = []
    for _ in range(TOP_K):
        m = jnp.max(cur, axis=0, keepdims=True)
        f = _first_index(cur == m, xi, float(N_EXPERTS), 0)
        hit = xi == f
        picks.append((f, hit))
        sel = jnp.where(hit, 1.0, sel)
        cur = jnp.where(hit, -jnp.inf, cur)
    before = _dot(sel.astype(BF16), tri_ref[...]) + run[...]
    run[...] = run[...] + jnp.sum(sel, axis=1, keepdims=True)
    ws = [jnp.sum(jnp.where(hit, scores, 0.0), axis=0, keepdims=True) for _, hit in picks]
    wsum = ws[0]
    for w in ws[1:]:
        wsum = wsum + w
    for k, (f, hit) in enumerate(picks):
        idx_ref[k:k + 1, :] = f.astype(I32)
        wt_ref[k:k + 1, :] = ws[k] / wsum * ROUTED_SCALE
        pos_ref[k:k + 1, :] = jnp.sum(jnp.where(hit, before, 0.0), axis=0, keepdims=True).astype(I32)
    cnt_ref[...] = jnp.broadcast_to(run[...], cnt_ref.shape).astype(I32)


def _route(h, w_router_t, router_bias):
    t = h.shape[0] // ROW_SLABS
    tm = ROUTER_TM
    tri = (jnp.arange(tm)[:, None] < jnp.arange(tm)[None, :]).astype(BF16)
    slot = pl.BlockSpec((TOP_K, tm), lambda i: (0, i))
    idx, wts, pos, cnt = pl.pallas_call(
        _router_kernel,
        grid=(t // tm,),
        in_specs=[pl.BlockSpec((tm * ROW_SLABS, LANES), lambda i: (i, 0)),
                  pl.BlockSpec((N_EXPERTS, D_MODEL), lambda i: (0, 0)),
                  pl.BlockSpec((N_EXPERTS, 1), lambda i: (0, 0)),
                  pl.BlockSpec((tm, tm), lambda i: (0, 0))],
        out_specs=[slot, slot, slot, pl.BlockSpec((N_EXPERTS, 128), lambda i: (0, 0))],
        out_shape=[jax.ShapeDtypeStruct((TOP_K, t), I32), jax.ShapeDtypeStruct((TOP_K, t), F32),
                   jax.ShapeDtypeStruct((TOP_K, t), I32), jax.ShapeDtypeStruct((N_EXPERTS, 128), I32)],
        scratch_shapes=[pltpu.VMEM((N_EXPERTS, 1), F32)],
        compiler_params=_params("arbitrary"),
        name="router",
    )(h, w_router_t, router_bias.reshape(N_EXPERTS, 1), tri)
    return idx, wts, pos, cnt[:, 0]


DISPATCH_TM = 256


def _dispatch_kernel(starts_ref, idx_ref, pos_ref, h_ref, xs_ref, sem):
    def issue(j, c):
        for k in range(TOP_K):
            dst = starts_ref[idx_ref[k, j]] + pos_ref[k, j]
            pltpu.make_async_copy(_row(h_ref, j), _row(xs_ref, dst), sem).start(priority=k % 2)
        return c

    lax.fori_loop(0, DISPATCH_TM, issue, 0)
    for k in range(TOP_K):
        pltpu.make_async_copy(h_ref, xs_ref.at[pl.ds(0, DISPATCH_TM * ROW_SLABS)], sem).wait()


def _dispatch(starts, idx, pos, h):
    t = h.shape[0] // ROW_SLABS
    tm = DISPATCH_TM
    slot = pl.BlockSpec((TOP_K, tm), lambda i, s: (0, i), memory_space=pltpu.SMEM)
    return pl.pallas_call(
        _dispatch_kernel,
        grid_spec=pltpu.PrefetchScalarGridSpec(
            num_scalar_prefetch=1,
            grid=(t // tm,),
            in_specs=[slot, slot, pl.BlockSpec((tm * ROW_SLABS, LANES), lambda i, s: (i, 0))],
            out_specs=pl.BlockSpec(memory_space=pl.ANY),
            scratch_shapes=[pltpu.SemaphoreType.DMA],
        ),
        out_shape=jax.ShapeDtypeStruct((t * TOP_K * ROW_SLABS, LANES), F32),
        compiler_params=_params("arbitrary"),
        name="dispatch",
    )(starts, idx, pos, h)


EXPERT_BM = 256


def _expert_kernel(blk_ref, exp_ref, lo_ref, hi_ref, first_ref, xs_ref, wg_ref, wu_ref, wd_ref, ys_ref):
    v = pl.program_id(0)
    x = _load_rows(xs_ref).astype(BF16)
    gate = _dot(x, wg_ref[0].astype(BF16))
    up = _dot(x, wu_ref[0].astype(BF16))
    hid = (_silu(gate) * up).astype(BF16)
    y = _dot(hid, wd_ref[0].astype(BF16))
    row = lax.broadcasted_iota(I32, (EXPERT_BM, 1), 0)
    y = jnp.where((row >= lo_ref[v]) & (row < hi_ref[v]), y, 0.0)

    @pl.when(first_ref[v] == 1)
    def _():
        _store_rows(ys_ref, y)

    @pl.when(first_ref[v] == 0)
    def _():
        _store_rows(ys_ref, _load_rows(ys_ref) + y)


def _visits(counts, n_rows):
    bm = EXPERT_BM
    nblk = n_rows // bm
    nv = nblk + N_EXPERTS - 1
    ends = jnp.cumsum(counts)
    starts = ends - counts
    nb_e = jnp.where(counts > 0, (ends - 1) // bm - starts // bm + 1, 0)
    vend = jnp.cumsum(nb_e)
    vstart = vend - nb_e
    v = jnp.arange(nv, dtype=I32)
    live = v < vend[-1]
    vc = jnp.minimum(v, vend[-1] - 1)
    e = jnp.minimum(jnp.sum(vend[None, :] <= vc[:, None], axis=1), N_EXPERTS - 1).astype(I32)
    of_e = lambda a: jnp.sum(jnp.where(e[:, None] == jnp.arange(N_EXPERTS)[None, :], a[None, :], 0), axis=1)
    starts_e, ends_e = of_e(starts), of_e(ends)
    blk = (starts_e // bm + (vc - of_e(vstart))).astype(I32)
    lo = jnp.clip(starts_e - blk * bm, 0, bm)
    hi = jnp.clip(ends_e - blk * bm, 0, bm)
    first = jnp.concatenate([jnp.ones((1,), I32), (blk[1:] != blk[:-1]).astype(I32)])
    lo = jnp.where(live, lo, 0).astype(I32)
    hi = jnp.where(live, hi, 0).astype(I32)
    first = jnp.where(live, first, 0).astype(I32)
    return starts.astype(I32), blk, e, lo, hi, first


def _experts(visits, xs, w_gate, w_up, w_down):
    blk, e, lo, hi, first = visits
    n_rows = xs.shape[0] // ROW_SLABS
    bm = EXPERT_BM
    nv = blk.shape[0]
    return pl.pallas_call(
        _expert_kernel,
        grid_spec=pltpu.PrefetchScalarGridSpec(
            num_scalar_prefetch=5,
            grid=(nv,),
            in_specs=[pl.BlockSpec((bm * ROW_SLABS, LANES), lambda v, b, e, *_: (b[v], 0)),
                      pl.BlockSpec((1, D_MODEL, EXPERT_FF), lambda v, b, e, *_: (e[v], 0, 0)),
                      pl.BlockSpec((1, D_MODEL, EXPERT_FF), lambda v, b, e, *_: (e[v], 0, 0)),
                      pl.BlockSpec((1, EXPERT_FF, D_MODEL), lambda v, b, e, *_: (e[v], 0, 0))],
            out_specs=pl.BlockSpec((bm * ROW_SLABS, LANES), lambda v, b, e, *_: (b[v], 0)),
        ),
        out_shape=jax.ShapeDtypeStruct((n_rows * ROW_SLABS, LANES), F32),
        compiler_params=_params("arbitrary"),
        name="experts",
    )(blk, e, lo, hi, first, xs, w_gate, w_up, w_down)


COMBINE_TM = 128


def _combine_kernel(starts_ref, idx_ref, pos_ref, wt_ref, h_ref, sg_ref, su_ref, sd_ref, g_ref, beta_ref,
                    ys_ref, o_ref, ybuf, sem):
    tm = COMBINE_TM

    def issue(j, c):
        for k in range(TOP_K):
            src = starts_ref[idx_ref[k, j]] + pos_ref[k, j]
            pltpu.make_async_copy(_row(ys_ref, src), _row(ybuf.at[k], j), sem).start(priority=k % 2)
        return c

    lax.fori_loop(0, tm, issue, 0)

    h = _load_rows(h_ref)
    hb = h.astype(BF16)
    hid = (_silu(_dot(hb, sg_ref[...])) * _dot(hb, su_ref[...])).astype(BF16)
    acc = DN_ALPHA * h + _dot(hid, sd_ref[...])

    for k in range(TOP_K):
        pltpu.make_async_copy(ys_ref.at[pl.ds(0, tm * ROW_SLABS)], ybuf.at[k], sem).wait()

    for k in range(TOP_K):
        acc = acc + wt_ref[:, k:k + 1] * _load_rows(ybuf.at[k])
    o_ref[...] = _layer_norm(acc, g_ref[...], beta_ref[...])


def _combine(starts, idx, pos, wts_tk, h, ys, ws_gate, ws_up, ws_down, ln_g, ln_b, t0, n):
    tm = COMBINE_TM
    b0 = t0 // tm
    slot = pl.BlockSpec((TOP_K, tm), lambda i, s: (0, b0 + i), memory_space=pltpu.SMEM)
    vec = pl.BlockSpec((1, D_MODEL), lambda i, s: (0, 0))
    return pl.pallas_call(
        _combine_kernel,
        grid_spec=pltpu.PrefetchScalarGridSpec(
            num_scalar_prefetch=1,
            grid=(n // tm,),
            in_specs=[slot, slot,
                      pl.BlockSpec((tm, TOP_K), lambda i, s: (b0 + i, 0)),
                      pl.BlockSpec((tm * ROW_SLABS, LANES), lambda i, s: (b0 + i, 0)),
                      pl.BlockSpec((D_MODEL, EXPERT_FF), lambda i, s: (0, 0)),
                      pl.BlockSpec((D_MODEL, EXPERT_FF), lambda i, s: (0, 0)),
                      pl.BlockSpec((EXPERT_FF, D_MODEL), lambda i, s: (0, 0)),
                      vec, vec,
                      pl.BlockSpec(memory_space=pl.ANY)],
            out_specs=pl.BlockSpec((tm, D_MODEL), lambda i, s: (i, 0)),
            scratch_shapes=[pltpu.VMEM((TOP_K, tm * ROW_SLABS, LANES), F32), pltpu.SemaphoreType.DMA],
        ),
        out_shape=jax.ShapeDtypeStruct((n, D_MODEL), F32),
        compiler_params=_params("arbitrary"),
        name="combine",
    )(starts, idx, pos, wts_tk, h, ws_gate, ws_up, ws_down, ln_g, ln_b, ys)


def _kv_cache(kv_tail, g, keep):
    k = kv_tail[-keep:, g * GROUP_W:(g + 1) * GROUP_W]
    v = kv_tail[-keep:, ATTN_W + g * GROUP_W:ATTN_W + (g + 1) * GROUP_W]
    return jnp.stack([k, v], axis=1).reshape(keep, 2, HEADS_PER_GROUP, HEAD_DIM)


def kernel(x_prompt, x_sample, cache_kv_w128, cache_kv_w512, cache_kv_w2048, state_conv, w_in, b_in, w_dw, b_dw,
           conv_ln_g, conv_ln_b, w_o, b_o, ln1_g, ln1_b, w_router, router_bias, w_gate, w_up, w_down, ws_gate,
           ws_up, ws_down, ln2_g, ln2_b):
    assert w_in.shape[0] == DEPTH == 1
    batch, seq, _ = x_prompt.shape
    dec_batch, dec_seq, _ = x_sample.shape
    assert batch == 1
    n_s = dec_batch * dec_seq
    caches = (cache_kv_w128, cache_kv_w512, cache_kv_w2048)
    row = lambda a: a[0].reshape(1, -1)

    w_in_b = w_in[0].astype(BF16)
    b_in_r = row(b_in)
    w_o_b = w_o[0].astype(BF16)
    conv_w = (w_dw[0], row(b_dw), row(conv_ln_g), row(conv_ln_b))
    ln1 = (row(ln1_g), row(ln1_b))

    xp = x_prompt[0]
    keep_p = min(max(w for w, _ in DIL_GROUPS), seq)
    dils = tuple(d for _, d in DIL_GROUPS)
    *qkv_p, u_p, kv_p = _project(xp, w_in_b, b_in_r, 512, keep_p, BF16, dils)
    attn_p = [_attention_prompt(qkv_p[g], g) for g in range(N_GROUPS)]
    conv_p = _conv_prompt(u_p, *conv_w, 512)
    h_p = _outproj([a[0] for a in attn_p], [a[1] for a in attn_p], conv_p, xp, w_o_b, row(b_o), *ln1, 512, dils)

    xs = x_sample.reshape(n_s, D_MODEL)
    ones = (1,) * N_GROUPS
    *qkv_s, u_s, kv_s = _project(xs, w_in_b, b_in_r, n_s, n_s, F32, ones)
    attn_s = [_attention_sample(qkv_s[g], caches[g][0].reshape(dec_batch, -1, 2 * GROUP_W), g, dec_seq)
              for g in range(N_GROUPS)]
    u_hist = jnp.concatenate([state_conv[0], u_s.reshape(dec_batch, dec_seq, CONV_CH)], axis=1)
    conv_s = _conv_sample(u_hist, *conv_w, dec_seq)
    h_s = _outproj([a[0] for a in attn_s], [a[1] for a in attn_s], conv_s, xs, w_o_b, row(b_o), *ln1, n_s, ones)

    h = jnp.concatenate([h_p, h_s], axis=0)
    idx, wts, pos, counts = _route(h, w_router[0].T.astype(BF16), router_bias[0])
    starts, *visits = _visits(counts, h.shape[0] // ROW_SLABS * TOP_K)
    x_sorted = _dispatch(starts, idx, pos, h)
    y_sorted = _experts(visits, x_sorted, w_gate[0], w_up[0], w_down[0])
    shared = (ws_gate[0].astype(BF16), ws_up[0].astype(BF16), ws_down[0].astype(BF16))
    comb = functools.partial(_combine, starts, idx, pos, wts.T, h, y_sorted, *shared, row(ln2_g), row(ln2_b))
    y_p = comb(0, seq)
    y_s = comb(seq, n_s)

    kv_prompt = [_kv_cache(kv_p, g, min(w, seq))[None, None] for g, (w, _) in enumerate(DIL_GROUPS)]
    assert seq >= CONV_WIDTH - 1
    conv_prompt = u_p[-(CONV_WIDTH - 1):]
    kv_s4 = kv_s.reshape(dec_batch, dec_seq, 2, N_GROUPS, HEADS_PER_GROUP, HEAD_DIM)
    kv_sample = [kv_s4[:, :, :, g][None] for g in range(N_GROUPS)]
    conv_sample = u_hist[:, -(CONV_WIDTH - 1):]
    return (y_p[None], y_s.reshape(dec_batch, dec_seq, D_MODEL), *kv_prompt, conv_prompt[None, None],
            *kv_sample, conv_sample[None])
```

```python
import functools

import jax
import jax.numpy as jnp
import numpy as np
from jax import lax
from jax.experimental import pallas as pl
from jax.experimental.pallas import tpu as pltpu

F32 = jnp.float32
BF16 = jnp.bfloat16
I32 = jnp.int32

D_MODEL = 1024
HEAD_DIM = 64
HEADS_PER_GROUP = 4
GROUP_W = HEADS_PER_GROUP * HEAD_DIM
DIL_GROUPS = ((128, 1), (512, 4), (2048, 16))
N_GROUPS = len(DIL_GROUPS)
ATTN_W = N_GROUPS * GROUP_W
CONV_CH = D_MODEL - ATTN_W
CONV_WIDTH = 31
IN_W = 3 * ATTN_W + 2 * CONV_CH
BAND = 128
N_EXPERTS = 256
TOP_K = 8
N_EXPERT_GROUPS = 8
EXPERTS_PER_GROUP = N_EXPERTS // N_EXPERT_GROUPS
TOPK_GROUPS = 4
EXPERT_FF = 256
ROUTED_SCALE = 2.5
DEPTH = 1
DN_ALPHA = (2 * DEPTH) ** 0.25
LN_EPS = 1e-5
MASKED = -1e30

VMEM_LIMIT_BYTES = 56 * 1024 * 1024


def _params(*sem):
    return pltpu.CompilerParams(dimension_semantics=sem, vmem_limit_bytes=VMEM_LIMIT_BYTES)


def _dot(a, b):
    return jnp.dot(a, b, preferred_element_type=F32)


def _dot_nt(a, b):
    return lax.dot_general(a, b, (((1,), (1,)), ((), ())), preferred_element_type=F32)


def _layer_norm(x, g, b):
    mu = jnp.mean(x, axis=-1, keepdims=True)
    xc = x - mu
    var = jnp.mean(xc * xc, axis=-1, keepdims=True)
    return xc * lax.rsqrt(var + LN_EPS) * g + b


def _silu(x):
    return x * jax.nn.sigmoid(x)


def _alibi_slopes():
    n = N_GROUPS * HEADS_PER_GROUP
    h = jnp.arange(1, n + 1, dtype=F32)
    return (2.0 ** (-8.0 * h / n)).reshape(N_GROUPS, HEADS_PER_GROUP)


LANES = 128


def _proj_kernel(x_ref, w_ref, b_ref, *refs, dils, tm):
    qkv_refs, (u_ref, kv_ref, zs) = refs[:N_GROUPS], refs[N_GROUPS:]
    x = x_ref[...].astype(BF16)
    for part in range(3):
        for g in range(N_GROUPS):
            c0 = part * ATTN_W + g * GROUP_W
            z = _dot(x, w_ref[:, c0:c0 + GROUP_W]) + b_ref[:, c0:c0 + GROUP_W]
            if part > 0:
                kv_ref[:, c0 - ATTN_W:c0 - ATTN_W + GROUP_W] = z
            out, dil = qkv_refs[g], dils[g]
            if dil == 1:
                out[:, part * GROUP_W:(part + 1) * GROUP_W] = z.astype(out.dtype)
                continue
            for half in range(GROUP_W // LANES):
                zs[half] = z[:, half * LANES:(half + 1) * LANES]
            for r in range(dil):
                for half in range(GROUP_W // LANES):
                    c = r * 3 * GROUP_W + part * GROUP_W + half * LANES
                    out[:, c:c + LANES] = zs[half, pl.ds(r, tm // dil, stride=dil), :].astype(out.dtype)
    c0 = 3 * ATTN_W
    a = _dot(x, w_ref[:, c0:c0 + CONV_CH]) + b_ref[:, c0:c0 + CONV_CH]
    gate = _dot(x, w_ref[:, c0 + CONV_CH:]) + b_ref[:, c0 + CONV_CH:]
    u_ref[...] = a * jax.nn.sigmoid(gate)


def _project(x, w_in, b_in, tm, keep, qkv_dtype, dils):
    n = x.shape[0]
    nt = n // tm
    nk = keep // tm
    return pl.pallas_call(
        functools.partial(_proj_kernel, dils=dils, tm=tm),
        grid=(nt,),
        in_specs=[
            pl.BlockSpec((tm, D_MODEL), lambda i: (i, 0)),
            pl.BlockSpec((D_MODEL, IN_W), lambda i: (0, 0)),
            pl.BlockSpec((1, IN_W), lambda i: (0, 0)),
        ],
        out_specs=[pl.BlockSpec((tm // d, d * 3 * GROUP_W), lambda i: (i, 0)) for d in dils] + [
            pl.BlockSpec((tm, CONV_CH), lambda i: (i, 0)),
            pl.BlockSpec((tm, 2 * ATTN_W), lambda i: (jnp.maximum(i - (nt - nk), 0), 0)),
        ],
        out_shape=[jax.ShapeDtypeStruct((n // d, d * 3 * GROUP_W), qkv_dtype) for d in dils] + [
            jax.ShapeDtypeStruct((n, CONV_CH), F32),
            jax.ShapeDtypeStruct((keep, 2 * ATTN_W), F32),
        ],
        scratch_shapes=[pltpu.VMEM((GROUP_W // LANES, tm, LANES), F32)],
        compiler_params=_params("arbitrary"),
        name="proj",
    )(x, w_in, b_in)


def _head_select(parts, rows):
    col = lax.broadcasted_iota(I32, (rows, GROUP_W), 1) // HEAD_DIM
    out = jnp.broadcast_to(parts[-1], (rows, GROUP_W))
    for h in range(HEADS_PER_GROUP - 2, -1, -1):
        out = jnp.where(col == h, parts[h], out)
    return out


def _head_rows(q, rows):
    col = lax.broadcasted_iota(I32, (rows, GROUP_W), 1) // HEAD_DIM
    return jnp.concatenate([jnp.where(col == h, q, jnp.zeros_like(q)) for h in range(HEADS_PER_GROUP)], axis=0)


def _softmax_pv(s, v, rows):
    m = jnp.max(s, axis=-1, keepdims=True)
    e = jnp.exp(s - m)
    l = jnp.sum(e, axis=-1, keepdims=True)
    pv = _dot(e.astype(BF16), v) / l
    lse = m + jnp.log(l)
    o = _head_select([pv[h * rows:(h + 1) * rows] for h in range(HEADS_PER_GROUP)], rows)
    lse_x = _head_select([lse[h * rows:(h + 1) * rows] for h in range(HEADS_PER_GROUP)], rows)
    return o, lse_x


def _attn_kernel(q_ref, kp_ref, kc_ref, vp_ref, vc_ref, bias_ref, o_ref, lse_ref):
    qm = _head_rows(q_ref[...], BAND)
    k = jnp.concatenate([kp_ref[...], kc_ref[...]], axis=0)
    v = jnp.concatenate([vp_ref[...], vc_ref[...]], axis=0)
    s = _dot_nt(qm, k) * HEAD_DIM ** -0.5 + bias_ref[0]
    o, lse_x = _softmax_pv(s, v, BAND)
    o_ref[...] = o.astype(o_ref.dtype)
    lse_ref[...] = lse_x


def _prompt_bias(g, dil):
    slopes = _alibi_slopes()[g]
    qi = jnp.arange(BAND)[:, None]
    kj = jnp.arange(2 * BAND)[None, :]
    steps = qi + BAND - kj
    nk = DIL_GROUPS[g][0] // dil
    valid = (steps >= 0) & (steps <= nk)
    bias = -slopes[:, None, None] * (steps * dil).astype(F32)
    inner = jnp.where(valid[None], bias, MASKED)
    first = jnp.where((valid & (kj >= BAND))[None], bias, MASKED)
    return jnp.stack([first, inner]).reshape(2, HEADS_PER_GROUP * BAND, 2 * BAND)


def _attention_prompt(view, g):
    win, dil = DIL_GROUPS[g]
    n_cls = view.shape[0]
    assert win // dil <= BAND and n_cls % BAND == 0 and view.shape[1] == dil * 3 * GROUP_W
    cur = lambda part: pl.BlockSpec((BAND, GROUP_W), lambda r, b: (b, r * 3 + part))
    prev = lambda part: pl.BlockSpec((BAND, GROUP_W), lambda r, b: (jnp.maximum(b - 1, 0), r * 3 + part))
    return pl.pallas_call(
        _attn_kernel,
        grid=(dil, n_cls // BAND),
        in_specs=[cur(0), prev(1), cur(1), prev(2), cur(2),
                  pl.BlockSpec((1, HEADS_PER_GROUP * BAND, 2 * BAND), lambda r, b: (jnp.minimum(b, 1), 0, 0))],
        out_specs=[pl.BlockSpec((BAND, GROUP_W), lambda r, b: (b, r)),
                   pl.BlockSpec((BAND, GROUP_W), lambda r, b: (b, r))],
        out_shape=[jax.ShapeDtypeStruct((n_cls, dil * GROUP_W), BF16),
                   jax.ShapeDtypeStruct((n_cls, dil * GROUP_W), F32)],
        compiler_params=_params("arbitrary", "arbitrary"),
        name=f"attn_g{g}",
    )(view, view, view, view, view, _prompt_bias(g, dil))


def _attn_sample_kernel(q_ref, k_ref, v_ref, cache_ref, bias_ref, o_ref, lse_ref, *, n_buf, t):
    pad = jnp.zeros((BAND - t, GROUP_W), F32)
    k_new = jnp.concatenate([k_ref[...], pad], axis=0).astype(BF16)
    v_new = jnp.concatenate([v_ref[...], pad], axis=0).astype(BF16)
    qm = _head_rows(q_ref[...], t).astype(BF16)
    scale = HEAD_DIM ** -0.5
    s_old = _dot(qm, cache_ref[0, 0:GROUP_W, :].astype(BF16)) * scale + bias_ref[:, 0:n_buf]
    s_new = _dot_nt(qm, k_new) * scale + bias_ref[:, n_buf:]
    m = jnp.maximum(jnp.max(s_old, axis=-1, keepdims=True), jnp.max(s_new, axis=-1, keepdims=True))
    e_old = jnp.exp(s_old - m)
    e_new = jnp.exp(s_new - m)
    l = jnp.sum(e_old, axis=-1, keepdims=True) + jnp.sum(e_new, axis=-1, keepdims=True)
    pv = _dot_nt(e_old.astype(BF16), cache_ref[0, GROUP_W:, :].astype(BF16)) + _dot(e_new.astype(BF16), v_new)
    pv = pv / l
    lse = m + jnp.log(l)
    o_ref[...] = _head_select([pv[h * t:(h + 1) * t] for h in range(HEADS_PER_GROUP)], t)
    lse_ref[...] = _head_select([lse[h * t:(h + 1) * t] for h in range(HEADS_PER_GROUP)], t)


def _sample_bias(g, n_buf, t):
    win, dil = DIL_GROUPS[g]
    slopes = _alibi_slopes()[g]
    tq = jnp.arange(t)[:, None]
    j = jnp.arange(n_buf + BAND)[None, :]
    dist = n_buf + tq - j
    valid = (dist >= 0) & (dist % dil == 0) & (dist <= win) & (j < n_buf + t)
    bias = -slopes[:, None, None] * dist.astype(F32)
    return jnp.where(valid[None], bias, MASKED).reshape(HEADS_PER_GROUP * t, n_buf + BAND)


def _attention_sample(qkv, cache_t, g, t):
    nb, n_buf = cache_t.shape[0], cache_t.shape[2]
    col = lambda part: pl.BlockSpec((t, GROUP_W), lambda i: (i, part))
    kern = functools.partial(_attn_sample_kernel, n_buf=n_buf, t=t)
    return pl.pallas_call(
        kern,
        grid=(nb,),
        in_specs=[col(0), col(1), col(2),
                  pl.BlockSpec((1, 2 * GROUP_W, n_buf), lambda i: (i, 0, 0)),
                  pl.BlockSpec((HEADS_PER_GROUP * t, n_buf + BAND), lambda i: (0, 0))],
        out_specs=[pl.BlockSpec((t, GROUP_W), lambda i: (i, 0)),
                   pl.BlockSpec((t, GROUP_W), lambda i: (i, 0))],
        out_shape=[jax.ShapeDtypeStruct((nb * t, GROUP_W), F32),
                   jax.ShapeDtypeStruct((nb * t, GROUP_W), F32)],
        compiler_params=_params("arbitrary"),
        name=f"attn_sample_g{g}",
    )(qkv, qkv, qkv, cache_t, _sample_bias(g, n_buf, t))


CONV_HALO = 32


def _conv_tail(acc, b_ref, g_ref, beta_ref):
    return _silu(_layer_norm(acc + b_ref[...], g_ref[...], beta_ref[...]))


def _conv_prompt_kernel(halo_ref, u_ref, w_ref, b_ref, g_ref, beta_ref, o_ref, hist, *, tm):
    i = pl.program_id(0)
    hist[0:CONV_HALO, :] = jnp.where(i == 0, 0.0, halo_ref[...])
    hist[CONV_HALO:, :] = u_ref[...]
    off = CONV_HALO - (CONV_WIDTH - 1)
    acc = jnp.zeros((tm, CONV_CH), F32)
    for j in range(CONV_WIDTH):
        acc = acc + w_ref[j:j + 1, :] * hist[off + j:off + j + tm, :]
    o_ref[...] = _conv_tail(acc, b_ref, g_ref, beta_ref).astype(o_ref.dtype)


def _conv_prompt(u, w_dw, b_dw, ln_g, ln_b, tm):
    n = u.shape[0]
    vec = pl.BlockSpec((1, CONV_CH), lambda i: (0, 0))
    return pl.pallas_call(
        functools.partial(_conv_prompt_kernel, tm=tm),
        grid=(n // tm,),
        in_specs=[pl.BlockSpec((CONV_HALO, CONV_CH), lambda i: (jnp.maximum(i * (tm // CONV_HALO) - 1, 0), 0)),
                  pl.BlockSpec((tm, CONV_CH), lambda i: (i, 0)),
                  pl.BlockSpec((CONV_WIDTH, CONV_CH), lambda i: (0, 0)), vec, vec, vec],
        out_specs=pl.BlockSpec((tm, CONV_CH), lambda i: (i, 0)),
        out_shape=jax.ShapeDtypeStruct((n, CONV_CH), BF16),
        scratch_shapes=[pltpu.VMEM((CONV_HALO + tm, CONV_CH), F32)],
        compiler_params=_params("arbitrary"),
        name="conv_prompt",
    )(u, u, w_dw, b_dw, ln_g, ln_b)


def _conv_sample_kernel(hist_ref, w_ref, b_ref, g_ref, beta_ref, o_ref, *, t):
    acc = jnp.zeros((t, CONV_CH), F32)
    for j in range(CONV_WIDTH):
        acc = acc + w_ref[j:j + 1, :] * hist_ref[0, j:j + t, :]
    o_ref[...] = _conv_tail(acc, b_ref, g_ref, beta_ref)


def _conv_sample(u_hist, w_dw, b_dw, ln_g, ln_b, t):
    nb, rows = u_hist.shape[0], u_hist.shape[1]
    vec = pl.BlockSpec((1, CONV_CH), lambda i: (0, 0))
    return pl.pallas_call(
        functools.partial(_conv_sample_kernel, t=t),
        grid=(nb,),
        in_specs=[pl.BlockSpec((1, rows, CONV_CH), lambda i: (i, 0, 0)),
                  pl.BlockSpec((CONV_WIDTH, CONV_CH), lambda i: (0, 0)), vec, vec, vec],
        out_specs=pl.BlockSpec((t, CONV_CH), lambda i: (i, 0)),
        out_shape=jax.ShapeDtypeStruct((nb * t, CONV_CH), F32),
        compiler_params=_params("arbitrary"),
        name="conv_sample",
    )(u_hist, w_dw, b_dw, ln_g, ln_b)


ROW_SLABS = D_MODEL // LANES


def _token_order(ref, dil, scr, tm):
    if dil == 1:
        return ref[...].astype(F32)
    for r in range(dil):
        for half in range(GROUP_W // LANES):
            c = r * GROUP_W + half * LANES
            scr[half, pl.ds(r, tm // dil, stride=dil), :] = ref[:, c:c + LANES].astype(F32)
    return jnp.concatenate([scr[half] for half in range(GROUP_W // LANES)], axis=1)


def _store_rows(ref, val):
    rows = val.shape[0]
    for c in range(ROW_SLABS):
        ref[pl.ds(c, rows, stride=ROW_SLABS), :] = val[:, c * LANES:(c + 1) * LANES]


def _load_rows(ref):
    rows = ref.shape[0] // ROW_SLABS
    return jnp.concatenate([ref[pl.ds(c, rows, stride=ROW_SLABS), :] for c in range(ROW_SLABS)], axis=1)


def _row(ref, i):
    return ref.at[pl.ds(pl.multiple_of(i * ROW_SLABS, ROW_SLABS), ROW_SLABS)]


def _outproj_kernel(o0, o1, o2, l0, l1, l2, c_ref, x_ref, w_ref, b_ref, g_ref, beta_ref, h_ref, *scr, dils, tm):
    os_ = [_token_order(r, d, scr[2 * g], tm) for g, (r, d) in enumerate(zip((o0, o1, o2), dils))]
    ls = [_token_order(r, d, scr[2 * g + 1], tm) for g, (r, d) in enumerate(zip((l0, l1, l2), dils))]
    m = jnp.maximum(jnp.maximum(ls[0], ls[1]), ls[2])
    es = [jnp.exp(l - m) for l in ls]
    inv = 1.0 / (es[0] + es[1] + es[2])
    mixed = b_ref[...]
    for g in range(N_GROUPS):
        a = (os_[g] * (es[g] * inv)).astype(BF16)
        mixed = mixed + _dot(a, w_ref[g * GROUP_W:(g + 1) * GROUP_W, :])
    mixed = mixed + _dot(c_ref[...].astype(BF16), w_ref[ATTN_W:, :])
    _store_rows(h_ref, _layer_norm(DN_ALPHA * x_ref[...] + mixed, g_ref[...], beta_ref[...]))


def _outproj(os_, lses, conv, x, w_o, b_o, ln_g, ln_b, tm, dils):
    n = x.shape[0]
    grp = [pl.BlockSpec((tm // d, d * GROUP_W), lambda i: (i, 0)) for d in dils]
    vec = pl.BlockSpec((1, D_MODEL), lambda i: (0, 0))
    return pl.pallas_call(
        functools.partial(_outproj_kernel, dils=dils, tm=tm),
        grid=(n // tm,),
        in_specs=grp + grp + [pl.BlockSpec((tm, CONV_CH), lambda i: (i, 0)),
                              pl.BlockSpec((tm, D_MODEL), lambda i: (i, 0)),
                              pl.BlockSpec((D_MODEL, D_MODEL), lambda i: (0, 0)), vec, vec, vec],
        out_specs=pl.BlockSpec((tm * ROW_SLABS, LANES), lambda i: (i, 0)),
        out_shape=jax.ShapeDtypeStruct((n * ROW_SLABS, LANES), F32),
        scratch_shapes=[pltpu.VMEM((GROUP_W // LANES, tm, LANES), F32) for _ in range(2 * N_GROUPS)],
        compiler_params=_params("arbitrary"),
        name="outproj",
    )(*os_, *lses, conv, x, w_o, b_o, ln_g, ln_b)


ROUTER_TM = 256


def _first_index(hit, idx, limit, axis):
    return jnp.min(jnp.where(hit, idx, limit), axis=axis, keepdims=True)


def _router_kernel(ha_ref, hb_ref, w_ref, rb_ref, tri_ref, idx_ref, wt_ref, pos_ref, cnt_ref, run, *, tiles_a):
    i = pl.program_id(0)
    tm = ROUTER_TM

    @pl.when(i == 0)
    def _():
        run[...] = jnp.zeros_like(run)

    h = jnp.where(i < tiles_a, _load_rows(ha_ref), _load_rows(hb_ref))
    logits = _dot_nt(w_ref[...], h.astype(BF16))
    scores = jax.nn.sigmoid(logits)
    biased = scores + rb_ref[...]
    groups = [biased[g * EXPERTS_PER_GROUP:(g + 1) * EXPERTS_PER_GROUP] for g in range(N_EXPERT_GROUPS)]
    ei = lax.broadcasted_iota(I32, (EXPERTS_PER_GROUP, tm), 0).astype(F32)
    gs = []
    for bg in groups:
        m1 = jnp.max(bg, axis=0, keepdims=True)
        f1 = _first_index(bg == m1, ei, float(EXPERTS_PER_GROUP), 0)
        m2 = jnp.max(jnp.where(ei == f1, -jnp.inf, bg), axis=0, keepdims=True)
        gs.append(m1 + m2)
    gs = jnp.concatenate(gs, axis=0)
    gi = lax.broadcasted_iota(I32, gs.shape, 0).astype(F32)
    keep = jnp.zeros(gs.shape, F32)
    cur = gs
    for _ in range(TOPK_GROUPS):
        m = jnp.max(cur, axis=0, keepdims=True)
        f = _first_index(cur == m, gi, float(N_EXPERT_GROUPS), 0)
        hit = gi == f
        keep = jnp.where(hit, 1.0, keep)
        cur = jnp.where(hit, -jnp.inf, cur)
    masked = jnp.concatenate([jnp.where(keep[g:g + 1] > 0.0, bg, -jnp.inf) for g, bg in enumerate(groups)], axis=0)
    xi = lax.broadcasted_iota(I32, (N_EXPERTS, tm), 0).astype(F32)
    cur = masked
    sel = jnp.zeros((N_EXPERTS, tm), F32)
    picks = []
    for _ in range(TOP_K):
        m = jnp.max(cur, axis=0, keepdims=True)
        f = _first_index(cur == m, xi, float(N_EXPERTS), 0)
        hit = xi == f
        picks.append((f, hit))
        sel = jnp.where(hit, 1.0, sel)
        cur = jnp.where(hit, -jnp.inf, cur)
    before = _dot(sel.astype(BF16), tri_ref[...]) + run[...]
    run[...] = run[...] + jnp.sum(sel, axis=1, keepdims=True)
    ws = [jnp.sum(jnp.where(hit, scores, 0.0), axis=0, keepdims=True) for _, hit in picks]
    wsum = ws[0]
    for w in ws[1:]:
        wsum = wsum + w
    for k, (f, hit) in enumerate(picks):
        idx_ref[k:k + 1, :] = f.astype(I32)
        wt_ref[k:k + 1, :] = ws[k] / wsum * ROUTED_SCALE
        pos_ref[k:k + 1, :] = jnp.sum(jnp.where(hit, before, 0.0), axis=0, keepdims=True).astype(I32)
    cnt_ref[...] = jnp.broadcast_to(run[...], cnt_ref.shape).astype(I32)


def _two_part_specs(ha, hb, tm, index_of):
    tiles_a = ha.shape[0] // (tm * ROW_SLABS)
    tiles_b = hb.shape[0] // (tm * ROW_SLABS)
    block = (tm * ROW_SLABS, LANES)
    spec_a = pl.BlockSpec(block, lambda *a: (jnp.minimum(index_of(*a), tiles_a - 1), 0))
    spec_b = pl.BlockSpec(block, lambda *a: (jnp.maximum(index_of(*a) - tiles_a, 0), 0))
    return tiles_a, tiles_b, spec_a, spec_b


def _route(ha, hb, w_router_t, router_bias):
    tm = ROUTER_TM
    tiles_a, tiles_b, spec_a, spec_b = _two_part_specs(ha, hb, tm, lambda i: i)
    t = (tiles_a + tiles_b) * tm
    tri = (jnp.arange(tm)[:, None] < jnp.arange(tm)[None, :]).astype(BF16)
    slot = pl.BlockSpec((TOP_K, tm), lambda i: (0, i))
    idx, wts, pos, cnt = pl.pallas_call(
        functools.partial(_router_kernel, tiles_a=tiles_a),
        grid=(t // tm,),
        in_specs=[spec_a, spec_b,
                  pl.BlockSpec((N_EXPERTS, D_MODEL), lambda i: (0, 0)),
                  pl.BlockSpec((N_EXPERTS, 1), lambda i: (0, 0)),
                  pl.BlockSpec((tm, tm), lambda i: (0, 0))],
        out_specs=[slot, slot, slot, pl.BlockSpec((N_EXPERTS, 128), lambda i: (0, 0))],
        out_shape=[jax.ShapeDtypeStruct((TOP_K, t), I32), jax.ShapeDtypeStruct((TOP_K, t), F32),
                   jax.ShapeDtypeStruct((TOP_K, t), I32), jax.ShapeDtypeStruct((N_EXPERTS, 128), I32)],
        scratch_shapes=[pltpu.VMEM((N_EXPERTS, 1), F32)],
        compiler_params=_params("arbitrary"),
        name="router",
    )(ha, hb, w_router_t, router_bias.reshape(N_EXPERTS, 1), tri)
    return idx, wts, pos, cnt[:, 0]


DISPATCH_TM = 256


def _dispatch_kernel(starts_ref, idx_ref, pos_ref, ha_ref, hb_ref, xs_ref, zbuf, sem, *, tiles_a):
    i = pl.program_id(0)

    @pl.when(i == 0)
    def _():
        zbuf[...] = jnp.zeros(zbuf.shape, F32)
        spare = pl.ds(xs_ref.shape[0] - zbuf.shape[0], zbuf.shape[0])
        zero = pltpu.make_async_copy(zbuf, xs_ref.at[spare], sem)
        zero.start()
        zero.wait()

    def scatter_tile(h_ref):
        def issue(j, c):
            for k in range(TOP_K):
                dst = starts_ref[idx_ref[k, j]] + pos_ref[k, j]
                pltpu.make_async_copy(_row(h_ref, j), _row(xs_ref, dst), sem).start(priority=k % 2)
            return c

        lax.fori_loop(0, DISPATCH_TM, issue, 0)
        for k in range(TOP_K):
            pltpu.make_async_copy(h_ref, xs_ref.at[pl.ds(0, DISPATCH_TM * ROW_SLABS)], sem).wait()

    @pl.when(i < tiles_a)
    def _():
        scatter_tile(ha_ref)

    @pl.when(i >= tiles_a)
    def _():
        scatter_tile(hb_ref)


def _dispatch(starts, idx, pos, ha, hb, spare_rows):
    tm = DISPATCH_TM
    tiles_a, tiles_b, spec_a, spec_b = _two_part_specs(ha, hb, tm, lambda i, s: i)
    t = (tiles_a + tiles_b) * tm
    slot = pl.BlockSpec((TOP_K, tm), lambda i, s: (0, i), memory_space=pltpu.SMEM)
    return pl.pallas_call(
        functools.partial(_dispatch_kernel, tiles_a=tiles_a),
        grid_spec=pltpu.PrefetchScalarGridSpec(
            num_scalar_prefetch=1,
            grid=(t // tm,),
            in_specs=[slot, slot, spec_a, spec_b],
            out_specs=pl.BlockSpec(memory_space=pl.ANY),
            scratch_shapes=[pltpu.VMEM((spare_rows * ROW_SLABS, LANES), F32), pltpu.SemaphoreType.DMA],
        ),
        out_shape=jax.ShapeDtypeStruct(((t * TOP_K + spare_rows) * ROW_SLABS, LANES), F32),
        compiler_params=_params("arbitrary"),
        name="dispatch",
    )(starts, idx, pos, ha, hb)


EXPERT_CH = 256


def _expert_kernel(row0_ref, exp_ref, live_ref, fresh_ref, xs_ref, wg_ref, wu_ref, wd_ref, ys_ref,
                   xbuf, ybuf, wgb, wub, wdb, xsem, ysem):
    v = pl.program_id(0)
    nv = pl.num_programs(0)
    slot = v % 2
    rows = lambda u: pl.ds(pl.multiple_of(row0_ref[u] * ROW_SLABS, ROW_SLABS), EXPERT_CH * ROW_SLABS)
    x_copy = lambda u, s: pltpu.make_async_copy(xs_ref.at[rows(u)], xbuf.at[s], xsem.at[s])
    y_copy = lambda u, s: pltpu.make_async_copy(ybuf.at[s], ys_ref.at[rows(u)], ysem.at[s])
    spare = pl.ds(ys_ref.shape[0] - EXPERT_CH * ROW_SLABS, EXPERT_CH * ROW_SLABS)

    @pl.when(v == 0)
    def _():
        ybuf[0] = jnp.zeros(ybuf.shape[1:], F32)
        zero = pltpu.make_async_copy(ybuf.at[0], ys_ref.at[spare], ysem.at[0])
        zero.start()
        zero.wait()
        x_copy(0, 0).start()

    nxt = jnp.minimum(v + 1, nv - 1)

    @pl.when((v + 1 < nv) & (live_ref[nxt] == 1))
    def _():
        x_copy(nxt, 1 - slot).start()

    @pl.when(fresh_ref[v] == 1)
    def _():
        wgb[...] = wg_ref[0].astype(BF16)
        wub[...] = wu_ref[0].astype(BF16)
        wdb[...] = wd_ref[0].astype(BF16)

    @pl.when((v > 0) & (live_ref[jnp.maximum(v - 1, 0)] == 1))
    def _():
        y_copy(jnp.maximum(v - 1, 0), 1 - slot).wait()

    @pl.when(live_ref[v] == 1)
    def _():
        x_copy(v, slot).wait()
        x = _load_rows(xbuf.at[slot]).astype(BF16)
        hid = (_silu(_dot(x, wgb[...])) * _dot(x, wub[...])).astype(BF16)
        _store_rows(ybuf.at[slot], _dot(hid, wdb[...]))
        y_copy(v, slot).start()

        @pl.when(v == nv - 1)
        def _():
            y_copy(v, slot).wait()


def _chunks(counts, n_rows):
    ch = EXPERT_CH
    nv = n_rows // ch + N_EXPERTS
    ends = jnp.cumsum(counts)
    starts = ends - counts
    nch = (counts + ch - 1) // ch
    cend = jnp.cumsum(nch)
    cstart = cend - nch
    v = jnp.arange(nv, dtype=I32)
    live = v < cend[-1]
    vc = jnp.minimum(v, cend[-1] - 1)
    e = jnp.minimum(jnp.sum(cend[None, :] <= vc[:, None], axis=1), N_EXPERTS - 1).astype(I32)
    of_e = lambda a: jnp.sum(jnp.where(e[:, None] == jnp.arange(N_EXPERTS)[None, :], a[None, :], 0), axis=1)
    row0 = (of_e(starts) + (vc - of_e(cstart)) * ch).astype(I32)
    fresh = jnp.concatenate([jnp.ones((1,), I32), (e[1:] != e[:-1]).astype(I32)])
    return starts.astype(I32), row0, e, live.astype(I32), jnp.where(live, fresh, 0).astype(I32)


def _experts(chunks, xs, w_gate, w_up, w_down):
    row0, e, live, fresh = chunks
    ch = EXPERT_CH
    buf = pltpu.VMEM((2, ch * ROW_SLABS, LANES), F32)
    return pl.pallas_call(
        _expert_kernel,
        grid_spec=pltpu.PrefetchScalarGridSpec(
            num_scalar_prefetch=4,
            grid=(row0.shape[0],),
            in_specs=[pl.BlockSpec(memory_space=pl.ANY),
                      pl.BlockSpec((1, D_MODEL, EXPERT_FF), lambda v, r, e, *_: (e[v], 0, 0)),
                      pl.BlockSpec((1, D_MODEL, EXPERT_FF), lambda v, r, e, *_: (e[v], 0, 0)),
                      pl.BlockSpec((1, EXPERT_FF, D_MODEL), lambda v, r, e, *_: (e[v], 0, 0))],
            out_specs=pl.BlockSpec(memory_space=pl.ANY),
            scratch_shapes=[buf, buf,
                            pltpu.VMEM((D_MODEL, EXPERT_FF), BF16), pltpu.VMEM((D_MODEL, EXPERT_FF), BF16),
                            pltpu.VMEM((EXPERT_FF, D_MODEL), BF16),
                            pltpu.SemaphoreType.DMA((2,)), pltpu.SemaphoreType.DMA((2,))],
        ),
        out_shape=jax.ShapeDtypeStruct(xs.shape, F32),
        compiler_params=_params("arbitrary"),
        name="experts",
    )(row0, e, live, fresh, xs, w_gate, w_up, w_down)


COMBINE_TM = 128


def _combine_kernel(starts_ref, idx_ref, pos_ref, wt_ref, h_ref, sg_ref, su_ref, sd_ref, g_ref, beta_ref,
                    ys_ref, o_ref, ybuf, sem):
    tm = COMBINE_TM

    def issue(j, c):
        for k in range(TOP_K):
            src = starts_ref[idx_ref[k, j]] + pos_ref[k, j]
            pltpu.make_async_copy(_row(ys_ref, src), _row(ybuf.at[k], j), sem).start(priority=k % 2)
        return c

    lax.fori_loop(0, tm, issue, 0)

    h = _load_rows(h_ref)
    hb = h.astype(BF16)
    hid = (_silu(_dot(hb, sg_ref[...])) * _dot(hb, su_ref[...])).astype(BF16)
    acc = DN_ALPHA * h + _dot(hid, sd_ref[...])

    for k in range(TOP_K):
        pltpu.make_async_copy(ys_ref.at[pl.ds(0, tm * ROW_SLABS)], ybuf.at[k], sem).wait()

    for k in range(TOP_K):
        acc = acc + wt_ref[:, k:k + 1] * _load_rows(ybuf.at[k])
    o_ref[...] = _layer_norm(acc, g_ref[...], beta_ref[...])


def _combine(starts, idx, pos, wts_tk, ys, ws_gate, ws_up, ws_down, ln_g, ln_b, h, t0):
    tm = COMBINE_TM
    n = h.shape[0] // ROW_SLABS
    b0 = t0 // tm
    slot = pl.BlockSpec((TOP_K, tm), lambda i, s: (0, b0 + i), memory_space=pltpu.SMEM)
    vec = pl.BlockSpec((1, D_MODEL), lambda i, s: (0, 0))
    return pl.pallas_call(
        _combine_kernel,
        grid_spec=pltpu.PrefetchScalarGridSpec(
            num_scalar_prefetch=1,
            grid=(n // tm,),
            in_specs=[slot, slot,
                      pl.BlockSpec((tm, TOP_K), lambda i, s: (b0 + i, 0)),
                      pl.BlockSpec((tm * ROW_SLABS, LANES), lambda i, s: (i, 0)),
                      pl.BlockSpec((D_MODEL, EXPERT_FF), lambda i, s: (0, 0)),
                      pl.BlockSpec((D_MODEL, EXPERT_FF), lambda i, s: (0, 0)),
                      pl.BlockSpec((EXPERT_FF, D_MODEL), lambda i, s: (0, 0)),
                      vec, vec,
                      pl.BlockSpec(memory_space=pl.ANY)],
            out_specs=pl.BlockSpec((tm, D_MODEL), lambda i, s: (i, 0)),
            scratch_shapes=[pltpu.VMEM((TOP_K, tm * ROW_SLABS, LANES), F32), pltpu.SemaphoreType.DMA],
        ),
        out_shape=jax.ShapeDtypeStruct((n, D_MODEL), F32),
        compiler_params=_params("arbitrary"),
        name="combine",
    )(starts, idx, pos, wts_tk, h, ws_gate, ws_up, ws_down, ln_g, ln_b, ys)


def _kv_cache(kv_tail, g, keep):
    k = kv_tail[-keep:, g * GROUP_W:(g + 1) * GROUP_W]
    v = kv_tail[-keep:, ATTN_W + g * GROUP_W:ATTN_W + (g + 1) * GROUP_W]
    return jnp.stack([k, v], axis=1).reshape(keep, 2, HEADS_PER_GROUP, HEAD_DIM)


def kernel(x_prompt, x_sample, cache_kv_w128, cache_kv_w512, cache_kv_w2048, state_conv, w_in, b_in, w_dw, b_dw,
           conv_ln_g, conv_ln_b, w_o, b_o, ln1_g, ln1_b, w_router, router_bias, w_gate, w_up, w_down, ws_gate,
           ws_up, ws_down, ln2_g, ln2_b):
    assert w_in.shape[0] == DEPTH == 1
    batch, seq, _ = x_prompt.shape
    dec_batch, dec_seq, _ = x_sample.shape
    assert batch == 1
    n_s = dec_batch * dec_seq
    caches = (cache_kv_w128, cache_kv_w512, cache_kv_w2048)
    row = lambda a: a[0].reshape(1, -1)

    w_in_b = w_in[0].astype(BF16)
    b_in_r = row(b_in)
    w_o_b = w_o[0].astype(BF16)
    conv_w = (w_dw[0], row(b_dw), row(conv_ln_g), row(conv_ln_b))
    ln1 = (row(ln1_g), row(ln1_b))

    xp = x_prompt[0]
    keep_p = min(max(w for w, _ in DIL_GROUPS), seq)
    dils = tuple(d for _, d in DIL_GROUPS)
    *qkv_p, u_p, kv_p = _project(xp, w_in_b, b_in_r, 512, keep_p, BF16, dils)
    attn_p = [_attention_prompt(qkv_p[g], g) for g in range(N_GROUPS)]
    conv_p = _conv_prompt(u_p, *conv_w, 512)
    h_p = _outproj([a[0] for a in attn_p], [a[1] for a in attn_p], conv_p, xp, w_o_b, row(b_o), *ln1, 512, dils)

    xs = x_sample.reshape(n_s, D_MODEL)
    ones = (1,) * N_GROUPS
    *qkv_s, u_s, kv_s = _project(xs, w_in_b, b_in_r, n_s, n_s, F32, ones)
    caches_t = [jnp.transpose(c[0].reshape(dec_batch, -1, 2 * GROUP_W), (0, 2, 1)) for c in caches]
    attn_s = [_attention_sample(qkv_s[g], caches_t[g], g, dec_seq) for g in range(N_GROUPS)]
    u_hist = jnp.concatenate([state_conv[0], u_s.reshape(dec_batch, dec_seq, CONV_CH)], axis=1)
    conv_s = _conv_sample(u_hist, *conv_w, dec_seq)
    h_s = _outproj([a[0] for a in attn_s], [a[1] for a in attn_s], conv_s, xs, w_o_b, row(b_o), *ln1, n_s, ones)

    idx, wts, pos, counts = _route(h_p, h_s, w_router[0].T.astype(BF16), router_bias[0])
    starts, *chunks = _chunks(counts, (seq + n_s) * TOP_K)
    x_sorted = _dispatch(starts, idx, pos, h_p, h_s, EXPERT_CH)
    y_sorted = _experts(chunks, x_sorted, w_gate[0], w_up[0], w_down[0])
    shared = (ws_gate[0].astype(BF16), ws_up[0].astype(BF16), ws_down[0].astype(BF16))
    comb = functools.partial(_combine, starts, idx, pos, wts.T, y_sorted, *shared, row(ln2_g), row(ln2_b))
    y_p = comb(h_p, 0)
    y_s = comb(h_s, seq)

    kv_prompt = [_kv_cache(kv_p, g, min(w, seq))[None, None] for g, (w, _) in enumerate(DIL_GROUPS)]
    assert seq >= CONV_WIDTH - 1
    conv_prompt = u_p[-(CONV_WIDTH - 1):]
    kv_s4 = kv_s.reshape(dec_batch, dec_seq, 2, N_GROUPS, HEADS_PER_GROUP, HEAD_DIM)
    kv_sample = [kv_s4[:, :, :, g][None] for g in range(N_GROUPS)]
    conv_sample = u_hist[:, -(CONV_WIDTH - 1):]
    return (y_p[None], y_s.reshape(dec_batch, dec_seq, D_MODEL), *kv_prompt, conv_prompt[None, None],
            *kv_sample, conv_sample[None])
```

```python
import functools

import jax
import jax.numpy as jnp
import numpy as np
from jax import lax
from jax.experimental import pallas as pl
from jax.experimental.pallas import tpu as pltpu

F32 = jnp.float32
BF16 = jnp.bfloat16
I32 = jnp.int32

D_MODEL = 1024
HEAD_DIM = 64
HEADS_PER_GROUP = 4
GROUP_W = HEADS_PER_GROUP * HEAD_DIM
DIL_GROUPS = ((128, 1), (512, 4), (2048, 16))
N_GROUPS = len(DIL_GROUPS)
ATTN_W = N_GROUPS * GROUP_W
CONV_CH = D_MODEL - ATTN_W
CONV_WIDTH = 31
IN_W = 3 * ATTN_W + 2 * CONV_CH
BAND = 128
N_EXPERTS = 256
TOP_K = 8
N_EXPERT_GROUPS = 8
EXPERTS_PER_GROUP = N_EXPERTS // N_EXPERT_GROUPS
TOPK_GROUPS = 4
EXPERT_FF = 256
ROUTED_SCALE = 2.5
DEPTH = 1
DN_ALPHA = (2 * DEPTH) ** 0.25
LN_EPS = 1e-5
MASKED = -1e30

VMEM_LIMIT_BYTES = 56 * 1024 * 1024


def _params(*sem):
    return pltpu.CompilerParams(dimension_semantics=sem, vmem_limit_bytes=VMEM_LIMIT_BYTES)


def _dot(a, b):
    return jnp.dot(a, b, preferred_element_type=F32)


def _dot_nt(a, b):
    return lax.dot_general(a, b, (((1,), (1,)), ((), ())), preferred_element_type=F32)


def _layer_norm(x, g, b):
    mu = jnp.mean(x, axis=-1, keepdims=True)
    xc = x - mu
    var = jnp.mean(xc * xc, axis=-1, keepdims=True)
    return xc * lax.rsqrt(var + LN_EPS) * g + b


def _silu(x):
    return x * jax.nn.sigmoid(x)


def _alibi_slopes():
    n = N_GROUPS * HEADS_PER_GROUP
    h = jnp.arange(1, n + 1, dtype=F32)
    return (2.0 ** (-8.0 * h / n)).reshape(N_GROUPS, HEADS_PER_GROUP)


LANES = 128


def _proj_kernel(x_ref, w_ref, b_ref, *refs, dils, tm):
    qkv_refs, (u_ref, kv_ref, zs) = refs[:N_GROUPS], refs[N_GROUPS:]
    x = x_ref[...].astype(BF16)
    for part in range(3):
        for g in range(N_GROUPS):
            c0 = part * ATTN_W + g * GROUP_W
            z = _dot(x, w_ref[:, c0:c0 + GROUP_W]) + b_ref[:, c0:c0 + GROUP_W]
            if part > 0:
                kv_ref[:, c0 - ATTN_W:c0 - ATTN_W + GROUP_W] = z
            out, dil = qkv_refs[g], dils[g]
            if dil == 1:
                out[:, part * GROUP_W:(part + 1) * GROUP_W] = z.astype(out.dtype)
                continue
            for half in range(GROUP_W // LANES):
                zs[half] = z[:, half * LANES:(half + 1) * LANES]
            for r in range(dil):
                for half in range(GROUP_W // LANES):
                    c = r * 3 * GROUP_W + part * GROUP_W + half * LANES
                    out[:, c:c + LANES] = zs[half, pl.ds(r, tm // dil, stride=dil), :].astype(out.dtype)
    c0 = 3 * ATTN_W
    a = _dot(x, w_ref[:, c0:c0 + CONV_CH]) + b_ref[:, c0:c0 + CONV_CH]
    gate = _dot(x, w_ref[:, c0 + CONV_CH:]) + b_ref[:, c0 + CONV_CH:]
    u_ref[...] = a * jax.nn.sigmoid(gate)


def _project(x, w_in, b_in, tm, keep, qkv_dtype, dils):
    n = x.shape[0]
    nt = n // tm
    nk = keep // tm
    return pl.pallas_call(
        functools.partial(_proj_kernel, dils=dils, tm=tm),
        grid=(nt,),
        in_specs=[
            pl.BlockSpec((tm, D_MODEL), lambda i: (i, 0)),
            pl.BlockSpec((D_MODEL, IN_W), lambda i: (0, 0)),
            pl.BlockSpec((1, IN_W), lambda i: (0, 0)),
        ],
        out_specs=[pl.BlockSpec((tm // d, d * 3 * GROUP_W), lambda i: (i, 0)) for d in dils] + [
            pl.BlockSpec((tm, CONV_CH), lambda i: (i, 0)),
            pl.BlockSpec((tm, 2 * ATTN_W), lambda i: (jnp.maximum(i - (nt - nk), 0), 0)),
        ],
        out_shape=[jax.ShapeDtypeStruct((n // d, d * 3 * GROUP_W), qkv_dtype) for d in dils] + [
            jax.ShapeDtypeStruct((n, CONV_CH), F32),
            jax.ShapeDtypeStruct((keep, 2 * ATTN_W), F32),
        ],
        scratch_shapes=[pltpu.VMEM((GROUP_W // LANES, tm, LANES), F32)],
        compiler_params=_params("arbitrary"),
        name="proj",
    )(x, w_in, b_in)


def _head_select(parts, rows):
    col = lax.broadcasted_iota(I32, (rows, GROUP_W), 1) // HEAD_DIM
    out = jnp.broadcast_to(parts[-1], (rows, GROUP_W))
    for h in range(HEADS_PER_GROUP - 2, -1, -1):
        out = jnp.where(col == h, parts[h], out)
    return out


def _head_rows(q, rows):
    col = lax.broadcasted_iota(I32, (rows, GROUP_W), 1) // HEAD_DIM
    return jnp.concatenate([jnp.where(col == h, q, jnp.zeros_like(q)) for h in range(HEADS_PER_GROUP)], axis=0)


def _softmax_pv(s, v, rows):
    m = jnp.max(s, axis=-1, keepdims=True)
    e = jnp.exp(s - m)
    l = jnp.sum(e, axis=-1, keepdims=True)
    pv = _dot(e.astype(BF16), v) / l
    lse = m + jnp.log(l)
    o = _head_select([pv[h * rows:(h + 1) * rows] for h in range(HEADS_PER_GROUP)], rows)
    lse_x = _head_select([lse[h * rows:(h + 1) * rows] for h in range(HEADS_PER_GROUP)], rows)
    return o, lse_x


ATTN_QB = 4


def _attn_kernel(q_ref, kp_ref, kc_ref, vp_ref, vc_ref, bias_ref, o_ref, lse_ref):
    b = pl.program_id(1)
    for j in range(ATTN_QB):
        rows = slice(j * BAND, (j + 1) * BAND)
        qm = _head_rows(q_ref[rows, :], BAND)
        if j == 0:
            k = jnp.concatenate([kp_ref[...], kc_ref[rows, :]], axis=0)
            v = jnp.concatenate([vp_ref[...], vc_ref[rows, :]], axis=0)
            bias = bias_ref[jnp.minimum(b, 1)]
        else:
            k = kc_ref[(j - 1) * BAND:(j + 1) * BAND, :]
            v = vc_ref[(j - 1) * BAND:(j + 1) * BAND, :]
            bias = bias_ref[1]
        s = _dot_nt(qm, k) * HEAD_DIM ** -0.5 + bias
        o, lse_x = _softmax_pv(s, v, BAND)
        o_ref[rows, :] = o.astype(o_ref.dtype)
        lse_ref[rows, :] = lse_x


def _prompt_bias(g, dil):
    slopes = _alibi_slopes()[g]
    qi = jnp.arange(BAND)[:, None]
    kj = jnp.arange(2 * BAND)[None, :]
    steps = qi + BAND - kj
    nk = DIL_GROUPS[g][0] // dil
    valid = (steps >= 0) & (steps <= nk)
    bias = -slopes[:, None, None] * (steps * dil).astype(F32)
    inner = jnp.where(valid[None], bias, MASKED)
    first = jnp.where((valid & (kj >= BAND))[None], bias, MASKED)
    return jnp.stack([first, inner]).reshape(2, HEADS_PER_GROUP * BAND, 2 * BAND)


def _attention_prompt(view, g):
    win, dil = DIL_GROUPS[g]
    n_cls = view.shape[0]
    rows = ATTN_QB * BAND
    assert win // dil <= BAND and n_cls % rows == 0 and view.shape[1] == dil * 3 * GROUP_W
    cur = lambda part: pl.BlockSpec((rows, GROUP_W), lambda r, b: (b, r * 3 + part))
    prev = lambda part: pl.BlockSpec((BAND, GROUP_W), lambda r, b: (jnp.maximum(b * ATTN_QB - 1, 0), r * 3 + part))
    return pl.pallas_call(
        _attn_kernel,
        grid=(dil, n_cls // rows),
        in_specs=[cur(0), prev(1), cur(1), prev(2), cur(2),
                  pl.BlockSpec((2, HEADS_PER_GROUP * BAND, 2 * BAND), lambda r, b: (0, 0, 0))],
        out_specs=[pl.BlockSpec((rows, GROUP_W), lambda r, b: (b, r)),
                   pl.BlockSpec((rows, GROUP_W), lambda r, b: (b, r))],
        out_shape=[jax.ShapeDtypeStruct((n_cls, dil * GROUP_W), BF16),
                   jax.ShapeDtypeStruct((n_cls, dil * GROUP_W), F32)],
        compiler_params=_params("arbitrary", "arbitrary"),
        name=f"attn_g{g}",
    )(view, view, view, view, view, _prompt_bias(g, dil))


def _attn_sample_kernel(q_ref, k_ref, v_ref, cache_ref, bias_ref, o_ref, lse_ref, *, n_buf, t):
    pad = jnp.zeros((BAND - t, GROUP_W), F32)
    k_new = jnp.concatenate([k_ref[...], pad], axis=0).astype(BF16)
    v_new = jnp.concatenate([v_ref[...], pad], axis=0).astype(BF16)
    qm = _head_rows(q_ref[...], t).astype(BF16)
    scale = HEAD_DIM ** -0.5
    s_old = _dot(qm, cache_ref[0, 0:GROUP_W, :].astype(BF16)) * scale + bias_ref[:, 0:n_buf]
    s_new = _dot_nt(qm, k_new) * scale + bias_ref[:, n_buf:]
    m = jnp.maximum(jnp.max(s_old, axis=-1, keepdims=True), jnp.max(s_new, axis=-1, keepdims=True))
    e_old = jnp.exp(s_old - m)
    e_new = jnp.exp(s_new - m)
    l = jnp.sum(e_old, axis=-1, keepdims=True) + jnp.sum(e_new, axis=-1, keepdims=True)
    pv = _dot_nt(e_old.astype(BF16), cache_ref[0, GROUP_W:, :].astype(BF16)) + _dot(e_new.astype(BF16), v_new)
    pv = pv / l
    lse = m + jnp.log(l)
    o_ref[...] = _head_select([pv[h * t:(h + 1) * t] for h in range(HEADS_PER_GROUP)], t)
    lse_ref[...] = _head_select([lse[h * t:(h + 1) * t] for h in range(HEADS_PER_GROUP)], t)


def _sample_bias(g, n_buf, t):
    win, dil = DIL_GROUPS[g]
    slopes = _alibi_slopes()[g]
    tq = jnp.arange(t)[:, None]
    j = jnp.arange(n_buf + BAND)[None, :]
    dist = n_buf + tq - j
    valid = (dist >= 0) & (dist % dil == 0) & (dist <= win) & (j < n_buf + t)
    bias = -slopes[:, None, None] * dist.astype(F32)
    return jnp.where(valid[None], bias, MASKED).reshape(HEADS_PER_GROUP * t, n_buf + BAND)


def _attention_sample(qkv, cache_t, g, t):
    nb, n_buf = cache_t.shape[0], cache_t.shape[2]
    col = lambda part: pl.BlockSpec((t, GROUP_W), lambda i: (i, part))
    kern = functools.partial(_attn_sample_kernel, n_buf=n_buf, t=t)
    return pl.pallas_call(
        kern,
        grid=(nb,),
        in_specs=[col(0), col(1), col(2),
                  pl.BlockSpec((1, 2 * GROUP_W, n_buf), lambda i: (i, 0, 0)),
                  pl.BlockSpec((HEADS_PER_GROUP * t, n_buf + BAND), lambda i: (0, 0))],
        out_specs=[pl.BlockSpec((t, GROUP_W), lambda i: (i, 0)),
                   pl.BlockSpec((t, GROUP_W), lambda i: (i, 0))],
        out_shape=[jax.ShapeDtypeStruct((nb * t, GROUP_W), F32),
                   jax.ShapeDtypeStruct((nb * t, GROUP_W), F32)],
        compiler_params=_params("arbitrary"),
        name=f"attn_sample_g{g}",
    )(qkv, qkv, qkv, cache_t, _sample_bias(g, n_buf, t))


CONV_HALO = 32


def _conv_tail(acc, b_ref, g_ref, beta_ref):
    return _silu(_layer_norm(acc + b_ref[...], g_ref[...], beta_ref[...]))


def _conv_prompt_kernel(halo_ref, u_ref, w_ref, b_ref, g_ref, beta_ref, o_ref, hist, *, tm):
    i = pl.program_id(0)
    hist[0:CONV_HALO, :] = jnp.where(i == 0, 0.0, halo_ref[...])
    hist[CONV_HALO:, :] = u_ref[...]
    off = CONV_HALO - (CONV_WIDTH - 1)
    acc = jnp.zeros((tm, CONV_CH), F32)
    for j in range(CONV_WIDTH):
        acc = acc + w_ref[j:j + 1, :] * hist[off + j:off + j + tm, :]
    o_ref[...] = _conv_tail(acc, b_ref, g_ref, beta_ref).astype(o_ref.dtype)


def _conv_prompt(u, w_dw, b_dw, ln_g, ln_b, tm):
    n = u.shape[0]
    vec = pl.BlockSpec((1, CONV_CH), lambda i: (0, 0))
    return pl.pallas_call(
        functools.partial(_conv_prompt_kernel, tm=tm),
        grid=(n // tm,),
        in_specs=[pl.BlockSpec((CONV_HALO, CONV_CH), lambda i: (jnp.maximum(i * (tm // CONV_HALO) - 1, 0), 0)),
                  pl.BlockSpec((tm, CONV_CH), lambda i: (i, 0)),
                  pl.BlockSpec((CONV_WIDTH, CONV_CH), lambda i: (0, 0)), vec, vec, vec],
        out_specs=pl.BlockSpec((tm, CONV_CH), lambda i: (i, 0)),
        out_shape=jax.ShapeDtypeStruct((n, CONV_CH), BF16),
        scratch_shapes=[pltpu.VMEM((CONV_HALO + tm, CONV_CH), F32)],
        compiler_params=_params("arbitrary"),
        name="conv_prompt",
    )(u, u, w_dw, b_dw, ln_g, ln_b)


def _conv_sample_kernel(hist_ref, w_ref, b_ref, g_ref, beta_ref, o_ref, *, t):
    acc = jnp.zeros((t, CONV_CH), F32)
    for j in range(CONV_WIDTH):
        acc = acc + w_ref[j:j + 1, :] * hist_ref[0, j:j + t, :]
    o_ref[...] = _conv_tail(acc, b_ref, g_ref, beta_ref)


def _conv_sample(u_hist, w_dw, b_dw, ln_g, ln_b, t):
    nb, rows = u_hist.shape[0], u_hist.shape[1]
    vec = pl.BlockSpec((1, CONV_CH), lambda i: (0, 0))
    return pl.pallas_call(
        functools.partial(_conv_sample_kernel, t=t),
        grid=(nb,),
        in_specs=[pl.BlockSpec((1, rows, CONV_CH), lambda i: (i, 0, 0)),
                  pl.BlockSpec((CONV_WIDTH, CONV_CH), lambda i: (0, 0)), vec, vec, vec],
        out_specs=pl.BlockSpec((t, CONV_CH), lambda i: (i, 0)),
        out_shape=jax.ShapeDtypeStruct((nb * t, CONV_CH), F32),
        compiler_params=_params("arbitrary"),
        name="conv_sample",
    )(u_hist, w_dw, b_dw, ln_g, ln_b)


ROW_SLABS = D_MODEL // LANES


def _token_order(ref, dil, scr, tm):
    if dil == 1:
        return ref[...].astype(F32)
    for r in range(dil):
        for half in range(GROUP_W // LANES):
            c = r * GROUP_W + half * LANES
            scr[half, pl.ds(r, tm // dil, stride=dil), :] = ref[:, c:c + LANES].astype(F32)
    return jnp.concatenate([scr[half] for half in range(GROUP_W // LANES)], axis=1)


def _store_rows(ref, val):
    rows = val.shape[0]
    for c in range(ROW_SLABS):
        ref[pl.ds(c, rows, stride=ROW_SLABS), :] = val[:, c * LANES:(c + 1) * LANES]


def _load_rows(ref):
    rows = ref.shape[0] // ROW_SLABS
    return jnp.concatenate([ref[pl.ds(c, rows, stride=ROW_SLABS), :] for c in range(ROW_SLABS)], axis=1)


def _row(ref, i):
    return ref.at[pl.ds(pl.multiple_of(i * ROW_SLABS, ROW_SLABS), ROW_SLABS)]


def _outproj_kernel(o0, o1, o2, l0, l1, l2, c_ref, x_ref, w_ref, b_ref, g_ref, beta_ref, h_ref, *scr, dils, tm):
    os_ = [_token_order(r, d, scr[2 * g], tm) for g, (r, d) in enumerate(zip((o0, o1, o2), dils))]
    ls = [_token_order(r, d, scr[2 * g + 1], tm) for g, (r, d) in enumerate(zip((l0, l1, l2), dils))]
    m = jnp.maximum(jnp.maximum(ls[0], ls[1]), ls[2])
    es = [jnp.exp(l - m) for l in ls]
    inv = 1.0 / (es[0] + es[1] + es[2])
    mixed = b_ref[...]
    for g in range(N_GROUPS):
        a = (os_[g] * (es[g] * inv)).astype(BF16)
        mixed = mixed + _dot(a, w_ref[g * GROUP_W:(g + 1) * GROUP_W, :])
    mixed = mixed + _dot(c_ref[...].astype(BF16), w_ref[ATTN_W:, :])
    _store_rows(h_ref, _layer_norm(DN_ALPHA * x_ref[...] + mixed, g_ref[...], beta_ref[...]))


def _outproj(os_, lses, conv, x, w_o, b_o, ln_g, ln_b, tm, dils):
    n = x.shape[0]
    grp = [pl.BlockSpec((tm // d, d * GROUP_W), lambda i: (i, 0)) for d in dils]
    vec = pl.BlockSpec((1, D_MODEL), lambda i: (0, 0))
    return pl.pallas_call(
        functools.partial(_outproj_kernel, dils=dils, tm=tm),
        grid=(n // tm,),
        in_specs=grp + grp + [pl.BlockSpec((tm, CONV_CH), lambda i: (i, 0)),
                              pl.BlockSpec((tm, D_MODEL), lambda i: (i, 0)),
                              pl.BlockSpec((D_MODEL, D_MODEL), lambda i: (0, 0)), vec, vec, vec],
        out_specs=pl.BlockSpec((tm * ROW_SLABS, LANES), lambda i: (i, 0)),
        out_shape=jax.ShapeDtypeStruct((n * ROW_SLABS, LANES), F32),
        scratch_shapes=[pltpu.VMEM((GROUP_W // LANES, tm, LANES), F32) for _ in range(2 * N_GROUPS)],
        compiler_params=_params("arbitrary"),
        name="outproj",
    )(*os_, *lses, conv, x, w_o, b_o, ln_g, ln_b)


ROUTER_TM = 256


def _first_index(hit, idx, limit, axis):
    return jnp.min(jnp.where(hit, idx, limit), axis=axis, keepdims=True)


def _router_kernel(ha_ref, hb_ref, w_ref, rb_ref, tri_ref, idx_ref, wt_ref, pos_ref, cnt_ref, run, *, tiles_a):
    i = pl.program_id(0)
    tm = ROUTER_TM

    @pl.when(i == 0)
    def _():
        run[...] = jnp.zeros_like(run)

    h = jnp.where(i < tiles_a, _load_rows(ha_ref), _load_rows(hb_ref))
    logits = _dot_nt(w_ref[...], h.astype(BF16))
    scores = jax.nn.sigmoid(logits)
    biased = scores + rb_ref[...]
    groups = [biased[g * EXPERTS_PER_GROUP:(g + 1) * EXPERTS_PER_GROUP] for g in range(N_EXPERT_GROUPS)]
    ei = lax.broadcasted_iota(I32, (EXPERTS_PER_GROUP, tm), 0).astype(F32)
    gs = []
    for bg in groups:
        m1 = jnp.max(bg, axis=0, keepdims=True)
        f1 = _first_index(bg == m1, ei, float(EXPERTS_PER_GROUP), 0)
        m2 = jnp.max(jnp.where(ei == f1, -jnp.inf, bg), axis=0, keepdims=True)
        gs.append(m1 + m2)
    gs = jnp.concatenate(gs, axis=0)
    gi = lax.broadcasted_iota(I32, gs.shape, 0).astype(F32)
    keep = jnp.zeros(gs.shape, F32)
    cur = gs
    for _ in range(TOPK_GROUPS):
        m = jnp.max(cur, axis=0, keepdims=True)
        f = _first_index(cur == m, gi, float(N_EXPERT_GROUPS), 0)
        hit = gi == f
        keep = jnp.where(hit, 1.0, keep)
        cur = jnp.where(hit, -jnp.inf, cur)
    masked = jnp.concatenate([jnp.where(keep[g:g + 1] > 0.0, bg, -jnp.inf) for g, bg in enumerate(groups)], axis=0)
    xi = lax.broadcasted_iota(I32, (N_EXPERTS, tm), 0).astype(F32)
    cur = masked
    sel = jnp.zeros((N_EXPERTS, tm), F32)
    picks = []
    for _ in range(TOP_K):
        m = jnp.max(cur, axis=0, keepdims=True)
        f = _first_index(cur == m, xi, float(N_EXPERTS), 0)
        hit = xi == f
        picks.append((f, hit))
        sel = jnp.where(hit, 1.0, sel)
        cur = jnp.where(hit, -jnp.inf, cur)
    before = _dot(sel.astype(BF16), tri_ref[...]) + run[...]
    run[...] = run[...] + jnp.sum(sel, axis=1, keepdims=True)
    ws = [jnp.sum(jnp.where(hit, scores, 0.0), axis=0, keepdims=True) for _, hit in picks]
    wsum = ws[0]
    for w in ws[1:]:
        wsum = wsum + w
    for k, (f, hit) in enumerate(picks):
        idx_ref[k:k + 1, :] = f.astype(I32)
        wt_ref[k:k + 1, :] = ws[k] / wsum * ROUTED_SCALE
        pos_ref[k:k + 1, :] = jnp.sum(jnp.where(hit, before, 0.0), axis=0, keepdims=True).astype(I32)
    cnt_ref[...] = jnp.broadcast_to(run[...], cnt_ref.shape).astype(I32)


def _two_part_specs(ha, hb, tm, index_of):
    tiles_a = ha.shape[0] // (tm * ROW_SLABS)
    tiles_b = hb.shape[0] // (tm * ROW_SLABS)
    block = (tm * ROW_SLABS, LANES)
    spec_a = pl.BlockSpec(block, lambda *a: (jnp.minimum(index_of(*a), tiles_a - 1), 0))
    spec_b = pl.BlockSpec(block, lambda *a: (jnp.maximum(index_of(*a) - tiles_a, 0), 0))
    return tiles_a, tiles_b, spec_a, spec_b


def _route(ha, hb, w_router_t, router_bias):
    tm = ROUTER_TM
    tiles_a, tiles_b, spec_a, spec_b = _two_part_specs(ha, hb, tm, lambda i: i)
    t = (tiles_a + tiles_b) * tm
    tri = (jnp.arange(tm)[:, None] < jnp.arange(tm)[None, :]).astype(BF16)
    slot = pl.BlockSpec((TOP_K, tm), lambda i: (0, i))
    idx, wts, pos, cnt = pl.pallas_call(
        functools.partial(_router_kernel, tiles_a=tiles_a),
        grid=(t // tm,),
        in_specs=[spec_a, spec_b,
                  pl.BlockSpec((N_EXPERTS, D_MODEL), lambda i: (0, 0)),
                  pl.BlockSpec((N_EXPERTS, 1), lambda i: (0, 0)),
                  pl.BlockSpec((tm, tm), lambda i: (0, 0))],
        out_specs=[slot, slot, slot, pl.BlockSpec((N_EXPERTS, 128), lambda i: (0, 0))],
        out_shape=[jax.ShapeDtypeStruct((TOP_K, t), I32), jax.ShapeDtypeStruct((TOP_K, t), F32),
                   jax.ShapeDtypeStruct((TOP_K, t), I32), jax.ShapeDtypeStruct((N_EXPERTS, 128), I32)],
        scratch_shapes=[pltpu.VMEM((N_EXPERTS, 1), F32)],
        compiler_params=_params("arbitrary"),
        name="router",
    )(ha, hb, w_router_t, router_bias.reshape(N_EXPERTS, 1), tri)
    return idx, wts, pos, cnt[:, 0]


DISPATCH_TM = 256


def _dispatch_kernel(starts_ref, idx_ref, pos_ref, ha_ref, hb_ref, xs_ref, zbuf, sem, *, tiles_a):
    i = pl.program_id(0)

    @pl.when(i == 0)
    def _():
        zbuf[...] = jnp.zeros(zbuf.shape, F32)
        spare = pl.ds(xs_ref.shape[0] - zbuf.shape[0], zbuf.shape[0])
        zero = pltpu.make_async_copy(zbuf, xs_ref.at[spare], sem)
        zero.start()
        zero.wait()

    def scatter_tile(h_ref):
        def issue(j, c):
            for k in range(TOP_K):
                dst = starts_ref[idx_ref[k, j]] + pos_ref[k, j]
                pltpu.make_async_copy(_row(h_ref, j), _row(xs_ref, dst), sem).start(priority=k % 2)
            return c

        lax.fori_loop(0, DISPATCH_TM, issue, 0)
        for k in range(TOP_K):
            pltpu.make_async_copy(h_ref, xs_ref.at[pl.ds(0, DISPATCH_TM * ROW_SLABS)], sem).wait()

    @pl.when(i < tiles_a)
    def _():
        scatter_tile(ha_ref)

    @pl.when(i >= tiles_a)
    def _():
        scatter_tile(hb_ref)


def _dispatch(starts, idx, pos, ha, hb, spare_rows):
    tm = DISPATCH_TM
    tiles_a, tiles_b, spec_a, spec_b = _two_part_specs(ha, hb, tm, lambda i, s: i)
    t = (tiles_a + tiles_b) * tm
    slot = pl.BlockSpec((TOP_K, tm), lambda i, s: (0, i), memory_space=pltpu.SMEM)
    return pl.pallas_call(
        functools.partial(_dispatch_kernel, tiles_a=tiles_a),
        grid_spec=pltpu.PrefetchScalarGridSpec(
            num_scalar_prefetch=1,
            grid=(t // tm,),
            in_specs=[slot, slot, spec_a, spec_b],
            out_specs=pl.BlockSpec(memory_space=pl.ANY),
            scratch_shapes=[pltpu.VMEM((spare_rows * ROW_SLABS, LANES), F32), pltpu.SemaphoreType.DMA],
        ),
        out_shape=jax.ShapeDtypeStruct(((t * TOP_K + spare_rows) * ROW_SLABS, LANES), F32),
        compiler_params=_params("arbitrary"),
        name="dispatch",
    )(starts, idx, pos, ha, hb)


EXPERT_CH = 256


def _expert_kernel(row0_ref, exp_ref, live_ref, fresh_ref, next_ref, par_ref, xs_ref, wg_ref, wu_ref, wd_ref, ys_ref,
                   xbuf, ybuf, sg, su, sd, wgb, wub, wdb, xsem, ysem, wsem):
    v = pl.program_id(0)
    nv = pl.num_programs(0)
    slot = v % 2
    rows = lambda u: pl.ds(pl.multiple_of(row0_ref[u] * ROW_SLABS, ROW_SLABS), EXPERT_CH * ROW_SLABS)
    x_copy = lambda u, s: pltpu.make_async_copy(xs_ref.at[rows(u)], xbuf.at[s], xsem.at[s])
    y_copy = lambda u, s: pltpu.make_async_copy(ybuf.at[s], ys_ref.at[rows(u)], ysem.at[s])
    w_copies = lambda e, s: [pltpu.make_async_copy(w.at[e], stage.at[s], wsem.at[s])
                             for w, stage in ((wg_ref, sg), (wu_ref, su), (wd_ref, sd))]
    spare = pl.ds(ys_ref.shape[0] - EXPERT_CH * ROW_SLABS, EXPERT_CH * ROW_SLABS)

    @pl.when(v == 0)
    def _():
        for c in w_copies(exp_ref[0], 0):
            c.start()
        x_copy(0, 0).start()
        ybuf[0] = jnp.zeros(ybuf.shape[1:], F32)
        zero = pltpu.make_async_copy(ybuf.at[0], ys_ref.at[spare], ysem.at[0])
        zero.start()
        zero.wait()

    nxt = jnp.minimum(v + 1, nv - 1)

    @pl.when((v + 1 < nv) & (live_ref[nxt] == 1))
    def _():
        x_copy(nxt, 1 - slot).start()

    @pl.when(fresh_ref[v] == 1)
    def _():
        p = par_ref[v]
        for c in w_copies(exp_ref[v], p):
            c.wait()

        @pl.when(next_ref[v] >= 0)
        def _():
            for c in w_copies(next_ref[v], 1 - p):
                c.start()

        wgb[...] = sg[p].astype(BF16)
        wub[...] = su[p].astype(BF16)
        wdb[...] = sd[p].astype(BF16)

    @pl.when(live_ref[v] == 1)
    def _():
        x_copy(v, slot).wait()
        x = _load_rows(xbuf.at[slot]).astype(BF16)
        hid = (_silu(_dot(x, wgb[...])) * _dot(x, wub[...])).astype(BF16)
        _store_rows(ybuf.at[slot], _dot(hid, wdb[...]))

    @pl.when((v > 0) & (live_ref[jnp.maximum(v - 1, 0)] == 1))
    def _():
        y_copy(jnp.maximum(v - 1, 0), 1 - slot).wait()

    @pl.when(live_ref[v] == 1)
    def _():
        y_copy(v, slot).start()

        @pl.when(v == nv - 1)
        def _():
            y_copy(v, slot).wait()


def _chunks(counts, n_rows):
    ch = EXPERT_CH
    nv = n_rows // ch + N_EXPERTS
    ends = jnp.cumsum(counts)
    starts = ends - counts
    nch = (counts + ch - 1) // ch
    cend = jnp.cumsum(nch)
    cstart = cend - nch
    v = jnp.arange(nv, dtype=I32)
    live = v < cend[-1]
    vc = jnp.minimum(v, cend[-1] - 1)
    ids = jnp.arange(N_EXPERTS, dtype=I32)
    e = jnp.minimum(jnp.sum(cend[None, :] <= vc[:, None], axis=1), N_EXPERTS - 1).astype(I32)
    of_e = lambda a: jnp.sum(jnp.where(e[:, None] == ids[None, :], a[None, :], 0), axis=1)
    row0 = (of_e(starts) + (vc - of_e(cstart)) * ch).astype(I32)
    fresh = jnp.concatenate([jnp.ones((1,), I32), (e[1:] != e[:-1]).astype(I32)])
    order = jnp.cumsum(counts > 0) - 1
    order_e = of_e(order)
    follows = (counts > 0)[None, :] & (order[None, :] == order_e[:, None] + 1)
    nxt = jnp.sum(jnp.where(follows, ids[None, :] + 1, 0), axis=1) - 1
    return (starts.astype(I32), row0, e, live.astype(I32), jnp.where(live, fresh, 0).astype(I32),
            nxt.astype(I32), (order_e % 2).astype(I32))


def _experts(chunks, xs, w_gate, w_up, w_down):
    ch = EXPERT_CH
    buf = pltpu.VMEM((2, ch * ROW_SLABS, LANES), F32)
    anywhere = pl.BlockSpec(memory_space=pl.ANY)
    return pl.pallas_call(
        _expert_kernel,
        grid_spec=pltpu.PrefetchScalarGridSpec(
            num_scalar_prefetch=len(chunks),
            grid=(chunks[0].shape[0],),
            in_specs=[anywhere] * 4,
            out_specs=anywhere,
            scratch_shapes=[buf, buf,
                            pltpu.VMEM((2, D_MODEL, EXPERT_FF), F32), pltpu.VMEM((2, D_MODEL, EXPERT_FF), F32),
                            pltpu.VMEM((2, EXPERT_FF, D_MODEL), F32),
                            pltpu.VMEM((D_MODEL, EXPERT_FF), BF16), pltpu.VMEM((D_MODEL, EXPERT_FF), BF16),
                            pltpu.VMEM((EXPERT_FF, D_MODEL), BF16),
                            pltpu.SemaphoreType.DMA((2,)), pltpu.SemaphoreType.DMA((2,)),
                            pltpu.SemaphoreType.DMA((2,))],
        ),
        out_shape=jax.ShapeDtypeStruct(xs.shape, F32),
        compiler_params=_params("arbitrary"),
        name="experts",
    )(*chunks, xs, w_gate, w_up, w_down)


COMBINE_TM = 128


def _combine_kernel(starts_ref, idx_ref, pos_ref, wt_ref, h_ref, sg_ref, su_ref, sd_ref, g_ref, beta_ref,
                    ys_ref, o_ref, ybuf, sem):
    tm = COMBINE_TM

    def issue(j, c):
        for k in range(TOP_K):
            src = starts_ref[idx_ref[k, j]] + pos_ref[k, j]
            pltpu.make_async_copy(_row(ys_ref, src), _row(ybuf.at[k], j), sem).start(priority=k % 2)
        return c

    lax.fori_loop(0, tm, issue, 0)

    h = _load_rows(h_ref)
    hb = h.astype(BF16)
    hid = (_silu(_dot(hb, sg_ref[...])) * _dot(hb, su_ref[...])).astype(BF16)
    acc = DN_ALPHA * h + _dot(hid, sd_ref[...])

    for k in range(TOP_K):
        pltpu.make_async_copy(ys_ref.at[pl.ds(0, tm * ROW_SLABS)], ybuf.at[k], sem).wait()

    for k in range(TOP_K):
        acc = acc + wt_ref[:, k:k + 1] * _load_rows(ybuf.at[k])
    o_ref[...] = _layer_norm(acc, g_ref[...], beta_ref[...])


def _combine(starts, idx, pos, wts_tk, ys, ws_gate, ws_up, ws_down, ln_g, ln_b, h, t0):
    tm = COMBINE_TM
    n = h.shape[0] // ROW_SLABS
    b0 = t0 // tm
    slot = pl.BlockSpec((TOP_K, tm), lambda i, s: (0, b0 + i), memory_space=pltpu.SMEM)
    vec = pl.BlockSpec((1, D_MODEL), lambda i, s: (0, 0))
    return pl.pallas_call(
        _combine_kernel,
        grid_spec=pltpu.PrefetchScalarGridSpec(
            num_scalar_prefetch=1,
            grid=(n // tm,),
            in_specs=[slot, slot,
                      pl.BlockSpec((tm, TOP_K), lambda i, s: (b0 + i, 0)),
                      pl.BlockSpec((tm * ROW_SLABS, LANES), lambda i, s: (i, 0)),
                      pl.BlockSpec((D_MODEL, EXPERT_FF), lambda i, s: (0, 0)),
                      pl.BlockSpec((D_MODEL, EXPERT_FF), lambda i, s: (0, 0)),
                      pl.BlockSpec((EXPERT_FF, D_MODEL), lambda i, s: (0, 0)),
                      vec, vec,
                      pl.BlockSpec(memory_space=pl.ANY)],
            out_specs=pl.BlockSpec((tm, D_MODEL), lambda i, s: (i, 0)),
            scratch_shapes=[pltpu.VMEM((TOP_K, tm * ROW_SLABS, LANES), F32), pltpu.SemaphoreType.DMA],
        ),
        out_shape=jax.ShapeDtypeStruct((n, D_MODEL), F32),
        compiler_params=_params("arbitrary"),
        name="combine",
    )(starts, idx, pos, wts_tk, h, ws_gate, ws_up, ws_down, ln_g, ln_b, ys)


def _kv_cache(kv_tail, g, keep):
    k = kv_tail[-keep:, g * GROUP_W:(g + 1) * GROUP_W]
    v = kv_tail[-keep:, ATTN_W + g * GROUP_W:ATTN_W + (g + 1) * GROUP_W]
    return jnp.stack([k, v], axis=1).reshape(keep, 2, HEADS_PER_GROUP, HEAD_DIM)


def kernel(x_prompt, x_sample, cache_kv_w128, cache_kv_w512, cache_kv_w2048, state_conv, w_in, b_in, w_dw, b_dw,
           conv_ln_g, conv_ln_b, w_o, b_o, ln1_g, ln1_b, w_router, router_bias, w_gate, w_up, w_down, ws_gate,
           ws_up, ws_down, ln2_g, ln2_b):
    assert w_in.shape[0] == DEPTH == 1
    batch, seq, _ = x_prompt.shape
    dec_batch, dec_seq, _ = x_sample.shape
    assert batch == 1
    n_s = dec_batch * dec_seq
    caches = (cache_kv_w128, cache_kv_w512, cache_kv_w2048)
    row = lambda a: a[0].reshape(1, -1)

    w_in_b = w_in[0].astype(BF16)
    b_in_r = row(b_in)
    w_o_b = w_o[0].astype(BF16)
    conv_w = (w_dw[0], row(b_dw), row(conv_ln_g), row(conv_ln_b))
    ln1 = (row(ln1_g), row(ln1_b))

    xp = x_prompt[0]
    keep_p = min(max(w for w, _ in DIL_GROUPS), seq)
    dils = tuple(d for _, d in DIL_GROUPS)
    *qkv_p, u_p, kv_p = _project(xp, w_in_b, b_in_r, 512, keep_p, BF16, dils)
    attn_p = [_attention_prompt(qkv_p[g], g) for g in range(N_GROUPS)]
    conv_p = _conv_prompt(u_p, *conv_w, 512)
    h_p = _outproj([a[0] for a in attn_p], [a[1] for a in attn_p], conv_p, xp, w_o_b, row(b_o), *ln1, 512, dils)

    xs = x_sample.reshape(n_s, D_MODEL)
    ones = (1,) * N_GROUPS
    *qkv_s, u_s, kv_s = _project(xs, w_in_b, b_in_r, n_s, n_s, F32, ones)
    caches_t = [jnp.transpose(c[0].reshape(dec_batch, -1, 2 * GROUP_W), (0, 2, 1)) for c in caches]
    attn_s = [_attention_sample(qkv_s[g], caches_t[g], g, dec_seq) for g in range(N_GROUPS)]
    u_hist = jnp.concatenate([state_conv[0], u_s.reshape(dec_batch, dec_seq, CONV_CH)], axis=1)
    conv_s = _conv_sample(u_hist, *conv_w, dec_seq)
    h_s = _outproj([a[0] for a in attn_s], [a[1] for a in attn_s], conv_s, xs, w_o_b, row(b_o), *ln1, n_s, ones)

    idx, wts, pos, counts = _route(h_p, h_s, w_router[0].T.astype(BF16), router_bias[0])
    starts, *chunks = _chunks(counts, (seq + n_s) * TOP_K)
    x_sorted = _dispatch(starts, idx, pos, h_p, h_s, EXPERT_CH)
    y_sorted = _experts(chunks, x_sorted, w_gate[0], w_up[0], w_down[0])
    shared = (ws_gate[0].astype(BF16), ws_up[0].astype(BF16), ws_down[0].astype(BF16))
    comb = functools.partial(_combine, starts, idx, pos, wts.T, y_sorted, *shared, row(ln2_g), row(ln2_b))
    y_p = comb(h_p, 0)
    y_s = comb(h_s, seq)

    kv_prompt = [_kv_cache(kv_p, g, min(w, seq))[None, None] for g, (w, _) in enumerate(DIL_GROUPS)]
    assert seq >= CONV_WIDTH - 1
    conv_prompt = u_p[-(CONV_WIDTH - 1):]
    kv_s4 = kv_s.reshape(dec_batch, dec_seq, 2, N_GROUPS, HEADS_PER_GROUP, HEAD_DIM)
    kv_sample = [kv_s4[:, :, :, g][None] for g in range(N_GROUPS)]
    conv_sample = u_hist[:, -(CONV_WIDTH - 1):]
    return (y_p[None], y_s.reshape(dec_batch, dec_seq, D_MODEL), *kv_prompt, conv_prompt[None, None],
            *kv_sample, conv_sample[None])
```

```python
import functools

import jax
import jax.numpy as jnp
import numpy as np
from jax import lax
from jax.experimental import pallas as pl
from jax.experimental.pallas import tpu as pltpu
from jax.experimental.pallas import tpu_sc as plsc

F32 = jnp.float32
BF16 = jnp.bfloat16
I32 = jnp.int32

D_MODEL = 1024
HEAD_DIM = 64
HEADS_PER_GROUP = 4
GROUP_W = HEADS_PER_GROUP * HEAD_DIM
DIL_GROUPS = ((128, 1), (512, 4), (2048, 16))
N_GROUPS = len(DIL_GROUPS)
ATTN_W = N_GROUPS * GROUP_W
CONV_CH = D_MODEL - ATTN_W
CONV_WIDTH = 31
IN_W = 3 * ATTN_W + 2 * CONV_CH
BAND = 128
N_EXPERTS = 256
TOP_K = 8
N_EXPERT_GROUPS = 8
EXPERTS_PER_GROUP = N_EXPERTS // N_EXPERT_GROUPS
TOPK_GROUPS = 4
EXPERT_FF = 256
ROUTED_SCALE = 2.5
DEPTH = 1
DN_ALPHA = (2 * DEPTH) ** 0.25
LN_EPS = 1e-5
MASKED = -1e30

VMEM_LIMIT_BYTES = 56 * 1024 * 1024


def _params(*sem):
    return pltpu.CompilerParams(dimension_semantics=sem, vmem_limit_bytes=VMEM_LIMIT_BYTES)


def _dot(a, b):
    return jnp.dot(a, b, preferred_element_type=F32)


def _dot_nt(a, b):
    return lax.dot_general(a, b, (((1,), (1,)), ((), ())), preferred_element_type=F32)


def _layer_norm(x, g, b):
    mu = jnp.mean(x, axis=-1, keepdims=True)
    xc = x - mu
    var = jnp.mean(xc * xc, axis=-1, keepdims=True)
    return xc * lax.rsqrt(var + LN_EPS) * g + b


def _silu(x):
    return x * jax.nn.sigmoid(x)


def _alibi_slopes():
    n = N_GROUPS * HEADS_PER_GROUP
    h = jnp.arange(1, n + 1, dtype=F32)
    return (2.0 ** (-8.0 * h / n)).reshape(N_GROUPS, HEADS_PER_GROUP)


LANES = 128


def _proj_kernel(x_ref, w_ref, b_ref, *refs, dils, tm):
    qkv_refs, (u_ref, kv_ref, zs) = refs[:N_GROUPS], refs[N_GROUPS:]
    x = x_ref[...].astype(BF16)
    for part in range(3):
        for g in range(N_GROUPS):
            c0 = part * ATTN_W + g * GROUP_W
            z = _dot(x, w_ref[:, c0:c0 + GROUP_W]) + b_ref[:, c0:c0 + GROUP_W]
            if part > 0:
                kv_ref[:, c0 - ATTN_W:c0 - ATTN_W + GROUP_W] = z
            out, dil = qkv_refs[g], dils[g]
            if dil == 1:
                out[:, part * GROUP_W:(part + 1) * GROUP_W] = z.astype(out.dtype)
                continue
            for half in range(GROUP_W // LANES):
                zs[half] = z[:, half * LANES:(half + 1) * LANES]
            for r in range(dil):
                for half in range(GROUP_W // LANES):
                    c = r * 3 * GROUP_W + part * GROUP_W + half * LANES
                    out[:, c:c + LANES] = zs[half, pl.ds(r, tm // dil, stride=dil), :].astype(out.dtype)
    c0 = 3 * ATTN_W
    a = _dot(x, w_ref[:, c0:c0 + CONV_CH]) + b_ref[:, c0:c0 + CONV_CH]
    gate = _dot(x, w_ref[:, c0 + CONV_CH:]) + b_ref[:, c0 + CONV_CH:]
    u_ref[...] = a * jax.nn.sigmoid(gate)


def _project(x, w_in, b_in, tm, keep, qkv_dtype, dils):
    n = x.shape[0]
    nt = n // tm
    nk = keep // tm
    return pl.pallas_call(
        functools.partial(_proj_kernel, dils=dils, tm=tm),
        grid=(nt,),
        in_specs=[
            pl.BlockSpec((tm, D_MODEL), lambda i: (i, 0)),
            pl.BlockSpec((D_MODEL, IN_W), lambda i: (0, 0)),
            pl.BlockSpec((1, IN_W), lambda i: (0, 0)),
        ],
        out_specs=[pl.BlockSpec((tm // d, d * 3 * GROUP_W), lambda i: (i, 0)) for d in dils] + [
            pl.BlockSpec((tm, CONV_CH), lambda i: (i, 0)),
            pl.BlockSpec((tm, 2 * ATTN_W), lambda i: (jnp.maximum(i - (nt - nk), 0), 0)),
        ],
        out_shape=[jax.ShapeDtypeStruct((n // d, d * 3 * GROUP_W), qkv_dtype) for d in dils] + [
            jax.ShapeDtypeStruct((n, CONV_CH), F32),
            jax.ShapeDtypeStruct((keep, 2 * ATTN_W), F32),
        ],
        scratch_shapes=[pltpu.VMEM((GROUP_W // LANES, tm, LANES), F32)],
        compiler_params=_params("arbitrary"),
        name="proj",
    )(x, w_in, b_in)


def _head_select(parts, rows):
    col = lax.broadcasted_iota(I32, (rows, GROUP_W), 1) // HEAD_DIM
    out = jnp.broadcast_to(parts[-1], (rows, GROUP_W))
    for h in range(HEADS_PER_GROUP - 2, -1, -1):
        out = jnp.where(col == h, parts[h], out)
    return out


def _head_rows(q, rows):
    col = lax.broadcasted_iota(I32, (rows, GROUP_W), 1) // HEAD_DIM
    return jnp.concatenate([jnp.where(col == h, q, jnp.zeros_like(q)) for h in range(HEADS_PER_GROUP)], axis=0)


def _softmax_pv(s, v, rows):
    m = jnp.max(s, axis=-1, keepdims=True)
    e = jnp.exp(s - m)
    l = jnp.sum(e, axis=-1, keepdims=True)
    pv = _dot(e.astype(BF16), v) / l
    lse = m + jnp.log(l)
    o = _head_select([pv[h * rows:(h + 1) * rows] for h in range(HEADS_PER_GROUP)], rows)
    lse_x = _head_select([lse[h * rows:(h + 1) * rows] for h in range(HEADS_PER_GROUP)], rows)
    return o, lse_x


ATTN_QB = 4


def _attn_kernel(q_ref, kp_ref, kc_ref, vp_ref, vc_ref, bias_ref, o_ref, lse_ref):
    b = pl.program_id(1)
    for j in range(ATTN_QB):
        rows = slice(j * BAND, (j + 1) * BAND)
        qm = _head_rows(q_ref[rows, :], BAND)
        if j == 0:
            k = jnp.concatenate([kp_ref[...], kc_ref[rows, :]], axis=0)
            v = jnp.concatenate([vp_ref[...], vc_ref[rows, :]], axis=0)
            bias = bias_ref[jnp.minimum(b, 1)]
        else:
            k = kc_ref[(j - 1) * BAND:(j + 1) * BAND, :]
            v = vc_ref[(j - 1) * BAND:(j + 1) * BAND, :]
            bias = bias_ref[1]
        s = _dot_nt(qm, k) * HEAD_DIM ** -0.5 + bias
        o, lse_x = _softmax_pv(s, v, BAND)
        o_ref[rows, :] = o.astype(o_ref.dtype)
        lse_ref[rows, :] = lse_x


def _prompt_bias(g, dil):
    slopes = _alibi_slopes()[g]
    qi = jnp.arange(BAND)[:, None]
    kj = jnp.arange(2 * BAND)[None, :]
    steps = qi + BAND - kj
    nk = DIL_GROUPS[g][0] // dil
    valid = (steps >= 0) & (steps <= nk)
    bias = -slopes[:, None, None] * (steps * dil).astype(F32)
    inner = jnp.where(valid[None], bias, MASKED)
    first = jnp.where((valid & (kj >= BAND))[None], bias, MASKED)
    return jnp.stack([first, inner]).reshape(2, HEADS_PER_GROUP * BAND, 2 * BAND)


def _attention_prompt(view, g):
    win, dil = DIL_GROUPS[g]
    n_cls = view.shape[0]
    rows = ATTN_QB * BAND
    assert win // dil <= BAND and n_cls % rows == 0 and view.shape[1] == dil * 3 * GROUP_W
    cur = lambda part: pl.BlockSpec((rows, GROUP_W), lambda r, b: (b, r * 3 + part))
    prev = lambda part: pl.BlockSpec((BAND, GROUP_W), lambda r, b: (jnp.maximum(b * ATTN_QB - 1, 0), r * 3 + part))
    return pl.pallas_call(
        _attn_kernel,
        grid=(dil, n_cls // rows),
        in_specs=[cur(0), prev(1), cur(1), prev(2), cur(2),
                  pl.BlockSpec((2, HEADS_PER_GROUP * BAND, 2 * BAND), lambda r, b: (0, 0, 0))],
        out_specs=[pl.BlockSpec((rows, GROUP_W), lambda r, b: (b, r)),
                   pl.BlockSpec((rows, GROUP_W), lambda r, b: (b, r))],
        out_shape=[jax.ShapeDtypeStruct((n_cls, dil * GROUP_W), BF16),
                   jax.ShapeDtypeStruct((n_cls, dil * GROUP_W), F32)],
        compiler_params=_params("arbitrary", "arbitrary"),
        name=f"attn_g{g}",
    )(view, view, view, view, view, _prompt_bias(g, dil))


def _attn_sample_kernel(q_ref, k_ref, v_ref, cache_ref, bias_ref, o_ref, lse_ref, *, n_buf, t):
    pad = jnp.zeros((BAND - t, GROUP_W), F32)
    k_new = jnp.concatenate([k_ref[...], pad], axis=0).astype(BF16)
    v_new = jnp.concatenate([v_ref[...], pad], axis=0).astype(BF16)
    qm = _head_rows(q_ref[...], t).astype(BF16)
    scale = HEAD_DIM ** -0.5
    s_old = _dot(qm, cache_ref[0, 0:GROUP_W, :].astype(BF16)) * scale + bias_ref[:, 0:n_buf]
    s_new = _dot_nt(qm, k_new) * scale + bias_ref[:, n_buf:]
    m = jnp.maximum(jnp.max(s_old, axis=-1, keepdims=True), jnp.max(s_new, axis=-1, keepdims=True))
    e_old = jnp.exp(s_old - m)
    e_new = jnp.exp(s_new - m)
    l = jnp.sum(e_old, axis=-1, keepdims=True) + jnp.sum(e_new, axis=-1, keepdims=True)
    pv = _dot_nt(e_old.astype(BF16), cache_ref[0, GROUP_W:, :].astype(BF16)) + _dot(e_new.astype(BF16), v_new)
    pv = pv / l
    lse = m + jnp.log(l)
    o_ref[...] = _head_select([pv[h * t:(h + 1) * t] for h in range(HEADS_PER_GROUP)], t)
    lse_ref[...] = _head_select([lse[h * t:(h + 1) * t] for h in range(HEADS_PER_GROUP)], t)


def _sample_bias(g, n_buf, t):
    win, dil = DIL_GROUPS[g]
    slopes = _alibi_slopes()[g]
    tq = jnp.arange(t)[:, None]
    j = jnp.arange(n_buf + BAND)[None, :]
    dist = n_buf + tq - j
    valid = (dist >= 0) & (dist % dil == 0) & (dist <= win) & (j < n_buf + t)
    bias = -slopes[:, None, None] * dist.astype(F32)
    return jnp.where(valid[None], bias, MASKED).reshape(HEADS_PER_GROUP * t, n_buf + BAND)


def _attention_sample(qkv, cache_t, g, t):
    nb, n_buf = cache_t.shape[0], cache_t.shape[2]
    col = lambda part: pl.BlockSpec((t, GROUP_W), lambda i: (i, part))
    kern = functools.partial(_attn_sample_kernel, n_buf=n_buf, t=t)
    return pl.pallas_call(
        kern,
        grid=(nb,),
        in_specs=[col(0), col(1), col(2),
                  pl.BlockSpec((1, 2 * GROUP_W, n_buf), lambda i: (i, 0, 0)),
                  pl.BlockSpec((HEADS_PER_GROUP * t, n_buf + BAND), lambda i: (0, 0))],
        out_specs=[pl.BlockSpec((t, GROUP_W), lambda i: (i, 0)),
                   pl.BlockSpec((t, GROUP_W), lambda i: (i, 0))],
        out_shape=[jax.ShapeDtypeStruct((nb * t, GROUP_W), F32),
                   jax.ShapeDtypeStruct((nb * t, GROUP_W), F32)],
        compiler_params=_params("arbitrary"),
        name=f"attn_sample_g{g}",
    )(qkv, qkv, qkv, cache_t, _sample_bias(g, n_buf, t))


CONV_HALO = 32


def _conv_tail(acc, b_ref, g_ref, beta_ref):
    return _silu(_layer_norm(acc + b_ref[...], g_ref[...], beta_ref[...]))


def _conv_prompt_kernel(halo_ref, u_ref, w_ref, b_ref, g_ref, beta_ref, o_ref, hist, *, tm):
    i = pl.program_id(0)
    hist[0:CONV_HALO, :] = jnp.where(i == 0, 0.0, halo_ref[...])
    hist[CONV_HALO:, :] = u_ref[...]
    off = CONV_HALO - (CONV_WIDTH - 1)
    acc = jnp.zeros((tm, CONV_CH), F32)
    for j in range(CONV_WIDTH):
        acc = acc + w_ref[j:j + 1, :] * hist[off + j:off + j + tm, :]
    o_ref[...] = _conv_tail(acc, b_ref, g_ref, beta_ref).astype(o_ref.dtype)


def _conv_prompt(u, w_dw, b_dw, ln_g, ln_b, tm):
    n = u.shape[0]
    vec = pl.BlockSpec((1, CONV_CH), lambda i: (0, 0))
    return pl.pallas_call(
        functools.partial(_conv_prompt_kernel, tm=tm),
        grid=(n // tm,),
        in_specs=[pl.BlockSpec((CONV_HALO, CONV_CH), lambda i: (jnp.maximum(i * (tm // CONV_HALO) - 1, 0), 0)),
                  pl.BlockSpec((tm, CONV_CH), lambda i: (i, 0)),
                  pl.BlockSpec((CONV_WIDTH, CONV_CH), lambda i: (0, 0)), vec, vec, vec],
        out_specs=pl.BlockSpec((tm, CONV_CH), lambda i: (i, 0)),
        out_shape=jax.ShapeDtypeStruct((n, CONV_CH), BF16),
        scratch_shapes=[pltpu.VMEM((CONV_HALO + tm, CONV_CH), F32)],
        compiler_params=_params("arbitrary"),
        name="conv_prompt",
    )(u, u, w_dw, b_dw, ln_g, ln_b)


def _conv_sample_kernel(hist_ref, w_ref, b_ref, g_ref, beta_ref, o_ref, *, t):
    acc = jnp.zeros((t, CONV_CH), F32)
    for j in range(CONV_WIDTH):
        acc = acc + w_ref[j:j + 1, :] * hist_ref[0, j:j + t, :]
    o_ref[...] = _conv_tail(acc, b_ref, g_ref, beta_ref)


def _conv_sample(u_hist, w_dw, b_dw, ln_g, ln_b, t):
    nb, rows = u_hist.shape[0], u_hist.shape[1]
    vec = pl.BlockSpec((1, CONV_CH), lambda i: (0, 0))
    return pl.pallas_call(
        functools.partial(_conv_sample_kernel, t=t),
        grid=(nb,),
        in_specs=[pl.BlockSpec((1, rows, CONV_CH), lambda i: (i, 0, 0)),
                  pl.BlockSpec((CONV_WIDTH, CONV_CH), lambda i: (0, 0)), vec, vec, vec],
        out_specs=pl.BlockSpec((t, CONV_CH), lambda i: (i, 0)),
        out_shape=jax.ShapeDtypeStruct((nb * t, CONV_CH), F32),
        compiler_params=_params("arbitrary"),
        name="conv_sample",
    )(u_hist, w_dw, b_dw, ln_g, ln_b)


ROW_SLABS = D_MODEL // LANES


def _token_order(ref, dil, scr, tm):
    if dil == 1:
        return ref[...].astype(F32)
    for r in range(dil):
        for half in range(GROUP_W // LANES):
            c = r * GROUP_W + half * LANES
            scr[half, pl.ds(r, tm // dil, stride=dil), :] = ref[:, c:c + LANES].astype(F32)
    return jnp.concatenate([scr[half] for half in range(GROUP_W // LANES)], axis=1)


def _store_rows(ref, val):
    rows = val.shape[0]
    for c in range(ROW_SLABS):
        ref[pl.ds(c, rows, stride=ROW_SLABS), :] = val[:, c * LANES:(c + 1) * LANES]


def _load_rows(ref):
    rows = ref.shape[0] // ROW_SLABS
    return jnp.concatenate([ref[pl.ds(c, rows, stride=ROW_SLABS), :] for c in range(ROW_SLABS)], axis=1)


def _row(ref, i):
    return ref.at[pl.ds(pl.multiple_of(i * ROW_SLABS, ROW_SLABS), ROW_SLABS)]


def _outproj_kernel(o0, o1, o2, l0, l1, l2, c_ref, x_ref, w_ref, b_ref, g_ref, beta_ref, h_ref, *scr, dils, tm):
    os_ = [_token_order(r, d, scr[2 * g], tm) for g, (r, d) in enumerate(zip((o0, o1, o2), dils))]
    ls = [_token_order(r, d, scr[2 * g + 1], tm) for g, (r, d) in enumerate(zip((l0, l1, l2), dils))]
    m = jnp.maximum(jnp.maximum(ls[0], ls[1]), ls[2])
    es = [jnp.exp(l - m) for l in ls]
    inv = 1.0 / (es[0] + es[1] + es[2])
    mixed = b_ref[...]
    for g in range(N_GROUPS):
        a = (os_[g] * (es[g] * inv)).astype(BF16)
        mixed = mixed + _dot(a, w_ref[g * GROUP_W:(g + 1) * GROUP_W, :])
    mixed = mixed + _dot(c_ref[...].astype(BF16), w_ref[ATTN_W:, :])
    _store_rows(h_ref, _layer_norm(DN_ALPHA * x_ref[...] + mixed, g_ref[...], beta_ref[...]))


def _outproj(os_, lses, conv, x, w_o, b_o, ln_g, ln_b, tm, dils):
    n = x.shape[0]
    grp = [pl.BlockSpec((tm // d, d * GROUP_W), lambda i: (i, 0)) for d in dils]
    vec = pl.BlockSpec((1, D_MODEL), lambda i: (0, 0))
    return pl.pallas_call(
        functools.partial(_outproj_kernel, dils=dils, tm=tm),
        grid=(n // tm,),
        in_specs=grp + grp + [pl.BlockSpec((tm, CONV_CH), lambda i: (i, 0)),
                              pl.BlockSpec((tm, D_MODEL), lambda i: (i, 0)),
                              pl.BlockSpec((D_MODEL, D_MODEL), lambda i: (0, 0)), vec, vec, vec],
        out_specs=pl.BlockSpec((tm * ROW_SLABS, LANES), lambda i: (i, 0)),
        out_shape=jax.ShapeDtypeStruct((n * ROW_SLABS, LANES), F32),
        scratch_shapes=[pltpu.VMEM((GROUP_W // LANES, tm, LANES), F32) for _ in range(2 * N_GROUPS)],
        compiler_params=_params("arbitrary"),
        name="outproj",
    )(*os_, *lses, conv, x, w_o, b_o, ln_g, ln_b)


ROUTER_TM = 256


def _first_index(hit, idx, limit, axis):
    return jnp.min(jnp.where(hit, idx, limit), axis=axis, keepdims=True)


def _router_kernel(ha_ref, hb_ref, w_ref, rb_ref, tri_ref, idx_ref, wt_ref, pos_ref, cnt_ref, run, *, tiles_a):
    i = pl.program_id(0)
    tm = ROUTER_TM

    @pl.when(i == 0)
    def _():
        run[...] = jnp.zeros_like(run)

    h = jnp.where(i < tiles_a, _load_rows(ha_ref), _load_rows(hb_ref))
    logits = _dot_nt(w_ref[...], h.astype(BF16))
    scores = jax.nn.sigmoid(logits)
    biased = scores + rb_ref[...]
    groups = [biased[g * EXPERTS_PER_GROUP:(g + 1) * EXPERTS_PER_GROUP] for g in range(N_EXPERT_GROUPS)]
    ei = lax.broadcasted_iota(I32, (EXPERTS_PER_GROUP, tm), 0).astype(F32)
    gs = []
    for bg in groups:
        m1 = jnp.max(bg, axis=0, keepdims=True)
        f1 = _first_index(bg == m1, ei, float(EXPERTS_PER_GROUP), 0)
        m2 = jnp.max(jnp.where(ei == f1, -jnp.inf, bg), axis=0, keepdims=True)
        gs.append(m1 + m2)
    gs = jnp.concatenate(gs, axis=0)
    gi = lax.broadcasted_iota(I32, gs.shape, 0).astype(F32)
    keep = jnp.zeros(gs.shape, F32)
    cur = gs
    for _ in range(TOPK_GROUPS):
        m = jnp.max(cur, axis=0, keepdims=True)
        f = _first_index(cur == m, gi, float(N_EXPERT_GROUPS), 0)
        hit = gi == f
        keep = jnp.where(hit, 1.0, keep)
        cur = jnp.where(hit, -jnp.inf, cur)
    masked = jnp.concatenate([jnp.where(keep[g:g + 1] > 0.0, bg, -jnp.inf) for g, bg in enumerate(groups)], axis=0)
    xi = lax.broadcasted_iota(I32, (N_EXPERTS, tm), 0).astype(F32)
    cur = masked
    sel = jnp.zeros((N_EXPERTS, tm), F32)
    picks = []
    for _ in range(TOP_K):
        m = jnp.max(cur, axis=0, keepdims=True)
        f = _first_index(cur == m, xi, float(N_EXPERTS), 0)
        hit = xi == f
        picks.append((f, hit))
        sel = jnp.where(hit, 1.0, sel)
        cur = jnp.where(hit, -jnp.inf, cur)
    before = _dot(sel.astype(BF16), tri_ref[...]) + run[...]
    run[...] = run[...] + jnp.sum(sel, axis=1, keepdims=True)
    ws = [jnp.sum(jnp.where(hit, scores, 0.0), axis=0, keepdims=True) for _, hit in picks]
    wsum = ws[0]
    for w in ws[1:]:
        wsum = wsum + w
    for k, (f, hit) in enumerate(picks):
        idx_ref[k:k + 1, :] = f.astype(I32)
        wt_ref[k:k + 1, :] = ws[k] / wsum * ROUTED_SCALE
        pos_ref[k:k + 1, :] = jnp.sum(jnp.where(hit, before, 0.0), axis=0, keepdims=True).astype(I32)
    cnt_ref[...] = jnp.broadcast_to(run[...], cnt_ref.shape).astype(I32)


def _two_part_specs(ha, hb, tm, index_of):
    tiles_a = ha.shape[0] // (tm * ROW_SLABS)
    tiles_b = hb.shape[0] // (tm * ROW_SLABS)
    block = (tm * ROW_SLABS, LANES)
    spec_a = pl.BlockSpec(block, lambda *a: (jnp.minimum(index_of(*a), tiles_a - 1), 0))
    spec_b = pl.BlockSpec(block, lambda *a: (jnp.maximum(index_of(*a) - tiles_a, 0), 0))
    return tiles_a, tiles_b, spec_a, spec_b


def _route(ha, hb, w_router_t, router_bias):
    tm = ROUTER_TM
    tiles_a, tiles_b, spec_a, spec_b = _two_part_specs(ha, hb, tm, lambda i: i)
    t = (tiles_a + tiles_b) * tm
    tri = (jnp.arange(tm)[:, None] < jnp.arange(tm)[None, :]).astype(BF16)
    slot = pl.BlockSpec((TOP_K, tm), lambda i: (0, i))
    idx, wts, pos, cnt = pl.pallas_call(
        functools.partial(_router_kernel, tiles_a=tiles_a),
        grid=(t // tm,),
        in_specs=[spec_a, spec_b,
                  pl.BlockSpec((N_EXPERTS, D_MODEL), lambda i: (0, 0)),
                  pl.BlockSpec((N_EXPERTS, 1), lambda i: (0, 0)),
                  pl.BlockSpec((tm, tm), lambda i: (0, 0))],
        out_specs=[slot, slot, slot, pl.BlockSpec((N_EXPERTS, 128), lambda i: (0, 0))],
        out_shape=[jax.ShapeDtypeStruct((TOP_K, t), I32), jax.ShapeDtypeStruct((TOP_K, t), F32),
                   jax.ShapeDtypeStruct((TOP_K, t), I32), jax.ShapeDtypeStruct((N_EXPERTS, 128), I32)],
        scratch_shapes=[pltpu.VMEM((N_EXPERTS, 1), F32)],
        compiler_params=_params("arbitrary"),
        name="router",
    )(ha, hb, w_router_t, router_bias.reshape(N_EXPERTS, 1), tri)
    return idx, wts, pos, cnt[:, 0]


DEST_TM = 1280


def _dest_kernel(idx_ref, pos_ref, starts_ref, dest_ref):
    tm = idx_ref.shape[1]
    ei = lax.broadcasted_iota(I32, (N_EXPERTS, tm), 0)
    starts = starts_ref[...]
    for k in range(TOP_K):
        first = jnp.sum(jnp.where(ei == idx_ref[k:k + 1, :], starts, 0.0), axis=0, keepdims=True)
        dest_ref[k:k + 1, :] = first.astype(I32) + pos_ref[k:k + 1, :]


def _dest_rows(idx, pos, starts):
    t = idx.shape[1]
    tm = DEST_TM
    slot = pl.BlockSpec((TOP_K, tm), lambda i: (0, i))
    return pl.pallas_call(
        _dest_kernel,
        grid=(t // tm,),
        in_specs=[slot, slot, pl.BlockSpec((N_EXPERTS, 1), lambda i: (0, 0))],
        out_specs=slot,
        out_shape=jax.ShapeDtypeStruct((TOP_K, t), I32),
        compiler_params=_params("arbitrary"),
        name="dest_rows",
    )(idx, pos, starts.astype(F32).reshape(N_EXPERTS, 1))


SC_CORES = 2
SC_SUBCORES = 16
SC_WORKERS = SC_CORES * SC_SUBCORES
SC_WINDOW = 32


def _sc_worker():
    return lax.axis_index("s") * SC_CORES + lax.axis_index("c")


def _sc_mesh():
    return plsc.VectorSubcoreMesh(core_axis_name="c", subcore_axis_name="s")


def _sc_scatter_rows(ha, hb, dest_w, zeros, n_out):
    w = SC_WINDOW
    na, nb = ha.shape[0], hb.shape[0]
    spare = zeros.shape[0]
    wa, wb = na // w, nb // w
    assert na % (SC_WORKERS * 2 * w) == 0 and nb % w == 0 and wb <= SC_WORKERS
    per_w = wa // SC_WORKERS
    rows_t = pltpu.VMEM((w, ROW_SLABS, LANES), F32)
    idx_t = pltpu.VMEM((TOP_K, w), I32)

    @functools.partial(
        pl.kernel, mesh=_sc_mesh(),
        out_type=jax.ShapeDtypeStruct((n_out + spare, ROW_SLABS, LANES), F32),
        scratch_types=[rows_t, rows_t, idx_t, idx_t, pltpu.SemaphoreType.DMA, pltpu.SemaphoreType.DMA],
    )
    def scatter(ha_hbm, hb_hbm, dest_hbm, zeros_hbm, out_hbm, rows0, rows1, idx0, idx1, sem0, sem1):
        wid = _sc_worker()
        bufs = ((rows0, idx0, sem0), (rows1, idx1, sem1))

        @pl.when(wid == SC_WORKERS - 1)
        def _():
            pltpu.sync_copy(zeros_hbm, out_hbm.at[pl.ds(n_out, spare)])

        def load(src_hbm, row0, win, b):
            pltpu.sync_copy(src_hbm.at[pl.ds(pl.multiple_of(row0, 8), w)], bufs[b][0])
            pltpu.sync_copy(dest_hbm.at[win], bufs[b][1])

        def copies(b):
            return [pltpu.make_async_copy(bufs[b][0], out_hbm.at[bufs[b][1].at[k]], bufs[b][2])
                    for k in range(TOP_K)]

        win0 = wid * per_w
        load(ha_hbm, win0 * w, win0, 0)

        @pl.loop(0, per_w, step=2)
        def _(i0):
            for b in range(2):
                i = i0 + b
                for c in copies(b):
                    c.start()

                @pl.when(i + 1 < per_w)
                def _():
                    load(ha_hbm, (win0 + i + 1) * w, win0 + i + 1, 1 - b)

                for c in copies(b):
                    c.wait()

        @pl.when(wid < wb)
        def _():
            load(hb_hbm, wid * w, wa + wid, 0)
            for c in copies(0):
                c.start()
            for c in copies(0):
                c.wait()

    return scatter(ha, hb, dest_w, zeros)


def _sc_gather_rows(table, idx):
    n = idx.shape[0]
    w = SC_WINDOW
    assert n % (SC_WORKERS * 2 * w) == 0
    per_w = n // SC_WORKERS
    nwin = per_w // w
    rows_t = pltpu.VMEM((w, ROW_SLABS, LANES), F32)

    @functools.partial(
        pl.kernel, mesh=_sc_mesh(),
        out_type=jax.ShapeDtypeStruct((n, ROW_SLABS, LANES), F32),
        scratch_types=[pltpu.VMEM((per_w,), I32), rows_t, rows_t, pltpu.SemaphoreType.DMA, pltpu.SemaphoreType.DMA],
    )
    def gather(table_hbm, idx_hbm, out_hbm, idx_v, rows0, rows1, sem0, sem1):
        base = pl.multiple_of(_sc_worker() * per_w, 8)
        pltpu.sync_copy(idx_hbm.at[pl.ds(base, per_w)], idx_v)
        bufs = ((rows0, sem0), (rows1, sem1))
        fetch = lambda i, b: pltpu.make_async_copy(
            table_hbm.at[idx_v.at[pl.ds(pl.multiple_of(i * w, 8), w)]], bufs[b][0], bufs[b][1])
        fetch(0, 0).start()

        @pl.loop(0, nwin, step=2)
        def _(i0):
            for b in range(2):
                i = i0 + b
                fetch(i, b).wait()

                @pl.when(i + 1 < nwin)
                def _():
                    fetch(i + 1, 1 - b).start()

                pltpu.sync_copy(bufs[b][0], out_hbm.at[pl.ds(pl.multiple_of(base + i * w, 8), w)])

    return gather(table, idx)


EXPERT_CH = 256


def _expert_kernel(row0_ref, exp_ref, live_ref, fresh_ref, next_ref, par_ref, xs_ref, wg_ref, wu_ref, wd_ref, ys_ref,
                   xbuf, ybuf, sg, su, sd, wgb, wub, wdb, xsem, ysem, wsem):
    v = pl.program_id(0)
    nv = pl.num_programs(0)
    slot = v % 2
    rows = lambda u: pl.ds(pl.multiple_of(row0_ref[u] * ROW_SLABS, ROW_SLABS), EXPERT_CH * ROW_SLABS)
    x_copy = lambda u, s: pltpu.make_async_copy(xs_ref.at[rows(u)], xbuf.at[s], xsem.at[s])
    y_copy = lambda u, s: pltpu.make_async_copy(ybuf.at[s], ys_ref.at[rows(u)], ysem.at[s])
    w_copies = lambda e, s: [pltpu.make_async_copy(w.at[e], stage.at[s], wsem.at[s])
                             for w, stage in ((wg_ref, sg), (wu_ref, su), (wd_ref, sd))]
    spare = pl.ds(ys_ref.shape[0] - EXPERT_CH * ROW_SLABS, EXPERT_CH * ROW_SLABS)

    @pl.when(v == 0)
    def _():
        for c in w_copies(exp_ref[0], 0):
            c.start()
        x_copy(0, 0).start()
        ybuf[0] = jnp.zeros(ybuf.shape[1:], F32)
        zero = pltpu.make_async_copy(ybuf.at[0], ys_ref.at[spare], ysem.at[0])
        zero.start()
        zero.wait()

    nxt = jnp.minimum(v + 1, nv - 1)

    @pl.when((v + 1 < nv) & (live_ref[nxt] == 1))
    def _():
        x_copy(nxt, 1 - slot).start()

    @pl.when(fresh_ref[v] == 1)
    def _():
        p = par_ref[v]
        for c in w_copies(exp_ref[v], p):
            c.wait()

        @pl.when(next_ref[v] >= 0)
        def _():
            for c in w_copies(next_ref[v], 1 - p):
                c.start()

        wgb[...] = sg[p].astype(BF16)
        wub[...] = su[p].astype(BF16)
        wdb[...] = sd[p].astype(BF16)

    @pl.when(live_ref[v] == 1)
    def _():
        x_copy(v, slot).wait()
        x = _load_rows(xbuf.at[slot]).astype(BF16)
        hid = (_silu(_dot(x, wgb[...])) * _dot(x, wub[...])).astype(BF16)
        _store_rows(ybuf.at[slot], _dot(hid, wdb[...]))

    @pl.when((v > 0) & (live_ref[jnp.maximum(v - 1, 0)] == 1))
    def _():
        y_copy(jnp.maximum(v - 1, 0), 1 - slot).wait()

    @pl.when(live_ref[v] == 1)
    def _():
        y_copy(v, slot).start()

        @pl.when(v == nv - 1)
        def _():
            y_copy(v, slot).wait()


def _chunks(counts, n_rows):
    ch = EXPERT_CH
    nv = n_rows // ch + N_EXPERTS
    ends = jnp.cumsum(counts)
    starts = ends - counts
    nch = (counts + ch - 1) // ch
    cend = jnp.cumsum(nch)
    cstart = cend - nch
    v = jnp.arange(nv, dtype=I32)
    live = v < cend[-1]
    vc = jnp.minimum(v, cend[-1] - 1)
    ids = jnp.arange(N_EXPERTS, dtype=I32)
    e = jnp.minimum(jnp.sum(cend[None, :] <= vc[:, None], axis=1), N_EXPERTS - 1).astype(I32)
    of_e = lambda a: jnp.sum(jnp.where(e[:, None] == ids[None, :], a[None, :], 0), axis=1)
    row0 = (of_e(starts) + (vc - of_e(cstart)) * ch).astype(I32)
    fresh = jnp.concatenate([jnp.ones((1,), I32), (e[1:] != e[:-1]).astype(I32)])
    order = jnp.cumsum(counts > 0) - 1
    order_e = of_e(order)
    follows = (counts > 0)[None, :] & (order[None, :] == order_e[:, None] + 1)
    nxt = jnp.sum(jnp.where(follows, ids[None, :] + 1, 0), axis=1) - 1
    return (starts.astype(I32), row0, e, live.astype(I32), jnp.where(live, fresh, 0).astype(I32),
            nxt.astype(I32), (order_e % 2).astype(I32))


def _experts(chunks, xs, w_gate, w_up, w_down):
    ch = EXPERT_CH
    buf = pltpu.VMEM((2, ch * ROW_SLABS, LANES), F32)
    anywhere = pl.BlockSpec(memory_space=pl.ANY)
    return pl.pallas_call(
        _expert_kernel,
        grid_spec=pltpu.PrefetchScalarGridSpec(
            num_scalar_prefetch=len(chunks),
            grid=(chunks[0].shape[0],),
            in_specs=[anywhere] * 4,
            out_specs=anywhere,
            scratch_shapes=[buf, buf,
                            pltpu.VMEM((2, D_MODEL, EXPERT_FF), F32), pltpu.VMEM((2, D_MODEL, EXPERT_FF), F32),
                            pltpu.VMEM((2, EXPERT_FF, D_MODEL), F32),
                            pltpu.VMEM((D_MODEL, EXPERT_FF), BF16), pltpu.VMEM((D_MODEL, EXPERT_FF), BF16),
                            pltpu.VMEM((EXPERT_FF, D_MODEL), BF16),
                            pltpu.SemaphoreType.DMA((2,)), pltpu.SemaphoreType.DMA((2,)),
                            pltpu.SemaphoreType.DMA((2,))],
        ),
        out_shape=jax.ShapeDtypeStruct(xs.shape, F32),
        compiler_params=_params("arbitrary"),
        name="experts",
    )(*chunks, xs, w_gate, w_up, w_down)


COMBINE_TM = 128


def _combine_kernel(wt_ref, h_ref, sg_ref, su_ref, sd_ref, g_ref, beta_ref, *refs):
    y_refs, o_ref = refs[:TOP_K], refs[TOP_K]
    h = _load_rows(h_ref)
    hb = h.astype(BF16)
    hid = (_silu(_dot(hb, sg_ref[...])) * _dot(hb, su_ref[...])).astype(BF16)
    acc = DN_ALPHA * h + _dot(hid, sd_ref[...])
    for k in range(TOP_K):
        acc = acc + wt_ref[:, k:k + 1] * _load_rows(y_refs[k])
    o_ref[...] = _layer_norm(acc, g_ref[...], beta_ref[...])


def _combine(wts_tk, yg, ws_gate, ws_up, ws_down, ln_g, ln_b, h, t0):
    tm = COMBINE_TM
    n = h.shape[0] // ROW_SLABS
    b0 = t0 // tm
    per_slot = yg.shape[0] // (TOP_K * tm * ROW_SLABS)
    rows = pl.BlockSpec((tm * ROW_SLABS, LANES), lambda i: (i, 0))
    slot_rows = [pl.BlockSpec((tm * ROW_SLABS, LANES), lambda i, k=k: (k * per_slot + b0 + i, 0))
                 for k in range(TOP_K)]
    vec = pl.BlockSpec((1, D_MODEL), lambda i: (0, 0))
    return pl.pallas_call(
        _combine_kernel,
        grid=(n // tm,),
        in_specs=[pl.BlockSpec((tm, TOP_K), lambda i: (b0 + i, 0)), rows,
                  pl.BlockSpec((D_MODEL, EXPERT_FF), lambda i: (0, 0)),
                  pl.BlockSpec((D_MODEL, EXPERT_FF), lambda i: (0, 0)),
                  pl.BlockSpec((EXPERT_FF, D_MODEL), lambda i: (0, 0)),
                  vec, vec] + slot_rows,
        out_specs=pl.BlockSpec((tm, D_MODEL), lambda i: (i, 0)),
        out_shape=jax.ShapeDtypeStruct((n, D_MODEL), F32),
        compiler_params=_params("arbitrary"),
        name="combine",
    )(wts_tk, h, ws_gate, ws_up, ws_down, ln_g, ln_b, *([yg] * TOP_K))


def _kv_cache(kv_tail, g, keep):
    k = kv_tail[-keep:, g * GROUP_W:(g + 1) * GROUP_W]
    v = kv_tail[-keep:, ATTN_W + g * GROUP_W:ATTN_W + (g + 1) * GROUP_W]
    return jnp.stack([k, v], axis=1).reshape(keep, 2, HEADS_PER_GROUP, HEAD_DIM)


def kernel(x_prompt, x_sample, cache_kv_w128, cache_kv_w512, cache_kv_w2048, state_conv, w_in, b_in, w_dw, b_dw,
           conv_ln_g, conv_ln_b, w_o, b_o, ln1_g, ln1_b, w_router, router_bias, w_gate, w_up, w_down, ws_gate,
           ws_up, ws_down, ln2_g, ln2_b):
    assert w_in.shape[0] == DEPTH == 1
    batch, seq, _ = x_prompt.shape
    dec_batch, dec_seq, _ = x_sample.shape
    assert batch == 1
    n_s = dec_batch * dec_seq
    caches = (cache_kv_w128, cache_kv_w512, cache_kv_w2048)
    row = lambda a: a[0].reshape(1, -1)

    w_in_b = w_in[0].astype(BF16)
    b_in_r = row(b_in)
    w_o_b = w_o[0].astype(BF16)
    conv_w = (w_dw[0], row(b_dw), row(conv_ln_g), row(conv_ln_b))
    ln1 = (row(ln1_g), row(ln1_b))

    xp = x_prompt[0]
    keep_p = min(max(w for w, _ in DIL_GROUPS), seq)
    dils = tuple(d for _, d in DIL_GROUPS)
    *qkv_p, u_p, kv_p = _project(xp, w_in_b, b_in_r, 512, keep_p, BF16, dils)
    attn_p = [_attention_prompt(qkv_p[g], g) for g in range(N_GROUPS)]
    conv_p = _conv_prompt(u_p, *conv_w, 512)
    h_p = _outproj([a[0] for a in attn_p], [a[1] for a in attn_p], conv_p, xp, w_o_b, row(b_o), *ln1, 512, dils)

    xs = x_sample.reshape(n_s, D_MODEL)
    ones = (1,) * N_GROUPS
    *qkv_s, u_s, kv_s = _project(xs, w_in_b, b_in_r, n_s, n_s, F32, ones)
    caches_t = [jnp.transpose(c[0].reshape(dec_batch, -1, 2 * GROUP_W), (0, 2, 1)) for c in caches]
    attn_s = [_attention_sample(qkv_s[g], caches_t[g], g, dec_seq) for g in range(N_GROUPS)]
    u_hist = jnp.concatenate([state_conv[0], u_s.reshape(dec_batch, dec_seq, CONV_CH)], axis=1)
    conv_s = _conv_sample(u_hist, *conv_w, dec_seq)
    h_s = _outproj([a[0] for a in attn_s], [a[1] for a in attn_s], conv_s, xs, w_o_b, row(b_o), *ln1, n_s, ones)

    idx, wts, pos, counts = _route(h_p, h_s, w_router[0].T.astype(BF16), router_bias[0])
    n_tok = seq + n_s
    starts, *chunks = _chunks(counts, n_tok * TOP_K)
    dest = _dest_rows(idx, pos, starts)
    dest_w = dest.reshape(TOP_K, n_tok // SC_WINDOW, SC_WINDOW).transpose(1, 0, 2)
    tiles = lambda a: a.reshape(-1, ROW_SLABS, LANES)
    flat = lambda a: a.reshape(-1, LANES)
    x_sorted = _sc_scatter_rows(tiles(h_p), tiles(h_s), dest_w, jnp.zeros((EXPERT_CH, ROW_SLABS, LANES), F32),
                                n_tok * TOP_K)
    y_sorted = _experts(chunks, flat(x_sorted), w_gate[0], w_up[0], w_down[0])
    y_slots = _sc_gather_rows(tiles(y_sorted), dest.reshape(-1))
    shared = (ws_gate[0].astype(BF16), ws_up[0].astype(BF16), ws_down[0].astype(BF16))
    comb = functools.partial(_combine, wts.T, flat(y_slots), *shared, row(ln2_g), row(ln2_b))
    y_p = comb(h_p, 0)
    y_s = comb(h_s, seq)

    kv_prompt = [_kv_cache(kv_p, g, min(w, seq))[None, None] for g, (w, _) in enumerate(DIL_GROUPS)]
    assert seq >= CONV_WIDTH - 1
    conv_prompt = u_p[-(CONV_WIDTH - 1):]
    kv_s4 = kv_s.reshape(dec_batch, dec_seq, 2, N_GROUPS, HEADS_PER_GROUP, HEAD_DIM)
    kv_sample = [kv_s4[:, :, :, g][None] for g in range(N_GROUPS)]
    conv_sample = u_hist[:, -(CONV_WIDTH - 1):]
    return (y_p[None], y_s.reshape(dec_batch, dec_seq, D_MODEL), *kv_prompt, conv_prompt[None, None],
            *kv_sample, conv_sample[None])
```

```python
import functools

import jax
import jax.numpy as jnp
from jax import lax
from jax.experimental import pallas as pl
from jax.experimental.pallas import tpu as pltpu
from jax.experimental.pallas import tpu_sc as plsc

F32 = jnp.float32
BF16 = jnp.bfloat16
I32 = jnp.int32

D_MODEL = 1024
HEAD_DIM = 64
HEADS_PER_GROUP = 4
GROUP_W = HEADS_PER_GROUP * HEAD_DIM
DIL_GROUPS = ((128, 1), (512, 4), (2048, 16))
N_GROUPS = len(DIL_GROUPS)
ATTN_W = N_GROUPS * GROUP_W
CONV_CH = D_MODEL - ATTN_W
CONV_WIDTH = 31
IN_W = 3 * ATTN_W + 2 * CONV_CH
BAND = 128
N_EXPERTS = 256
TOP_K = 8
N_EXPERT_GROUPS = 8
EXPERTS_PER_GROUP = N_EXPERTS // N_EXPERT_GROUPS
TOPK_GROUPS = 4
EXPERT_FF = 256
ROUTED_SCALE = 2.5
DEPTH = 1
DN_ALPHA = (2 * DEPTH) ** 0.25
LN_EPS = 1e-5
MASKED = -1e30

VMEM_LIMIT_BYTES = 56 * 1024 * 1024


def _params(*sem):
    return pltpu.CompilerParams(dimension_semantics=sem, vmem_limit_bytes=VMEM_LIMIT_BYTES)


def _dot(a, b):
    return jnp.dot(a, b, preferred_element_type=F32)


def _dot_nt(a, b):
    return lax.dot_general(a, b, (((1,), (1,)), ((), ())), preferred_element_type=F32)


def _layer_norm(x, g, b):
    mu = jnp.mean(x, axis=-1, keepdims=True)
    xc = x - mu
    var = jnp.mean(xc * xc, axis=-1, keepdims=True)
    return xc * lax.rsqrt(var + LN_EPS) * g + b


def _silu(x):
    return x * jax.nn.sigmoid(x)


def _alibi_slopes():
    n = N_GROUPS * HEADS_PER_GROUP
    h = jnp.arange(1, n + 1, dtype=F32)
    return (2.0 ** (-8.0 * h / n)).reshape(N_GROUPS, HEADS_PER_GROUP)


LANES = 128


def _proj_kernel(x_ref, w_ref, b_ref, *refs, dils, tm):
    qkv_refs, (u_ref, kv_ref, zs) = refs[:N_GROUPS], refs[N_GROUPS:]
    x = x_ref[...].astype(BF16)
    for part in range(3):
        for g in range(N_GROUPS):
            c0 = part * ATTN_W + g * GROUP_W
            z = _dot(x, w_ref[:, c0:c0 + GROUP_W]) + b_ref[:, c0:c0 + GROUP_W]
            if part > 0:
                kv_ref[:, c0 - ATTN_W:c0 - ATTN_W + GROUP_W] = z
            out, dil = qkv_refs[g], dils[g]
            if dil == 1:
                out[:, part * GROUP_W:(part + 1) * GROUP_W] = z.astype(out.dtype)
                continue
            for half in range(GROUP_W // LANES):
                zs[half] = z[:, half * LANES:(half + 1) * LANES]
            for r in range(dil):
                for half in range(GROUP_W // LANES):
                    c = r * 3 * GROUP_W + part * GROUP_W + half * LANES
                    out[:, c:c + LANES] = zs[half, pl.ds(r, tm // dil, stride=dil), :].astype(out.dtype)
    c0 = 3 * ATTN_W
    a = _dot(x, w_ref[:, c0:c0 + CONV_CH]) + b_ref[:, c0:c0 + CONV_CH]
    gate = _dot(x, w_ref[:, c0 + CONV_CH:]) + b_ref[:, c0 + CONV_CH:]
    u_ref[...] = a * jax.nn.sigmoid(gate)


def _project(x, w_in, b_in, tm, keep, qkv_dtype, dils):
    n = x.shape[0]
    nt = n // tm
    nk = keep // tm
    return pl.pallas_call(
        functools.partial(_proj_kernel, dils=dils, tm=tm),
        grid=(nt,),
        in_specs=[
            pl.BlockSpec((tm, D_MODEL), lambda i: (i, 0)),
            pl.BlockSpec((D_MODEL, IN_W), lambda i: (0, 0)),
            pl.BlockSpec((1, IN_W), lambda i: (0, 0)),
        ],
        out_specs=[pl.BlockSpec((tm // d, d * 3 * GROUP_W), lambda i: (i, 0)) for d in dils] + [
            pl.BlockSpec((tm, CONV_CH), lambda i: (i, 0)),
            pl.BlockSpec((tm, 2 * ATTN_W), lambda i: (jnp.maximum(i - (nt - nk), 0), 0)),
        ],
        out_shape=[jax.ShapeDtypeStruct((n // d, d * 3 * GROUP_W), qkv_dtype) for d in dils] + [
            jax.ShapeDtypeStruct((n, CONV_CH), F32),
            jax.ShapeDtypeStruct((keep, 2 * ATTN_W), F32),
        ],
        scratch_shapes=[pltpu.VMEM((GROUP_W // LANES, tm, LANES), F32)],
        compiler_params=_params("arbitrary"),
        name="proj",
    )(x, w_in, b_in)


def _head_select(parts, rows):
    col = lax.broadcasted_iota(I32, (rows, GROUP_W), 1) // HEAD_DIM
    out = jnp.broadcast_to(parts[-1], (rows, GROUP_W))
    for h in range(HEADS_PER_GROUP - 2, -1, -1):
        out = jnp.where(col == h, parts[h], out)
    return out


def _head_rows(q, rows):
    col = lax.broadcasted_iota(I32, (rows, GROUP_W), 1) // HEAD_DIM
    return jnp.concatenate([jnp.where(col == h, q, jnp.zeros_like(q)) for h in range(HEADS_PER_GROUP)], axis=0)


def _softmax_pv(s, v, rows):
    m = jnp.max(s, axis=-1, keepdims=True)
    e = jnp.exp(s - m)
    l = jnp.sum(e, axis=-1, keepdims=True)
    pv = _dot(e.astype(BF16), v) / l
    lse = m + jnp.log(l)
    o = _head_select([pv[h * rows:(h + 1) * rows] for h in range(HEADS_PER_GROUP)], rows)
    lse_x = _head_select([lse[h * rows:(h + 1) * rows] for h in range(HEADS_PER_GROUP)], rows)
    return o, lse_x


ATTN_QB = 4


def _attn_kernel(q_ref, kp_ref, kc_ref, vp_ref, vc_ref, bias_ref, o_ref, lse_ref):
    b = pl.program_id(1)
    for j in range(ATTN_QB):
        rows = slice(j * BAND, (j + 1) * BAND)
        qm = _head_rows(q_ref[rows, :], BAND)
        if j == 0:
            k = jnp.concatenate([kp_ref[...], kc_ref[rows, :]], axis=0)
            v = jnp.concatenate([vp_ref[...], vc_ref[rows, :]], axis=0)
            bias = bias_ref[jnp.minimum(b, 1)]
        else:
            k = kc_ref[(j - 1) * BAND:(j + 1) * BAND, :]
            v = vc_ref[(j - 1) * BAND:(j + 1) * BAND, :]
            bias = bias_ref[1]
        s = _dot_nt(qm, k) * HEAD_DIM ** -0.5 + bias
        o, lse_x = _softmax_pv(s, v, BAND)
        o_ref[rows, :] = o.astype(o_ref.dtype)
        lse_ref[rows, :] = lse_x


def _prompt_bias(g, dil):
    slopes = _alibi_slopes()[g]
    qi = jnp.arange(BAND)[:, None]
    kj = jnp.arange(2 * BAND)[None, :]
    steps = qi + BAND - kj
    nk = DIL_GROUPS[g][0] // dil
    valid = (steps >= 0) & (steps <= nk)
    bias = -slopes[:, None, None] * (steps * dil).astype(F32)
    inner = jnp.where(valid[None], bias, MASKED)
    first = jnp.where((valid & (kj >= BAND))[None], bias, MASKED)
    return jnp.stack([first, inner]).reshape(2, HEADS_PER_GROUP * BAND, 2 * BAND)


def _attention_prompt(view, g):
    win, dil = DIL_GROUPS[g]
    n_cls = view.shape[0]
    rows = ATTN_QB * BAND
    assert win // dil <= BAND and n_cls % rows == 0 and view.shape[1] == dil * 3 * GROUP_W
    cur = lambda part: pl.BlockSpec((rows, GROUP_W), lambda r, b: (b, r * 3 + part))
    prev = lambda part: pl.BlockSpec((BAND, GROUP_W), lambda r, b: (jnp.maximum(b * ATTN_QB - 1, 0), r * 3 + part))
    return pl.pallas_call(
        _attn_kernel,
        grid=(dil, n_cls // rows),
        in_specs=[cur(0), prev(1), cur(1), prev(2), cur(2),
                  pl.BlockSpec((2, HEADS_PER_GROUP * BAND, 2 * BAND), lambda r, b: (0, 0, 0))],
        out_specs=[pl.BlockSpec((rows, GROUP_W), lambda r, b: (b, r)),
                   pl.BlockSpec((rows, GROUP_W), lambda r, b: (b, r))],
        out_shape=[jax.ShapeDtypeStruct((n_cls, dil * GROUP_W), BF16),
                   jax.ShapeDtypeStruct((n_cls, dil * GROUP_W), F32)],
        compiler_params=_params("arbitrary", "arbitrary"),
        name=f"attn_g{g}",
    )(view, view, view, view, view, _prompt_bias(g, dil))


def _attn_sample_kernel(q_ref, k_ref, v_ref, cache_ref, bias_ref, o_ref, lse_ref, *, n_buf, t):
    pad = jnp.zeros((BAND - t, GROUP_W), F32)
    k_new = jnp.concatenate([k_ref[...], pad], axis=0).astype(BF16)
    v_new = jnp.concatenate([v_ref[...], pad], axis=0).astype(BF16)
    qm = _head_rows(q_ref[...], t).astype(BF16)
    scale = HEAD_DIM ** -0.5
    s_old = _dot(qm, cache_ref[0, 0:GROUP_W, :].astype(BF16)) * scale + bias_ref[:, 0:n_buf]
    s_new = _dot_nt(qm, k_new) * scale + bias_ref[:, n_buf:]
    m = jnp.maximum(jnp.max(s_old, axis=-1, keepdims=True), jnp.max(s_new, axis=-1, keepdims=True))
    e_old = jnp.exp(s_old - m)
    e_new = jnp.exp(s_new - m)
    l = jnp.sum(e_old, axis=-1, keepdims=True) + jnp.sum(e_new, axis=-1, keepdims=True)
    pv = _dot_nt(e_old.astype(BF16), cache_ref[0, GROUP_W:, :].astype(BF16)) + _dot(e_new.astype(BF16), v_new)
    pv = pv / l
    lse = m + jnp.log(l)
    o_ref[...] = _head_select([pv[h * t:(h + 1) * t] for h in range(HEADS_PER_GROUP)], t)
    lse_ref[...] = _head_select([lse[h * t:(h + 1) * t] for h in range(HEADS_PER_GROUP)], t)


def _sample_bias(g, n_buf, t):
    win, dil = DIL_GROUPS[g]
    slopes = _alibi_slopes()[g]
    tq = jnp.arange(t)[:, None]
    j = jnp.arange(n_buf + BAND)[None, :]
    dist = n_buf + tq - j
    valid = (dist >= 0) & (dist % dil == 0) & (dist <= win) & (j < n_buf + t)
    bias = -slopes[:, None, None] * dist.astype(F32)
    return jnp.where(valid[None], bias, MASKED).reshape(HEADS_PER_GROUP * t, n_buf + BAND)


def _attention_sample(qkv, cache_t, g, t):
    nb, n_buf = cache_t.shape[0], cache_t.shape[2]
    col = lambda part: pl.BlockSpec((t, GROUP_W), lambda i: (i, part))
    kern = functools.partial(_attn_sample_kernel, n_buf=n_buf, t=t)
    return pl.pallas_call(
        kern,
        grid=(nb,),
        in_specs=[col(0), col(1), col(2),
                  pl.BlockSpec((1, 2 * GROUP_W, n_buf), lambda i: (i, 0, 0)),
                  pl.BlockSpec((HEADS_PER_GROUP * t, n_buf + BAND), lambda i: (0, 0))],
        out_specs=[pl.BlockSpec((t, GROUP_W), lambda i: (i, 0)),
                   pl.BlockSpec((t, GROUP_W), lambda i: (i, 0))],
        out_shape=[jax.ShapeDtypeStruct((nb * t, GROUP_W), F32),
                   jax.ShapeDtypeStruct((nb * t, GROUP_W), F32)],
        compiler_params=_params("arbitrary"),
        name=f"attn_sample_g{g}",
    )(qkv, qkv, qkv, cache_t, _sample_bias(g, n_buf, t))


CONV_HALO = 32


def _conv_tail(acc, b_ref, g_ref, beta_ref):
    return _silu(_layer_norm(acc + b_ref[...], g_ref[...], beta_ref[...]))


def _conv_prompt_kernel(halo_ref, u_ref, w_ref, b_ref, g_ref, beta_ref, o_ref, hist, *, tm):
    i = pl.program_id(0)
    hist[0:CONV_HALO, :] = jnp.where(i == 0, 0.0, halo_ref[...])
    hist[CONV_HALO:, :] = u_ref[...]
    off = CONV_HALO - (CONV_WIDTH - 1)
    acc = jnp.zeros((tm, CONV_CH), F32)
    for j in range(CONV_WIDTH):
        acc = acc + w_ref[j:j + 1, :] * hist[off + j:off + j + tm, :]
    o_ref[...] = _conv_tail(acc, b_ref, g_ref, beta_ref).astype(o_ref.dtype)


def _conv_prompt(u, w_dw, b_dw, ln_g, ln_b, tm):
    n = u.shape[0]
    vec = pl.BlockSpec((1, CONV_CH), lambda i: (0, 0))
    return pl.pallas_call(
        functools.partial(_conv_prompt_kernel, tm=tm),
        grid=(n // tm,),
        in_specs=[pl.BlockSpec((CONV_HALO, CONV_CH), lambda i: (jnp.maximum(i * (tm // CONV_HALO) - 1, 0), 0)),
                  pl.BlockSpec((tm, CONV_CH), lambda i: (i, 0)),
                  pl.BlockSpec((CONV_WIDTH, CONV_CH), lambda i: (0, 0)), vec, vec, vec],
        out_specs=pl.BlockSpec((tm, CONV_CH), lambda i: (i, 0)),
        out_shape=jax.ShapeDtypeStruct((n, CONV_CH), BF16),
        scratch_shapes=[pltpu.VMEM((CONV_HALO + tm, CONV_CH), F32)],
        compiler_params=_params("arbitrary"),
        name="conv_prompt",
    )(u, u, w_dw, b_dw, ln_g, ln_b)


def _conv_sample_kernel(hist_ref, w_ref, b_ref, g_ref, beta_ref, o_ref, *, t):
    acc = jnp.zeros((t, CONV_CH), F32)
    for j in range(CONV_WIDTH):
        acc = acc + w_ref[j:j + 1, :] * hist_ref[0, j:j + t, :]
    o_ref[...] = _conv_tail(acc, b_ref, g_ref, beta_ref)


def _conv_sample(u_hist, w_dw, b_dw, ln_g, ln_b, t):
    nb, rows = u_hist.shape[0], u_hist.shape[1]
    vec = pl.BlockSpec((1, CONV_CH), lambda i: (0, 0))
    return pl.pallas_call(
        functools.partial(_conv_sample_kernel, t=t),
        grid=(nb,),
        in_specs=[pl.BlockSpec((1, rows, CONV_CH), lambda i: (i, 0, 0)),
                  pl.BlockSpec((CONV_WIDTH, CONV_CH), lambda i: (0, 0)), vec, vec, vec],
        out_specs=pl.BlockSpec((t, CONV_CH), lambda i: (i, 0)),
        out_shape=jax.ShapeDtypeStruct((nb * t, CONV_CH), F32),
        compiler_params=_params("arbitrary"),
        name="conv_sample",
    )(u_hist, w_dw, b_dw, ln_g, ln_b)


U32 = jnp.uint32
HALF_D = D_MODEL // 2
PACK_SLABS = HALF_D // LANES


def _token_order(ref, dil, scr, tm):
    if dil == 1:
        return ref[...].astype(F32)
    for r in range(dil):
        for half in range(GROUP_W // LANES):
            c = r * GROUP_W + half * LANES
            scr[half, pl.ds(r, tm // dil, stride=dil), :] = ref[:, c:c + LANES].astype(F32)
    return jnp.concatenate([scr[half] for half in range(GROUP_W // LANES)], axis=1)


def _pack_rows(ref, val):
    rows = val.shape[0]
    bits = lax.bitcast_convert_type(val.astype(BF16).astype(F32), U32)
    packed = lax.shift_right_logical(bits[:, :HALF_D], jnp.uint32(16)) | bits[:, HALF_D:]
    for c in range(PACK_SLABS):
        ref[pl.ds(c, rows, stride=PACK_SLABS), :] = packed[:, c * LANES:(c + 1) * LANES]


def _unpack_rows(ref):
    rows = ref.shape[0] // PACK_SLABS
    u = jnp.concatenate([ref[pl.ds(c, rows, stride=PACK_SLABS), :] for c in range(PACK_SLABS)], axis=1)
    lo = lax.bitcast_convert_type(lax.shift_left(u, jnp.uint32(16)), F32)
    hi = lax.bitcast_convert_type(u & jnp.uint32(0xFFFF0000), F32)
    return lo, hi


def _outproj_kernel(o0, o1, o2, l0, l1, l2, c_ref, x_ref, w_ref, b_ref, g_ref, beta_ref, h_ref, hp_ref, *scr,
                    dils, tm):
    os_ = [_token_order(r, d, scr[2 * g], tm) for g, (r, d) in enumerate(zip((o0, o1, o2), dils))]
    ls = [_token_order(r, d, scr[2 * g + 1], tm) for g, (r, d) in enumerate(zip((l0, l1, l2), dils))]
    m = jnp.maximum(jnp.maximum(ls[0], ls[1]), ls[2])
    es = [jnp.exp(l - m) for l in ls]
    inv = 1.0 / (es[0] + es[1] + es[2])
    mixed = b_ref[...]
    for g in range(N_GROUPS):
        a = (os_[g] * (es[g] * inv)).astype(BF16)
        mixed = mixed + _dot(a, w_ref[g * GROUP_W:(g + 1) * GROUP_W, :])
    mixed = mixed + _dot(c_ref[...].astype(BF16), w_ref[ATTN_W:, :])
    h = _layer_norm(DN_ALPHA * x_ref[...] + mixed, g_ref[...], beta_ref[...])
    h_ref[...] = h
    _pack_rows(hp_ref, h)


def _outproj(os_, lses, conv, x, w_o, b_o, ln_g, ln_b, tm, dils):
    n = x.shape[0]
    grp = [pl.BlockSpec((tm // d, d * GROUP_W), lambda i: (i, 0)) for d in dils]
    vec = pl.BlockSpec((1, D_MODEL), lambda i: (0, 0))
    return pl.pallas_call(
        functools.partial(_outproj_kernel, dils=dils, tm=tm),
        grid=(n // tm,),
        in_specs=grp + grp + [pl.BlockSpec((tm, CONV_CH), lambda i: (i, 0)),
                              pl.BlockSpec((tm, D_MODEL), lambda i: (i, 0)),
                              pl.BlockSpec((D_MODEL, D_MODEL), lambda i: (0, 0)), vec, vec, vec],
        out_specs=[pl.BlockSpec((tm, D_MODEL), lambda i: (i, 0)),
                   pl.BlockSpec((tm * PACK_SLABS, LANES), lambda i: (i, 0))],
        out_shape=[jax.ShapeDtypeStruct((n, D_MODEL), F32),
                   jax.ShapeDtypeStruct((n * PACK_SLABS, LANES), U32)],
        scratch_shapes=[pltpu.VMEM((GROUP_W // LANES, tm, LANES), F32) for _ in range(2 * N_GROUPS)],
        compiler_params=_params("arbitrary"),
        name="outproj",
    )(*os_, *lses, conv, x, w_o, b_o, ln_g, ln_b)


ROUTER_TM = 256


def _first_index(hit, idx, limit, axis):
    return jnp.min(jnp.where(hit, idx, limit), axis=axis, keepdims=True)


def _router_kernel(ha_ref, hb_ref, w_ref, rb_ref, tri_ref, idx_ref, wt_ref, pos_ref, cnt_ref, run, *, tiles_a):
    i = pl.program_id(0)
    tm = ROUTER_TM

    @pl.when(i == 0)
    def _():
        run[...] = jnp.zeros_like(run)

    h = jnp.where(i < tiles_a, ha_ref[...], hb_ref[...])
    logits = _dot_nt(w_ref[...], h.astype(BF16))
    scores = jax.nn.sigmoid(logits)
    biased = scores + rb_ref[...]
    groups = [biased[g * EXPERTS_PER_GROUP:(g + 1) * EXPERTS_PER_GROUP] for g in range(N_EXPERT_GROUPS)]
    ei = lax.broadcasted_iota(I32, (EXPERTS_PER_GROUP, tm), 0).astype(F32)
    gs = []
    for bg in groups:
        m1 = jnp.max(bg, axis=0, keepdims=True)
        f1 = _first_index(bg == m1, ei, float(EXPERTS_PER_GROUP), 0)
        m2 = jnp.max(jnp.where(ei == f1, -jnp.inf, bg), axis=0, keepdims=True)
        gs.append(m1 + m2)
    gs = jnp.concatenate(gs, axis=0)
    gi = lax.broadcasted_iota(I32, gs.shape, 0).astype(F32)
    keep = jnp.zeros(gs.shape, F32)
    cur = gs
    for _ in range(TOPK_GROUPS):
        m = jnp.max(cur, axis=0, keepdims=True)
        f = _first_index(cur == m, gi, float(N_EXPERT_GROUPS), 0)
        hit = gi == f
        keep = jnp.where(hit, 1.0, keep)
        cur = jnp.where(hit, -jnp.inf, cur)
    masked = jnp.concatenate([jnp.where(keep[g:g + 1] > 0.0, bg, -jnp.inf) for g, bg in enumerate(groups)], axis=0)
    xi = lax.broadcasted_iota(I32, (N_EXPERTS, tm), 0).astype(F32)
    cur = masked
    sel = jnp.zeros((N_EXPERTS, tm), F32)
    picks = []
    for _ in range(TOP_K):
        m = jnp.max(cur, axis=0, keepdims=True)
        f = _first_index(cur == m, xi, float(N_EXPERTS), 0)
        hit = xi == f
        picks.append((f, hit))
        sel = jnp.where(hit, 1.0, sel)
        cur = jnp.where(hit, -jnp.inf, cur)
    before = _dot(sel.astype(BF16), tri_ref[...]) + run[...]
    run[...] = run[...] + jnp.sum(sel, axis=1, keepdims=True)
    ws = [jnp.sum(jnp.where(hit, scores, 0.0), axis=0, keepdims=True) for _, hit in picks]
    wsum = ws[0]
    for w in ws[1:]:
        wsum = wsum + w
    for k, (f, hit) in enumerate(picks):
        idx_ref[k:k + 1, :] = f.astype(I32)
        wt_ref[k:k + 1, :] = ws[k] / wsum * ROUTED_SCALE
        pos_ref[k:k + 1, :] = jnp.sum(jnp.where(hit, before, 0.0), axis=0, keepdims=True).astype(I32)
    cnt_ref[...] = jnp.broadcast_to(run[...], cnt_ref.shape).astype(I32)


def _route(ha, hb, w_router_t, router_bias):
    tm = ROUTER_TM
    tiles_a, tiles_b = ha.shape[0] // tm, hb.shape[0] // tm
    spec_a = pl.BlockSpec((tm, D_MODEL), lambda i: (jnp.minimum(i, tiles_a - 1), 0))
    spec_b = pl.BlockSpec((tm, D_MODEL), lambda i: (jnp.maximum(i - tiles_a, 0), 0))
    t = (tiles_a + tiles_b) * tm
    tri = (jnp.arange(tm)[:, None] < jnp.arange(tm)[None, :]).astype(BF16)
    slot = pl.BlockSpec((TOP_K, tm), lambda i: (0, i))
    idx, wts, pos, cnt = pl.pallas_call(
        functools.partial(_router_kernel, tiles_a=tiles_a),
        grid=(t // tm,),
        in_specs=[spec_a, spec_b,
                  pl.BlockSpec((N_EXPERTS, D_MODEL), lambda i: (0, 0)),
                  pl.BlockSpec((N_EXPERTS, 1), lambda i: (0, 0)),
                  pl.BlockSpec((tm, tm), lambda i: (0, 0))],
        out_specs=[slot, slot, slot, pl.BlockSpec((N_EXPERTS, 128), lambda i: (0, 0))],
        out_shape=[jax.ShapeDtypeStruct((TOP_K, t), I32), jax.ShapeDtypeStruct((TOP_K, t), F32),
                   jax.ShapeDtypeStruct((TOP_K, t), I32), jax.ShapeDtypeStruct((N_EXPERTS, 128), I32)],
        scratch_shapes=[pltpu.VMEM((N_EXPERTS, 1), F32)],
        compiler_params=_params("arbitrary"),
        name="router",
    )(ha, hb, w_router_t, router_bias.reshape(N_EXPERTS, 1), tri)
    return idx, wts, pos, cnt[:, 0]


DEST_TM = 1280


def _dest_kernel(idx_ref, pos_ref, starts_ref, dest_ref):
    tm = idx_ref.shape[1]
    ei = lax.broadcasted_iota(I32, (N_EXPERTS, tm), 0)
    starts = starts_ref[...]
    for k in range(TOP_K):
        first = jnp.sum(jnp.where(ei == idx_ref[k:k + 1, :], starts, 0.0), axis=0, keepdims=True)
        dest_ref[k:k + 1, :] = first.astype(I32) + pos_ref[k:k + 1, :]


def _dest_rows(idx, pos, starts):
    t = idx.shape[1]
    tm = DEST_TM if t % DEST_TM == 0 else ROUTER_TM
    assert t % tm == 0
    slot = pl.BlockSpec((TOP_K, tm), lambda i: (0, i))
    return pl.pallas_call(
        _dest_kernel,
        grid=(t // tm,),
        in_specs=[slot, slot, pl.BlockSpec((N_EXPERTS, 1), lambda i: (0, 0))],
        out_specs=slot,
        out_shape=jax.ShapeDtypeStruct((TOP_K, t), I32),
        compiler_params=_params("arbitrary"),
        name="dest_rows",
    )(idx, pos, starts.astype(F32).reshape(N_EXPERTS, 1))


SC_CORES = 2
SC_SUBCORES = 16
SC_WORKERS = SC_CORES * SC_SUBCORES
SC_WINDOW = 32


def _sc_worker():
    return lax.axis_index("s") * SC_CORES + lax.axis_index("c")


def _sc_mesh():
    return plsc.VectorSubcoreMesh(core_axis_name="c", subcore_axis_name="s")


def _sc_scatter_rows(ha, hb, dest_w, zeros, n_out):
    w = SC_WINDOW
    na, nb = ha.shape[0], hb.shape[0]
    spare = zeros.shape[0]
    wa, wb = na // w, nb // w
    assert na % (SC_WORKERS * 2 * w) == 0 and nb % w == 0 and wb <= SC_WORKERS
    per_w = wa // SC_WORKERS
    rows_t = pltpu.VMEM((w,) + ha.shape[1:], ha.dtype)
    idx_t = pltpu.VMEM((TOP_K, w), I32)

    @functools.partial(
        pl.kernel, mesh=_sc_mesh(),
        out_type=jax.ShapeDtypeStruct((n_out + spare,) + ha.shape[1:], ha.dtype),
        scratch_types=[rows_t, rows_t, idx_t, idx_t, pltpu.SemaphoreType.DMA, pltpu.SemaphoreType.DMA],
    )
    def scatter(ha_hbm, hb_hbm, dest_hbm, zeros_hbm, out_hbm, rows0, rows1, idx0, idx1, sem0, sem1):
        wid = _sc_worker()
        bufs = ((rows0, idx0, sem0), (rows1, idx1, sem1))

        @pl.when(wid == SC_WORKERS - 1)
        def _():
            pltpu.sync_copy(zeros_hbm, out_hbm.at[pl.ds(n_out, spare)])

        def load(src_hbm, row0, win, b):
            pltpu.sync_copy(src_hbm.at[pl.ds(pl.multiple_of(row0, 8), w)], bufs[b][0])
            pltpu.sync_copy(dest_hbm.at[win], bufs[b][1])

        def copies(b):
            return [pltpu.make_async_copy(bufs[b][0], out_hbm.at[bufs[b][1].at[k]], bufs[b][2])
                    for k in range(TOP_K)]

        win0 = wid * per_w
        load(ha_hbm, win0 * w, win0, 0)

        @pl.loop(0, per_w, step=2)
        def _(i0):
            for b in range(2):
                i = i0 + b
                for c in copies(b):
                    c.start()

                @pl.when(i + 1 < per_w)
                def _():
                    load(ha_hbm, (win0 + i + 1) * w, win0 + i + 1, 1 - b)

                for c in copies(b):
                    c.wait()

        @pl.when(wid < wb)
        def _():
            load(hb_hbm, wid * w, wa + wid, 0)
            for c in copies(0):
                c.start()
            for c in copies(0):
                c.wait()

    return scatter(ha, hb, dest_w, zeros)


def _sc_gather_rows(table, idx):
    n = idx.shape[0]
    w = SC_WINDOW
    assert n % (SC_WORKERS * 2 * w) == 0
    per_w = n // SC_WORKERS
    nwin = per_w // w
    rows_t = pltpu.VMEM((w,) + table.shape[1:], table.dtype)

    @functools.partial(
        pl.kernel, mesh=_sc_mesh(),
        out_type=jax.ShapeDtypeStruct((n,) + table.shape[1:], table.dtype),
        scratch_types=[pltpu.VMEM((per_w,), I32), rows_t, rows_t, pltpu.SemaphoreType.DMA, pltpu.SemaphoreType.DMA],
    )
    def gather(table_hbm, idx_hbm, out_hbm, idx_v, rows0, rows1, sem0, sem1):
        base = pl.multiple_of(_sc_worker() * per_w, 8)
        pltpu.sync_copy(idx_hbm.at[pl.ds(base, per_w)], idx_v)
        bufs = ((rows0, sem0), (rows1, sem1))
        fetch = lambda i, b: pltpu.make_async_copy(
            table_hbm.at[idx_v.at[pl.ds(pl.multiple_of(i * w, 8), w)]], bufs[b][0], bufs[b][1])
        fetch(0, 0).start()

        @pl.loop(0, nwin, step=2)
        def _(i0):
            for b in range(2):
                i = i0 + b
                fetch(i, b).wait()

                @pl.when(i + 1 < nwin)
                def _():
                    fetch(i + 1, 1 - b).start()

                pltpu.sync_copy(bufs[b][0], out_hbm.at[pl.ds(pl.multiple_of(base + i * w, 8), w)])

    return gather(table, idx)


EXPERT_CH = 256


def _expert_kernel(row0_ref, exp_ref, live_ref, fresh_ref, next_ref, par_ref, xs_ref, wg_ref, wu_ref, wd_ref, ys_ref,
                   xbuf, ybuf, sg, su, sd, wgb, wub, wdb, xsem, ysem, wsem):
    v = pl.program_id(0)
    nv = pl.num_programs(0)
    slot = v % 2
    rows = lambda u: pl.ds(pl.multiple_of(row0_ref[u] * PACK_SLABS, PACK_SLABS), EXPERT_CH * PACK_SLABS)
    x_copy = lambda u, s: pltpu.make_async_copy(xs_ref.at[rows(u)], xbuf.at[s], xsem.at[s])
    y_copy = lambda u, s: pltpu.make_async_copy(ybuf.at[s], ys_ref.at[rows(u)], ysem.at[s])
    w_copies = lambda e, s: [pltpu.make_async_copy(w.at[e], stage.at[s], wsem.at[s])
                             for w, stage in ((wg_ref, sg), (wu_ref, su), (wd_ref, sd))]
    spare = pl.ds(ys_ref.shape[0] - EXPERT_CH * PACK_SLABS, EXPERT_CH * PACK_SLABS)

    @pl.when(v == 0)
    def _():
        for c in w_copies(exp_ref[0], 0):
            c.start()
        x_copy(0, 0).start()
        ybuf[0] = jnp.zeros(ybuf.shape[1:], U32)
        zero = pltpu.make_async_copy(ybuf.at[0], ys_ref.at[spare], ysem.at[0])
        zero.start()
        zero.wait()

    nxt = jnp.minimum(v + 1, nv - 1)

    @pl.when((v + 1 < nv) & (live_ref[nxt] == 1))
    def _():
        x_copy(nxt, 1 - slot).start()

    @pl.when(fresh_ref[v] == 1)
    def _():
        p = par_ref[v]
        for c in w_copies(exp_ref[v], p):
            c.wait()

        @pl.when(next_ref[v] >= 0)
        def _():
            for c in w_copies(next_ref[v], 1 - p):
                c.start()

        wgb[...] = sg[p].astype(BF16)
        wub[...] = su[p].astype(BF16)
        wdb[...] = sd[p].astype(BF16)

    @pl.when(live_ref[v] == 1)
    def _():
        x_copy(v, slot).wait()
        x = jnp.concatenate(_unpack_rows(xbuf.at[slot]), axis=1).astype(BF16)
        hid = (_silu(_dot(x, wgb[...])) * _dot(x, wub[...])).astype(BF16)
        _pack_rows(ybuf.at[slot], _dot(hid, wdb[...]))

    @pl.when((v > 0) & (live_ref[jnp.maximum(v - 1, 0)] == 1))
    def _():
        y_copy(jnp.maximum(v - 1, 0), 1 - slot).wait()

    @pl.when(live_ref[v] == 1)
    def _():
        y_copy(v, slot).start()

        @pl.when(v == nv - 1)
        def _():
            y_copy(v, slot).wait()


def _chunks(counts, n_rows):
    ch = EXPERT_CH
    nv = n_rows // ch + N_EXPERTS
    ends = jnp.cumsum(counts)
    starts = ends - counts
    nch = (counts + ch - 1) // ch
    cend = jnp.cumsum(nch)
    cstart = cend - nch
    v = jnp.arange(nv, dtype=I32)
    live = v < cend[-1]
    vc = jnp.minimum(v, cend[-1] - 1)
    ids = jnp.arange(N_EXPERTS, dtype=I32)
    e = jnp.minimum(jnp.sum(cend[None, :] <= vc[:, None], axis=1), N_EXPERTS - 1).astype(I32)
    of_e = lambda a: jnp.sum(jnp.where(e[:, None] == ids[None, :], a[None, :], 0), axis=1)
    row0 = (of_e(starts) + (vc - of_e(cstart)) * ch).astype(I32)
    fresh = jnp.concatenate([jnp.ones((1,), I32), (e[1:] != e[:-1]).astype(I32)])
    order = jnp.cumsum(counts > 0) - 1
    order_e = of_e(order)
    follows = (counts > 0)[None, :] & (order[None, :] == order_e[:, None] + 1)
    nxt = jnp.sum(jnp.where(follows, ids[None, :] + 1, 0), axis=1) - 1
    return (starts.astype(I32), row0, e, live.astype(I32), jnp.where(live, fresh, 0).astype(I32),
            nxt.astype(I32), (order_e % 2).astype(I32))


def _experts(chunks, xs, w_gate, w_up, w_down):
    ch = EXPERT_CH
    buf = pltpu.VMEM((2, ch * PACK_SLABS, LANES), U32)
    anywhere = pl.BlockSpec(memory_space=pl.ANY)
    return pl.pallas_call(
        _expert_kernel,
        grid_spec=pltpu.PrefetchScalarGridSpec(
            num_scalar_prefetch=len(chunks),
            grid=(chunks[0].shape[0],),
            in_specs=[anywhere] * 4,
            out_specs=anywhere,
            scratch_shapes=[buf, buf,
                            pltpu.VMEM((2, D_MODEL, EXPERT_FF), F32), pltpu.VMEM((2, D_MODEL, EXPERT_FF), F32),
                            pltpu.VMEM((2, EXPERT_FF, D_MODEL), F32),
                            pltpu.VMEM((D_MODEL, EXPERT_FF), BF16), pltpu.VMEM((D_MODEL, EXPERT_FF), BF16),
                            pltpu.VMEM((EXPERT_FF, D_MODEL), BF16),
                            pltpu.SemaphoreType.DMA((2,)), pltpu.SemaphoreType.DMA((2,)),
                            pltpu.SemaphoreType.DMA((2,))],
        ),
        out_shape=jax.ShapeDtypeStruct(xs.shape, U32),
        compiler_params=_params("arbitrary"),
        name="experts",
    )(*chunks, xs, w_gate, w_up, w_down)


COMBINE_TM = 128


def _combine_kernel(wt_ref, h_ref, sg_ref, su_ref, sd_ref, g_ref, beta_ref, *refs):
    y_refs, o_ref = refs[:TOP_K], refs[TOP_K]
    h = h_ref[...]
    hb = h.astype(BF16)
    hid = (_silu(_dot(hb, sg_ref[...])) * _dot(hb, su_ref[...])).astype(BF16)
    acc = DN_ALPHA * h + _dot(hid, sd_ref[...])
    lo = acc[:, :HALF_D]
    hi = acc[:, HALF_D:]
    for k in range(TOP_K):
        y_lo, y_hi = _unpack_rows(y_refs[k])
        w = wt_ref[:, k:k + 1]
        lo = lo + w * y_lo
        hi = hi + w * y_hi
    o_ref[...] = _layer_norm(jnp.concatenate([lo, hi], axis=1), g_ref[...], beta_ref[...])


def _combine(wts_tk, yg, ws_gate, ws_up, ws_down, ln_g, ln_b, h, t0):
    tm = COMBINE_TM
    n = h.shape[0]
    b0 = t0 // tm
    per_slot = yg.shape[0] // (TOP_K * tm * PACK_SLABS)
    rows = pl.BlockSpec((tm, D_MODEL), lambda i: (i, 0))
    slot_rows = [pl.BlockSpec((tm * PACK_SLABS, LANES), lambda i, k=k: (k * per_slot + b0 + i, 0))
                 for k in range(TOP_K)]
    vec = pl.BlockSpec((1, D_MODEL), lambda i: (0, 0))
    return pl.pallas_call(
        _combine_kernel,
        grid=(n // tm,),
        in_specs=[pl.BlockSpec((tm, TOP_K), lambda i: (b0 + i, 0)), rows,
                  pl.BlockSpec((D_MODEL, EXPERT_FF), lambda i: (0, 0)),
                  pl.BlockSpec((D_MODEL, EXPERT_FF), lambda i: (0, 0)),
                  pl.BlockSpec((EXPERT_FF, D_MODEL), lambda i: (0, 0)),
                  vec, vec] + slot_rows,
        out_specs=pl.BlockSpec((tm, D_MODEL), lambda i: (i, 0)),
        out_shape=jax.ShapeDtypeStruct((n, D_MODEL), F32),
        compiler_params=_params("arbitrary"),
        name="combine",
    )(wts_tk, h, ws_gate, ws_up, ws_down, ln_g, ln_b, *([yg] * TOP_K))


def _kv_cache(kv_tail, g, keep):
    k = kv_tail[-keep:, g * GROUP_W:(g + 1) * GROUP_W]
    v = kv_tail[-keep:, ATTN_W + g * GROUP_W:ATTN_W + (g + 1) * GROUP_W]
    return jnp.stack([k, v], axis=1).reshape(keep, 2, HEADS_PER_GROUP, HEAD_DIM)


def kernel(x_prompt, x_sample, cache_kv_w128, cache_kv_w512, cache_kv_w2048, state_conv, w_in, b_in, w_dw, b_dw,
           conv_ln_g, conv_ln_b, w_o, b_o, ln1_g, ln1_b, w_router, router_bias, w_gate, w_up, w_down, ws_gate,
           ws_up, ws_down, ln2_g, ln2_b):
    assert w_in.shape[0] == DEPTH == 1
    batch, seq, _ = x_prompt.shape
    dec_batch, dec_seq, _ = x_sample.shape
    assert batch == 1
    n_s = dec_batch * dec_seq
    caches = (cache_kv_w128, cache_kv_w512, cache_kv_w2048)
    row = lambda a: a[0].reshape(1, -1)

    w_in_b = w_in[0].astype(BF16)
    b_in_r = row(b_in)
    w_o_b = w_o[0].astype(BF16)
    conv_w = (w_dw[0], row(b_dw), row(conv_ln_g), row(conv_ln_b))
    ln1 = (row(ln1_g), row(ln1_b))

    xp = x_prompt[0]
    keep_p = min(max(w for w, _ in DIL_GROUPS), seq)
    dils = tuple(d for _, d in DIL_GROUPS)
    *qkv_p, u_p, kv_p = _project(xp, w_in_b, b_in_r, 512, keep_p, BF16, dils)
    attn_p = [_attention_prompt(qkv_p[g], g) for g in range(N_GROUPS)]
    conv_p = _conv_prompt(u_p, *conv_w, 512)
    h_p, hpk_p = _outproj([a[0] for a in attn_p], [a[1] for a in attn_p], conv_p, xp, w_o_b, row(b_o), *ln1, 512,
                          dils)

    xs = x_sample.reshape(n_s, D_MODEL)
    ones = (1,) * N_GROUPS
    *qkv_s, u_s, kv_s = _project(xs, w_in_b, b_in_r, n_s, n_s, F32, ones)
    caches_t = [jnp.transpose(c[0].reshape(dec_batch, -1, 2 * GROUP_W), (0, 2, 1)) for c in caches]
    attn_s = [_attention_sample(qkv_s[g], caches_t[g], g, dec_seq) for g in range(N_GROUPS)]
    u_hist = jnp.concatenate([state_conv[0], u_s.reshape(dec_batch, dec_seq, CONV_CH)], axis=1)
    conv_s = _conv_sample(u_hist, *conv_w, dec_seq)
    h_s, hpk_s = _outproj([a[0] for a in attn_s], [a[1] for a in attn_s], conv_s, xs, w_o_b, row(b_o), *ln1, n_s,
                          ones)

    idx, wts, pos, counts = _route(h_p, h_s, w_router[0].T.astype(BF16), router_bias[0])
    n_tok = seq + n_s
    starts, *chunks = _chunks(counts, n_tok * TOP_K)
    dest = _dest_rows(idx, pos, starts)
    dest_w = dest.reshape(TOP_K, n_tok // SC_WINDOW, SC_WINDOW).transpose(1, 0, 2)
    tiles = lambda a: a.reshape(-1, PACK_SLABS, LANES)
    flat = lambda a: a.reshape(-1, LANES)
    x_sorted = _sc_scatter_rows(tiles(hpk_p), tiles(hpk_s), dest_w, jnp.zeros((EXPERT_CH, PACK_SLABS, LANES), U32),
                                n_tok * TOP_K)
    y_sorted = _experts(chunks, flat(x_sorted), w_gate[0], w_up[0], w_down[0])
    y_slots = _sc_gather_rows(tiles(y_sorted), dest.reshape(-1))
    shared = (ws_gate[0].astype(BF16), ws_up[0].astype(BF16), ws_down[0].astype(BF16))
    comb = functools.partial(_combine, wts.T, flat(y_slots), *shared, row(ln2_g), row(ln2_b))
    y_p = comb(h_p, 0)
    y_s = comb(h_s, seq)

    kv_prompt = [_kv_cache(kv_p, g, min(w, seq))[None, None] for g, (w, _) in enumerate(DIL_GROUPS)]
    assert seq >= CONV_WIDTH - 1
    conv_prompt = u_p[-(CONV_WIDTH - 1):]
    kv_s4 = kv_s.reshape(dec_batch, dec_seq, 2, N_GROUPS, HEADS_PER_GROUP, HEAD_DIM)
    kv_sample = [kv_s4[:, :, :, g][None] for g in range(N_GROUPS)]
    conv_sample = u_hist[:, -(CONV_WIDTH - 1):]
    return (y_p[None], y_s.reshape(dec_batch, dec_seq, D_MODEL), *kv_prompt, conv_prompt[None, None],
            *kv_sample, conv_sample[None])
```

```python
import functools

import jax
import jax.numpy as jnp
from jax import lax
from jax.experimental import pallas as pl
from jax.experimental.pallas import tpu as pltpu
from jax.experimental.pallas import tpu_sc as plsc

F32 = jnp.float32
BF16 = jnp.bfloat16
I32 = jnp.int32

D_MODEL = 1024
HEAD_DIM = 64
HEADS_PER_GROUP = 4
GROUP_W = HEADS_PER_GROUP * HEAD_DIM
DIL_GROUPS = ((128, 1), (512, 4), (2048, 16))
N_GROUPS = len(DIL_GROUPS)
ATTN_W = N_GROUPS * GROUP_W
CONV_CH = D_MODEL - ATTN_W
CONV_WIDTH = 31
IN_W = 3 * ATTN_W + 2 * CONV_CH
BAND = 128
N_EXPERTS = 256
TOP_K = 8
N_EXPERT_GROUPS = 8
EXPERTS_PER_GROUP = N_EXPERTS // N_EXPERT_GROUPS
TOPK_GROUPS = 4
EXPERT_FF = 256
ROUTED_SCALE = 2.5
DEPTH = 1
DN_ALPHA = (2 * DEPTH) ** 0.25
LN_EPS = 1e-5
MASKED = -1e30

VMEM_LIMIT_BYTES = 56 * 1024 * 1024


def _params(*sem):
    return pltpu.CompilerParams(dimension_semantics=sem, vmem_limit_bytes=VMEM_LIMIT_BYTES)


def _dot(a, b):
    return jnp.dot(a, b, preferred_element_type=F32)


def _dot_nt(a, b):
    return lax.dot_general(a, b, (((1,), (1,)), ((), ())), preferred_element_type=F32)


def _layer_norm(x, g, b):
    mu = jnp.mean(x, axis=-1, keepdims=True)
    xc = x - mu
    var = jnp.mean(xc * xc, axis=-1, keepdims=True)
    return xc * lax.rsqrt(var + LN_EPS) * g + b


def _silu(x):
    return x * jax.nn.sigmoid(x)


def _alibi_slopes():
    n = N_GROUPS * HEADS_PER_GROUP
    h = jnp.arange(1, n + 1, dtype=F32)
    return (2.0 ** (-8.0 * h / n)).reshape(N_GROUPS, HEADS_PER_GROUP)


LANES = 128


def _proj_kernel(x_ref, w_ref, b_ref, *refs, dils, tm):
    qkv_refs, (u_ref, kv_ref, zs) = refs[:N_GROUPS], refs[N_GROUPS:]
    x = x_ref[...].astype(BF16)
    for part in range(3):
        for g in range(N_GROUPS):
            c0 = part * ATTN_W + g * GROUP_W
            z = _dot(x, w_ref[:, c0:c0 + GROUP_W]) + b_ref[:, c0:c0 + GROUP_W]
            if part > 0:
                kv_ref[:, c0 - ATTN_W:c0 - ATTN_W + GROUP_W] = z
            out, dil = qkv_refs[g], dils[g]
            if dil == 1:
                out[:, part * GROUP_W:(part + 1) * GROUP_W] = z.astype(out.dtype)
                continue
            for half in range(GROUP_W // LANES):
                zs[half] = z[:, half * LANES:(half + 1) * LANES]
            for r in range(dil):
                for half in range(GROUP_W // LANES):
                    c = r * 3 * GROUP_W + part * GROUP_W + half * LANES
                    out[:, c:c + LANES] = zs[half, pl.ds(r, tm // dil, stride=dil), :].astype(out.dtype)
    c0 = 3 * ATTN_W
    a = _dot(x, w_ref[:, c0:c0 + CONV_CH]) + b_ref[:, c0:c0 + CONV_CH]
    gate = _dot(x, w_ref[:, c0 + CONV_CH:]) + b_ref[:, c0 + CONV_CH:]
    u_ref[...] = a * jax.nn.sigmoid(gate)


def _project(x, w_in, b_in, tm, keep, qkv_dtype, dils):
    n = x.shape[0]
    nt = n // tm
    nk = keep // tm
    return pl.pallas_call(
        functools.partial(_proj_kernel, dils=dils, tm=tm),
        grid=(nt,),
        in_specs=[
            pl.BlockSpec((tm, D_MODEL), lambda i: (i, 0)),
            pl.BlockSpec((D_MODEL, IN_W), lambda i: (0, 0)),
            pl.BlockSpec((1, IN_W), lambda i: (0, 0)),
        ],
        out_specs=[pl.BlockSpec((tm // d, d * 3 * GROUP_W), lambda i: (i, 0)) for d in dils] + [
            pl.BlockSpec((tm, CONV_CH), lambda i: (i, 0)),
            pl.BlockSpec((tm, 2 * ATTN_W), lambda i: (jnp.maximum(i - (nt - nk), 0), 0)),
        ],
        out_shape=[jax.ShapeDtypeStruct((n // d, d * 3 * GROUP_W), qkv_dtype) for d in dils] + [
            jax.ShapeDtypeStruct((n, CONV_CH), F32),
            jax.ShapeDtypeStruct((keep, 2 * ATTN_W), F32),
        ],
        scratch_shapes=[pltpu.VMEM((GROUP_W // LANES, tm, LANES), F32)],
        compiler_params=_params("arbitrary"),
        name="proj",
    )(x, w_in, b_in)


def _head_select(parts, rows):
    col = lax.broadcasted_iota(I32, (rows, GROUP_W), 1) // HEAD_DIM
    out = jnp.broadcast_to(parts[-1], (rows, GROUP_W))
    for h in range(HEADS_PER_GROUP - 2, -1, -1):
        out = jnp.where(col == h, parts[h], out)
    return out


def _head_rows(q, rows):
    col = lax.broadcasted_iota(I32, (rows, GROUP_W), 1) // HEAD_DIM
    return jnp.concatenate([jnp.where(col == h, q, jnp.zeros_like(q)) for h in range(HEADS_PER_GROUP)], axis=0)


def _softmax_pv(s, v, rows):
    m = jnp.max(s, axis=-1, keepdims=True)
    e = jnp.exp(s - m)
    l = jnp.sum(e, axis=-1, keepdims=True)
    pv = _dot(e.astype(BF16), v) / l
    lse = m + jnp.log(l)
    o = _head_select([pv[h * rows:(h + 1) * rows] for h in range(HEADS_PER_GROUP)], rows)
    lse_x = _head_select([lse[h * rows:(h + 1) * rows] for h in range(HEADS_PER_GROUP)], rows)
    return o, lse_x


ATTN_QB = 4


def _attn_kernel(q_ref, kp_ref, kc_ref, vp_ref, vc_ref, bias_ref, o_ref, lse_ref):
    b = pl.program_id(1)
    for j in range(ATTN_QB):
        rows = slice(j * BAND, (j + 1) * BAND)
        qm = _head_rows(q_ref[rows, :], BAND)
        if j == 0:
            k = jnp.concatenate([kp_ref[...], kc_ref[rows, :]], axis=0)
            v = jnp.concatenate([vp_ref[...], vc_ref[rows, :]], axis=0)
            bias = bias_ref[jnp.minimum(b, 1)]
        else:
            k = kc_ref[(j - 1) * BAND:(j + 1) * BAND, :]
            v = vc_ref[(j - 1) * BAND:(j + 1) * BAND, :]
            bias = bias_ref[1]
        s = _dot_nt(qm, k) * HEAD_DIM ** -0.5 + bias
        o, lse_x = _softmax_pv(s, v, BAND)
        o_ref[rows, :] = o.astype(o_ref.dtype)
        lse_ref[rows, :] = lse_x


def _prompt_bias(g, dil):
    slopes = _alibi_slopes()[g]
    qi = jnp.arange(BAND)[:, None]
    kj = jnp.arange(2 * BAND)[None, :]
    steps = qi + BAND - kj
    nk = DIL_GROUPS[g][0] // dil
    valid = (steps >= 0) & (steps <= nk)
    bias = -slopes[:, None, None] * (steps * dil).astype(F32)
    inner = jnp.where(valid[None], bias, MASKED)
    first = jnp.where((valid & (kj >= BAND))[None], bias, MASKED)
    return jnp.stack([first, inner]).reshape(2, HEADS_PER_GROUP * BAND, 2 * BAND)


def _attention_prompt(view, g):
    win, dil = DIL_GROUPS[g]
    n_cls = view.shape[0]
    rows = ATTN_QB * BAND
    assert win // dil <= BAND and n_cls % rows == 0 and view.shape[1] == dil * 3 * GROUP_W
    cur = lambda part: pl.BlockSpec((rows, GROUP_W), lambda r, b: (b, r * 3 + part))
    prev = lambda part: pl.BlockSpec((BAND, GROUP_W), lambda r, b: (jnp.maximum(b * ATTN_QB - 1, 0), r * 3 + part))
    return pl.pallas_call(
        _attn_kernel,
        grid=(dil, n_cls // rows),
        in_specs=[cur(0), prev(1), cur(1), prev(2), cur(2),
                  pl.BlockSpec((2, HEADS_PER_GROUP * BAND, 2 * BAND), lambda r, b: (0, 0, 0))],
        out_specs=[pl.BlockSpec((rows, GROUP_W), lambda r, b: (b, r)),
                   pl.BlockSpec((rows, GROUP_W), lambda r, b: (b, r))],
        out_shape=[jax.ShapeDtypeStruct((n_cls, dil * GROUP_W), BF16),
                   jax.ShapeDtypeStruct((n_cls, dil * GROUP_W), F32)],
        compiler_params=_params("arbitrary", "arbitrary"),
        name=f"attn_g{g}",
    )(view, view, view, view, view, _prompt_bias(g, dil))


def _attn_sample_kernel(q_ref, k_ref, v_ref, cache_ref, bias_ref, o_ref, lse_ref, *, n_buf, t):
    pad = jnp.zeros((BAND - t, GROUP_W), F32)
    k_new = jnp.concatenate([k_ref[...], pad], axis=0).astype(BF16)
    v_new = jnp.concatenate([v_ref[...], pad], axis=0).astype(BF16)
    qm = _head_rows(q_ref[...], t).astype(BF16)
    scale = HEAD_DIM ** -0.5
    s_old = _dot(qm, cache_ref[0, 0:GROUP_W, :].astype(BF16)) * scale + bias_ref[:, 0:n_buf]
    s_new = _dot_nt(qm, k_new) * scale + bias_ref[:, n_buf:]
    m = jnp.maximum(jnp.max(s_old, axis=-1, keepdims=True), jnp.max(s_new, axis=-1, keepdims=True))
    e_old = jnp.exp(s_old - m)
    e_new = jnp.exp(s_new - m)
    l = jnp.sum(e_old, axis=-1, keepdims=True) + jnp.sum(e_new, axis=-1, keepdims=True)
    pv = _dot_nt(e_old.astype(BF16), cache_ref[0, GROUP_W:, :].astype(BF16)) + _dot(e_new.astype(BF16), v_new)
    pv = pv / l
    lse = m + jnp.log(l)
    o_ref[...] = _head_select([pv[h * t:(h + 1) * t] for h in range(HEADS_PER_GROUP)], t)
    lse_ref[...] = _head_select([lse[h * t:(h + 1) * t] for h in range(HEADS_PER_GROUP)], t)


def _sample_bias(g, n_buf, t):
    win, dil = DIL_GROUPS[g]
    slopes = _alibi_slopes()[g]
    tq = jnp.arange(t)[:, None]
    j = jnp.arange(n_buf + BAND)[None, :]
    dist = n_buf + tq - j
    valid = (dist >= 0) & (dist % dil == 0) & (dist <= win) & (j < n_buf + t)
    bias = -slopes[:, None, None] * dist.astype(F32)
    return jnp.where(valid[None], bias, MASKED).reshape(HEADS_PER_GROUP * t, n_buf + BAND)


def _attention_sample(qkv, cache_t, g, t):
    nb, n_buf = cache_t.shape[0], cache_t.shape[2]
    col = lambda part: pl.BlockSpec((t, GROUP_W), lambda i: (i, part))
    kern = functools.partial(_attn_sample_kernel, n_buf=n_buf, t=t)
    return pl.pallas_call(
        kern,
        grid=(nb,),
        in_specs=[col(0), col(1), col(2),
                  pl.BlockSpec((1, 2 * GROUP_W, n_buf), lambda i: (i, 0, 0)),
                  pl.BlockSpec((HEADS_PER_GROUP * t, n_buf + BAND), lambda i: (0, 0))],
        out_specs=[pl.BlockSpec((t, GROUP_W), lambda i: (i, 0)),
                   pl.BlockSpec((t, GROUP_W), lambda i: (i, 0))],
        out_shape=[jax.ShapeDtypeStruct((nb * t, GROUP_W), F32),
                   jax.ShapeDtypeStruct((nb * t, GROUP_W), F32)],
        compiler_params=_params("arbitrary"),
        name=f"attn_sample_g{g}",
    )(qkv, qkv, qkv, cache_t, _sample_bias(g, n_buf, t))


CONV_HALO = 32


def _conv_tail(acc, b_ref, g_ref, beta_ref):
    return _silu(_layer_norm(acc + b_ref[...], g_ref[...], beta_ref[...]))


SUBLANES = 8


def _conv_prompt_kernel(halo_ref, u_ref, w_ref, b_ref, g_ref, beta_ref, o_ref, hist, part, *, tm):
    i = pl.program_id(0)
    hist[0:CONV_HALO, :] = jnp.where(i == 0, 0.0, halo_ref[...])
    hist[CONV_HALO:CONV_HALO + tm, :] = u_ref[...]
    hist[CONV_HALO + tm:, :] = jnp.zeros((SUBLANES, CONV_CH), F32)
    off = CONV_HALO - (CONV_WIDTH - 1)
    acc = None
    for s in range(SUBLANES):
        group = None
        for m in range(s, off + CONV_WIDTH, SUBLANES):
            j = m - off
            if j < 0:
                continue
            term = w_ref[j:j + 1, :] * hist[m - s:m - s + tm + SUBLANES, :]
            group = term if group is None else group + term
        part[...] = group
        shifted = part[s:s + tm, :]
        acc = shifted if acc is None else acc + shifted
    o_ref[...] = _conv_tail(acc, b_ref, g_ref, beta_ref).astype(o_ref.dtype)


def _conv_prompt(u, w_dw, b_dw, ln_g, ln_b, tm):
    n = u.shape[0]
    vec = pl.BlockSpec((1, CONV_CH), lambda i: (0, 0))
    return pl.pallas_call(
        functools.partial(_conv_prompt_kernel, tm=tm),
        grid=(n // tm,),
        in_specs=[pl.BlockSpec((CONV_HALO, CONV_CH), lambda i: (jnp.maximum(i * (tm // CONV_HALO) - 1, 0), 0)),
                  pl.BlockSpec((tm, CONV_CH), lambda i: (i, 0)),
                  pl.BlockSpec((CONV_WIDTH, CONV_CH), lambda i: (0, 0)), vec, vec, vec],
        out_specs=pl.BlockSpec((tm, CONV_CH), lambda i: (i, 0)),
        out_shape=jax.ShapeDtypeStruct((n, CONV_CH), BF16),
        scratch_shapes=[pltpu.VMEM((CONV_HALO + tm + SUBLANES, CONV_CH), F32),
                        pltpu.VMEM((tm + SUBLANES, CONV_CH), F32)],
        compiler_params=_params("arbitrary"),
        name="conv_prompt",
    )(u, u, w_dw, b_dw, ln_g, ln_b)


def _conv_sample_kernel(hist_ref, w_ref, b_ref, g_ref, beta_ref, o_ref, *, t):
    acc = jnp.zeros((t, CONV_CH), F32)
    for j in range(CONV_WIDTH):
        acc = acc + w_ref[j:j + 1, :] * hist_ref[0, j:j + t, :]
    o_ref[...] = _conv_tail(acc, b_ref, g_ref, beta_ref)


def _conv_sample(u_hist, w_dw, b_dw, ln_g, ln_b, t):
    nb, rows = u_hist.shape[0], u_hist.shape[1]
    vec = pl.BlockSpec((1, CONV_CH), lambda i: (0, 0))
    return pl.pallas_call(
        functools.partial(_conv_sample_kernel, t=t),
        grid=(nb,),
        in_specs=[pl.BlockSpec((1, rows, CONV_CH), lambda i: (i, 0, 0)),
                  pl.BlockSpec((CONV_WIDTH, CONV_CH), lambda i: (0, 0)), vec, vec, vec],
        out_specs=pl.BlockSpec((t, CONV_CH), lambda i: (i, 0)),
        out_shape=jax.ShapeDtypeStruct((nb * t, CONV_CH), F32),
        compiler_params=_params("arbitrary"),
        name="conv_sample",
    )(u_hist, w_dw, b_dw, ln_g, ln_b)


U32 = jnp.uint32
HALF_D = D_MODEL // 2
PACK_SLABS = HALF_D // LANES


def _token_order(ref, dil, scr, tm):
    if dil == 1:
        return ref[...].astype(F32)
    for r in range(dil):
        for half in range(GROUP_W // LANES):
            c = r * GROUP_W + half * LANES
            scr[half, pl.ds(r, tm // dil, stride=dil), :] = ref[:, c:c + LANES].astype(F32)
    return jnp.concatenate([scr[half] for half in range(GROUP_W // LANES)], axis=1)


def _pack_rows(ref, val):
    rows = val.shape[0]
    bits = lax.bitcast_convert_type(val.astype(BF16).astype(F32), U32)
    packed = lax.shift_right_logical(bits[:, :HALF_D], jnp.uint32(16)) | bits[:, HALF_D:]
    for c in range(PACK_SLABS):
        ref[pl.ds(c, rows, stride=PACK_SLABS), :] = packed[:, c * LANES:(c + 1) * LANES]


def _unpack_rows(ref):
    rows = ref.shape[0] // PACK_SLABS
    u = jnp.concatenate([ref[pl.ds(c, rows, stride=PACK_SLABS), :] for c in range(PACK_SLABS)], axis=1)
    lo = lax.bitcast_convert_type(lax.shift_left(u, jnp.uint32(16)), F32)
    hi = lax.bitcast_convert_type(u & jnp.uint32(0xFFFF0000), F32)
    return lo, hi


def _outproj_kernel(o0, o1, o2, l0, l1, l2, c_ref, x_ref, w_ref, b_ref, g_ref, beta_ref, h_ref, hp_ref, *scr,
                    dils, tm):
    os_ = [_token_order(r, d, scr[2 * g], tm) for g, (r, d) in enumerate(zip((o0, o1, o2), dils))]
    ls = [_token_order(r, d, scr[2 * g + 1], tm) for g, (r, d) in enumerate(zip((l0, l1, l2), dils))]
    m = jnp.maximum(jnp.maximum(ls[0], ls[1]), ls[2])
    es = [jnp.exp(l - m) for l in ls]
    inv = 1.0 / (es[0] + es[1] + es[2])
    mixed = b_ref[...]
    for g in range(N_GROUPS):
        a = (os_[g] * (es[g] * inv)).astype(BF16)
        mixed = mixed + _dot(a, w_ref[g * GROUP_W:(g + 1) * GROUP_W, :])
    mixed = mixed + _dot(c_ref[...].astype(BF16), w_ref[ATTN_W:, :])
    h = _layer_norm(DN_ALPHA * x_ref[...] + mixed, g_ref[...], beta_ref[...])
    h_ref[...] = h
    _pack_rows(hp_ref, h)


def _outproj(os_, lses, conv, x, w_o, b_o, ln_g, ln_b, tm, dils):
    n = x.shape[0]
    grp = [pl.BlockSpec((tm // d, d * GROUP_W), lambda i: (i, 0)) for d in dils]
    vec = pl.BlockSpec((1, D_MODEL), lambda i: (0, 0))
    return pl.pallas_call(
        functools.partial(_outproj_kernel, dils=dils, tm=tm),
        grid=(n // tm,),
        in_specs=grp + grp + [pl.BlockSpec((tm, CONV_CH), lambda i: (i, 0)),
                              pl.BlockSpec((tm, D_MODEL), lambda i: (i, 0)),
                              pl.BlockSpec((D_MODEL, D_MODEL), lambda i: (0, 0)), vec, vec, vec],
        out_specs=[pl.BlockSpec((tm, D_MODEL), lambda i: (i, 0)),
                   pl.BlockSpec((tm * PACK_SLABS, LANES), lambda i: (i, 0))],
        out_shape=[jax.ShapeDtypeStruct((n, D_MODEL), F32),
                   jax.ShapeDtypeStruct((n * PACK_SLABS, LANES), U32)],
        scratch_shapes=[pltpu.VMEM((GROUP_W // LANES, tm, LANES), F32) for _ in range(2 * N_GROUPS)],
        compiler_params=_params("arbitrary"),
        name="outproj",
    )(*os_, *lses, conv, x, w_o, b_o, ln_g, ln_b)


ROUTER_TM = 256


def _first_index(hit, idx, limit, axis):
    return jnp.min(jnp.where(hit, idx, limit), axis=axis, keepdims=True)


def _router_kernel(ha_ref, hb_ref, w_ref, rb_ref, tri_ref, idx_ref, wt_ref, pos_ref, cnt_ref, run, *, tiles_a):
    i = pl.program_id(0)
    tm = ROUTER_TM

    @pl.when(i == 0)
    def _():
        run[...] = jnp.zeros_like(run)

    h = jnp.where(i < tiles_a, ha_ref[...], hb_ref[...])
    logits = _dot_nt(w_ref[...], h.astype(BF16))
    scores = jax.nn.sigmoid(logits)
    biased = scores + rb_ref[...]
    groups = [biased[g * EXPERTS_PER_GROUP:(g + 1) * EXPERTS_PER_GROUP] for g in range(N_EXPERT_GROUPS)]
    ei = lax.broadcasted_iota(I32, (EXPERTS_PER_GROUP, tm), 0).astype(F32)
    gs = []
    for bg in groups:
        m1 = jnp.max(bg, axis=0, keepdims=True)
        f1 = _first_index(bg == m1, ei, float(EXPERTS_PER_GROUP), 0)
        m2 = jnp.max(jnp.where(ei == f1, -jnp.inf, bg), axis=0, keepdims=True)
        gs.append(m1 + m2)
    gs = jnp.concatenate(gs, axis=0)
    gi = lax.broadcasted_iota(I32, gs.shape, 0).astype(F32)
    keep = jnp.zeros(gs.shape, F32)
    cur = gs
    for _ in range(TOPK_GROUPS):
        m = jnp.max(cur, axis=0, keepdims=True)
        f = _first_index(cur == m, gi, float(N_EXPERT_GROUPS), 0)
        hit = gi == f
        keep = jnp.where(hit, 1.0, keep)
        cur = jnp.where(hit, -jnp.inf, cur)
    masked = jnp.concatenate([jnp.where(keep[g:g + 1] > 0.0, bg, -jnp.inf) for g, bg in enumerate(groups)], axis=0)
    xi = lax.broadcasted_iota(I32, (N_EXPERTS, tm), 0).astype(F32)
    cur = masked
    sel = jnp.zeros((N_EXPERTS, tm), F32)
    picks = []
    for _ in range(TOP_K):
        m = jnp.max(cur, axis=0, keepdims=True)
        f = _first_index(cur == m, xi, float(N_EXPERTS), 0)
        hit = xi == f
        picks.append((f, hit))
        sel = jnp.where(hit, 1.0, sel)
        cur = jnp.where(hit, -jnp.inf, cur)
    before = _dot(sel.astype(BF16), tri_ref[...]) + run[...]
    run[...] = run[...] + jnp.sum(sel, axis=1, keepdims=True)
    ws = [jnp.sum(jnp.where(hit, scores, 0.0), axis=0, keepdims=True) for _, hit in picks]
    wsum = ws[0]
    for w in ws[1:]:
        wsum = wsum + w
    for k, (f, hit) in enumerate(picks):
        idx_ref[k:k + 1, :] = f.astype(I32)
        wt_ref[k:k + 1, :] = ws[k] / wsum * ROUTED_SCALE
        pos_ref[k:k + 1, :] = jnp.sum(jnp.where(hit, before, 0.0), axis=0, keepdims=True).astype(I32)
    cnt_ref[...] = jnp.broadcast_to(run[...], cnt_ref.shape).astype(I32)


def _route(ha, hb, w_router_t, router_bias):
    tm = ROUTER_TM
    tiles_a, tiles_b = ha.shape[0] // tm, hb.shape[0] // tm
    spec_a = pl.BlockSpec((tm, D_MODEL), lambda i: (jnp.minimum(i, tiles_a - 1), 0))
    spec_b = pl.BlockSpec((tm, D_MODEL), lambda i: (jnp.maximum(i - tiles_a, 0), 0))
    t = (tiles_a + tiles_b) * tm
    tri = (jnp.arange(tm)[:, None] < jnp.arange(tm)[None, :]).astype(BF16)
    slot = pl.BlockSpec((TOP_K, tm), lambda i: (0, i))
    idx, wts, pos, cnt = pl.pallas_call(
        functools.partial(_router_kernel, tiles_a=tiles_a),
        grid=(t // tm,),
        in_specs=[spec_a, spec_b,
                  pl.BlockSpec((N_EXPERTS, D_MODEL), lambda i: (0, 0)),
                  pl.BlockSpec((N_EXPERTS, 1), lambda i: (0, 0)),
                  pl.BlockSpec((tm, tm), lambda i: (0, 0))],
        out_specs=[slot, slot, slot, pl.BlockSpec((N_EXPERTS, 128), lambda i: (0, 0))],
        out_shape=[jax.ShapeDtypeStruct((TOP_K, t), I32), jax.ShapeDtypeStruct((TOP_K, t), F32),
                   jax.ShapeDtypeStruct((TOP_K, t), I32), jax.ShapeDtypeStruct((N_EXPERTS, 128), I32)],
        scratch_shapes=[pltpu.VMEM((N_EXPERTS, 1), F32)],
        compiler_params=_params("arbitrary"),
        name="router",
    )(ha, hb, w_router_t, router_bias.reshape(N_EXPERTS, 1), tri)
    return idx, wts, pos, cnt[:, 0]


DEST_TM = 1280


def _dest_kernel(idx_ref, pos_ref, starts_ref, dest_ref):
    tm = idx_ref.shape[1]
    ei = lax.broadcasted_iota(I32, (N_EXPERTS, tm), 0)
    starts = starts_ref[...]
    for k in range(TOP_K):
        first = jnp.sum(jnp.where(ei == idx_ref[k:k + 1, :], starts, 0.0), axis=0, keepdims=True)
        dest_ref[k:k + 1, :] = first.astype(I32) + pos_ref[k:k + 1, :]


def _dest_rows(idx, pos, starts):
    t = idx.shape[1]
    tm = DEST_TM if t % DEST_TM == 0 else ROUTER_TM
    assert t % tm == 0
    slot = pl.BlockSpec((TOP_K, tm), lambda i: (0, i))
    return pl.pallas_call(
        _dest_kernel,
        grid=(t // tm,),
        in_specs=[slot, slot, pl.BlockSpec((N_EXPERTS, 1), lambda i: (0, 0))],
        out_specs=slot,
        out_shape=jax.ShapeDtypeStruct((TOP_K, t), I32),
        compiler_params=_params("arbitrary"),
        name="dest_rows",
    )(idx, pos, starts.astype(F32).reshape(N_EXPERTS, 1))


SC_CORES = 2
SC_SUBCORES = 16
SC_WORKERS = SC_CORES * SC_SUBCORES
SC_WINDOW = 32


def _sc_worker():
    return lax.axis_index("s") * SC_CORES + lax.axis_index("c")


def _sc_mesh():
    return plsc.VectorSubcoreMesh(core_axis_name="c", subcore_axis_name="s")


def _sc_scatter_rows(ha, hb, dest_w, zeros, n_out):
    w = SC_WINDOW
    na, nb = ha.shape[0], hb.shape[0]
    spare = zeros.shape[0]
    wa, wb = na // w, nb // w
    assert na % (SC_WORKERS * 2 * w) == 0 and nb % w == 0 and wb <= SC_WORKERS
    per_w = wa // SC_WORKERS
    rows_t = pltpu.VMEM((w,) + ha.shape[1:], ha.dtype)
    idx_t = pltpu.VMEM((TOP_K, w), I32)

    @functools.partial(
        pl.kernel, mesh=_sc_mesh(),
        out_type=jax.ShapeDtypeStruct((n_out + spare,) + ha.shape[1:], ha.dtype),
        scratch_types=[rows_t, rows_t, idx_t, idx_t, pltpu.SemaphoreType.DMA, pltpu.SemaphoreType.DMA],
    )
    def scatter(ha_hbm, hb_hbm, dest_hbm, zeros_hbm, out_hbm, rows0, rows1, idx0, idx1, sem0, sem1):
        wid = _sc_worker()
        bufs = ((rows0, idx0, sem0), (rows1, idx1, sem1))

        @pl.when(wid == SC_WORKERS - 1)
        def _():
            pltpu.sync_copy(zeros_hbm, out_hbm.at[pl.ds(n_out, spare)])

        def load(src_hbm, row0, win, b):
            pltpu.sync_copy(src_hbm.at[pl.ds(pl.multiple_of(row0, 8), w)], bufs[b][0])
            pltpu.sync_copy(dest_hbm.at[win], bufs[b][1])

        def copies(b):
            return [pltpu.make_async_copy(bufs[b][0], out_hbm.at[bufs[b][1].at[k]], bufs[b][2])
                    for k in range(TOP_K)]

        win0 = wid * per_w
        load(ha_hbm, win0 * w, win0, 0)

        @pl.loop(0, per_w, step=2)
        def _(i0):
            for b in range(2):
                i = i0 + b
                for c in copies(b):
                    c.start()

                @pl.when(i + 1 < per_w)
                def _():
                    load(ha_hbm, (win0 + i + 1) * w, win0 + i + 1, 1 - b)

                for c in copies(b):
                    c.wait()

        @pl.when(wid < wb)
        def _():
            load(hb_hbm, wid * w, wa + wid, 0)
            for c in copies(0):
                c.start()
            for c in copies(0):
                c.wait()

    return scatter(ha, hb, dest_w, zeros)


def _sc_gather_rows(table, idx):
    n = idx.shape[0]
    w = SC_WINDOW
    assert n % (SC_WORKERS * 2 * w) == 0
    per_w = n // SC_WORKERS
    nwin = per_w // w
    rows_t = pltpu.VMEM((w,) + table.shape[1:], table.dtype)

    @functools.partial(
        pl.kernel, mesh=_sc_mesh(),
        out_type=jax.ShapeDtypeStruct((n,) + table.shape[1:], table.dtype),
        scratch_types=[pltpu.VMEM((per_w,), I32), rows_t, rows_t, pltpu.SemaphoreType.DMA, pltpu.SemaphoreType.DMA],
    )
    def gather(table_hbm, idx_hbm, out_hbm, idx_v, rows0, rows1, sem0, sem1):
        base = pl.multiple_of(_sc_worker() * per_w, 8)
        pltpu.sync_copy(idx_hbm.at[pl.ds(base, per_w)], idx_v)
        bufs = ((rows0, sem0), (rows1, sem1))
        fetch = lambda i, b: pltpu.make_async_copy(
            table_hbm.at[idx_v.at[pl.ds(pl.multiple_of(i * w, 8), w)]], bufs[b][0], bufs[b][1])
        fetch(0, 0).start()

        @pl.loop(0, nwin, step=2)
        def _(i0):
            for b in range(2):
                i = i0 + b
                fetch(i, b).wait()

                @pl.when(i + 1 < nwin)
                def _():
                    fetch(i + 1, 1 - b).start()

                pltpu.sync_copy(bufs[b][0], out_hbm.at[pl.ds(pl.multiple_of(base + i * w, 8), w)])

    return gather(table, idx)


EXPERT_CH = 256


def _expert_kernel(row0_ref, exp_ref, live_ref, fresh_ref, next_ref, par_ref, xs_ref, wg_ref, wu_ref, wd_ref, ys_ref,
                   xbuf, ybuf, sg, su, sd, wgb, wub, wdb, xsem, ysem, wsem):
    v = pl.program_id(0)
    nv = pl.num_programs(0)
    slot = v % 2
    rows = lambda u: pl.ds(pl.multiple_of(row0_ref[u] * PACK_SLABS, PACK_SLABS), EXPERT_CH * PACK_SLABS)
    x_copy = lambda u, s: pltpu.make_async_copy(xs_ref.at[rows(u)], xbuf.at[s], xsem.at[s])
    y_copy = lambda u, s: pltpu.make_async_copy(ybuf.at[s], ys_ref.at[rows(u)], ysem.at[s])
    w_copies = lambda e, s: [pltpu.make_async_copy(w.at[e], stage.at[s], wsem.at[s])
                             for w, stage in ((wg_ref, sg), (wu_ref, su), (wd_ref, sd))]
    spare = pl.ds(ys_ref.shape[0] - EXPERT_CH * PACK_SLABS, EXPERT_CH * PACK_SLABS)

    @pl.when(v == 0)
    def _():
        for c in w_copies(exp_ref[0], 0):
            c.start(priority=1)
        x_copy(0, 0).start()
        ybuf[0] = jnp.zeros(ybuf.shape[1:], U32)
        zero = pltpu.make_async_copy(ybuf.at[0], ys_ref.at[spare], ysem.at[0])
        zero.start()
        zero.wait()

    nxt = jnp.minimum(v + 1, nv - 1)

    @pl.when((v + 1 < nv) & (live_ref[nxt] == 1))
    def _():
        x_copy(nxt, 1 - slot).start()

    @pl.when(fresh_ref[v] == 1)
    def _():
        p = par_ref[v]
        for c in w_copies(exp_ref[v], p):
            c.wait()

        @pl.when(next_ref[v] >= 0)
        def _():
            for c in w_copies(next_ref[v], 1 - p):
                c.start(priority=1)

        wgb[...] = sg[p].astype(BF16)
        wub[...] = su[p].astype(BF16)
        wdb[...] = sd[p].astype(BF16)

    @pl.when(live_ref[v] == 1)
    def _():
        x_copy(v, slot).wait()
        x = jnp.concatenate(_unpack_rows(xbuf.at[slot]), axis=1).astype(BF16)
        hid = (_silu(_dot(x, wgb[...])) * _dot(x, wub[...])).astype(BF16)
        _pack_rows(ybuf.at[slot], _dot(hid, wdb[...]))

    @pl.when((v > 0) & (live_ref[jnp.maximum(v - 1, 0)] == 1))
    def _():
        y_copy(jnp.maximum(v - 1, 0), 1 - slot).wait()

    @pl.when(live_ref[v] == 1)
    def _():
        y_copy(v, slot).start()

        @pl.when(v == nv - 1)
        def _():
            y_copy(v, slot).wait()


def _chunks(counts, n_rows):
    ch = EXPERT_CH
    nv = n_rows // ch + N_EXPERTS
    ends = jnp.cumsum(counts)
    starts = ends - counts
    nch = (counts + ch - 1) // ch
    cend = jnp.cumsum(nch)
    cstart = cend - nch
    v = jnp.arange(nv, dtype=I32)
    live = v < cend[-1]
    vc = jnp.minimum(v, cend[-1] - 1)
    ids = jnp.arange(N_EXPERTS, dtype=I32)
    e = jnp.minimum(jnp.sum(cend[None, :] <= vc[:, None], axis=1), N_EXPERTS - 1).astype(I32)
    of_e = lambda a: jnp.sum(jnp.where(e[:, None] == ids[None, :], a[None, :], 0), axis=1)
    row0 = (of_e(starts) + (vc - of_e(cstart)) * ch).astype(I32)
    fresh = jnp.concatenate([jnp.ones((1,), I32), (e[1:] != e[:-1]).astype(I32)])
    order = jnp.cumsum(counts > 0) - 1
    order_e = of_e(order)
    follows = (counts > 0)[None, :] & (order[None, :] == order_e[:, None] + 1)
    nxt = jnp.sum(jnp.where(follows, ids[None, :] + 1, 0), axis=1) - 1
    return (starts.astype(I32), row0, e, live.astype(I32), jnp.where(live, fresh, 0).astype(I32),
            nxt.astype(I32), (order_e % 2).astype(I32))


def _experts(chunks, xs, w_gate, w_up, w_down):
    ch = EXPERT_CH
    buf = pltpu.VMEM((2, ch * PACK_SLABS, LANES), U32)
    anywhere = pl.BlockSpec(memory_space=pl.ANY)
    return pl.pallas_call(
        _expert_kernel,
        grid_spec=pltpu.PrefetchScalarGridSpec(
            num_scalar_prefetch=len(chunks),
            grid=(chunks[0].shape[0],),
            in_specs=[anywhere] * 4,
            out_specs=anywhere,
            scratch_shapes=[buf, buf,
                            pltpu.VMEM((2, D_MODEL, EXPERT_FF), F32), pltpu.VMEM((2, D_MODEL, EXPERT_FF), F32),
                            pltpu.VMEM((2, EXPERT_FF, D_MODEL), F32),
                            pltpu.VMEM((D_MODEL, EXPERT_FF), BF16), pltpu.VMEM((D_MODEL, EXPERT_FF), BF16),
                            pltpu.VMEM((EXPERT_FF, D_MODEL), BF16),
                            pltpu.SemaphoreType.DMA((2,)), pltpu.SemaphoreType.DMA((2,)),
                            pltpu.SemaphoreType.DMA((2,))],
        ),
        out_shape=jax.ShapeDtypeStruct(xs.shape, U32),
        compiler_params=_params("arbitrary"),
        name="experts",
    )(*chunks, xs, w_gate, w_up, w_down)


COMBINE_TM = 256


def _combine_kernel(wt_ref, h_ref, sg_ref, su_ref, sd_ref, g_ref, beta_ref, *refs):
    y_refs, o_ref = refs[:TOP_K], refs[TOP_K]
    h = h_ref[...]
    hb = h.astype(BF16)
    hid = (_silu(_dot(hb, sg_ref[...])) * _dot(hb, su_ref[...])).astype(BF16)
    acc = DN_ALPHA * h + _dot(hid, sd_ref[...])
    lo = acc[:, :HALF_D]
    hi = acc[:, HALF_D:]
    for k in range(TOP_K):
        y_lo, y_hi = _unpack_rows(y_refs[k])
        w = wt_ref[:, k:k + 1]
        lo = lo + w * y_lo
        hi = hi + w * y_hi
    o_ref[...] = _layer_norm(jnp.concatenate([lo, hi], axis=1), g_ref[...], beta_ref[...])


def _combine(wts_tk, yg, ws_gate, ws_up, ws_down, ln_g, ln_b, h, t0):
    tm = COMBINE_TM
    n = h.shape[0]
    b0 = t0 // tm
    per_slot = yg.shape[0] // (TOP_K * tm * PACK_SLABS)
    rows = pl.BlockSpec((tm, D_MODEL), lambda i: (i, 0))
    slot_rows = [pl.BlockSpec((tm * PACK_SLABS, LANES), lambda i, k=k: (k * per_slot + b0 + i, 0))
                 for k in range(TOP_K)]
    vec = pl.BlockSpec((1, D_MODEL), lambda i: (0, 0))
    return pl.pallas_call(
        _combine_kernel,
        grid=(n // tm,),
        in_specs=[pl.BlockSpec((tm, TOP_K), lambda i: (b0 + i, 0)), rows,
                  pl.BlockSpec((D_MODEL, EXPERT_FF), lambda i: (0, 0)),
                  pl.BlockSpec((D_MODEL, EXPERT_FF), lambda i: (0, 0)),
                  pl.BlockSpec((EXPERT_FF, D_MODEL), lambda i: (0, 0)),
                  vec, vec] + slot_rows,
        out_specs=pl.BlockSpec((tm, D_MODEL), lambda i: (i, 0)),
        out_shape=jax.ShapeDtypeStruct((n, D_MODEL), F32),
        compiler_params=_params("arbitrary"),
        name="combine",
    )(wts_tk, h, ws_gate, ws_up, ws_down, ln_g, ln_b, *([yg] * TOP_K))


def _kv_cache(kv_tail, g, keep):
    k = kv_tail[-keep:, g * GROUP_W:(g + 1) * GROUP_W]
    v = kv_tail[-keep:, ATTN_W + g * GROUP_W:ATTN_W + (g + 1) * GROUP_W]
    return jnp.stack([k, v], axis=1).reshape(keep, 2, HEADS_PER_GROUP, HEAD_DIM)


def kernel(x_prompt, x_sample, cache_kv_w128, cache_kv_w512, cache_kv_w2048, state_conv, w_in, b_in, w_dw, b_dw,
           conv_ln_g, conv_ln_b, w_o, b_o, ln1_g, ln1_b, w_router, router_bias, w_gate, w_up, w_down, ws_gate,
           ws_up, ws_down, ln2_g, ln2_b):
    assert w_in.shape[0] == DEPTH == 1
    batch, seq, _ = x_prompt.shape
    dec_batch, dec_seq, _ = x_sample.shape
    assert batch == 1
    n_s = dec_batch * dec_seq
    caches = (cache_kv_w128, cache_kv_w512, cache_kv_w2048)
    row = lambda a: a[0].reshape(1, -1)

    w_in_b = w_in[0].astype(BF16)
    b_in_r = row(b_in)
    w_o_b = w_o[0].astype(BF16)
    conv_w = (w_dw[0], row(b_dw), row(conv_ln_g), row(conv_ln_b))
    ln1 = (row(ln1_g), row(ln1_b))

    xp = x_prompt[0]
    keep_p = min(max(w for w, _ in DIL_GROUPS), seq)
    dils = tuple(d for _, d in DIL_GROUPS)
    *qkv_p, u_p, kv_p = _project(xp, w_in_b, b_in_r, 512, keep_p, BF16, dils)
    attn_p = [_attention_prompt(qkv_p[g], g) for g in range(N_GROUPS)]
    conv_p = _conv_prompt(u_p, *conv_w, 512)
    h_p, hpk_p = _outproj([a[0] for a in attn_p], [a[1] for a in attn_p], conv_p, xp, w_o_b, row(b_o), *ln1, 512,
                          dils)

    xs = x_sample.reshape(n_s, D_MODEL)
    ones = (1,) * N_GROUPS
    *qkv_s, u_s, kv_s = _project(xs, w_in_b, b_in_r, n_s, n_s, F32, ones)
    caches_t = [jnp.transpose(c[0].reshape(dec_batch, -1, 2 * GROUP_W), (0, 2, 1)) for c in caches]
    attn_s = [_attention_sample(qkv_s[g], caches_t[g], g, dec_seq) for g in range(N_GROUPS)]
    u_hist = jnp.concatenate([state_conv[0], u_s.reshape(dec_batch, dec_seq, CONV_CH)], axis=1)
    conv_s = _conv_sample(u_hist, *conv_w, dec_seq)
    h_s, hpk_s = _outproj([a[0] for a in attn_s], [a[1] for a in attn_s], conv_s, xs, w_o_b, row(b_o), *ln1, n_s,
                          ones)

    idx, wts, pos, counts = _route(h_p, h_s, w_router[0].T.astype(BF16), router_bias[0])
    n_tok = seq + n_s
    starts, *chunks = _chunks(counts, n_tok * TOP_K)
    dest = _dest_rows(idx, pos, starts)
    dest_w = dest.reshape(TOP_K, n_tok // SC_WINDOW, SC_WINDOW).transpose(1, 0, 2)
    tiles = lambda a: a.reshape(-1, PACK_SLABS, LANES)
    flat = lambda a: a.reshape(-1, LANES)
    x_sorted = _sc_scatter_rows(tiles(hpk_p), tiles(hpk_s), dest_w, jnp.zeros((EXPERT_CH, PACK_SLABS, LANES), U32),
                                n_tok * TOP_K)
    y_sorted = _experts(chunks, flat(x_sorted), w_gate[0], w_up[0], w_down[0])
    y_slots = _sc_gather_rows(tiles(y_sorted), dest.reshape(-1))
    shared = (ws_gate[0].astype(BF16), ws_up[0].astype(BF16), ws_down[0].astype(BF16))
    comb = functools.partial(_combine, wts.T, flat(y_slots), *shared, row(ln2_g), row(ln2_b))
    y_p = comb(h_p, 0)
    y_s = comb(h_s, seq)

    kv_prompt = [_kv_cache(kv_p, g, min(w, seq))[None, None] for g, (w, _) in enumerate(DIL_GROUPS)]
    assert seq >= CONV_WIDTH - 1
    conv_prompt = u_p[-(CONV_WIDTH - 1):]
    kv_s4 = kv_s.reshape(dec_batch, dec_seq, 2, N_GROUPS, HEADS_PER_GROUP, HEAD_DIM)
    kv_sample = [kv_s4[:, :, :, g][None] for g in range(N_GROUPS)]
    conv_sample = u_hist[:, -(CONV_WIDTH - 1):]
    return (y_p[None], y_s.reshape(dec_batch, dec_seq, D_MODEL), *kv_prompt, conv_prompt[None, None],
            *kv_sample, conv_sample[None])
```

```python
import functools

import jax
import jax.numpy as jnp
from jax import lax
from jax.experimental import pallas as pl
from jax.experimental.pallas import tpu as pltpu
from jax.experimental.pallas import tpu_sc as plsc

F32 = jnp.float32
BF16 = jnp.bfloat16
I32 = jnp.int32

D_MODEL = 1024
HEAD_DIM = 64
HEADS_PER_GROUP = 4
GROUP_W = HEADS_PER_GROUP * HEAD_DIM
DIL_GROUPS = ((128, 1), (512, 4), (2048, 16))
N_GROUPS = len(DIL_GROUPS)
ATTN_W = N_GROUPS * GROUP_W
CONV_CH = D_MODEL - ATTN_W
CONV_WIDTH = 31
IN_W = 3 * ATTN_W + 2 * CONV_CH
BAND = 128
N_EXPERTS = 256
TOP_K = 8
N_EXPERT_GROUPS = 8
EXPERTS_PER_GROUP = N_EXPERTS // N_EXPERT_GROUPS
TOPK_GROUPS = 4
EXPERT_FF = 256
ROUTED_SCALE = 2.5
DEPTH = 1
DN_ALPHA = (2 * DEPTH) ** 0.25
LN_EPS = 1e-5
MASKED = -1e30

VMEM_LIMIT_BYTES = 56 * 1024 * 1024


def _params(*sem):
    return pltpu.CompilerParams(dimension_semantics=sem, vmem_limit_bytes=VMEM_LIMIT_BYTES)


def _dot(a, b):
    return jnp.dot(a, b, preferred_element_type=F32)


def _dot_nt(a, b):
    return lax.dot_general(a, b, (((1,), (1,)), ((), ())), preferred_element_type=F32)


def _layer_norm(x, g, b):
    mu = jnp.mean(x, axis=-1, keepdims=True)
    xc = x - mu
    var = jnp.mean(xc * xc, axis=-1, keepdims=True)
    return xc * lax.rsqrt(var + LN_EPS) * g + b


def _silu(x):
    return x * jax.nn.sigmoid(x)


def _alibi_slopes():
    n = N_GROUPS * HEADS_PER_GROUP
    h = jnp.arange(1, n + 1, dtype=F32)
    return (2.0 ** (-8.0 * h / n)).reshape(N_GROUPS, HEADS_PER_GROUP)


LANES = 128


def _proj_kernel(x_ref, w_ref, b_ref, *refs, dils, tm):
    qkv_refs, (u_ref, kv_ref, zs) = refs[:N_GROUPS], refs[N_GROUPS:]
    x = x_ref[...].astype(BF16)
    for part in range(3):
        for g in range(N_GROUPS):
            c0 = part * ATTN_W + g * GROUP_W
            z = _dot(x, w_ref[:, c0:c0 + GROUP_W]) + b_ref[:, c0:c0 + GROUP_W]
            if part > 0:
                kv_ref[:, c0 - ATTN_W:c0 - ATTN_W + GROUP_W] = z
            out, dil = qkv_refs[g], dils[g]
            if dil == 1:
                out[:, part * GROUP_W:(part + 1) * GROUP_W] = z.astype(out.dtype)
                continue
            for half in range(GROUP_W // LANES):
                zs[half] = z[:, half * LANES:(half + 1) * LANES]
            for r in range(dil):
                for half in range(GROUP_W // LANES):
                    c = r * 3 * GROUP_W + part * GROUP_W + half * LANES
                    out[:, c:c + LANES] = zs[half, pl.ds(r, tm // dil, stride=dil), :].astype(out.dtype)
    c0 = 3 * ATTN_W
    a = _dot(x, w_ref[:, c0:c0 + CONV_CH]) + b_ref[:, c0:c0 + CONV_CH]
    gate = _dot(x, w_ref[:, c0 + CONV_CH:]) + b_ref[:, c0 + CONV_CH:]
    u_ref[...] = a * jax.nn.sigmoid(gate)


def _project(x, w_in, b_in, tm, keep, qkv_dtype, dils):
    n = x.shape[0]
    nt = n // tm
    nk = keep // tm
    return pl.pallas_call(
        functools.partial(_proj_kernel, dils=dils, tm=tm),
        grid=(nt,),
        in_specs=[
            pl.BlockSpec((tm, D_MODEL), lambda i: (i, 0)),
            pl.BlockSpec((D_MODEL, IN_W), lambda i: (0, 0)),
            pl.BlockSpec((1, IN_W), lambda i: (0, 0)),
        ],
        out_specs=[pl.BlockSpec((tm // d, d * 3 * GROUP_W), lambda i: (i, 0)) for d in dils] + [
            pl.BlockSpec((tm, CONV_CH), lambda i: (i, 0)),
            pl.BlockSpec((tm, 2 * ATTN_W), lambda i: (jnp.maximum(i - (nt - nk), 0), 0)),
        ],
        out_shape=[jax.ShapeDtypeStruct((n // d, d * 3 * GROUP_W), qkv_dtype) for d in dils] + [
            jax.ShapeDtypeStruct((n, CONV_CH), F32),
            jax.ShapeDtypeStruct((keep, 2 * ATTN_W), F32),
        ],
        scratch_shapes=[pltpu.VMEM((GROUP_W // LANES, tm, LANES), F32)],
        compiler_params=_params("arbitrary"),
        name="proj",
    )(x, w_in, b_in)


def _head_select(parts, rows):
    col = lax.broadcasted_iota(I32, (rows, GROUP_W), 1) // HEAD_DIM
    out = jnp.broadcast_to(parts[-1], (rows, GROUP_W))
    for h in range(HEADS_PER_GROUP - 2, -1, -1):
        out = jnp.where(col == h, parts[h], out)
    return out


def _head_rows(q, rows):
    col = lax.broadcasted_iota(I32, (rows, GROUP_W), 1) // HEAD_DIM
    return jnp.concatenate([jnp.where(col == h, q, jnp.zeros_like(q)) for h in range(HEADS_PER_GROUP)], axis=0)


def _softmax_pv(s, v, rows):
    m = jnp.max(s, axis=-1, keepdims=True)
    e = jnp.exp(s - m)
    l = jnp.sum(e, axis=-1, keepdims=True)
    pv = _dot(e.astype(BF16), v) / l
    lse = m + jnp.log(l)
    o = _head_select([pv[h * rows:(h + 1) * rows] for h in range(HEADS_PER_GROUP)], rows)
    lse_x = _head_select([lse[h * rows:(h + 1) * rows] for h in range(HEADS_PER_GROUP)], rows)
    return o, lse_x


ATTN_QB = 4


def _attn_kernel(q_ref, kp_ref, kc_ref, vp_ref, vc_ref, bias_ref, o_ref, lse_ref):
    b = pl.program_id(1)
    for j in range(ATTN_QB):
        rows = slice(j * BAND, (j + 1) * BAND)
        qm = _head_rows(q_ref[rows, :], BAND)
        if j == 0:
            k = jnp.concatenate([kp_ref[...], kc_ref[rows, :]], axis=0)
            v = jnp.concatenate([vp_ref[...], vc_ref[rows, :]], axis=0)
            bias = bias_ref[jnp.minimum(b, 1)]
        else:
            k = kc_ref[(j - 1) * BAND:(j + 1) * BAND, :]
            v = vc_ref[(j - 1) * BAND:(j + 1) * BAND, :]
            bias = bias_ref[1]
        s = _dot_nt(qm, k) * HEAD_DIM ** -0.5 + bias
        o, lse_x = _softmax_pv(s, v, BAND)
        o_ref[rows, :] = o.astype(o_ref.dtype)
        lse_ref[rows, :] = lse_x


def _prompt_bias(g, dil):
    slopes = _alibi_slopes()[g]
    qi = jnp.arange(BAND)[:, None]
    kj = jnp.arange(2 * BAND)[None, :]
    steps = qi + BAND - kj
    nk = DIL_GROUPS[g][0] // dil
    valid = (steps >= 0) & (steps <= nk)
    bias = -slopes[:, None, None] * (steps * dil).astype(F32)
    inner = jnp.where(valid[None], bias, MASKED)
    first = jnp.where((valid & (kj >= BAND))[None], bias, MASKED)
    return jnp.stack([first, inner]).reshape(2, HEADS_PER_GROUP * BAND, 2 * BAND)


def _attention_prompt(view, g):
    win, dil = DIL_GROUPS[g]
    n_cls = view.shape[0]
    rows = ATTN_QB * BAND
    assert win // dil <= BAND and n_cls % rows == 0 and view.shape[1] == dil * 3 * GROUP_W
    cur = lambda part: pl.BlockSpec((rows, GROUP_W), lambda r, b: (b, r * 3 + part))
    prev = lambda part: pl.BlockSpec((BAND, GROUP_W), lambda r, b: (jnp.maximum(b * ATTN_QB - 1, 0), r * 3 + part))
    return pl.pallas_call(
        _attn_kernel,
        grid=(dil, n_cls // rows),
        in_specs=[cur(0), prev(1), cur(1), prev(2), cur(2),
                  pl.BlockSpec((2, HEADS_PER_GROUP * BAND, 2 * BAND), lambda r, b: (0, 0, 0))],
        out_specs=[pl.BlockSpec((rows, GROUP_W), lambda r, b: (b, r)),
                   pl.BlockSpec((rows, GROUP_W), lambda r, b: (b, r))],
        out_shape=[jax.ShapeDtypeStruct((n_cls, dil * GROUP_W), BF16),
                   jax.ShapeDtypeStruct((n_cls, dil * GROUP_W), F32)],
        compiler_params=_params("arbitrary", "arbitrary"),
        name=f"attn_g{g}",
    )(view, view, view, view, view, _prompt_bias(g, dil))


def _attn_sample_kernel(q_ref, k_ref, v_ref, cache_ref, bias_ref, o_ref, lse_ref, *, n_buf, t):
    pad = jnp.zeros((BAND - t, GROUP_W), F32)
    k_new = jnp.concatenate([k_ref[...], pad], axis=0).astype(BF16)
    v_new = jnp.concatenate([v_ref[...], pad], axis=0).astype(BF16)
    qm = _head_rows(q_ref[...], t).astype(BF16)
    scale = HEAD_DIM ** -0.5
    s_old = _dot(qm, cache_ref[0, 0:GROUP_W, :].astype(BF16)) * scale + bias_ref[:, 0:n_buf]
    s_new = _dot_nt(qm, k_new) * scale + bias_ref[:, n_buf:]
    m = jnp.maximum(jnp.max(s_old, axis=-1, keepdims=True), jnp.max(s_new, axis=-1, keepdims=True))
    e_old = jnp.exp(s_old - m)
    e_new = jnp.exp(s_new - m)
    l = jnp.sum(e_old, axis=-1, keepdims=True) + jnp.sum(e_new, axis=-1, keepdims=True)
    pv = _dot_nt(e_old.astype(BF16), cache_ref[0, GROUP_W:, :].astype(BF16)) + _dot(e_new.astype(BF16), v_new)
    pv = pv / l
    lse = m + jnp.log(l)
    o_ref[...] = _head_select([pv[h * t:(h + 1) * t] for h in range(HEADS_PER_GROUP)], t)
    lse_ref[...] = _head_select([lse[h * t:(h + 1) * t] for h in range(HEADS_PER_GROUP)], t)


def _sample_bias(g, n_buf, t):
    win, dil = DIL_GROUPS[g]
    slopes = _alibi_slopes()[g]
    tq = jnp.arange(t)[:, None]
    j = jnp.arange(n_buf + BAND)[None, :]
    dist = n_buf + tq - j
    valid = (dist >= 0) & (dist % dil == 0) & (dist <= win) & (j < n_buf + t)
    bias = -slopes[:, None, None] * dist.astype(F32)
    return jnp.where(valid[None], bias, MASKED).reshape(HEADS_PER_GROUP * t, n_buf + BAND)


def _attention_sample(qkv, cache_t, g, t):
    nb, n_buf = cache_t.shape[0], cache_t.shape[2]
    col = lambda part: pl.BlockSpec((t, GROUP_W), lambda i: (i, part))
    kern = functools.partial(_attn_sample_kernel, n_buf=n_buf, t=t)
    return pl.pallas_call(
        kern,
        grid=(nb,),
        in_specs=[col(0), col(1), col(2),
                  pl.BlockSpec((1, 2 * GROUP_W, n_buf), lambda i: (i, 0, 0)),
                  pl.BlockSpec((HEADS_PER_GROUP * t, n_buf + BAND), lambda i: (0, 0))],
        out_specs=[pl.BlockSpec((t, GROUP_W), lambda i: (i, 0)),
                   pl.BlockSpec((t, GROUP_W), lambda i: (i, 0))],
        out_shape=[jax.ShapeDtypeStruct((nb * t, GROUP_W), F32),
                   jax.ShapeDtypeStruct((nb * t, GROUP_W), F32)],
        compiler_params=_params("arbitrary"),
        name=f"attn_sample_g{g}",
    )(qkv, qkv, qkv, cache_t, _sample_bias(g, n_buf, t))


CONV_HALO = 32


def _conv_tail(acc, b_ref, g_ref, beta_ref):
    return _silu(_layer_norm(acc + b_ref[...], g_ref[...], beta_ref[...]))


SUBLANES = 8


def _conv_prompt_kernel(halo_ref, u_ref, w_ref, b_ref, g_ref, beta_ref, o_ref, hist, part, *, tm):
    i = pl.program_id(0)
    hist[0:CONV_HALO, :] = jnp.where(i == 0, 0.0, halo_ref[...])
    hist[CONV_HALO:CONV_HALO + tm, :] = u_ref[...]
    hist[CONV_HALO + tm:, :] = jnp.zeros((SUBLANES, CONV_CH), F32)
    off = CONV_HALO - (CONV_WIDTH - 1)
    acc = None
    for s in range(SUBLANES):
        group = None
        for m in range(s, off + CONV_WIDTH, SUBLANES):
            j = m - off
            if j < 0:
                continue
            term = w_ref[j:j + 1, :] * hist[m - s:m - s + tm + SUBLANES, :]
            group = term if group is None else group + term
        part[...] = group
        shifted = part[s:s + tm, :]
        acc = shifted if acc is None else acc + shifted
    o_ref[...] = _conv_tail(acc, b_ref, g_ref, beta_ref).astype(o_ref.dtype)


def _conv_prompt(u, w_dw, b_dw, ln_g, ln_b, tm):
    n = u.shape[0]
    vec = pl.BlockSpec((1, CONV_CH), lambda i: (0, 0))
    return pl.pallas_call(
        functools.partial(_conv_prompt_kernel, tm=tm),
        grid=(n // tm,),
        in_specs=[pl.BlockSpec((CONV_HALO, CONV_CH), lambda i: (jnp.maximum(i * (tm // CONV_HALO) - 1, 0), 0)),
                  pl.BlockSpec((tm, CONV_CH), lambda i: (i, 0)),
                  pl.BlockSpec((CONV_WIDTH, CONV_CH), lambda i: (0, 0)), vec, vec, vec],
        out_specs=pl.BlockSpec((tm, CONV_CH), lambda i: (i, 0)),
        out_shape=jax.ShapeDtypeStruct((n, CONV_CH), BF16),
        scratch_shapes=[pltpu.VMEM((CONV_HALO + tm + SUBLANES, CONV_CH), F32),
                        pltpu.VMEM((tm + SUBLANES, CONV_CH), F32)],
        compiler_params=_params("arbitrary"),
        name="conv_prompt",
    )(u, u, w_dw, b_dw, ln_g, ln_b)


def _conv_sample_kernel(hist_ref, w_ref, b_ref, g_ref, beta_ref, o_ref, *, t):
    acc = jnp.zeros((t, CONV_CH), F32)
    for j in range(CONV_WIDTH):
        acc = acc + w_ref[j:j + 1, :] * hist_ref[0, j:j + t, :]
    o_ref[...] = _conv_tail(acc, b_ref, g_ref, beta_ref)


def _conv_sample(u_hist, w_dw, b_dw, ln_g, ln_b, t):
    nb, rows = u_hist.shape[0], u_hist.shape[1]
    vec = pl.BlockSpec((1, CONV_CH), lambda i: (0, 0))
    return pl.pallas_call(
        functools.partial(_conv_sample_kernel, t=t),
        grid=(nb,),
        in_specs=[pl.BlockSpec((1, rows, CONV_CH), lambda i: (i, 0, 0)),
                  pl.BlockSpec((CONV_WIDTH, CONV_CH), lambda i: (0, 0)), vec, vec, vec],
        out_specs=pl.BlockSpec((t, CONV_CH), lambda i: (i, 0)),
        out_shape=jax.ShapeDtypeStruct((nb * t, CONV_CH), F32),
        compiler_params=_params("arbitrary"),
        name="conv_sample",
    )(u_hist, w_dw, b_dw, ln_g, ln_b)


U32 = jnp.uint32
HALF_D = D_MODEL // 2
PACK_SLABS = HALF_D // LANES


def _token_order(ref, dil, scr, tm):
    if dil == 1:
        return ref[...].astype(F32)
    for r in range(dil):
        for half in range(GROUP_W // LANES):
            c = r * GROUP_W + half * LANES
            scr[half, pl.ds(r, tm // dil, stride=dil), :] = ref[:, c:c + LANES].astype(F32)
    return jnp.concatenate([scr[half] for half in range(GROUP_W // LANES)], axis=1)


def _pack_rows(ref, val):
    rows = val.shape[0]
    bits = lax.bitcast_convert_type(val.astype(BF16).astype(F32), U32)
    packed = lax.shift_right_logical(bits[:, :HALF_D], jnp.uint32(16)) | bits[:, HALF_D:]
    for c in range(PACK_SLABS):
        ref[pl.ds(c, rows, stride=PACK_SLABS), :] = packed[:, c * LANES:(c + 1) * LANES]


def _unpack_rows(ref):
    rows = ref.shape[0] // PACK_SLABS
    u = jnp.concatenate([ref[pl.ds(c, rows, stride=PACK_SLABS), :] for c in range(PACK_SLABS)], axis=1)
    lo = lax.bitcast_convert_type(lax.shift_left(u, jnp.uint32(16)), F32)
    hi = lax.bitcast_convert_type(u & jnp.uint32(0xFFFF0000), F32)
    return lo, hi


def _outproj_kernel(o0, o1, o2, l0, l1, l2, c_ref, x_ref, w_ref, b_ref, g_ref, beta_ref, h_ref, hp_ref, *scr,
                    dils, tm):
    os_ = [_token_order(r, d, scr[2 * g], tm) for g, (r, d) in enumerate(zip((o0, o1, o2), dils))]
    ls = [_token_order(r, d, scr[2 * g + 1], tm) for g, (r, d) in enumerate(zip((l0, l1, l2), dils))]
    m = jnp.maximum(jnp.maximum(ls[0], ls[1]), ls[2])
    es = [jnp.exp(l - m) for l in ls]
    inv = 1.0 / (es[0] + es[1] + es[2])
    mixed = b_ref[...]
    for g in range(N_GROUPS):
        a = (os_[g] * (es[g] * inv)).astype(BF16)
        mixed = mixed + _dot(a, w_ref[g * GROUP_W:(g + 1) * GROUP_W, :])
    mixed = mixed + _dot(c_ref[...].astype(BF16), w_ref[ATTN_W:, :])
    h = _layer_norm(DN_ALPHA * x_ref[...] + mixed, g_ref[...], beta_ref[...])
    h_ref[...] = h
    _pack_rows(hp_ref, h)


def _outproj(os_, lses, conv, x, w_o, b_o, ln_g, ln_b, tm, dils):
    n = x.shape[0]
    grp = [pl.BlockSpec((tm // d, d * GROUP_W), lambda i: (i, 0)) for d in dils]
    vec = pl.BlockSpec((1, D_MODEL), lambda i: (0, 0))
    return pl.pallas_call(
        functools.partial(_outproj_kernel, dils=dils, tm=tm),
        grid=(n // tm,),
        in_specs=grp + grp + [pl.BlockSpec((tm, CONV_CH), lambda i: (i, 0)),
                              pl.BlockSpec((tm, D_MODEL), lambda i: (i, 0)),
                              pl.BlockSpec((D_MODEL, D_MODEL), lambda i: (0, 0)), vec, vec, vec],
        out_specs=[pl.BlockSpec((tm, D_MODEL), lambda i: (i, 0)),
                   pl.BlockSpec((tm * PACK_SLABS, LANES), lambda i: (i, 0))],
        out_shape=[jax.ShapeDtypeStruct((n, D_MODEL), F32),
                   jax.ShapeDtypeStruct((n * PACK_SLABS, LANES), U32)],
        scratch_shapes=[pltpu.VMEM((GROUP_W // LANES, tm, LANES), F32) for _ in range(2 * N_GROUPS)],
        compiler_params=_params("arbitrary"),
        name="outproj",
    )(*os_, *lses, conv, x, w_o, b_o, ln_g, ln_b)


ROUTER_TM = 256


def _first_index(hit, idx, limit, axis):
    return jnp.min(jnp.where(hit, idx, limit), axis=axis, keepdims=True)


def _router_kernel(ha_ref, hb_ref, w_ref, rb_ref, tri_ref, idx_ref, wt_ref, pos_ref, cnt_ref, run, *, tiles_a):
    i = pl.program_id(0)
    tm = ROUTER_TM

    @pl.when(i == 0)
    def _():
        run[...] = jnp.zeros_like(run)

    h = jnp.where(i < tiles_a, ha_ref[...], hb_ref[...])
    logits = _dot_nt(w_ref[...], h.astype(BF16))
    scores = jax.nn.sigmoid(logits)
    biased = scores + rb_ref[...]
    groups = [biased[g * EXPERTS_PER_GROUP:(g + 1) * EXPERTS_PER_GROUP] for g in range(N_EXPERT_GROUPS)]
    ei = lax.broadcasted_iota(I32, (EXPERTS_PER_GROUP, tm), 0).astype(F32)
    gs = []
    for bg in groups:
        m1 = jnp.max(bg, axis=0, keepdims=True)
        f1 = _first_index(bg == m1, ei, float(EXPERTS_PER_GROUP), 0)
        m2 = jnp.max(jnp.where(ei == f1, -jnp.inf, bg), axis=0, keepdims=True)
        gs.append(m1 + m2)
    gs = jnp.concatenate(gs, axis=0)
    gi = lax.broadcasted_iota(I32, gs.shape, 0).astype(F32)
    keep = jnp.zeros(gs.shape, F32)
    cur = gs
    for _ in range(TOPK_GROUPS):
        m = jnp.max(cur, axis=0, keepdims=True)
        f = _first_index(cur == m, gi, float(N_EXPERT_GROUPS), 0)
        hit = gi == f
        keep = jnp.where(hit, 1.0, keep)
        cur = jnp.where(hit, -jnp.inf, cur)
    masked = jnp.concatenate([jnp.where(keep[g:g + 1] > 0.0, bg, -jnp.inf) for g, bg in enumerate(groups)], axis=0)
    xi = lax.broadcasted_iota(I32, (N_EXPERTS, tm), 0).astype(F32)
    cur = masked
    sel = jnp.zeros((N_EXPERTS, tm), F32)
    picks = []
    for _ in range(TOP_K):
        m = jnp.max(cur, axis=0, keepdims=True)
        f = _first_index(cur == m, xi, float(N_EXPERTS), 0)
        hit = xi == f
        picks.append((f, hit))
        sel = jnp.where(hit, 1.0, sel)
        cur = jnp.where(hit, -jnp.inf, cur)
    before = _dot(sel.astype(BF16), tri_ref[...]) + run[...]
    run[...] = run[...] + jnp.sum(sel, axis=1, keepdims=True)
    ws = [jnp.sum(jnp.where(hit, scores, 0.0), axis=0, keepdims=True) for _, hit in picks]
    wsum = ws[0]
    for w in ws[1:]:
        wsum = wsum + w
    for k, (f, hit) in enumerate(picks):
        idx_ref[k:k + 1, :] = f.astype(I32)
        wt_ref[k:k + 1, :] = ws[k] / wsum * ROUTED_SCALE
        pos_ref[k:k + 1, :] = jnp.sum(jnp.where(hit, before, 0.0), axis=0, keepdims=True).astype(I32)
    cnt_ref[...] = jnp.broadcast_to(run[...], cnt_ref.shape).astype(I32)


def _route(ha, hb, w_router_t, router_bias):
    tm = ROUTER_TM
    tiles_a, tiles_b = ha.shape[0] // tm, hb.shape[0] // tm
    spec_a = pl.BlockSpec((tm, D_MODEL), lambda i: (jnp.minimum(i, tiles_a - 1), 0))
    spec_b = pl.BlockSpec((tm, D_MODEL), lambda i: (jnp.maximum(i - tiles_a, 0), 0))
    t = (tiles_a + tiles_b) * tm
    tri = (jnp.arange(tm)[:, None] < jnp.arange(tm)[None, :]).astype(BF16)
    slot = pl.BlockSpec((TOP_K, tm), lambda i: (0, i))
    idx, wts, pos, cnt = pl.pallas_call(
        functools.partial(_router_kernel, tiles_a=tiles_a),
        grid=(t // tm,),
        in_specs=[spec_a, spec_b,
                  pl.BlockSpec((N_EXPERTS, D_MODEL), lambda i: (0, 0)),
                  pl.BlockSpec((N_EXPERTS, 1), lambda i: (0, 0)),
                  pl.BlockSpec((tm, tm), lambda i: (0, 0))],
        out_specs=[slot, slot, slot, pl.BlockSpec((N_EXPERTS, 128), lambda i: (0, 0))],
        out_shape=[jax.ShapeDtypeStruct((TOP_K, t), I32), jax.ShapeDtypeStruct((TOP_K, t), F32),
                   jax.ShapeDtypeStruct((TOP_K, t), I32), jax.ShapeDtypeStruct((N_EXPERTS, 128), I32)],
        scratch_shapes=[pltpu.VMEM((N_EXPERTS, 1), F32)],
        compiler_params=_params("arbitrary"),
        name="router",
    )(ha, hb, w_router_t, router_bias.reshape(N_EXPERTS, 1), tri)
    return idx, wts, pos, cnt[:, 0]


DEST_TM = 1280


def _dest_kernel(idx_ref, pos_ref, starts_ref, dest_ref):
    tm = idx_ref.shape[1]
    ei = lax.broadcasted_iota(I32, (N_EXPERTS, tm), 0)
    starts = starts_ref[...]
    for k in range(TOP_K):
        first = jnp.sum(jnp.where(ei == idx_ref[k:k + 1, :], starts, 0.0), axis=0, keepdims=True)
        dest_ref[k:k + 1, :] = first.astype(I32) + pos_ref[k:k + 1, :]


def _dest_rows(idx, pos, starts):
    t = idx.shape[1]
    tm = DEST_TM if t % DEST_TM == 0 else ROUTER_TM
    assert t % tm == 0
    slot = pl.BlockSpec((TOP_K, tm), lambda i: (0, i))
    return pl.pallas_call(
        _dest_kernel,
        grid=(t // tm,),
        in_specs=[slot, slot, pl.BlockSpec((N_EXPERTS, 1), lambda i: (0, 0))],
        out_specs=slot,
        out_shape=jax.ShapeDtypeStruct((TOP_K, t), I32),
        compiler_params=_params("arbitrary"),
        name="dest_rows",
    )(idx, pos, starts.astype(F32).reshape(N_EXPERTS, 1))


SC_CORES = 2
SC_SUBCORES = 16
SC_WORKERS = SC_CORES * SC_SUBCORES
SC_WINDOW = 32


def _sc_worker():
    return lax.axis_index("s") * SC_CORES + lax.axis_index("c")


def _sc_mesh():
    return plsc.VectorSubcoreMesh(core_axis_name="c", subcore_axis_name="s")


def _sc_scatter_rows(ha, hb, dest_w, zeros, n_out):
    w = SC_WINDOW
    na, nb = ha.shape[0], hb.shape[0]
    spare = zeros.shape[0]
    wa, wb = na // w, nb // w
    assert na % (SC_WORKERS * 2 * w) == 0 and nb % w == 0 and wb <= SC_WORKERS
    per_w = wa // SC_WORKERS
    rows_t = pltpu.VMEM((w,) + ha.shape[1:], ha.dtype)
    idx_t = pltpu.VMEM((TOP_K, w), I32)

    @functools.partial(
        pl.kernel, mesh=_sc_mesh(),
        out_type=jax.ShapeDtypeStruct((n_out + spare,) + ha.shape[1:], ha.dtype),
        scratch_types=[rows_t, rows_t, idx_t, idx_t, pltpu.SemaphoreType.DMA, pltpu.SemaphoreType.DMA],
    )
    def scatter(ha_hbm, hb_hbm, dest_hbm, zeros_hbm, out_hbm, rows0, rows1, idx0, idx1, sem0, sem1):
        wid = _sc_worker()
        bufs = ((rows0, idx0, sem0), (rows1, idx1, sem1))

        @pl.when(wid == SC_WORKERS - 1)
        def _():
            pltpu.sync_copy(zeros_hbm, out_hbm.at[pl.ds(n_out, spare)])

        def load(src_hbm, row0, win, b):
            pltpu.sync_copy(src_hbm.at[pl.ds(pl.multiple_of(row0, 8), w)], bufs[b][0])
            pltpu.sync_copy(dest_hbm.at[win], bufs[b][1])

        def copies(b):
            return [pltpu.make_async_copy(bufs[b][0], out_hbm.at[bufs[b][1].at[k]], bufs[b][2])
                    for k in range(TOP_K)]

        win0 = wid * per_w
        load(ha_hbm, win0 * w, win0, 0)

        @pl.loop(0, per_w, step=2)
        def _(i0):
            for b in range(2):
                i = i0 + b
                for c in copies(b):
                    c.start()

                @pl.when(i + 1 < per_w)
                def _():
                    load(ha_hbm, (win0 + i + 1) * w, win0 + i + 1, 1 - b)

                for c in copies(b):
                    c.wait()

        @pl.when(wid < wb)
        def _():
            load(hb_hbm, wid * w, wa + wid, 0)
            for c in copies(0):
                c.start()
            for c in copies(0):
                c.wait()

    return scatter(ha, hb, dest_w, zeros)


def _sc_gather_rows(table, idx):
    n = idx.shape[0]
    w = SC_WINDOW
    assert n % (SC_WORKERS * 2 * w) == 0
    per_w = n // SC_WORKERS
    nwin = per_w // w
    rows_t = pltpu.VMEM((w,) + table.shape[1:], table.dtype)

    @functools.partial(
        pl.kernel, mesh=_sc_mesh(),
        out_type=jax.ShapeDtypeStruct((n,) + table.shape[1:], table.dtype),
        scratch_types=[pltpu.VMEM((per_w,), I32), rows_t, rows_t, pltpu.SemaphoreType.DMA, pltpu.SemaphoreType.DMA],
    )
    def gather(table_hbm, idx_hbm, out_hbm, idx_v, rows0, rows1, sem0, sem1):
        base = pl.multiple_of(_sc_worker() * per_w, 8)
        pltpu.sync_copy(idx_hbm.at[pl.ds(base, per_w)], idx_v)
        bufs = ((rows0, sem0), (rows1, sem1))
        fetch = lambda i, b: pltpu.make_async_copy(
            table_hbm.at[idx_v.at[pl.ds(pl.multiple_of(i * w, 8), w)]], bufs[b][0], bufs[b][1])
        fetch(0, 0).start()

        @pl.loop(0, nwin, step=2)
        def _(i0):
            for b in range(2):
                i = i0 + b
                fetch(i, b).wait()

                @pl.when(i + 1 < nwin)
                def _():
                    fetch(i + 1, 1 - b).start()

                pltpu.sync_copy(bufs[b][0], out_hbm.at[pl.ds(pl.multiple_of(base + i * w, 8), w)])

    return gather(table, idx)


EXPERT_CH = 256


def _expert_kernel(row0_ref, exp_ref, live_ref, fresh_ref, next_ref, par_ref, ordered_ref, xs_ref, wg_ref, wu_ref,
                   wd_ref, ys_ref, xbuf, ybuf, sg, su, sd, wgb, wub, wdb, xsem, ysem, wsem):
    v = pl.program_id(0)
    nv = pl.num_programs(0)
    slot = v % 2
    yslot = lax.rem(v, 3)
    rows = lambda u: pl.ds(pl.multiple_of(row0_ref[u] * PACK_SLABS, PACK_SLABS), EXPERT_CH * PACK_SLABS)
    x_copy = lambda u, s: pltpu.make_async_copy(xs_ref.at[rows(u)], xbuf.at[s], xsem.at[s])
    y_copy = lambda u: pltpu.make_async_copy(ybuf.at[lax.rem(u, 3)], ys_ref.at[rows(u)], ysem.at[lax.rem(u, 3)])
    prev1, prev2 = jnp.maximum(v - 1, 0), jnp.maximum(v - 2, 0)
    w_copies = lambda e, s: [pltpu.make_async_copy(w.at[e], stage.at[s], wsem.at[s])
                             for w, stage in ((wg_ref, sg), (wu_ref, su), (wd_ref, sd))]
    spare = pl.ds(ys_ref.shape[0] - EXPERT_CH * PACK_SLABS, EXPERT_CH * PACK_SLABS)

    @pl.when(v == 0)
    def _():
        for c in w_copies(exp_ref[0], 0):
            c.start(priority=1)
        x_copy(0, 0).start()
        ybuf[2] = jnp.zeros(ybuf.shape[1:], U32)
        zero = pltpu.make_async_copy(ybuf.at[2], ys_ref.at[spare], ysem.at[2])
        zero.start()
        zero.wait()

    nxt = jnp.minimum(v + 1, nv - 1)

    @pl.when((v + 1 < nv) & (live_ref[nxt] == 1))
    def _():
        x_copy(nxt, 1 - slot).start()

    @pl.when(fresh_ref[v] == 1)
    def _():
        p = par_ref[v]
        for c in w_copies(exp_ref[v], p):
            c.wait()

        @pl.when(next_ref[v] >= 0)
        def _():
            for c in w_copies(next_ref[v], 1 - p):
                c.start(priority=1)

        wgb[...] = sg[p].astype(BF16)
        wub[...] = su[p].astype(BF16)
        wdb[...] = sd[p].astype(BF16)

    @pl.when(live_ref[v] == 1)
    def _():
        x_copy(v, slot).wait()
        x = jnp.concatenate(_unpack_rows(xbuf.at[slot]), axis=1).astype(BF16)
        hid = (_silu(_dot(x, wgb[...])) * _dot(x, wub[...])).astype(BF16)
        _pack_rows(ybuf.at[yslot], _dot(hid, wdb[...]))

    @pl.when((v >= 2) & (live_ref[prev2] == 1) & (ordered_ref[prev1] == 0))
    def _():
        y_copy(prev2).wait()

    @pl.when((v >= 1) & (live_ref[prev1] == 1) & (ordered_ref[v] == 1))
    def _():
        y_copy(prev1).wait()

    @pl.when(live_ref[v] == 1)
    def _():
        y_copy(v).start()

    @pl.when(v == nv - 1)
    def _():
        @pl.when((v >= 1) & (live_ref[prev1] == 1) & (ordered_ref[v] == 0))
        def _():
            y_copy(prev1).wait()

        @pl.when(live_ref[v] == 1)
        def _():
            y_copy(v).wait()


def _chunks(counts, n_rows):
    ch = EXPERT_CH
    nv = n_rows // ch + N_EXPERTS
    ends = jnp.cumsum(counts)
    starts = ends - counts
    nch = (counts + ch - 1) // ch
    cend = jnp.cumsum(nch)
    cstart = cend - nch
    v = jnp.arange(nv, dtype=I32)
    live = v < cend[-1]
    vc = jnp.minimum(v, cend[-1] - 1)
    ids = jnp.arange(N_EXPERTS, dtype=I32)
    e = jnp.minimum(jnp.sum(cend[None, :] <= vc[:, None], axis=1), N_EXPERTS - 1).astype(I32)
    of_e = lambda a: jnp.sum(jnp.where(e[:, None] == ids[None, :], a[None, :], 0), axis=1)
    starts_e, ends_e, count_e = of_e(starts), of_e(ends), of_e(counts)
    c = vc - of_e(cstart)
    is_last = c == of_e(nch) - 1
    row0 = jnp.where(is_last & (count_e >= ch), ends_e - ch, starts_e + c * ch).astype(I32)
    overrun = live & is_last & (count_e < ch)
    ordered = jnp.concatenate([jnp.zeros((1,), bool), overrun[:-1]]) & live
    fresh = jnp.concatenate([jnp.ones((1,), I32), (e[1:] != e[:-1]).astype(I32)])
    order = jnp.cumsum(counts > 0) - 1
    order_e = of_e(order)
    follows = (counts > 0)[None, :] & (order[None, :] == order_e[:, None] + 1)
    nxt = jnp.sum(jnp.where(follows, ids[None, :] + 1, 0), axis=1) - 1
    return (starts.astype(I32), row0, e, live.astype(I32), jnp.where(live, fresh, 0).astype(I32),
            nxt.astype(I32), (order_e % 2).astype(I32), ordered.astype(I32))


def _experts(chunks, xs, w_gate, w_up, w_down):
    ch = EXPERT_CH
    buf = lambda n: pltpu.VMEM((n, ch * PACK_SLABS, LANES), U32)
    anywhere = pl.BlockSpec(memory_space=pl.ANY)
    return pl.pallas_call(
        _expert_kernel,
        grid_spec=pltpu.PrefetchScalarGridSpec(
            num_scalar_prefetch=len(chunks),
            grid=(chunks[0].shape[0],),
            in_specs=[anywhere] * 4,
            out_specs=anywhere,
            scratch_shapes=[buf(2), buf(3),
                            pltpu.VMEM((2, D_MODEL, EXPERT_FF), F32), pltpu.VMEM((2, D_MODEL, EXPERT_FF), F32),
                            pltpu.VMEM((2, EXPERT_FF, D_MODEL), F32),
                            pltpu.VMEM((D_MODEL, EXPERT_FF), BF16), pltpu.VMEM((D_MODEL, EXPERT_FF), BF16),
                            pltpu.VMEM((EXPERT_FF, D_MODEL), BF16),
                            pltpu.SemaphoreType.DMA((2,)), pltpu.SemaphoreType.DMA((3,)),
                            pltpu.SemaphoreType.DMA((2,))],
        ),
        out_shape=jax.ShapeDtypeStruct(xs.shape, U32),
        compiler_params=_params("arbitrary"),
        name="experts",
    )(*chunks, xs, w_gate, w_up, w_down)


COMBINE_TM = 256


def _combine_kernel(wt_ref, h_ref, sg_ref, su_ref, sd_ref, g_ref, beta_ref, *refs):
    y_refs, o_ref = refs[:TOP_K], refs[TOP_K]
    h = h_ref[...]
    hb = h.astype(BF16)
    hid = (_silu(_dot(hb, sg_ref[...])) * _dot(hb, su_ref[...])).astype(BF16)
    acc = DN_ALPHA * h + _dot(hid, sd_ref[...])
    lo = acc[:, :HALF_D]
    hi = acc[:, HALF_D:]
    for k in range(TOP_K):
        y_lo, y_hi = _unpack_rows(y_refs[k])
        w = wt_ref[:, k:k + 1]
        lo = lo + w * y_lo
        hi = hi + w * y_hi
    o_ref[...] = _layer_norm(jnp.concatenate([lo, hi], axis=1), g_ref[...], beta_ref[...])


def _combine(wts_tk, yg, ws_gate, ws_up, ws_down, ln_g, ln_b, h, t0):
    tm = COMBINE_TM
    n = h.shape[0]
    b0 = t0 // tm
    per_slot = yg.shape[0] // (TOP_K * tm * PACK_SLABS)
    rows = pl.BlockSpec((tm, D_MODEL), lambda i: (i, 0))
    slot_rows = [pl.BlockSpec((tm * PACK_SLABS, LANES), lambda i, k=k: (k * per_slot + b0 + i, 0))
                 for k in range(TOP_K)]
    vec = pl.BlockSpec((1, D_MODEL), lambda i: (0, 0))
    return pl.pallas_call(
        _combine_kernel,
        grid=(n // tm,),
        in_specs=[pl.BlockSpec((tm, TOP_K), lambda i: (b0 + i, 0)), rows,
                  pl.BlockSpec((D_MODEL, EXPERT_FF), lambda i: (0, 0)),
                  pl.BlockSpec((D_MODEL, EXPERT_FF), lambda i: (0, 0)),
                  pl.BlockSpec((EXPERT_FF, D_MODEL), lambda i: (0, 0)),
                  vec, vec] + slot_rows,
        out_specs=pl.BlockSpec((tm, D_MODEL), lambda i: (i, 0)),
        out_shape=jax.ShapeDtypeStruct((n, D_MODEL), F32),
        compiler_params=_params("arbitrary"),
        name="combine",
    )(wts_tk, h, ws_gate, ws_up, ws_down, ln_g, ln_b, *([yg] * TOP_K))


def _kv_cache(kv_tail, g, keep):
    k = kv_tail[-keep:, g * GROUP_W:(g + 1) * GROUP_W]
    v = kv_tail[-keep:, ATTN_W + g * GROUP_W:ATTN_W + (g + 1) * GROUP_W]
    return jnp.stack([k, v], axis=1).reshape(keep, 2, HEADS_PER_GROUP, HEAD_DIM)


def kernel(x_prompt, x_sample, cache_kv_w128, cache_kv_w512, cache_kv_w2048, state_conv, w_in, b_in, w_dw, b_dw,
           conv_ln_g, conv_ln_b, w_o, b_o, ln1_g, ln1_b, w_router, router_bias, w_gate, w_up, w_down, ws_gate,
           ws_up, ws_down, ln2_g, ln2_b):
    assert w_in.shape[0] == DEPTH == 1
    batch, seq, _ = x_prompt.shape
    dec_batch, dec_seq, _ = x_sample.shape
    assert batch == 1
    n_s = dec_batch * dec_seq
    caches = (cache_kv_w128, cache_kv_w512, cache_kv_w2048)
    row = lambda a: a[0].reshape(1, -1)

    w_in_b = w_in[0].astype(BF16)
    b_in_r = row(b_in)
    w_o_b = w_o[0].astype(BF16)
    conv_w = (w_dw[0], row(b_dw), row(conv_ln_g), row(conv_ln_b))
    ln1 = (row(ln1_g), row(ln1_b))

    xp = x_prompt[0]
    keep_p = min(max(w for w, _ in DIL_GROUPS), seq)
    dils = tuple(d for _, d in DIL_GROUPS)
    *qkv_p, u_p, kv_p = _project(xp, w_in_b, b_in_r, 512, keep_p, BF16, dils)
    attn_p = [_attention_prompt(qkv_p[g], g) for g in range(N_GROUPS)]
    conv_p = _conv_prompt(u_p, *conv_w, 512)
    h_p, hpk_p = _outproj([a[0] for a in attn_p], [a[1] for a in attn_p], conv_p, xp, w_o_b, row(b_o), *ln1, 512,
                          dils)

    xs = x_sample.reshape(n_s, D_MODEL)
    ones = (1,) * N_GROUPS
    *qkv_s, u_s, kv_s = _project(xs, w_in_b, b_in_r, n_s, n_s, F32, ones)
    caches_t = [jnp.transpose(c[0].reshape(dec_batch, -1, 2 * GROUP_W), (0, 2, 1)) for c in caches]
    attn_s = [_attention_sample(qkv_s[g], caches_t[g], g, dec_seq) for g in range(N_GROUPS)]
    u_hist = jnp.concatenate([state_conv[0], u_s.reshape(dec_batch, dec_seq, CONV_CH)], axis=1)
    conv_s = _conv_sample(u_hist, *conv_w, dec_seq)
    h_s, hpk_s = _outproj([a[0] for a in attn_s], [a[1] for a in attn_s], conv_s, xs, w_o_b, row(b_o), *ln1, n_s,
                          ones)

    idx, wts, pos, counts = _route(h_p, h_s, w_router[0].T.astype(BF16), router_bias[0])
    n_tok = seq + n_s
    starts, *chunks = _chunks(counts, n_tok * TOP_K)
    dest = _dest_rows(idx, pos, starts)
    dest_w = dest.reshape(TOP_K, n_tok // SC_WINDOW, SC_WINDOW).transpose(1, 0, 2)
    tiles = lambda a: a.reshape(-1, PACK_SLABS, LANES)
    flat = lambda a: a.reshape(-1, LANES)
    x_sorted = _sc_scatter_rows(tiles(hpk_p), tiles(hpk_s), dest_w, jnp.zeros((EXPERT_CH, PACK_SLABS, LANES), U32),
                                n_tok * TOP_K)
    y_sorted = _experts(chunks, flat(x_sorted), w_gate[0], w_up[0], w_down[0])
    y_slots = _sc_gather_rows(tiles(y_sorted), dest.reshape(-1))
    shared = (ws_gate[0].astype(BF16), ws_up[0].astype(BF16), ws_down[0].astype(BF16))
    comb = functools.partial(_combine, wts.T, flat(y_slots), *shared, row(ln2_g), row(ln2_b))
    y_p = comb(h_p, 0)
    y_s = comb(h_s, seq)

    kv_prompt = [_kv_cache(kv_p, g, min(w, seq))[None, None] for g, (w, _) in enumerate(DIL_GROUPS)]
    assert seq >= CONV_WIDTH - 1
    conv_prompt = u_p[-(CONV_WIDTH - 1):]
    kv_s4 = kv_s.reshape(dec_batch, dec_seq, 2, N_GROUPS, HEADS_PER_GROUP, HEAD_DIM)
    kv_sample = [kv_s4[:, :, :, g][None] for g in range(N_GROUPS)]
    conv_sample = u_hist[:, -(CONV_WIDTH - 1):]
    return (y_p[None], y_s.reshape(dec_batch, dec_seq, D_MODEL), *kv_prompt, conv_prompt[None, None],
            *kv_sample, conv_sample[None])
```

```python
import functools

import jax
import jax.numpy as jnp
from jax import lax
from jax.experimental import pallas as pl
from jax.experimental.pallas import tpu as pltpu
from jax.experimental.pallas import tpu_sc as plsc

F32 = jnp.float32
BF16 = jnp.bfloat16
I32 = jnp.int32

D_MODEL = 1024
HEAD_DIM = 64
HEADS_PER_GROUP = 4
GROUP_W = HEADS_PER_GROUP * HEAD_DIM
DIL_GROUPS = ((128, 1), (512, 4), (2048, 16))
N_GROUPS = len(DIL_GROUPS)
ATTN_W = N_GROUPS * GROUP_W
CONV_CH = D_MODEL - ATTN_W
CONV_WIDTH = 31
IN_W = 3 * ATTN_W + 2 * CONV_CH
BAND = 128
N_EXPERTS = 256
TOP_K = 8
N_EXPERT_GROUPS = 8
EXPERTS_PER_GROUP = N_EXPERTS // N_EXPERT_GROUPS
TOPK_GROUPS = 4
EXPERT_FF = 256
ROUTED_SCALE = 2.5
DEPTH = 1
DN_ALPHA = (2 * DEPTH) ** 0.25
LN_EPS = 1e-5
MASKED = -1e30

VMEM_LIMIT_BYTES = 56 * 1024 * 1024


def _params(*sem):
    return pltpu.CompilerParams(dimension_semantics=sem, vmem_limit_bytes=VMEM_LIMIT_BYTES)


def _dot(a, b):
    return jnp.dot(a, b, preferred_element_type=F32)


def _dot_nt(a, b):
    return lax.dot_general(a, b, (((1,), (1,)), ((), ())), preferred_element_type=F32)


def _layer_norm(x, g, b):
    mu = jnp.mean(x, axis=-1, keepdims=True)
    xc = x - mu
    var = jnp.mean(xc * xc, axis=-1, keepdims=True)
    return xc * lax.rsqrt(var + LN_EPS) * g + b


def _silu(x):
    return x * jax.nn.sigmoid(x)


def _alibi_slopes():
    n = N_GROUPS * HEADS_PER_GROUP
    h = jnp.arange(1, n + 1, dtype=F32)
    return (2.0 ** (-8.0 * h / n)).reshape(N_GROUPS, HEADS_PER_GROUP)


LANES = 128


def _proj_kernel(x_ref, w_ref, b_ref, *refs, dils, tm):
    qkv_refs, (u_ref, kv_ref, zs) = refs[:N_GROUPS], refs[N_GROUPS:]
    x = x_ref[...].astype(BF16)
    for part in range(3):
        for g in range(N_GROUPS):
            c0 = part * ATTN_W + g * GROUP_W
            z = _dot(x, w_ref[:, c0:c0 + GROUP_W]) + b_ref[:, c0:c0 + GROUP_W]
            if part > 0:
                kv_ref[:, c0 - ATTN_W:c0 - ATTN_W + GROUP_W] = z
            out, dil = qkv_refs[g], dils[g]
            if dil == 1:
                out[:, part * GROUP_W:(part + 1) * GROUP_W] = z.astype(out.dtype)
                continue
            for half in range(GROUP_W // LANES):
                zs[half] = z[:, half * LANES:(half + 1) * LANES]
            for r in range(dil):
                for half in range(GROUP_W // LANES):
                    c = r * 3 * GROUP_W + part * GROUP_W + half * LANES
                    out[:, c:c + LANES] = zs[half, pl.ds(r, tm // dil, stride=dil), :].astype(out.dtype)
    c0 = 3 * ATTN_W
    a = _dot(x, w_ref[:, c0:c0 + CONV_CH]) + b_ref[:, c0:c0 + CONV_CH]
    gate = _dot(x, w_ref[:, c0 + CONV_CH:]) + b_ref[:, c0 + CONV_CH:]
    u_ref[...] = a * jax.nn.sigmoid(gate)


def _project(x, w_in, b_in, tm, keep, qkv_dtype, dils):
    n = x.shape[0]
    nt = n // tm
    nk = keep // tm
    return pl.pallas_call(
        functools.partial(_proj_kernel, dils=dils, tm=tm),
        grid=(nt,),
        in_specs=[
            pl.BlockSpec((tm, D_MODEL), lambda i: (i, 0)),
            pl.BlockSpec((D_MODEL, IN_W), lambda i: (0, 0)),
            pl.BlockSpec((1, IN_W), lambda i: (0, 0)),
        ],
        out_specs=[pl.BlockSpec((tm // d, d * 3 * GROUP_W), lambda i: (i, 0)) for d in dils] + [
            pl.BlockSpec((tm, CONV_CH), lambda i: (i, 0)),
            pl.BlockSpec((tm, 2 * ATTN_W), lambda i: (jnp.maximum(i - (nt - nk), 0), 0)),
        ],
        out_shape=[jax.ShapeDtypeStruct((n // d, d * 3 * GROUP_W), qkv_dtype) for d in dils] + [
            jax.ShapeDtypeStruct((n, CONV_CH), F32),
            jax.ShapeDtypeStruct((keep, 2 * ATTN_W), F32),
        ],
        scratch_shapes=[pltpu.VMEM((GROUP_W // LANES, tm, LANES), F32)],
        compiler_params=_params("arbitrary"),
        name="proj",
    )(x, w_in, b_in)


def _head_select(parts, rows):
    col = lax.broadcasted_iota(I32, (rows, GROUP_W), 1) // HEAD_DIM
    out = jnp.broadcast_to(parts[-1], (rows, GROUP_W))
    for h in range(HEADS_PER_GROUP - 2, -1, -1):
        out = jnp.where(col == h, parts[h], out)
    return out


def _head_rows(q, rows):
    col = lax.broadcasted_iota(I32, (rows, GROUP_W), 1) // HEAD_DIM
    return jnp.concatenate([jnp.where(col == h, q, jnp.zeros_like(q)) for h in range(HEADS_PER_GROUP)], axis=0)


def _softmax_pv(s, v, rows):
    m = jnp.max(s, axis=-1, keepdims=True)
    e = jnp.exp(s - m)
    l = jnp.sum(e, axis=-1, keepdims=True)
    pv = _dot(e.astype(BF16), v) / l
    lse = m + jnp.log(l)
    o = _head_select([pv[h * rows:(h + 1) * rows] for h in range(HEADS_PER_GROUP)], rows)
    lse_x = _head_select([lse[h * rows:(h + 1) * rows] for h in range(HEADS_PER_GROUP)], rows)
    return o, lse_x


ATTN_QB = 4


def _attn_kernel(q_ref, kp_ref, kc_ref, vp_ref, vc_ref, bias_ref, o_ref, lse_ref):
    b = pl.program_id(1)
    for j in range(ATTN_QB):
        rows = slice(j * BAND, (j + 1) * BAND)
        qm = _head_rows(q_ref[rows, :], BAND)
        if j == 0:
            k = jnp.concatenate([kp_ref[...], kc_ref[rows, :]], axis=0)
            v = jnp.concatenate([vp_ref[...], vc_ref[rows, :]], axis=0)
            bias = bias_ref[jnp.minimum(b, 1)]
        else:
            k = kc_ref[(j - 1) * BAND:(j + 1) * BAND, :]
            v = vc_ref[(j - 1) * BAND:(j + 1) * BAND, :]
            bias = bias_ref[1]
        s = _dot_nt(qm, k) * HEAD_DIM ** -0.5 + bias
        o, lse_x = _softmax_pv(s, v, BAND)
        o_ref[rows, :] = o.astype(o_ref.dtype)
        lse_ref[rows, :] = lse_x


def _prompt_bias(g, dil):
    slopes = _alibi_slopes()[g]
    qi = jnp.arange(BAND)[:, None]
    kj = jnp.arange(2 * BAND)[None, :]
    steps = qi + BAND - kj
    nk = DIL_GROUPS[g][0] // dil
    valid = (steps >= 0) & (steps <= nk)
    bias = -slopes[:, None, None] * (steps * dil).astype(F32)
    inner = jnp.where(valid[None], bias, MASKED)
    first = jnp.where((valid & (kj >= BAND))[None], bias, MASKED)
    return jnp.stack([first, inner]).reshape(2, HEADS_PER_GROUP * BAND, 2 * BAND)


def _attention_prompt(view, g):
    win, dil = DIL_GROUPS[g]
    n_cls = view.shape[0]
    rows = ATTN_QB * BAND
    assert win // dil <= BAND and n_cls % rows == 0 and view.shape[1] == dil * 3 * GROUP_W
    cur = lambda part: pl.BlockSpec((rows, GROUP_W), lambda r, b: (b, r * 3 + part))
    prev = lambda part: pl.BlockSpec((BAND, GROUP_W), lambda r, b: (jnp.maximum(b * ATTN_QB - 1, 0), r * 3 + part))
    return pl.pallas_call(
        _attn_kernel,
        grid=(dil, n_cls // rows),
        in_specs=[cur(0), prev(1), cur(1), prev(2), cur(2),
                  pl.BlockSpec((2, HEADS_PER_GROUP * BAND, 2 * BAND), lambda r, b: (0, 0, 0))],
        out_specs=[pl.BlockSpec((rows, GROUP_W), lambda r, b: (b, r)),
                   pl.BlockSpec((rows, GROUP_W), lambda r, b: (b, r))],
        out_shape=[jax.ShapeDtypeStruct((n_cls, dil * GROUP_W), BF16),
                   jax.ShapeDtypeStruct((n_cls, dil * GROUP_W), F32)],
        compiler_params=_params("arbitrary", "arbitrary"),
        name=f"attn_g{g}",
    )(view, view, view, view, view, _prompt_bias(g, dil))


def _attn_sample_kernel(q_ref, k_ref, v_ref, cache_ref, bias_ref, o_ref, lse_ref, *, n_buf, t):
    pad = jnp.zeros((BAND - t, GROUP_W), F32)
    k_new = jnp.concatenate([k_ref[...], pad], axis=0).astype(BF16)
    v_new = jnp.concatenate([v_ref[...], pad], axis=0).astype(BF16)
    qm = _head_rows(q_ref[...], t).astype(BF16)
    scale = HEAD_DIM ** -0.5
    s_old = _dot(qm, cache_ref[0, 0:GROUP_W, :].astype(BF16)) * scale + bias_ref[:, 0:n_buf]
    s_new = _dot_nt(qm, k_new) * scale + bias_ref[:, n_buf:]
    m = jnp.maximum(jnp.max(s_old, axis=-1, keepdims=True), jnp.max(s_new, axis=-1, keepdims=True))
    e_old = jnp.exp(s_old - m)
    e_new = jnp.exp(s_new - m)
    l = jnp.sum(e_old, axis=-1, keepdims=True) + jnp.sum(e_new, axis=-1, keepdims=True)
    pv = _dot_nt(e_old.astype(BF16), cache_ref[0, GROUP_W:, :].astype(BF16)) + _dot(e_new.astype(BF16), v_new)
    pv = pv / l
    lse = m + jnp.log(l)
    o_ref[...] = _head_select([pv[h * t:(h + 1) * t] for h in range(HEADS_PER_GROUP)], t)
    lse_ref[...] = _head_select([lse[h * t:(h + 1) * t] for h in range(HEADS_PER_GROUP)], t)


def _sample_bias(g, n_buf, t):
    win, dil = DIL_GROUPS[g]
    slopes = _alibi_slopes()[g]
    tq = jnp.arange(t)[:, None]
    j = jnp.arange(n_buf + BAND)[None, :]
    dist = n_buf + tq - j
    valid = (dist >= 0) & (dist % dil == 0) & (dist <= win) & (j < n_buf + t)
    bias = -slopes[:, None, None] * dist.astype(F32)
    return jnp.where(valid[None], bias, MASKED).reshape(HEADS_PER_GROUP * t, n_buf + BAND)


def _attention_sample(qkv, cache_t, g, t):
    nb, n_buf = cache_t.shape[0], cache_t.shape[2]
    col = lambda part: pl.BlockSpec((t, GROUP_W), lambda i: (i, part))
    kern = functools.partial(_attn_sample_kernel, n_buf=n_buf, t=t)
    return pl.pallas_call(
        kern,
        grid=(nb,),
        in_specs=[col(0), col(1), col(2),
                  pl.BlockSpec((1, 2 * GROUP_W, n_buf), lambda i: (i, 0, 0)),
                  pl.BlockSpec((HEADS_PER_GROUP * t, n_buf + BAND), lambda i: (0, 0))],
        out_specs=[pl.BlockSpec((t, GROUP_W), lambda i: (i, 0)),
                   pl.BlockSpec((t, GROUP_W), lambda i: (i, 0))],
        out_shape=[jax.ShapeDtypeStruct((nb * t, GROUP_W), F32),
                   jax.ShapeDtypeStruct((nb * t, GROUP_W), F32)],
        compiler_params=_params("arbitrary"),
        name=f"attn_sample_g{g}",
    )(qkv, qkv, qkv, cache_t, _sample_bias(g, n_buf, t))


CONV_HALO = 32


def _conv_tail(acc, b_ref, g_ref, beta_ref):
    return _silu(_layer_norm(acc + b_ref[...], g_ref[...], beta_ref[...]))


SUBLANES = 8


def _conv_prompt_kernel(halo_ref, u_ref, w_ref, b_ref, g_ref, beta_ref, o_ref, hist, part, *, tm):
    i = pl.program_id(0)
    hist[0:CONV_HALO, :] = jnp.where(i == 0, 0.0, halo_ref[...])
    hist[CONV_HALO:CONV_HALO + tm, :] = u_ref[...]
    hist[CONV_HALO + tm:, :] = jnp.zeros((SUBLANES, CONV_CH), F32)
    off = CONV_HALO - (CONV_WIDTH - 1)
    acc = None
    for s in range(SUBLANES):
        group = None
        for m in range(s, off + CONV_WIDTH, SUBLANES):
            j = m - off
            if j < 0:
                continue
            term = w_ref[j:j + 1, :] * hist[m - s:m - s + tm + SUBLANES, :]
            group = term if group is None else group + term
        part[...] = group
        shifted = part[s:s + tm, :]
        acc = shifted if acc is None else acc + shifted
    o_ref[...] = _conv_tail(acc, b_ref, g_ref, beta_ref).astype(o_ref.dtype)


def _conv_prompt(u, w_dw, b_dw, ln_g, ln_b, tm):
    n = u.shape[0]
    vec = pl.BlockSpec((1, CONV_CH), lambda i: (0, 0))
    return pl.pallas_call(
        functools.partial(_conv_prompt_kernel, tm=tm),
        grid=(n // tm,),
        in_specs=[pl.BlockSpec((CONV_HALO, CONV_CH), lambda i: (jnp.maximum(i * (tm // CONV_HALO) - 1, 0), 0)),
                  pl.BlockSpec((tm, CONV_CH), lambda i: (i, 0)),
                  pl.BlockSpec((CONV_WIDTH, CONV_CH), lambda i: (0, 0)), vec, vec, vec],
        out_specs=pl.BlockSpec((tm, CONV_CH), lambda i: (i, 0)),
        out_shape=jax.ShapeDtypeStruct((n, CONV_CH), BF16),
        scratch_shapes=[pltpu.VMEM((CONV_HALO + tm + SUBLANES, CONV_CH), F32),
                        pltpu.VMEM((tm + SUBLANES, CONV_CH), F32)],
        compiler_params=_params("arbitrary"),
        name="conv_prompt",
    )(u, u, w_dw, b_dw, ln_g, ln_b)


def _conv_sample_kernel(hist_ref, w_ref, b_ref, g_ref, beta_ref, o_ref, *, t):
    acc = jnp.zeros((t, CONV_CH), F32)
    for j in range(CONV_WIDTH):
        acc = acc + w_ref[j:j + 1, :] * hist_ref[0, j:j + t, :]
    o_ref[...] = _conv_tail(acc, b_ref, g_ref, beta_ref)


def _conv_sample(u_hist, w_dw, b_dw, ln_g, ln_b, t):
    nb, rows = u_hist.shape[0], u_hist.shape[1]
    vec = pl.BlockSpec((1, CONV_CH), lambda i: (0, 0))
    return pl.pallas_call(
        functools.partial(_conv_sample_kernel, t=t),
        grid=(nb,),
        in_specs=[pl.BlockSpec((1, rows, CONV_CH), lambda i: (i, 0, 0)),
                  pl.BlockSpec((CONV_WIDTH, CONV_CH), lambda i: (0, 0)), vec, vec, vec],
        out_specs=pl.BlockSpec((t, CONV_CH), lambda i: (i, 0)),
        out_shape=jax.ShapeDtypeStruct((nb * t, CONV_CH), F32),
        compiler_params=_params("arbitrary"),
        name="conv_sample",
    )(u_hist, w_dw, b_dw, ln_g, ln_b)


U32 = jnp.uint32
HALF_D = D_MODEL // 2
PACK_SLABS = HALF_D // LANES


def _token_order(ref, dil, scr, tm):
    if dil == 1:
        return ref[...].astype(F32)
    for r in range(dil):
        for half in range(GROUP_W // LANES):
            c = r * GROUP_W + half * LANES
            scr[half, pl.ds(r, tm // dil, stride=dil), :] = ref[:, c:c + LANES].astype(F32)
    return jnp.concatenate([scr[half] for half in range(GROUP_W // LANES)], axis=1)


def _pack_rows(ref, val):
    rows = val.shape[0]
    bits = lax.bitcast_convert_type(val.astype(BF16).astype(F32), U32)
    packed = lax.shift_right_logical(bits[:, :HALF_D], jnp.uint32(16)) | bits[:, HALF_D:]
    for c in range(PACK_SLABS):
        ref[pl.ds(c, rows, stride=PACK_SLABS), :] = packed[:, c * LANES:(c + 1) * LANES]


def _unpack_rows(ref):
    rows = ref.shape[0] // PACK_SLABS
    u = jnp.concatenate([ref[pl.ds(c, rows, stride=PACK_SLABS), :] for c in range(PACK_SLABS)], axis=1)
    lo = lax.bitcast_convert_type(lax.shift_left(u, jnp.uint32(16)), F32)
    hi = lax.bitcast_convert_type(u & jnp.uint32(0xFFFF0000), F32)
    return lo, hi


def _outproj_kernel(o0, o1, o2, l0, l1, l2, c_ref, x_ref, w_ref, b_ref, g_ref, beta_ref, h_ref, hp_ref, *scr,
                    dils, tm):
    os_ = [_token_order(r, d, scr[2 * g], tm) for g, (r, d) in enumerate(zip((o0, o1, o2), dils))]
    ls = [_token_order(r, d, scr[2 * g + 1], tm) for g, (r, d) in enumerate(zip((l0, l1, l2), dils))]
    m = jnp.maximum(jnp.maximum(ls[0], ls[1]), ls[2])
    es = [jnp.exp(l - m) for l in ls]
    inv = 1.0 / (es[0] + es[1] + es[2])
    mixed = b_ref[...]
    for g in range(N_GROUPS):
        a = (os_[g] * (es[g] * inv)).astype(BF16)
        mixed = mixed + _dot(a, w_ref[g * GROUP_W:(g + 1) * GROUP_W, :])
    mixed = mixed + _dot(c_ref[...].astype(BF16), w_ref[ATTN_W:, :])
    h = _layer_norm(DN_ALPHA * x_ref[...] + mixed, g_ref[...], beta_ref[...])
    h_ref[...] = h
    _pack_rows(hp_ref, h)


def _outproj(os_, lses, conv, x, w_o, b_o, ln_g, ln_b, tm, dils):
    n = x.shape[0]
    grp = [pl.BlockSpec((tm // d, d * GROUP_W), lambda i: (i, 0)) for d in dils]
    vec = pl.BlockSpec((1, D_MODEL), lambda i: (0, 0))
    return pl.pallas_call(
        functools.partial(_outproj_kernel, dils=dils, tm=tm),
        grid=(n // tm,),
        in_specs=grp + grp + [pl.BlockSpec((tm, CONV_CH), lambda i: (i, 0)),
                              pl.BlockSpec((tm, D_MODEL), lambda i: (i, 0)),
                              pl.BlockSpec((D_MODEL, D_MODEL), lambda i: (0, 0)), vec, vec, vec],
        out_specs=[pl.BlockSpec((tm, D_MODEL), lambda i: (i, 0)),
                   pl.BlockSpec((tm * PACK_SLABS, LANES), lambda i: (i, 0))],
        out_shape=[jax.ShapeDtypeStruct((n, D_MODEL), F32),
                   jax.ShapeDtypeStruct((n * PACK_SLABS, LANES), U32)],
        scratch_shapes=[pltpu.VMEM((GROUP_W // LANES, tm, LANES), F32) for _ in range(2 * N_GROUPS)],
        compiler_params=_params("arbitrary"),
        name="outproj",
    )(*os_, *lses, conv, x, w_o, b_o, ln_g, ln_b)


ROUTER_TM = 256


def _first_index(hit, idx, limit, axis):
    return jnp.min(jnp.where(hit, idx, limit), axis=axis, keepdims=True)


def _router_kernel(ha_ref, hb_ref, w_ref, rb_ref, tri_ref, idx_ref, wt_ref, pos_ref, cnt_ref, run, *, tiles_a):
    i = pl.program_id(0)
    tm = ROUTER_TM

    @pl.when(i == 0)
    def _():
        run[...] = jnp.zeros_like(run)

    h = jnp.where(i < tiles_a, ha_ref[...], hb_ref[...])
    logits = _dot_nt(w_ref[...], h.astype(BF16))
    scores = jax.nn.sigmoid(logits)
    biased = scores + rb_ref[...]
    groups = [biased[g * EXPERTS_PER_GROUP:(g + 1) * EXPERTS_PER_GROUP] for g in range(N_EXPERT_GROUPS)]
    ei = lax.broadcasted_iota(I32, (EXPERTS_PER_GROUP, tm), 0).astype(F32)
    gs = []
    for bg in groups:
        m1 = jnp.max(bg, axis=0, keepdims=True)
        f1 = _first_index(bg == m1, ei, float(EXPERTS_PER_GROUP), 0)
        m2 = jnp.max(jnp.where(ei == f1, -jnp.inf, bg), axis=0, keepdims=True)
        gs.append(m1 + m2)
    gs = jnp.concatenate(gs, axis=0)
    gi = lax.broadcasted_iota(I32, gs.shape, 0).astype(F32)
    keep = jnp.zeros(gs.shape, F32)
    cur = gs
    for _ in range(TOPK_GROUPS):
        m = jnp.max(cur, axis=0, keepdims=True)
        f = _first_index(cur == m, gi, float(N_EXPERT_GROUPS), 0)
        hit = gi == f
        keep = jnp.where(hit, 1.0, keep)
        cur = jnp.where(hit, -jnp.inf, cur)
    masked = jnp.concatenate([jnp.where(keep[g:g + 1] > 0.0, bg, -jnp.inf) for g, bg in enumerate(groups)], axis=0)
    xi = lax.broadcasted_iota(I32, (N_EXPERTS, tm), 0).astype(F32)
    cur = masked
    sel = jnp.zeros((N_EXPERTS, tm), F32)
    picks = []
    for _ in range(TOP_K):
        m = jnp.max(cur, axis=0, keepdims=True)
        f = _first_index(cur == m, xi, float(N_EXPERTS), 0)
        hit = xi == f
        picks.append((f, hit))
        sel = jnp.where(hit, 1.0, sel)
        cur = jnp.where(hit, -jnp.inf, cur)
    before = _dot(sel.astype(BF16), tri_ref[...]) + run[...]
    run[...] = run[...] + jnp.sum(sel, axis=1, keepdims=True)
    ws = [jnp.sum(jnp.where(hit, scores, 0.0), axis=0, keepdims=True) for _, hit in picks]
    wsum = ws[0]
    for w in ws[1:]:
        wsum = wsum + w
    for k, (f, hit) in enumerate(picks):
        idx_ref[k:k + 1, :] = f.astype(I32)
        wt_ref[k:k + 1, :] = ws[k] / wsum * ROUTED_SCALE
        pos_ref[k:k + 1, :] = jnp.sum(jnp.where(hit, before, 0.0), axis=0, keepdims=True).astype(I32)
    cnt_ref[...] = jnp.broadcast_to(run[...], cnt_ref.shape).astype(I32)


def _route(ha, hb, w_router_t, router_bias):
    tm = ROUTER_TM
    tiles_a, tiles_b = ha.shape[0] // tm, hb.shape[0] // tm
    spec_a = pl.BlockSpec((tm, D_MODEL), lambda i: (jnp.minimum(i, tiles_a - 1), 0))
    spec_b = pl.BlockSpec((tm, D_MODEL), lambda i: (jnp.maximum(i - tiles_a, 0), 0))
    t = (tiles_a + tiles_b) * tm
    tri = (jnp.arange(tm)[:, None] < jnp.arange(tm)[None, :]).astype(BF16)
    slot = pl.BlockSpec((TOP_K, tm), lambda i: (0, i))
    idx, wts, pos, cnt = pl.pallas_call(
        functools.partial(_router_kernel, tiles_a=tiles_a),
        grid=(t // tm,),
        in_specs=[spec_a, spec_b,
                  pl.BlockSpec((N_EXPERTS, D_MODEL), lambda i: (0, 0)),
                  pl.BlockSpec((N_EXPERTS, 1), lambda i: (0, 0)),
                  pl.BlockSpec((tm, tm), lambda i: (0, 0))],
        out_specs=[slot, slot, slot, pl.BlockSpec((N_EXPERTS, 128), lambda i: (0, 0))],
        out_shape=[jax.ShapeDtypeStruct((TOP_K, t), I32), jax.ShapeDtypeStruct((TOP_K, t), F32),
                   jax.ShapeDtypeStruct((TOP_K, t), I32), jax.ShapeDtypeStruct((N_EXPERTS, 128), I32)],
        scratch_shapes=[pltpu.VMEM((N_EXPERTS, 1), F32)],
        compiler_params=_params("arbitrary"),
        name="router",
    )(ha, hb, w_router_t, router_bias.reshape(N_EXPERTS, 1), tri)
    return idx, wts, pos, cnt[:, 0]


DEST_TM = 1280


def _dest_kernel(idx_ref, pos_ref, starts_ref, dest_ref):
    tm = idx_ref.shape[1]
    ei = lax.broadcasted_iota(I32, (N_EXPERTS, tm), 0)
    starts = starts_ref[...]
    for k in range(TOP_K):
        first = jnp.sum(jnp.where(ei == idx_ref[k:k + 1, :], starts, 0.0), axis=0, keepdims=True)
        dest_ref[k:k + 1, :] = first.astype(I32) + pos_ref[k:k + 1, :]


def _dest_rows(idx, pos, starts):
    t = idx.shape[1]
    tm = DEST_TM if t % DEST_TM == 0 else ROUTER_TM
    assert t % tm == 0
    slot = pl.BlockSpec((TOP_K, tm), lambda i: (0, i))
    return pl.pallas_call(
        _dest_kernel,
        grid=(t // tm,),
        in_specs=[slot, slot, pl.BlockSpec((N_EXPERTS, 1), lambda i: (0, 0))],
        out_specs=slot,
        out_shape=jax.ShapeDtypeStruct((TOP_K, t), I32),
        compiler_params=_params("arbitrary"),
        name="dest_rows",
    )(idx, pos, starts.astype(F32).reshape(N_EXPERTS, 1))


SC_CORES = 2
SC_SUBCORES = 16
SC_WORKERS = SC_CORES * SC_SUBCORES
SC_WINDOW = 32


def _sc_worker():
    return lax.axis_index("s") * SC_CORES + lax.axis_index("c")


def _sc_mesh():
    return plsc.VectorSubcoreMesh(core_axis_name="c", subcore_axis_name="s")


def _sc_scatter_rows(ha, hb, dest_w, zeros, n_out):
    w = SC_WINDOW
    na, nb = ha.shape[0], hb.shape[0]
    spare = zeros.shape[0]
    wa, wb = na // w, nb // w
    assert na % (SC_WORKERS * 2 * w) == 0 and nb % w == 0 and wb <= SC_WORKERS
    per_w = wa // SC_WORKERS
    rows_t = pltpu.VMEM((w,) + ha.shape[1:], ha.dtype)
    idx_t = pltpu.VMEM((TOP_K, w), I32)

    @functools.partial(
        pl.kernel, mesh=_sc_mesh(),
        out_type=jax.ShapeDtypeStruct((n_out + spare,) + ha.shape[1:], ha.dtype),
        scratch_types=[rows_t, rows_t, idx_t, idx_t, pltpu.SemaphoreType.DMA, pltpu.SemaphoreType.DMA],
    )
    def scatter(ha_hbm, hb_hbm, dest_hbm, zeros_hbm, out_hbm, rows0, rows1, idx0, idx1, sem0, sem1):
        wid = _sc_worker()
        bufs = ((rows0, idx0, sem0), (rows1, idx1, sem1))

        @pl.when(wid == SC_WORKERS - 1)
        def _():
            pltpu.sync_copy(zeros_hbm, out_hbm.at[pl.ds(n_out, spare)])

        def load(src_hbm, row0, win, b):
            pltpu.sync_copy(src_hbm.at[pl.ds(pl.multiple_of(row0, 8), w)], bufs[b][0])
            pltpu.sync_copy(dest_hbm.at[win], bufs[b][1])

        def copies(b):
            return [pltpu.make_async_copy(bufs[b][0], out_hbm.at[bufs[b][1].at[k]], bufs[b][2])
                    for k in range(TOP_K)]

        win0 = wid * per_w
        load(ha_hbm, win0 * w, win0, 0)

        @pl.loop(0, per_w, step=2)
        def _(i0):
            for b in range(2):
                i = i0 + b
                for c in copies(b):
                    c.start()

                @pl.when(i + 1 < per_w)
                def _():
                    load(ha_hbm, (win0 + i + 1) * w, win0 + i + 1, 1 - b)

                for c in copies(b):
                    c.wait()

        @pl.when(wid < wb)
        def _():
            load(hb_hbm, wid * w, wa + wid, 0)
            for c in copies(0):
                c.start()
            for c in copies(0):
                c.wait()

    return scatter(ha, hb, dest_w, zeros)


def _sc_gather_rows(table, idx):
    n = idx.shape[0]
    w = SC_WINDOW
    assert n % (SC_WORKERS * 2 * w) == 0
    per_w = n // SC_WORKERS
    nwin = per_w // w
    rows_t = pltpu.VMEM((w,) + table.shape[1:], table.dtype)

    @functools.partial(
        pl.kernel, mesh=_sc_mesh(),
        out_type=jax.ShapeDtypeStruct((n,) + table.shape[1:], table.dtype),
        scratch_types=[pltpu.VMEM((per_w,), I32), rows_t, rows_t, pltpu.SemaphoreType.DMA, pltpu.SemaphoreType.DMA],
    )
    def gather(table_hbm, idx_hbm, out_hbm, idx_v, rows0, rows1, sem0, sem1):
        base = pl.multiple_of(_sc_worker() * per_w, 8)
        pltpu.sync_copy(idx_hbm.at[pl.ds(base, per_w)], idx_v)
        bufs = ((rows0, sem0), (rows1, sem1))
        fetch = lambda i, b: pltpu.make_async_copy(
            table_hbm.at[idx_v.at[pl.ds(pl.multiple_of(i * w, 8), w)]], bufs[b][0], bufs[b][1])
        fetch(0, 0).start()

        @pl.loop(0, nwin, step=2)
        def _(i0):
            for b in range(2):
                i = i0 + b
                fetch(i, b).wait()

                @pl.when(i + 1 < nwin)
                def _():
                    fetch(i + 1, 1 - b).start()

                pltpu.sync_copy(bufs[b][0], out_hbm.at[pl.ds(pl.multiple_of(base + i * w, 8), w)])

    return gather(table, idx)


EXPERT_CH = 256
X_SLOTS = 4


def _expert_kernel(row0_ref, exp_ref, live_ref, fresh_ref, next_ref, par_ref, ordered_ref, xs_ref, wg_ref, wu_ref,
                   wd_ref, ys_ref, xbuf, ybuf, sg, su, sd, wgb, wub, wdb, xsem, ysem, wsem):
    v = pl.program_id(0)
    nv = pl.num_programs(0)
    slot = lax.rem(v, X_SLOTS)
    yslot = lax.rem(v, 3)
    rows = lambda u: pl.ds(pl.multiple_of(row0_ref[u] * PACK_SLABS, PACK_SLABS), EXPERT_CH * PACK_SLABS)
    x_copy = lambda u: pltpu.make_async_copy(xs_ref.at[rows(u)], xbuf.at[lax.rem(u, X_SLOTS)],
                                             xsem.at[lax.rem(u, X_SLOTS)])
    y_copy = lambda u: pltpu.make_async_copy(ybuf.at[lax.rem(u, 3)], ys_ref.at[rows(u)], ysem.at[lax.rem(u, 3)])
    prev1, prev2 = jnp.maximum(v - 1, 0), jnp.maximum(v - 2, 0)
    w_copies = lambda e, s: [pltpu.make_async_copy(w.at[e], stage.at[s], wsem.at[s])
                             for w, stage in ((wg_ref, sg), (wu_ref, su), (wd_ref, sd))]
    spare = pl.ds(ys_ref.shape[0] - EXPERT_CH * PACK_SLABS, EXPERT_CH * PACK_SLABS)

    @pl.when(v == 0)
    def _():
        for c in w_copies(exp_ref[0], 0):
            c.start(priority=1)
        x_copy(0).start()
        for u in range(1, X_SLOTS - 1):
            @pl.when((u < nv) & (live_ref[jnp.minimum(u, nv - 1)] == 1))
            def _():
                x_copy(u).start()
        ybuf[2] = jnp.zeros(ybuf.shape[1:], U32)
        zero = pltpu.make_async_copy(ybuf.at[2], ys_ref.at[spare], ysem.at[2])
        zero.start()
        zero.wait()

    ahead = jnp.minimum(v + X_SLOTS - 1, nv - 1)

    @pl.when((v + X_SLOTS - 1 < nv) & (live_ref[ahead] == 1))
    def _():
        x_copy(ahead).start()

    @pl.when(fresh_ref[v] == 1)
    def _():
        p = par_ref[v]
        for c in w_copies(exp_ref[v], p):
            c.wait()

        @pl.when(next_ref[v] >= 0)
        def _():
            for c in w_copies(next_ref[v], 1 - p):
                c.start(priority=1)

        wgb[...] = sg[p].astype(BF16)
        wub[...] = su[p].astype(BF16)
        wdb[...] = sd[p].astype(BF16)

    @pl.when(live_ref[v] == 1)
    def _():
        x_copy(v).wait()
        x = jnp.concatenate(_unpack_rows(xbuf.at[slot]), axis=1).astype(BF16)
        hid = (_silu(_dot(x, wgb[...])) * _dot(x, wub[...])).astype(BF16)
        _pack_rows(ybuf.at[yslot], _dot(hid, wdb[...]))

    @pl.when((v >= 2) & (live_ref[prev2] == 1) & (ordered_ref[prev1] == 0))
    def _():
        y_copy(prev2).wait()

    @pl.when((v >= 1) & (live_ref[prev1] == 1) & (ordered_ref[v] == 1))
    def _():
        y_copy(prev1).wait()

    @pl.when(live_ref[v] == 1)
    def _():
        y_copy(v).start()

    @pl.when(v == nv - 1)
    def _():
        @pl.when((v >= 1) & (live_ref[prev1] == 1) & (ordered_ref[v] == 0))
        def _():
            y_copy(prev1).wait()

        @pl.when(live_ref[v] == 1)
        def _():
            y_copy(v).wait()


def _chunks(counts, n_rows):
    ch = EXPERT_CH
    nv = n_rows // ch + N_EXPERTS
    ends = jnp.cumsum(counts)
    starts = ends - counts
    nch = (counts + ch - 1) // ch
    cend = jnp.cumsum(nch)
    cstart = cend - nch
    v = jnp.arange(nv, dtype=I32)
    live = v < cend[-1]
    vc = jnp.minimum(v, cend[-1] - 1)
    ids = jnp.arange(N_EXPERTS, dtype=I32)
    e = jnp.minimum(jnp.sum(cend[None, :] <= vc[:, None], axis=1), N_EXPERTS - 1).astype(I32)
    of_e = lambda a: jnp.sum(jnp.where(e[:, None] == ids[None, :], a[None, :], 0), axis=1)
    starts_e, ends_e, count_e = of_e(starts), of_e(ends), of_e(counts)
    c = vc - of_e(cstart)
    is_last = c == of_e(nch) - 1
    row0 = jnp.where(is_last & (count_e >= ch), ends_e - ch, starts_e + c * ch).astype(I32)
    overrun = live & is_last & (count_e < ch)
    ordered = jnp.concatenate([jnp.zeros((1,), bool), overrun[:-1]]) & live
    fresh = jnp.concatenate([jnp.ones((1,), I32), (e[1:] != e[:-1]).astype(I32)])
    order = jnp.cumsum(counts > 0) - 1
    order_e = of_e(order)
    follows = (counts > 0)[None, :] & (order[None, :] == order_e[:, None] + 1)
    nxt = jnp.sum(jnp.where(follows, ids[None, :] + 1, 0), axis=1) - 1
    return (starts.astype(I32), row0, e, live.astype(I32), jnp.where(live, fresh, 0).astype(I32),
            nxt.astype(I32), (order_e % 2).astype(I32), ordered.astype(I32))


def _experts(chunks, xs, w_gate, w_up, w_down):
    ch = EXPERT_CH
    buf = lambda n: pltpu.VMEM((n, ch * PACK_SLABS, LANES), U32)
    anywhere = pl.BlockSpec(memory_space=pl.ANY)
    return pl.pallas_call(
        _expert_kernel,
        grid_spec=pltpu.PrefetchScalarGridSpec(
            num_scalar_prefetch=len(chunks),
            grid=(chunks[0].shape[0],),
            in_specs=[anywhere] * 4,
            out_specs=anywhere,
            scratch_shapes=[buf(X_SLOTS), buf(3),
                            pltpu.VMEM((2, D_MODEL, EXPERT_FF), F32), pltpu.VMEM((2, D_MODEL, EXPERT_FF), F32),
                            pltpu.VMEM((2, EXPERT_FF, D_MODEL), F32),
                            pltpu.VMEM((D_MODEL, EXPERT_FF), BF16), pltpu.VMEM((D_MODEL, EXPERT_FF), BF16),
                            pltpu.VMEM((EXPERT_FF, D_MODEL), BF16),
                            pltpu.SemaphoreType.DMA((X_SLOTS,)), pltpu.SemaphoreType.DMA((3,)),
                            pltpu.SemaphoreType.DMA((2,))],
        ),
        out_shape=jax.ShapeDtypeStruct(xs.shape, U32),
        compiler_params=_params("arbitrary"),
        name="experts",
    )(*chunks, xs, w_gate, w_up, w_down)


COMBINE_TM = 256


def _combine_kernel(wt_ref, h_ref, sg_ref, su_ref, sd_ref, g_ref, beta_ref, *refs):
    y_refs, o_ref = refs[:TOP_K], refs[TOP_K]
    h = h_ref[...]
    hb = h.astype(BF16)
    hid = (_silu(_dot(hb, sg_ref[...])) * _dot(hb, su_ref[...])).astype(BF16)
    acc = DN_ALPHA * h + _dot(hid, sd_ref[...])
    lo = acc[:, :HALF_D]
    hi = acc[:, HALF_D:]
    for k in range(TOP_K):
        y_lo, y_hi = _unpack_rows(y_refs[k])
        w = wt_ref[:, k:k + 1]
        lo = lo + w * y_lo
        hi = hi + w * y_hi
    o_ref[...] = _layer_norm(jnp.concatenate([lo, hi], axis=1), g_ref[...], beta_ref[...])


def _combine(wts_tk, yg, ws_gate, ws_up, ws_down, ln_g, ln_b, h, t0):
    tm = COMBINE_TM
    n = h.shape[0]
    b0 = t0 // tm
    per_slot = yg.shape[0] // (TOP_K * tm * PACK_SLABS)
    rows = pl.BlockSpec((tm, D_MODEL), lambda i: (i, 0))
    slot_rows = [pl.BlockSpec((tm * PACK_SLABS, LANES), lambda i, k=k: (k * per_slot + b0 + i, 0))
                 for k in range(TOP_K)]
    vec = pl.BlockSpec((1, D_MODEL), lambda i: (0, 0))
    return pl.pallas_call(
        _combine_kernel,
        grid=(n // tm,),
        in_specs=[pl.BlockSpec((tm, TOP_K), lambda i: (b0 + i, 0)), rows,
                  pl.BlockSpec((D_MODEL, EXPERT_FF), lambda i: (0, 0)),
                  pl.BlockSpec((D_MODEL, EXPERT_FF), lambda i: (0, 0)),
                  pl.BlockSpec((EXPERT_FF, D_MODEL), lambda i: (0, 0)),
                  vec, vec] + slot_rows,
        out_specs=pl.BlockSpec((tm, D_MODEL), lambda i: (i, 0)),
        out_shape=jax.ShapeDtypeStruct((n, D_MODEL), F32),
        compiler_params=_params("arbitrary"),
        name="combine",
    )(wts_tk, h, ws_gate, ws_up, ws_down, ln_g, ln_b, *([yg] * TOP_K))


def _kv_cache(kv_tail, g, keep):
    k = kv_tail[-keep:, g * GROUP_W:(g + 1) * GROUP_W]
    v = kv_tail[-keep:, ATTN_W + g * GROUP_W:ATTN_W + (g + 1) * GROUP_W]
    return jnp.stack([k, v], axis=1).reshape(keep, 2, HEADS_PER_GROUP, HEAD_DIM)


def kernel(x_prompt, x_sample, cache_kv_w128, cache_kv_w512, cache_kv_w2048, state_conv, w_in, b_in, w_dw, b_dw,
           conv_ln_g, conv_ln_b, w_o, b_o, ln1_g, ln1_b, w_router, router_bias, w_gate, w_up, w_down, ws_gate,
           ws_up, ws_down, ln2_g, ln2_b):
    assert w_in.shape[0] == DEPTH == 1
    batch, seq, _ = x_prompt.shape
    dec_batch, dec_seq, _ = x_sample.shape
    assert batch == 1
    n_s = dec_batch * dec_seq
    caches = (cache_kv_w128, cache_kv_w512, cache_kv_w2048)
    row = lambda a: a[0].reshape(1, -1)

    w_in_b = w_in[0].astype(BF16)
    b_in_r = row(b_in)
    w_o_b = w_o[0].astype(BF16)
    conv_w = (w_dw[0], row(b_dw), row(conv_ln_g), row(conv_ln_b))
    ln1 = (row(ln1_g), row(ln1_b))

    xp = x_prompt[0]
    keep_p = min(max(w for w, _ in DIL_GROUPS), seq)
    dils = tuple(d for _, d in DIL_GROUPS)
    *qkv_p, u_p, kv_p = _project(xp, w_in_b, b_in_r, 512, keep_p, BF16, dils)
    attn_p = [_attention_prompt(qkv_p[g], g) for g in range(N_GROUPS)]
    conv_p = _conv_prompt(u_p, *conv_w, 512)
    h_p, hpk_p = _outproj([a[0] for a in attn_p], [a[1] for a in attn_p], conv_p, xp, w_o_b, row(b_o), *ln1, 512,
                          dils)

    xs = x_sample.reshape(n_s, D_MODEL)
    ones = (1,) * N_GROUPS
    *qkv_s, u_s, kv_s = _project(xs, w_in_b, b_in_r, n_s, n_s, F32, ones)
    caches_t = [jnp.transpose(c[0].reshape(dec_batch, -1, 2 * GROUP_W), (0, 2, 1)) for c in caches]
    attn_s = [_attention_sample(qkv_s[g], caches_t[g], g, dec_seq) for g in range(N_GROUPS)]
    u_hist = jnp.concatenate([state_conv[0], u_s.reshape(dec_batch, dec_seq, CONV_CH)], axis=1)
    conv_s = _conv_sample(u_hist, *conv_w, dec_seq)
    h_s, hpk_s = _outproj([a[0] for a in attn_s], [a[1] for a in attn_s], conv_s, xs, w_o_b, row(b_o), *ln1, n_s,
                          ones)

    idx, wts, pos, counts = _route(h_p, h_s, w_router[0].T.astype(BF16), router_bias[0])
    n_tok = seq + n_s
    starts, *chunks = _chunks(counts, n_tok * TOP_K)
    dest = _dest_rows(idx, pos, starts)
    dest_w = dest.reshape(TOP_K, n_tok // SC_WINDOW, SC_WINDOW).transpose(1, 0, 2)
    tiles = lambda a: a.reshape(-1, PACK_SLABS, LANES)
    flat = lambda a: a.reshape(-1, LANES)
    x_sorted = _sc_scatter_rows(tiles(hpk_p), tiles(hpk_s), dest_w, jnp.zeros((EXPERT_CH, PACK_SLABS, LANES), U32),
                                n_tok * TOP_K)
    y_sorted = _experts(chunks, flat(x_sorted), w_gate[0], w_up[0], w_down[0])
    y_slots = _sc_gather_rows(tiles(y_sorted), dest.reshape(-1))
    shared = (ws_gate[0].astype(BF16), ws_up[0].astype(BF16), ws_down[0].astype(BF16))
    comb = functools.partial(_combine, wts.T, flat(y_slots), *shared, row(ln2_g), row(ln2_b))
    y_p = comb(h_p, 0)
    y_s = comb(h_s, seq)

    kv_prompt = [_kv_cache(kv_p, g, min(w, seq))[None, None] for g, (w, _) in enumerate(DIL_GROUPS)]
    assert seq >= CONV_WIDTH - 1
    conv_prompt = u_p[-(CONV_WIDTH - 1):]
    kv_s4 = kv_s.reshape(dec_batch, dec_seq, 2, N_GROUPS, HEADS_PER_GROUP, HEAD_DIM)
    kv_sample = [kv_s4[:, :, :, g][None] for g in range(N_GROUPS)]
    conv_sample = u_hist[:, -(CONV_WIDTH - 1):]
    return (y_p[None], y_s.reshape(dec_batch, dec_seq, D_MODEL), *kv_prompt, conv_prompt[None, None],
            *kv_sample, conv_sample[None])
```

```python
import functools

import jax
import jax.numpy as jnp
from jax import lax
from jax.experimental import pallas as pl
from jax.experimental.pallas import tpu as pltpu
from jax.experimental.pallas import tpu_sc as plsc

F32 = jnp.float32
BF16 = jnp.bfloat16
I32 = jnp.int32

D_MODEL = 1024
HEAD_DIM = 64
HEADS_PER_GROUP = 4
GROUP_W = HEADS_PER_GROUP * HEAD_DIM
DIL_GROUPS = ((128, 1), (512, 4), (2048, 16))
N_GROUPS = len(DIL_GROUPS)
ATTN_W = N_GROUPS * GROUP_W
CONV_CH = D_MODEL - ATTN_W
CONV_WIDTH = 31
IN_W = 3 * ATTN_W + 2 * CONV_CH
BAND = 128
N_EXPERTS = 256
TOP_K = 8
N_EXPERT_GROUPS = 8
EXPERTS_PER_GROUP = N_EXPERTS // N_EXPERT_GROUPS
TOPK_GROUPS = 4
EXPERT_FF = 256
ROUTED_SCALE = 2.5
DEPTH = 1
DN_ALPHA = (2 * DEPTH) ** 0.25
LN_EPS = 1e-5
MASKED = -1e30

VMEM_LIMIT_BYTES = 56 * 1024 * 1024


def _params(*sem):
    return pltpu.CompilerParams(dimension_semantics=sem, vmem_limit_bytes=VMEM_LIMIT_BYTES)


def _dot(a, b):
    return jnp.dot(a, b, preferred_element_type=F32)


def _dot_nt(a, b):
    return lax.dot_general(a, b, (((1,), (1,)), ((), ())), preferred_element_type=F32)


def _layer_norm(x, g, b):
    mu = jnp.mean(x, axis=-1, keepdims=True)
    xc = x - mu
    var = jnp.mean(xc * xc, axis=-1, keepdims=True)
    return xc * lax.rsqrt(var + LN_EPS) * g + b


def _silu(x):
    return x * jax.nn.sigmoid(x)


def _alibi_slopes():
    n = N_GROUPS * HEADS_PER_GROUP
    h = jnp.arange(1, n + 1, dtype=F32)
    return (2.0 ** (-8.0 * h / n)).reshape(N_GROUPS, HEADS_PER_GROUP)


LANES = 128


def _proj_kernel(x_ref, w_ref, b_ref, *refs, dils, tm):
    qkv_refs, (u_ref, kv_ref, zs) = refs[:N_GROUPS], refs[N_GROUPS:]
    x = x_ref[...].astype(BF16)
    for part in range(3):
        for g in range(N_GROUPS):
            c0 = part * ATTN_W + g * GROUP_W
            z = _dot(x, w_ref[:, c0:c0 + GROUP_W]) + b_ref[:, c0:c0 + GROUP_W]
            if part > 0:
                kv_ref[:, c0 - ATTN_W:c0 - ATTN_W + GROUP_W] = z
            out, dil = qkv_refs[g], dils[g]
            if dil == 1:
                out[:, part * GROUP_W:(part + 1) * GROUP_W] = z.astype(out.dtype)
                continue
            for half in range(GROUP_W // LANES):
                zs[half] = z[:, half * LANES:(half + 1) * LANES]
            for r in range(dil):
                for half in range(GROUP_W // LANES):
                    c = r * 3 * GROUP_W + part * GROUP_W + half * LANES
                    out[:, c:c + LANES] = zs[half, pl.ds(r, tm // dil, stride=dil), :].astype(out.dtype)
    c0 = 3 * ATTN_W
    a = _dot(x, w_ref[:, c0:c0 + CONV_CH]) + b_ref[:, c0:c0 + CONV_CH]
    gate = _dot(x, w_ref[:, c0 + CONV_CH:]) + b_ref[:, c0 + CONV_CH:]
    u_ref[...] = a * jax.nn.sigmoid(gate)


def _project(x, w_in, b_in, tm, keep, qkv_dtype, dils):
    n = x.shape[0]
    nt = n // tm
    nk = keep // tm
    return pl.pallas_call(
        functools.partial(_proj_kernel, dils=dils, tm=tm),
        grid=(nt,),
        in_specs=[
            pl.BlockSpec((tm, D_MODEL), lambda i: (i, 0)),
            pl.BlockSpec((D_MODEL, IN_W), lambda i: (0, 0)),
            pl.BlockSpec((1, IN_W), lambda i: (0, 0)),
        ],
        out_specs=[pl.BlockSpec((tm // d, d * 3 * GROUP_W), lambda i: (i, 0)) for d in dils] + [
            pl.BlockSpec((tm, CONV_CH), lambda i: (i, 0)),
            pl.BlockSpec((tm, 2 * ATTN_W), lambda i: (jnp.maximum(i - (nt - nk), 0), 0)),
        ],
        out_shape=[jax.ShapeDtypeStruct((n // d, d * 3 * GROUP_W), qkv_dtype) for d in dils] + [
            jax.ShapeDtypeStruct((n, CONV_CH), F32),
            jax.ShapeDtypeStruct((keep, 2 * ATTN_W), F32),
        ],
        scratch_shapes=[pltpu.VMEM((GROUP_W // LANES, tm, LANES), F32)],
        compiler_params=_params("arbitrary"),
        name="proj",
    )(x, w_in, b_in)


def _head_select(parts, rows):
    col = lax.broadcasted_iota(I32, (rows, GROUP_W), 1) // HEAD_DIM
    out = jnp.broadcast_to(parts[-1], (rows, GROUP_W))
    for h in range(HEADS_PER_GROUP - 2, -1, -1):
        out = jnp.where(col == h, parts[h], out)
    return out


def _head_rows(q, rows):
    col = lax.broadcasted_iota(I32, (rows, GROUP_W), 1) // HEAD_DIM
    return jnp.concatenate([jnp.where(col == h, q, jnp.zeros_like(q)) for h in range(HEADS_PER_GROUP)], axis=0)


def _softmax_pv(s, v, rows):
    m = jnp.max(s, axis=-1, keepdims=True)
    e = jnp.exp(s - m)
    l = jnp.sum(e, axis=-1, keepdims=True)
    pv = _dot(e.astype(BF16), v) / l
    lse = m + jnp.log(l)
    o = _head_select([pv[h * rows:(h + 1) * rows] for h in range(HEADS_PER_GROUP)], rows)
    lse_x = _head_select([lse[h * rows:(h + 1) * rows] for h in range(HEADS_PER_GROUP)], rows)
    return o, lse_x


ATTN_QB = 4


def _attn_kernel(q_ref, kp_ref, kc_ref, vp_ref, vc_ref, bias_ref, o_ref, lse_ref):
    b = pl.program_id(1)
    for j in range(ATTN_QB):
        rows = slice(j * BAND, (j + 1) * BAND)
        qm = _head_rows(q_ref[rows, :], BAND)
        if j == 0:
            k = jnp.concatenate([kp_ref[...], kc_ref[rows, :]], axis=0)
            v = jnp.concatenate([vp_ref[...], vc_ref[rows, :]], axis=0)
            bias = bias_ref[jnp.minimum(b, 1)]
        else:
            k = kc_ref[(j - 1) * BAND:(j + 1) * BAND, :]
            v = vc_ref[(j - 1) * BAND:(j + 1) * BAND, :]
            bias = bias_ref[1]
        s = _dot_nt(qm, k) * HEAD_DIM ** -0.5 + bias
        o, lse_x = _softmax_pv(s, v, BAND)
        o_ref[rows, :] = o.astype(o_ref.dtype)
        lse_ref[rows, :] = lse_x


def _prompt_bias(g, dil):
    slopes = _alibi_slopes()[g]
    qi = jnp.arange(BAND)[:, None]
    kj = jnp.arange(2 * BAND)[None, :]
    steps = qi + BAND - kj
    nk = DIL_GROUPS[g][0] // dil
    valid = (steps >= 0) & (steps <= nk)
    bias = -slopes[:, None, None] * (steps * dil).astype(F32)
    inner = jnp.where(valid[None], bias, MASKED)
    first = jnp.where((valid & (kj >= BAND))[None], bias, MASKED)
    return jnp.stack([first, inner]).reshape(2, HEADS_PER_GROUP * BAND, 2 * BAND)


def _attention_prompt(view, g):
    win, dil = DIL_GROUPS[g]
    n_cls = view.shape[0]
    rows = ATTN_QB * BAND
    assert win // dil <= BAND and n_cls % rows == 0 and view.shape[1] == dil * 3 * GROUP_W
    cur = lambda part: pl.BlockSpec((rows, GROUP_W), lambda r, b: (b, r * 3 + part))
    prev = lambda part: pl.BlockSpec((BAND, GROUP_W), lambda r, b: (jnp.maximum(b * ATTN_QB - 1, 0), r * 3 + part))
    return pl.pallas_call(
        _attn_kernel,
        grid=(dil, n_cls // rows),
        in_specs=[cur(0), prev(1), cur(1), prev(2), cur(2),
                  pl.BlockSpec((2, HEADS_PER_GROUP * BAND, 2 * BAND), lambda r, b: (0, 0, 0))],
        out_specs=[pl.BlockSpec((rows, GROUP_W), lambda r, b: (b, r)),
                   pl.BlockSpec((rows, GROUP_W), lambda r, b: (b, r))],
        out_shape=[jax.ShapeDtypeStruct((n_cls, dil * GROUP_W), BF16),
                   jax.ShapeDtypeStruct((n_cls, dil * GROUP_W), F32)],
        compiler_params=_params("arbitrary", "arbitrary"),
        name=f"attn_g{g}",
    )(view, view, view, view, view, _prompt_bias(g, dil))


def _attn_sample_kernel(q_ref, k_ref, v_ref, cache_ref, bias_ref, o_ref, lse_ref, *, n_buf, t):
    pad = jnp.zeros((BAND - t, GROUP_W), F32)
    k_new = jnp.concatenate([k_ref[...], pad], axis=0).astype(BF16)
    v_new = jnp.concatenate([v_ref[...], pad], axis=0).astype(BF16)
    qm = _head_rows(q_ref[...], t).astype(BF16)
    scale = HEAD_DIM ** -0.5
    s_old = _dot(qm, cache_ref[0, 0:GROUP_W, :].astype(BF16)) * scale + bias_ref[:, 0:n_buf]
    s_new = _dot_nt(qm, k_new) * scale + bias_ref[:, n_buf:]
    m = jnp.maximum(jnp.max(s_old, axis=-1, keepdims=True), jnp.max(s_new, axis=-1, keepdims=True))
    e_old = jnp.exp(s_old - m)
    e_new = jnp.exp(s_new - m)
    l = jnp.sum(e_old, axis=-1, keepdims=True) + jnp.sum(e_new, axis=-1, keepdims=True)
    pv = _dot_nt(e_old.astype(BF16), cache_ref[0, GROUP_W:, :].astype(BF16)) + _dot(e_new.astype(BF16), v_new)
    pv = pv / l
    lse = m + jnp.log(l)
    o_ref[...] = _head_select([pv[h * t:(h + 1) * t] for h in range(HEADS_PER_GROUP)], t)
    lse_ref[...] = _head_select([lse[h * t:(h + 1) * t] for h in range(HEADS_PER_GROUP)], t)


def _sample_bias(g, n_buf, t):
    win, dil = DIL_GROUPS[g]
    slopes = _alibi_slopes()[g]
    tq = jnp.arange(t)[:, None]
    j = jnp.arange(n_buf + BAND)[None, :]
    dist = n_buf + tq - j
    valid = (dist >= 0) & (dist % dil == 0) & (dist <= win) & (j < n_buf + t)
    bias = -slopes[:, None, None] * dist.astype(F32)
    return jnp.where(valid[None], bias, MASKED).reshape(HEADS_PER_GROUP * t, n_buf + BAND)


def _attention_sample(qkv, cache_t, g, t):
    nb, n_buf = cache_t.shape[0], cache_t.shape[2]
    col = lambda part: pl.BlockSpec((t, GROUP_W), lambda i: (i, part))
    kern = functools.partial(_attn_sample_kernel, n_buf=n_buf, t=t)
    return pl.pallas_call(
        kern,
        grid=(nb,),
        in_specs=[col(0), col(1), col(2),
                  pl.BlockSpec((1, 2 * GROUP_W, n_buf), lambda i: (i, 0, 0)),
                  pl.BlockSpec((HEADS_PER_GROUP * t, n_buf + BAND), lambda i: (0, 0))],
        out_specs=[pl.BlockSpec((t, GROUP_W), lambda i: (i, 0)),
                   pl.BlockSpec((t, GROUP_W), lambda i: (i, 0))],
        out_shape=[jax.ShapeDtypeStruct((nb * t, GROUP_W), F32),
                   jax.ShapeDtypeStruct((nb * t, GROUP_W), F32)],
        compiler_params=_params("arbitrary"),
        name=f"attn_sample_g{g}",
    )(qkv, qkv, qkv, cache_t, _sample_bias(g, n_buf, t))


CONV_HALO = 32


def _conv_tail(acc, b_ref, g_ref, beta_ref):
    return _silu(_layer_norm(acc + b_ref[...], g_ref[...], beta_ref[...]))


SUBLANES = 8


def _conv_prompt_kernel(halo_ref, u_ref, w_ref, b_ref, g_ref, beta_ref, o_ref, hist, part, *, tm):
    i = pl.program_id(0)
    hist[0:CONV_HALO, :] = jnp.where(i == 0, 0.0, halo_ref[...])
    hist[CONV_HALO:CONV_HALO + tm, :] = u_ref[...]
    hist[CONV_HALO + tm:, :] = jnp.zeros((SUBLANES, CONV_CH), F32)
    off = CONV_HALO - (CONV_WIDTH - 1)
    acc = None
    for s in range(SUBLANES):
        group = None
        for m in range(s, off + CONV_WIDTH, SUBLANES):
            j = m - off
            if j < 0:
                continue
            term = w_ref[j:j + 1, :] * hist[m - s:m - s + tm + SUBLANES, :]
            group = term if group is None else group + term
        part[...] = group
        shifted = part[s:s + tm, :]
        acc = shifted if acc is None else acc + shifted
    o_ref[...] = _conv_tail(acc, b_ref, g_ref, beta_ref).astype(o_ref.dtype)


def _conv_prompt(u, w_dw, b_dw, ln_g, ln_b, tm):
    n = u.shape[0]
    vec = pl.BlockSpec((1, CONV_CH), lambda i: (0, 0))
    return pl.pallas_call(
        functools.partial(_conv_prompt_kernel, tm=tm),
        grid=(n // tm,),
        in_specs=[pl.BlockSpec((CONV_HALO, CONV_CH), lambda i: (jnp.maximum(i * (tm // CONV_HALO) - 1, 0), 0)),
                  pl.BlockSpec((tm, CONV_CH), lambda i: (i, 0)),
                  pl.BlockSpec((CONV_WIDTH, CONV_CH), lambda i: (0, 0)), vec, vec, vec],
        out_specs=pl.BlockSpec((tm, CONV_CH), lambda i: (i, 0)),
        out_shape=jax.ShapeDtypeStruct((n, CONV_CH), BF16),
        scratch_shapes=[pltpu.VMEM((CONV_HALO + tm + SUBLANES, CONV_CH), F32),
                        pltpu.VMEM((tm + SUBLANES, CONV_CH), F32)],
        compiler_params=_params("arbitrary"),
        name="conv_prompt",
    )(u, u, w_dw, b_dw, ln_g, ln_b)


def _conv_sample_kernel(hist_ref, w_ref, b_ref, g_ref, beta_ref, o_ref, *, t):
    acc = jnp.zeros((t, CONV_CH), F32)
    for j in range(CONV_WIDTH):
        acc = acc + w_ref[j:j + 1, :] * hist_ref[0, j:j + t, :]
    o_ref[...] = _conv_tail(acc, b_ref, g_ref, beta_ref)


def _conv_sample(u_hist, w_dw, b_dw, ln_g, ln_b, t):
    nb, rows = u_hist.shape[0], u_hist.shape[1]
    vec = pl.BlockSpec((1, CONV_CH), lambda i: (0, 0))
    return pl.pallas_call(
        functools.partial(_conv_sample_kernel, t=t),
        grid=(nb,),
        in_specs=[pl.BlockSpec((1, rows, CONV_CH), lambda i: (i, 0, 0)),
                  pl.BlockSpec((CONV_WIDTH, CONV_CH), lambda i: (0, 0)), vec, vec, vec],
        out_specs=pl.BlockSpec((t, CONV_CH), lambda i: (i, 0)),
        out_shape=jax.ShapeDtypeStruct((nb * t, CONV_CH), F32),
        compiler_params=_params("arbitrary"),
        name="conv_sample",
    )(u_hist, w_dw, b_dw, ln_g, ln_b)


U32 = jnp.uint32
HALF_D = D_MODEL // 2
PACK_SLABS = HALF_D // LANES


def _token_order(ref, dil, scr, tm):
    if dil == 1:
        return ref[...].astype(F32)
    for r in range(dil):
        for half in range(GROUP_W // LANES):
            c = r * GROUP_W + half * LANES
            scr[half, pl.ds(r, tm // dil, stride=dil), :] = ref[:, c:c + LANES].astype(F32)
    return jnp.concatenate([scr[half] for half in range(GROUP_W // LANES)], axis=1)


def _pack_rows(ref, val):
    rows = val.shape[0]
    bits = lax.bitcast_convert_type(val.astype(BF16).astype(F32), U32)
    packed = lax.shift_right_logical(bits[:, :HALF_D], jnp.uint32(16)) | bits[:, HALF_D:]
    for c in range(PACK_SLABS):
        ref[pl.ds(c, rows, stride=PACK_SLABS), :] = packed[:, c * LANES:(c + 1) * LANES]


def _unpack_rows(ref):
    rows = ref.shape[0] // PACK_SLABS
    u = jnp.concatenate([ref[pl.ds(c, rows, stride=PACK_SLABS), :] for c in range(PACK_SLABS)], axis=1)
    lo = lax.bitcast_convert_type(lax.shift_left(u, jnp.uint32(16)), F32)
    hi = lax.bitcast_convert_type(u & jnp.uint32(0xFFFF0000), F32)
    return lo, hi


def _outproj_kernel(o0, o1, o2, l0, l1, l2, c_ref, x_ref, w_ref, b_ref, g_ref, beta_ref, h_ref, hp_ref, *scr,
                    dils, tm):
    os_ = [_token_order(r, d, scr[2 * g], tm) for g, (r, d) in enumerate(zip((o0, o1, o2), dils))]
    ls = [_token_order(r, d, scr[2 * g + 1], tm) for g, (r, d) in enumerate(zip((l0, l1, l2), dils))]
    m = jnp.maximum(jnp.maximum(ls[0], ls[1]), ls[2])
    es = [jnp.exp(l - m) for l in ls]
    inv = 1.0 / (es[0] + es[1] + es[2])
    mixed = b_ref[...]
    for g in range(N_GROUPS):
        a = (os_[g] * (es[g] * inv)).astype(BF16)
        mixed = mixed + _dot(a, w_ref[g * GROUP_W:(g + 1) * GROUP_W, :])
    mixed = mixed + _dot(c_ref[...].astype(BF16), w_ref[ATTN_W:, :])
    h = _layer_norm(DN_ALPHA * x_ref[...] + mixed, g_ref[...], beta_ref[...])
    h_ref[...] = h
    _pack_rows(hp_ref, h)


def _outproj(os_, lses, conv, x, w_o, b_o, ln_g, ln_b, tm, dils):
    n = x.shape[0]
    grp = [pl.BlockSpec((tm // d, d * GROUP_W), lambda i: (i, 0)) for d in dils]
    vec = pl.BlockSpec((1, D_MODEL), lambda i: (0, 0))
    return pl.pallas_call(
        functools.partial(_outproj_kernel, dils=dils, tm=tm),
        grid=(n // tm,),
        in_specs=grp + grp + [pl.BlockSpec((tm, CONV_CH), lambda i: (i, 0)),
                              pl.BlockSpec((tm, D_MODEL), lambda i: (i, 0)),
                              pl.BlockSpec((D_MODEL, D_MODEL), lambda i: (0, 0)), vec, vec, vec],
        out_specs=[pl.BlockSpec((tm, D_MODEL), lambda i: (i, 0)),
                   pl.BlockSpec((tm * PACK_SLABS, LANES), lambda i: (i, 0))],
        out_shape=[jax.ShapeDtypeStruct((n, D_MODEL), F32),
                   jax.ShapeDtypeStruct((n * PACK_SLABS, LANES), U32)],
        scratch_shapes=[pltpu.VMEM((GROUP_W // LANES, tm, LANES), F32) for _ in range(2 * N_GROUPS)],
        compiler_params=_params("arbitrary"),
        name="outproj",
    )(*os_, *lses, conv, x, w_o, b_o, ln_g, ln_b)


ROUTER_TM = 256


def _first_index(hit, idx, limit, axis):
    return jnp.min(jnp.where(hit, idx, limit), axis=axis, keepdims=True)


def _router_kernel(ha_ref, hb_ref, w_ref, rb_ref, tri_ref, idx_ref, wt_ref, pos_ref, cnt_ref, run, *, tiles_a):
    i = pl.program_id(0)
    tm = ROUTER_TM

    @pl.when(i == 0)
    def _():
        run[...] = jnp.zeros_like(run)

    h = jnp.where(i < tiles_a, ha_ref[...], hb_ref[...])
    logits = _dot_nt(w_ref[...], h.astype(BF16))
    scores = jax.nn.sigmoid(logits)
    biased = scores + rb_ref[...]
    groups = [biased[g * EXPERTS_PER_GROUP:(g + 1) * EXPERTS_PER_GROUP] for g in range(N_EXPERT_GROUPS)]
    ei = lax.broadcasted_iota(I32, (EXPERTS_PER_GROUP, tm), 0).astype(F32)
    gs = []
    for bg in groups:
        m1 = jnp.max(bg, axis=0, keepdims=True)
        f1 = _first_index(bg == m1, ei, float(EXPERTS_PER_GROUP), 0)
        m2 = jnp.max(jnp.where(ei == f1, -jnp.inf, bg), axis=0, keepdims=True)
        gs.append(m1 + m2)
    gs = jnp.concatenate(gs, axis=0)
    gi = lax.broadcasted_iota(I32, gs.shape, 0).astype(F32)
    keep = jnp.zeros(gs.shape, F32)
    cur = gs
    for _ in range(TOPK_GROUPS):
        m = jnp.max(cur, axis=0, keepdims=True)
        f = _first_index(cur == m, gi, float(N_EXPERT_GROUPS), 0)
        hit = gi == f
        keep = jnp.where(hit, 1.0, keep)
        cur = jnp.where(hit, -jnp.inf, cur)
    masked = jnp.concatenate([jnp.where(keep[g:g + 1] > 0.0, bg, -jnp.inf) for g, bg in enumerate(groups)], axis=0)
    xi = lax.broadcasted_iota(I32, (N_EXPERTS, tm), 0).astype(F32)
    cur = masked
    sel = jnp.zeros((N_EXPERTS, tm), F32)
    picks = []
    for _ in range(TOP_K):
        m = jnp.max(cur, axis=0, keepdims=True)
        f = _first_index(cur == m, xi, float(N_EXPERTS), 0)
        hit = xi == f
        picks.append((f, hit))
        sel = jnp.where(hit, 1.0, sel)
        cur = jnp.where(hit, -jnp.inf, cur)
    before = _dot(sel.astype(BF16), tri_ref[...]) + run[...]
    run[...] = run[...] + jnp.sum(sel, axis=1, keepdims=True)
    ws = [jnp.sum(jnp.where(hit, scores, 0.0), axis=0, keepdims=True) for _, hit in picks]
    wsum = ws[0]
    for w in ws[1:]:
        wsum = wsum + w
    for k, (f, hit) in enumerate(picks):
        idx_ref[k:k + 1, :] = f.astype(I32)
        wt_ref[k:k + 1, :] = ws[k] / wsum * ROUTED_SCALE
        pos_ref[k:k + 1, :] = jnp.sum(jnp.where(hit, before, 0.0), axis=0, keepdims=True).astype(I32)
    cnt_ref[...] = jnp.broadcast_to(run[...], cnt_ref.shape).astype(I32)


def _route(ha, hb, w_router_t, router_bias):
    tm = ROUTER_TM
    tiles_a, tiles_b = ha.shape[0] // tm, hb.shape[0] // tm
    spec_a = pl.BlockSpec((tm, D_MODEL), lambda i: (jnp.minimum(i, tiles_a - 1), 0))
    spec_b = pl.BlockSpec((tm, D_MODEL), lambda i: (jnp.maximum(i - tiles_a, 0), 0))
    t = (tiles_a + tiles_b) * tm
    tri = (jnp.arange(tm)[:, None] < jnp.arange(tm)[None, :]).astype(BF16)
    slot = pl.BlockSpec((TOP_K, tm), lambda i: (0, i))
    idx, wts, pos, cnt = pl.pallas_call(
        functools.partial(_router_kernel, tiles_a=tiles_a),
        grid=(t // tm,),
        in_specs=[spec_a, spec_b,
                  pl.BlockSpec((N_EXPERTS, D_MODEL), lambda i: (0, 0)),
                  pl.BlockSpec((N_EXPERTS, 1), lambda i: (0, 0)),
                  pl.BlockSpec((tm, tm), lambda i: (0, 0))],
        out_specs=[slot, slot, slot, pl.BlockSpec((N_EXPERTS, 128), lambda i: (0, 0))],
        out_shape=[jax.ShapeDtypeStruct((TOP_K, t), I32), jax.ShapeDtypeStruct((TOP_K, t), F32),
                   jax.ShapeDtypeStruct((TOP_K, t), I32), jax.ShapeDtypeStruct((N_EXPERTS, 128), I32)],
        scratch_shapes=[pltpu.VMEM((N_EXPERTS, 1), F32)],
        compiler_params=_params("arbitrary"),
        name="router",
    )(ha, hb, w_router_t, router_bias.reshape(N_EXPERTS, 1), tri)
    return idx, wts, pos, cnt[:, 0]


DEST_TM = 1280


def _dest_kernel(idx_ref, pos_ref, starts_ref, dest_ref):
    tm = idx_ref.shape[1]
    ei = lax.broadcasted_iota(I32, (N_EXPERTS, tm), 0)
    starts = starts_ref[...]
    for k in range(TOP_K):
        first = jnp.sum(jnp.where(ei == idx_ref[k:k + 1, :], starts, 0.0), axis=0, keepdims=True)
        dest_ref[k:k + 1, :] = first.astype(I32) + pos_ref[k:k + 1, :]


def _dest_rows(idx, pos, starts):
    t = idx.shape[1]
    tm = DEST_TM if t % DEST_TM == 0 else ROUTER_TM
    assert t % tm == 0
    slot = pl.BlockSpec((TOP_K, tm), lambda i: (0, i))
    return pl.pallas_call(
        _dest_kernel,
        grid=(t // tm,),
        in_specs=[slot, slot, pl.BlockSpec((N_EXPERTS, 1), lambda i: (0, 0))],
        out_specs=slot,
        out_shape=jax.ShapeDtypeStruct((TOP_K, t), I32),
        compiler_params=_params("arbitrary"),
        name="dest_rows",
    )(idx, pos, starts.astype(F32).reshape(N_EXPERTS, 1))


SC_CORES = 2
SC_SUBCORES = 16
SC_WORKERS = SC_CORES * SC_SUBCORES
SC_WINDOW = 32


def _sc_worker():
    return lax.axis_index("s") * SC_CORES + lax.axis_index("c")


def _sc_mesh():
    return plsc.VectorSubcoreMesh(core_axis_name="c", subcore_axis_name="s")


def _sc_scatter_rows(ha, hb, dest_w, zeros, n_out):
    w = SC_WINDOW
    na, nb = ha.shape[0], hb.shape[0]
    spare = zeros.shape[0]
    wa, wb = na // w, nb // w
    assert na % (SC_WORKERS * 2 * w) == 0 and nb % w == 0 and wb <= SC_WORKERS
    per_w = wa // SC_WORKERS
    rows_t = pltpu.VMEM((w,) + ha.shape[1:], ha.dtype)
    idx_t = pltpu.VMEM((TOP_K, w), I32)

    @functools.partial(
        pl.kernel, mesh=_sc_mesh(),
        out_type=jax.ShapeDtypeStruct((n_out + spare,) + ha.shape[1:], ha.dtype),
        scratch_types=[rows_t, rows_t, idx_t, idx_t, pltpu.SemaphoreType.DMA, pltpu.SemaphoreType.DMA],
    )
    def scatter(ha_hbm, hb_hbm, dest_hbm, zeros_hbm, out_hbm, rows0, rows1, idx0, idx1, sem0, sem1):
        wid = _sc_worker()
        bufs = ((rows0, idx0, sem0), (rows1, idx1, sem1))

        @pl.when(wid == SC_WORKERS - 1)
        def _():
            pltpu.sync_copy(zeros_hbm, out_hbm.at[pl.ds(n_out, spare)])

        def load(src_hbm, row0, win, b):
            pltpu.sync_copy(src_hbm.at[pl.ds(pl.multiple_of(row0, 8), w)], bufs[b][0])
            pltpu.sync_copy(dest_hbm.at[win], bufs[b][1])

        def copies(b):
            return [pltpu.make_async_copy(bufs[b][0], out_hbm.at[bufs[b][1].at[k]], bufs[b][2])
                    for k in range(TOP_K)]

        win0 = wid * per_w
        load(ha_hbm, win0 * w, win0, 0)

        @pl.loop(0, per_w, step=2)
        def _(i0):
            for b in range(2):
                i = i0 + b
                for c in copies(b):
                    c.start()

                @pl.when(i + 1 < per_w)
                def _():
                    load(ha_hbm, (win0 + i + 1) * w, win0 + i + 1, 1 - b)

                for c in copies(b):
                    c.wait()

        @pl.when(wid < wb)
        def _():
            load(hb_hbm, wid * w, wa + wid, 0)
            for c in copies(0):
                c.start()
            for c in copies(0):
                c.wait()

    return scatter(ha, hb, dest_w, zeros)


def _sc_gather_rows(table, idx):
    n = idx.shape[0]
    w = SC_WINDOW
    assert n % (SC_WORKERS * 2 * w) == 0
    per_w = n // SC_WORKERS
    nwin = per_w // w
    rows_t = pltpu.VMEM((w,) + table.shape[1:], table.dtype)

    @functools.partial(
        pl.kernel, mesh=_sc_mesh(),
        out_type=jax.ShapeDtypeStruct((n,) + table.shape[1:], table.dtype),
        scratch_types=[pltpu.VMEM((per_w,), I32), rows_t, rows_t, pltpu.SemaphoreType.DMA, pltpu.SemaphoreType.DMA],
    )
    def gather(table_hbm, idx_hbm, out_hbm, idx_v, rows0, rows1, sem0, sem1):
        base = pl.multiple_of(_sc_worker() * per_w, 8)
        pltpu.sync_copy(idx_hbm.at[pl.ds(base, per_w)], idx_v)
        bufs = ((rows0, sem0), (rows1, sem1))
        fetch = lambda i, b: pltpu.make_async_copy(
            table_hbm.at[idx_v.at[pl.ds(pl.multiple_of(i * w, 8), w)]], bufs[b][0], bufs[b][1])
        fetch(0, 0).start()

        @pl.loop(0, nwin, step=2)
        def _(i0):
            for b in range(2):
                i = i0 + b
                fetch(i, b).wait()

                @pl.when(i + 1 < nwin)
                def _():
                    fetch(i + 1, 1 - b).start()

                pltpu.sync_copy(bufs[b][0], out_hbm.at[pl.ds(pl.multiple_of(base + i * w, 8), w)])

    return gather(table, idx)


EXPERT_CH = 256
X_SLOTS = 4
W_SLOTS = 3


def _expert_kernel(row0_ref, exp_ref, live_ref, fresh_ref, next1_ref, next2_ref, wslot_ref, ordered_ref, xs_ref,
                   wg_ref, wu_ref, wd_ref, ys_ref, xbuf, ybuf, sg, su, sd, wgb, wub, wdb, xsem, ysem, wsem):
    v = pl.program_id(0)
    nv = pl.num_programs(0)
    slot = lax.rem(v, X_SLOTS)
    yslot = lax.rem(v, 3)
    rows = lambda u: pl.ds(pl.multiple_of(row0_ref[u] * PACK_SLABS, PACK_SLABS), EXPERT_CH * PACK_SLABS)
    x_copy = lambda u: pltpu.make_async_copy(xs_ref.at[rows(u)], xbuf.at[lax.rem(u, X_SLOTS)],
                                             xsem.at[lax.rem(u, X_SLOTS)])
    y_copy = lambda u: pltpu.make_async_copy(ybuf.at[lax.rem(u, 3)], ys_ref.at[rows(u)], ysem.at[lax.rem(u, 3)])
    prev1, prev2 = jnp.maximum(v - 1, 0), jnp.maximum(v - 2, 0)
    w_copies = lambda e, s: [pltpu.make_async_copy(w.at[e], stage.at[s], wsem.at[s])
                             for w, stage in ((wg_ref, sg), (wu_ref, su), (wd_ref, sd))]
    spare = pl.ds(ys_ref.shape[0] - EXPERT_CH * PACK_SLABS, EXPERT_CH * PACK_SLABS)

    @pl.when(v == 0)
    def _():
        for c in w_copies(exp_ref[0], 0):
            c.start(priority=1)

        @pl.when(next1_ref[0] >= 0)
        def _():
            for c in w_copies(next1_ref[0], 1):
                c.start(priority=1)

        x_copy(0).start()
        for u in range(1, X_SLOTS - 1):
            @pl.when((u < nv) & (live_ref[jnp.minimum(u, nv - 1)] == 1))
            def _():
                x_copy(u).start()
        ybuf[2] = jnp.zeros(ybuf.shape[1:], U32)
        zero = pltpu.make_async_copy(ybuf.at[2], ys_ref.at[spare], ysem.at[2])
        zero.start()
        zero.wait()

    ahead = jnp.minimum(v + X_SLOTS - 1, nv - 1)

    @pl.when((v + X_SLOTS - 1 < nv) & (live_ref[ahead] == 1))
    def _():
        x_copy(ahead).start()

    @pl.when(fresh_ref[v] == 1)
    def _():
        p = wslot_ref[v]
        for c in w_copies(exp_ref[v], p):
            c.wait()

        @pl.when(next2_ref[v] >= 0)
        def _():
            for c in w_copies(next2_ref[v], lax.rem(p + 2, W_SLOTS)):
                c.start(priority=1)

        wgb[...] = sg[p].astype(BF16)
        wub[...] = su[p].astype(BF16)
        wdb[...] = sd[p].astype(BF16)

    @pl.when(live_ref[v] == 1)
    def _():
        x_copy(v).wait()
        x = jnp.concatenate(_unpack_rows(xbuf.at[slot]), axis=1).astype(BF16)
        hid = (_silu(_dot(x, wgb[...])) * _dot(x, wub[...])).astype(BF16)
        _pack_rows(ybuf.at[yslot], _dot(hid, wdb[...]))

    @pl.when((v >= 2) & (live_ref[prev2] == 1) & (ordered_ref[prev1] == 0))
    def _():
        y_copy(prev2).wait()

    @pl.when((v >= 1) & (live_ref[prev1] == 1) & (ordered_ref[v] == 1))
    def _():
        y_copy(prev1).wait()

    @pl.when(live_ref[v] == 1)
    def _():
        y_copy(v).start()

    @pl.when(v == nv - 1)
    def _():
        @pl.when((v >= 1) & (live_ref[prev1] == 1) & (ordered_ref[v] == 0))
        def _():
            y_copy(prev1).wait()

        @pl.when(live_ref[v] == 1)
        def _():
            y_copy(v).wait()


def _chunks(counts, n_rows):
    ch = EXPERT_CH
    nv = n_rows // ch + N_EXPERTS
    ends = jnp.cumsum(counts)
    starts = ends - counts
    nch = (counts + ch - 1) // ch
    cend = jnp.cumsum(nch)
    cstart = cend - nch
    v = jnp.arange(nv, dtype=I32)
    live = v < cend[-1]
    vc = jnp.minimum(v, cend[-1] - 1)
    ids = jnp.arange(N_EXPERTS, dtype=I32)
    e = jnp.minimum(jnp.sum(cend[None, :] <= vc[:, None], axis=1), N_EXPERTS - 1).astype(I32)
    of_e = lambda a: jnp.sum(jnp.where(e[:, None] == ids[None, :], a[None, :], 0), axis=1)
    starts_e, ends_e, count_e = of_e(starts), of_e(ends), of_e(counts)
    c = vc - of_e(cstart)
    is_last = c == of_e(nch) - 1
    row0 = jnp.where(is_last & (count_e >= ch), ends_e - ch, starts_e + c * ch).astype(I32)
    overrun = live & is_last & (count_e < ch)
    ordered = jnp.concatenate([jnp.zeros((1,), bool), overrun[:-1]]) & live
    fresh = jnp.concatenate([jnp.ones((1,), I32), (e[1:] != e[:-1]).astype(I32)])
    order = jnp.cumsum(counts > 0) - 1
    order_e = of_e(order)

    def after(k):
        hit = (counts > 0)[None, :] & (order[None, :] == order_e[:, None] + k)
        return (jnp.sum(jnp.where(hit, ids[None, :] + 1, 0), axis=1) - 1).astype(I32)

    return (starts.astype(I32), row0, e, live.astype(I32), jnp.where(live, fresh, 0).astype(I32),
            after(1), after(2), (order_e % W_SLOTS).astype(I32), ordered.astype(I32))


def _experts(chunks, xs, w_gate, w_up, w_down):
    ch = EXPERT_CH
    buf = lambda n: pltpu.VMEM((n, ch * PACK_SLABS, LANES), U32)
    anywhere = pl.BlockSpec(memory_space=pl.ANY)
    return pl.pallas_call(
        _expert_kernel,
        grid_spec=pltpu.PrefetchScalarGridSpec(
            num_scalar_prefetch=len(chunks),
            grid=(chunks[0].shape[0],),
            in_specs=[anywhere] * 4,
            out_specs=anywhere,
            scratch_shapes=[buf(X_SLOTS), buf(3),
                            pltpu.VMEM((W_SLOTS, D_MODEL, EXPERT_FF), F32),
                            pltpu.VMEM((W_SLOTS, D_MODEL, EXPERT_FF), F32),
                            pltpu.VMEM((W_SLOTS, EXPERT_FF, D_MODEL), F32),
                            pltpu.VMEM((D_MODEL, EXPERT_FF), BF16), pltpu.VMEM((D_MODEL, EXPERT_FF), BF16),
                            pltpu.VMEM((EXPERT_FF, D_MODEL), BF16),
                            pltpu.SemaphoreType.DMA((X_SLOTS,)), pltpu.SemaphoreType.DMA((3,)),
                            pltpu.SemaphoreType.DMA((W_SLOTS,))],
        ),
        out_shape=jax.ShapeDtypeStruct(xs.shape, U32),
        compiler_params=_params("arbitrary"),
        name="experts",
    )(*chunks, xs, w_gate, w_up, w_down)


COMBINE_TM = 256


def _combine_kernel(wt_ref, h_ref, sg_ref, su_ref, sd_ref, g_ref, beta_ref, *refs):
    y_refs, o_ref = refs[:TOP_K], refs[TOP_K]
    h = h_ref[...]
    hb = h.astype(BF16)
    hid = (_silu(_dot(hb, sg_ref[...])) * _dot(hb, su_ref[...])).astype(BF16)
    acc = DN_ALPHA * h + _dot(hid, sd_ref[...])
    lo = acc[:, :HALF_D]
    hi = acc[:, HALF_D:]
    for k in range(TOP_K):
        y_lo, y_hi = _unpack_rows(y_refs[k])
        w = wt_ref[:, k:k + 1]
        lo = lo + w * y_lo
        hi = hi + w * y_hi
    o_ref[...] = _layer_norm(jnp.concatenate([lo, hi], axis=1), g_ref[...], beta_ref[...])


def _combine(wts_tk, yg, ws_gate, ws_up, ws_down, ln_g, ln_b, h, t0):
    tm = COMBINE_TM
    n = h.shape[0]
    b0 = t0 // tm
    per_slot = yg.shape[0] // (TOP_K * tm * PACK_SLABS)
    rows = pl.BlockSpec((tm, D_MODEL), lambda i: (i, 0))
    slot_rows = [pl.BlockSpec((tm * PACK_SLABS, LANES), lambda i, k=k: (k * per_slot + b0 + i, 0))
                 for k in range(TOP_K)]
    vec = pl.BlockSpec((1, D_MODEL), lambda i: (0, 0))
    return pl.pallas_call(
        _combine_kernel,
        grid=(n // tm,),
        in_specs=[pl.BlockSpec((tm, TOP_K), lambda i: (b0 + i, 0)), rows,
                  pl.BlockSpec((D_MODEL, EXPERT_FF), lambda i: (0, 0)),
                  pl.BlockSpec((D_MODEL, EXPERT_FF), lambda i: (0, 0)),
                  pl.BlockSpec((EXPERT_FF, D_MODEL), lambda i: (0, 0)),
                  vec, vec] + slot_rows,
        out_specs=pl.BlockSpec((tm, D_MODEL), lambda i: (i, 0)),
        out_shape=jax.ShapeDtypeStruct((n, D_MODEL), F32),
        compiler_params=_params("arbitrary"),
        name="combine",
    )(wts_tk, h, ws_gate, ws_up, ws_down, ln_g, ln_b, *([yg] * TOP_K))


def _kv_cache(kv_tail, g, keep):
    k = kv_tail[-keep:, g * GROUP_W:(g + 1) * GROUP_W]
    v = kv_tail[-keep:, ATTN_W + g * GROUP_W:ATTN_W + (g + 1) * GROUP_W]
    return jnp.stack([k, v], axis=1).reshape(keep, 2, HEADS_PER_GROUP, HEAD_DIM)


def kernel(x_prompt, x_sample, cache_kv_w128, cache_kv_w512, cache_kv_w2048, state_conv, w_in, b_in, w_dw, b_dw,
           conv_ln_g, conv_ln_b, w_o, b_o, ln1_g, ln1_b, w_router, router_bias, w_gate, w_up, w_down, ws_gate,
           ws_up, ws_down, ln2_g, ln2_b):
    assert w_in.shape[0] == DEPTH == 1
    batch, seq, _ = x_prompt.shape
    dec_batch, dec_seq, _ = x_sample.shape
    assert batch == 1
    n_s = dec_batch * dec_seq
    caches = (cache_kv_w128, cache_kv_w512, cache_kv_w2048)
    row = lambda a: a[0].reshape(1, -1)

    w_in_b = w_in[0].astype(BF16)
    b_in_r = row(b_in)
    w_o_b = w_o[0].astype(BF16)
    conv_w = (w_dw[0], row(b_dw), row(conv_ln_g), row(conv_ln_b))
    ln1 = (row(ln1_g), row(ln1_b))

    xp = x_prompt[0]
    keep_p = min(max(w for w, _ in DIL_GROUPS), seq)
    dils = tuple(d for _, d in DIL_GROUPS)
    *qkv_p, u_p, kv_p = _project(xp, w_in_b, b_in_r, 512, keep_p, BF16, dils)
    attn_p = [_attention_prompt(qkv_p[g], g) for g in range(N_GROUPS)]
    conv_p = _conv_prompt(u_p, *conv_w, 512)
    h_p, hpk_p = _outproj([a[0] for a in attn_p], [a[1] for a in attn_p], conv_p, xp, w_o_b, row(b_o), *ln1, 512,
                          dils)

    xs = x_sample.reshape(n_s, D_MODEL)
    ones = (1,) * N_GROUPS
    *qkv_s, u_s, kv_s = _project(xs, w_in_b, b_in_r, n_s, n_s, F32, ones)
    caches_t = [jnp.transpose(c[0].reshape(dec_batch, -1, 2 * GROUP_W), (0, 2, 1)) for c in caches]
    attn_s = [_attention_sample(qkv_s[g], caches_t[g], g, dec_seq) for g in range(N_GROUPS)]
    u_hist = jnp.concatenate([state_conv[0], u_s.reshape(dec_batch, dec_seq, CONV_CH)], axis=1)
    conv_s = _conv_sample(u_hist, *conv_w, dec_seq)
    h_s, hpk_s = _outproj([a[0] for a in attn_s], [a[1] for a in attn_s], conv_s, xs, w_o_b, row(b_o), *ln1, n_s,
                          ones)

    idx, wts, pos, counts = _route(h_p, h_s, w_router[0].T.astype(BF16), router_bias[0])
    n_tok = seq + n_s
    starts, *chunks = _chunks(counts, n_tok * TOP_K)
    dest = _dest_rows(idx, pos, starts)
    dest_w = dest.reshape(TOP_K, n_tok // SC_WINDOW, SC_WINDOW).transpose(1, 0, 2)
    tiles = lambda a: a.reshape(-1, PACK_SLABS, LANES)
    flat = lambda a: a.reshape(-1, LANES)
    x_sorted = _sc_scatter_rows(tiles(hpk_p), tiles(hpk_s), dest_w, jnp.zeros((EXPERT_CH, PACK_SLABS, LANES), U32),
                                n_tok * TOP_K)
    y_sorted = _experts(chunks, flat(x_sorted), w_gate[0], w_up[0], w_down[0])
    y_slots = _sc_gather_rows(tiles(y_sorted), dest.reshape(-1))
    shared = (ws_gate[0].astype(BF16), ws_up[0].astype(BF16), ws_down[0].astype(BF16))
    comb = functools.partial(_combine, wts.T, flat(y_slots), *shared, row(ln2_g), row(ln2_b))
    y_p = comb(h_p, 0)
    y_s = comb(h_s, seq)

    kv_prompt = [_kv_cache(kv_p, g, min(w, seq))[None, None] for g, (w, _) in enumerate(DIL_GROUPS)]
    assert seq >= CONV_WIDTH - 1
    conv_prompt = u_p[-(CONV_WIDTH - 1):]
    kv_s4 = kv_s.reshape(dec_batch, dec_seq, 2, N_GROUPS, HEADS_PER_GROUP, HEAD_DIM)
    kv_sample = [kv_s4[:, :, :, g][None] for g in range(N_GROUPS)]
    conv_sample = u_hist[:, -(CONV_WIDTH - 1):]
    return (y_p[None], y_s.reshape(dec_batch, dec_seq, D_MODEL), *kv_prompt, conv_prompt[None, None],
            *kv_sample, conv_sample[None])
```

```python
import functools

import jax
import jax.numpy as jnp
from jax import lax
from jax.experimental import pallas as pl
from jax.experimental.pallas import tpu as pltpu
from jax.experimental.pallas import tpu_sc as plsc

F32 = jnp.float32
BF16 = jnp.bfloat16
I32 = jnp.int32

D_MODEL = 1024
HEAD_DIM = 64
HEADS_PER_GROUP = 4
GROUP_W = HEADS_PER_GROUP * HEAD_DIM
DIL_GROUPS = ((128, 1), (512, 4), (2048, 16))
N_GROUPS = len(DIL_GROUPS)
ATTN_W = N_GROUPS * GROUP_W
CONV_CH = D_MODEL - ATTN_W
CONV_WIDTH = 31
IN_W = 3 * ATTN_W + 2 * CONV_CH
BAND = 128
N_EXPERTS = 256
TOP_K = 8
N_EXPERT_GROUPS = 8
EXPERTS_PER_GROUP = N_EXPERTS // N_EXPERT_GROUPS
TOPK_GROUPS = 4
EXPERT_FF = 256
ROUTED_SCALE = 2.5
DEPTH = 1
DN_ALPHA = (2 * DEPTH) ** 0.25
LN_EPS = 1e-5
MASKED = -1e30

VMEM_LIMIT_BYTES = 56 * 1024 * 1024


def _params(*sem):
    return pltpu.CompilerParams(dimension_semantics=sem, vmem_limit_bytes=VMEM_LIMIT_BYTES)


def _dot(a, b):
    return jnp.dot(a, b, preferred_element_type=F32)


def _dot_nt(a, b):
    return lax.dot_general(a, b, (((1,), (1,)), ((), ())), preferred_element_type=F32)


def _layer_norm(x, g, b):
    mu = jnp.mean(x, axis=-1, keepdims=True)
    xc = x - mu
    var = jnp.mean(xc * xc, axis=-1, keepdims=True)
    return xc * lax.rsqrt(var + LN_EPS) * g + b


def _silu(x):
    return x * jax.nn.sigmoid(x)


def _alibi_slopes():
    n = N_GROUPS * HEADS_PER_GROUP
    h = jnp.arange(1, n + 1, dtype=F32)
    return (2.0 ** (-8.0 * h / n)).reshape(N_GROUPS, HEADS_PER_GROUP)


LANES = 128


def _proj_kernel(x_ref, w_ref, b_ref, *refs, dils, tm):
    qkv_refs, (u_ref, kv_ref, zs) = refs[:N_GROUPS], refs[N_GROUPS:]
    x = x_ref[...].astype(BF16)
    for part in range(3):
        for g in range(N_GROUPS):
            c0 = part * ATTN_W + g * GROUP_W
            z = _dot(x, w_ref[:, c0:c0 + GROUP_W]) + b_ref[:, c0:c0 + GROUP_W]
            if part > 0:
                kv_ref[:, c0 - ATTN_W:c0 - ATTN_W + GROUP_W] = z
            out, dil = qkv_refs[g], dils[g]
            if dil == 1:
                out[:, part * GROUP_W:(part + 1) * GROUP_W] = z.astype(out.dtype)
                continue
            for half in range(GROUP_W // LANES):
                zs[half] = z[:, half * LANES:(half + 1) * LANES]
            for r in range(dil):
                for half in range(GROUP_W // LANES):
                    c = r * 3 * GROUP_W + part * GROUP_W + half * LANES
                    out[:, c:c + LANES] = zs[half, pl.ds(r, tm // dil, stride=dil), :].astype(out.dtype)
    c0 = 3 * ATTN_W
    a = _dot(x, w_ref[:, c0:c0 + CONV_CH]) + b_ref[:, c0:c0 + CONV_CH]
    gate = _dot(x, w_ref[:, c0 + CONV_CH:]) + b_ref[:, c0 + CONV_CH:]
    u_ref[...] = a * jax.nn.sigmoid(gate)


def _project(x, w_in, b_in, tm, keep, qkv_dtype, dils):
    n = x.shape[0]
    nt = n // tm
    nk = keep // tm
    return pl.pallas_call(
        functools.partial(_proj_kernel, dils=dils, tm=tm),
        grid=(nt,),
        in_specs=[
            pl.BlockSpec((tm, D_MODEL), lambda i: (i, 0)),
            pl.BlockSpec((D_MODEL, IN_W), lambda i: (0, 0)),
            pl.BlockSpec((1, IN_W), lambda i: (0, 0)),
        ],
        out_specs=[pl.BlockSpec((tm // d, d * 3 * GROUP_W), lambda i: (i, 0)) for d in dils] + [
            pl.BlockSpec((tm, CONV_CH), lambda i: (i, 0)),
            pl.BlockSpec((tm, 2 * ATTN_W), lambda i: (jnp.maximum(i - (nt - nk), 0), 0)),
        ],
        out_shape=[jax.ShapeDtypeStruct((n // d, d * 3 * GROUP_W), qkv_dtype) for d in dils] + [
            jax.ShapeDtypeStruct((n, CONV_CH), F32),
            jax.ShapeDtypeStruct((keep, 2 * ATTN_W), F32),
        ],
        scratch_shapes=[pltpu.VMEM((GROUP_W // LANES, tm, LANES), F32)],
        compiler_params=_params("arbitrary"),
        name="proj",
    )(x, w_in, b_in)


def _head_select(parts, rows):
    col = lax.broadcasted_iota(I32, (rows, GROUP_W), 1) // HEAD_DIM
    out = jnp.broadcast_to(parts[-1], (rows, GROUP_W))
    for h in range(HEADS_PER_GROUP - 2, -1, -1):
        out = jnp.where(col == h, parts[h], out)
    return out


def _head_rows(q, rows):
    col = lax.broadcasted_iota(I32, (rows, GROUP_W), 1) // HEAD_DIM
    return jnp.concatenate([jnp.where(col == h, q, jnp.zeros_like(q)) for h in range(HEADS_PER_GROUP)], axis=0)


def _softmax_pv(s, v, rows):
    m = jnp.max(s, axis=-1, keepdims=True)
    e = jnp.exp(s - m)
    l = jnp.sum(e, axis=-1, keepdims=True)
    pv = _dot(e.astype(BF16), v) / l
    lse = m + jnp.log(l)
    o = _head_select([pv[h * rows:(h + 1) * rows] for h in range(HEADS_PER_GROUP)], rows)
    lse_x = _head_select([lse[h * rows:(h + 1) * rows] for h in range(HEADS_PER_GROUP)], rows)
    return o, lse_x


ATTN_QB = 4


def _attn_kernel(q_ref, kp_ref, kc_ref, vp_ref, vc_ref, bias_ref, o_ref, lse_ref):
    b = pl.program_id(1)
    for j in range(ATTN_QB):
        rows = slice(j * BAND, (j + 1) * BAND)
        qm = _head_rows(q_ref[rows, :], BAND)
        if j == 0:
            k = jnp.concatenate([kp_ref[...], kc_ref[rows, :]], axis=0)
            v = jnp.concatenate([vp_ref[...], vc_ref[rows, :]], axis=0)
            bias = bias_ref[jnp.minimum(b, 1)]
        else:
            k = kc_ref[(j - 1) * BAND:(j + 1) * BAND, :]
            v = vc_ref[(j - 1) * BAND:(j + 1) * BAND, :]
            bias = bias_ref[1]
        s = _dot_nt(qm, k) * HEAD_DIM ** -0.5 + bias
        o, lse_x = _softmax_pv(s, v, BAND)
        o_ref[rows, :] = o.astype(o_ref.dtype)
        lse_ref[rows, :] = lse_x


def _prompt_bias(g, dil):
    slopes = _alibi_slopes()[g]
    qi = jnp.arange(BAND)[:, None]
    kj = jnp.arange(2 * BAND)[None, :]
    steps = qi + BAND - kj
    nk = DIL_GROUPS[g][0] // dil
    valid = (steps >= 0) & (steps <= nk)
    bias = -slopes[:, None, None] * (steps * dil).astype(F32)
    inner = jnp.where(valid[None], bias, MASKED)
    first = jnp.where((valid & (kj >= BAND))[None], bias, MASKED)
    return jnp.stack([first, inner]).reshape(2, HEADS_PER_GROUP * BAND, 2 * BAND)


def _attention_prompt(view, g):
    win, dil = DIL_GROUPS[g]
    n_cls = view.shape[0]
    rows = ATTN_QB * BAND
    assert win // dil <= BAND and n_cls % rows == 0 and view.shape[1] == dil * 3 * GROUP_W
    cur = lambda part: pl.BlockSpec((rows, GROUP_W), lambda r, b: (b, r * 3 + part))
    prev = lambda part: pl.BlockSpec((BAND, GROUP_W), lambda r, b: (jnp.maximum(b * ATTN_QB - 1, 0), r * 3 + part))
    return pl.pallas_call(
        _attn_kernel,
        grid=(dil, n_cls // rows),
        in_specs=[cur(0), prev(1), cur(1), prev(2), cur(2),
                  pl.BlockSpec((2, HEADS_PER_GROUP * BAND, 2 * BAND), lambda r, b: (0, 0, 0))],
        out_specs=[pl.BlockSpec((rows, GROUP_W), lambda r, b: (b, r)),
                   pl.BlockSpec((rows, GROUP_W), lambda r, b: (b, r))],
        out_shape=[jax.ShapeDtypeStruct((n_cls, dil * GROUP_W), BF16),
                   jax.ShapeDtypeStruct((n_cls, dil * GROUP_W), F32)],
        compiler_params=_params("arbitrary", "arbitrary"),
        name=f"attn_g{g}",
    )(view, view, view, view, view, _prompt_bias(g, dil))


def _attn_sample_kernel(q_ref, k_ref, v_ref, cache_ref, bias_ref, o_ref, lse_ref, *, n_buf, t, seqs):
    pad = jnp.zeros((BAND - t, GROUP_W), F32)
    scale = HEAD_DIM ** -0.5
    for b in range(seqs):
        rows = slice(b * t, (b + 1) * t)
        k_new = jnp.concatenate([k_ref[rows, :], pad], axis=0).astype(BF16)
        v_new = jnp.concatenate([v_ref[rows, :], pad], axis=0).astype(BF16)
        qm = _head_rows(q_ref[rows, :], t).astype(BF16)
        s_old = _dot(qm, cache_ref[b, 0:GROUP_W, :].astype(BF16)) * scale + bias_ref[:, 0:n_buf]
        s_new = _dot_nt(qm, k_new) * scale + bias_ref[:, n_buf:]
        m = jnp.maximum(jnp.max(s_old, axis=-1, keepdims=True), jnp.max(s_new, axis=-1, keepdims=True))
        e_old = jnp.exp(s_old - m)
        e_new = jnp.exp(s_new - m)
        l = jnp.sum(e_old, axis=-1, keepdims=True) + jnp.sum(e_new, axis=-1, keepdims=True)
        pv = _dot_nt(e_old.astype(BF16), cache_ref[b, GROUP_W:, :].astype(BF16)) + _dot(e_new.astype(BF16), v_new)
        pv = pv / l
        lse = m + jnp.log(l)
        o_ref[rows, :] = _head_select([pv[h * t:(h + 1) * t] for h in range(HEADS_PER_GROUP)], t)
        lse_ref[rows, :] = _head_select([lse[h * t:(h + 1) * t] for h in range(HEADS_PER_GROUP)], t)


def _sample_bias(g, n_buf, t):
    win, dil = DIL_GROUPS[g]
    slopes = _alibi_slopes()[g]
    tq = jnp.arange(t)[:, None]
    j = jnp.arange(n_buf + BAND)[None, :]
    dist = n_buf + tq - j
    valid = (dist >= 0) & (dist % dil == 0) & (dist <= win) & (j < n_buf + t)
    bias = -slopes[:, None, None] * dist.astype(F32)
    return jnp.where(valid[None], bias, MASKED).reshape(HEADS_PER_GROUP * t, n_buf + BAND)


def _attention_sample(qkv, cache_t, g, t):
    nb, n_buf = cache_t.shape[0], cache_t.shape[2]
    seqs = max(1, min(4, 1024 // n_buf))
    assert nb % seqs == 0
    col = lambda part: pl.BlockSpec((seqs * t, GROUP_W), lambda i: (i, part))
    kern = functools.partial(_attn_sample_kernel, n_buf=n_buf, t=t, seqs=seqs)
    return pl.pallas_call(
        kern,
        grid=(nb // seqs,),
        in_specs=[col(0), col(1), col(2),
                  pl.BlockSpec((seqs, 2 * GROUP_W, n_buf), lambda i: (i, 0, 0)),
                  pl.BlockSpec((HEADS_PER_GROUP * t, n_buf + BAND), lambda i: (0, 0))],
        out_specs=[pl.BlockSpec((seqs * t, GROUP_W), lambda i: (i, 0)),
                   pl.BlockSpec((seqs * t, GROUP_W), lambda i: (i, 0))],
        out_shape=[jax.ShapeDtypeStruct((nb * t, GROUP_W), F32),
                   jax.ShapeDtypeStruct((nb * t, GROUP_W), F32)],
        compiler_params=_params("arbitrary"),
        name=f"attn_sample_g{g}",
    )(qkv, qkv, qkv, cache_t, _sample_bias(g, n_buf, t))


CONV_HALO = 32


def _conv_tail(acc, b_ref, g_ref, beta_ref):
    return _silu(_layer_norm(acc + b_ref[...], g_ref[...], beta_ref[...]))


SUBLANES = 8


def _conv_prompt_kernel(halo_ref, u_ref, w_ref, b_ref, g_ref, beta_ref, o_ref, hist, part, *, tm):
    i = pl.program_id(0)
    hist[0:CONV_HALO, :] = jnp.where(i == 0, 0.0, halo_ref[...])
    hist[CONV_HALO:CONV_HALO + tm, :] = u_ref[...]
    hist[CONV_HALO + tm:, :] = jnp.zeros((SUBLANES, CONV_CH), F32)
    off = CONV_HALO - (CONV_WIDTH - 1)
    acc = None
    for s in range(SUBLANES):
        group = None
        for m in range(s, off + CONV_WIDTH, SUBLANES):
            j = m - off
            if j < 0:
                continue
            term = w_ref[j:j + 1, :] * hist[m - s:m - s + tm + SUBLANES, :]
            group = term if group is None else group + term
        part[...] = group
        shifted = part[s:s + tm, :]
        acc = shifted if acc is None else acc + shifted
    o_ref[...] = _conv_tail(acc, b_ref, g_ref, beta_ref).astype(o_ref.dtype)


def _conv_prompt(u, w_dw, b_dw, ln_g, ln_b, tm):
    n = u.shape[0]
    vec = pl.BlockSpec((1, CONV_CH), lambda i: (0, 0))
    return pl.pallas_call(
        functools.partial(_conv_prompt_kernel, tm=tm),
        grid=(n // tm,),
        in_specs=[pl.BlockSpec((CONV_HALO, CONV_CH), lambda i: (jnp.maximum(i * (tm // CONV_HALO) - 1, 0), 0)),
                  pl.BlockSpec((tm, CONV_CH), lambda i: (i, 0)),
                  pl.BlockSpec((CONV_WIDTH, CONV_CH), lambda i: (0, 0)), vec, vec, vec],
        out_specs=pl.BlockSpec((tm, CONV_CH), lambda i: (i, 0)),
        out_shape=jax.ShapeDtypeStruct((n, CONV_CH), BF16),
        scratch_shapes=[pltpu.VMEM((CONV_HALO + tm + SUBLANES, CONV_CH), F32),
                        pltpu.VMEM((tm + SUBLANES, CONV_CH), F32)],
        compiler_params=_params("arbitrary"),
        name="conv_prompt",
    )(u, u, w_dw, b_dw, ln_g, ln_b)


def _conv_sample_kernel(hist_ref, w_ref, b_ref, g_ref, beta_ref, o_ref, *, t):
    acc = jnp.zeros((t, CONV_CH), F32)
    for j in range(CONV_WIDTH):
        acc = acc + w_ref[j:j + 1, :] * hist_ref[0, j:j + t, :]
    o_ref[...] = _conv_tail(acc, b_ref, g_ref, beta_ref)


def _conv_sample(u_hist, w_dw, b_dw, ln_g, ln_b, t):
    nb, rows = u_hist.shape[0], u_hist.shape[1]
    vec = pl.BlockSpec((1, CONV_CH), lambda i: (0, 0))
    return pl.pallas_call(
        functools.partial(_conv_sample_kernel, t=t),
        grid=(nb,),
        in_specs=[pl.BlockSpec((1, rows, CONV_CH), lambda i: (i, 0, 0)),
                  pl.BlockSpec((CONV_WIDTH, CONV_CH), lambda i: (0, 0)), vec, vec, vec],
        out_specs=pl.BlockSpec((t, CONV_CH), lambda i: (i, 0)),
        out_shape=jax.ShapeDtypeStruct((nb * t, CONV_CH), F32),
        compiler_params=_params("arbitrary"),
        name="conv_sample",
    )(u_hist, w_dw, b_dw, ln_g, ln_b)


U32 = jnp.uint32
HALF_D = D_MODEL // 2
PACK_SLABS = HALF_D // LANES


def _token_order(ref, dil, scr, tm):
    if dil == 1:
        return ref[...].astype(F32)
    for r in range(dil):
        for half in range(GROUP_W // LANES):
            c = r * GROUP_W + half * LANES
            scr[half, pl.ds(r, tm // dil, stride=dil), :] = ref[:, c:c + LANES].astype(F32)
    return jnp.concatenate([scr[half] for half in range(GROUP_W // LANES)], axis=1)


def _pack_rows(ref, val):
    rows = val.shape[0]
    bits = lax.bitcast_convert_type(val.astype(BF16).astype(F32), U32)
    packed = lax.shift_right_logical(bits[:, :HALF_D], jnp.uint32(16)) | bits[:, HALF_D:]
    for c in range(PACK_SLABS):
        ref[pl.ds(c, rows, stride=PACK_SLABS), :] = packed[:, c * LANES:(c + 1) * LANES]


def _unpack_rows(ref):
    rows = ref.shape[0] // PACK_SLABS
    u = jnp.concatenate([ref[pl.ds(c, rows, stride=PACK_SLABS), :] for c in range(PACK_SLABS)], axis=1)
    lo = lax.bitcast_convert_type(lax.shift_left(u, jnp.uint32(16)), F32)
    hi = lax.bitcast_convert_type(u & jnp.uint32(0xFFFF0000), F32)
    return lo, hi


def _outproj_kernel(o0, o1, o2, l0, l1, l2, c_ref, x_ref, w_ref, b_ref, g_ref, beta_ref, h_ref, hp_ref, *scr,
                    dils, tm):
    os_ = [_token_order(r, d, scr[2 * g], tm) for g, (r, d) in enumerate(zip((o0, o1, o2), dils))]
    ls = [_token_order(r, d, scr[2 * g + 1], tm) for g, (r, d) in enumerate(zip((l0, l1, l2), dils))]
    m = jnp.maximum(jnp.maximum(ls[0], ls[1]), ls[2])
    es = [jnp.exp(l - m) for l in ls]
    inv = 1.0 / (es[0] + es[1] + es[2])
    mixed = b_ref[...]
    for g in range(N_GROUPS):
        a = (os_[g] * (es[g] * inv)).astype(BF16)
        mixed = mixed + _dot(a, w_ref[g * GROUP_W:(g + 1) * GROUP_W, :])
    mixed = mixed + _dot(c_ref[...].astype(BF16), w_ref[ATTN_W:, :])
    h = _layer_norm(DN_ALPHA * x_ref[...] + mixed, g_ref[...], beta_ref[...])
    h_ref[...] = h
    _pack_rows(hp_ref, h)


def _outproj(os_, lses, conv, x, w_o, b_o, ln_g, ln_b, tm, dils):
    n = x.shape[0]
    grp = [pl.BlockSpec((tm // d, d * GROUP_W), lambda i: (i, 0)) for d in dils]
    vec = pl.BlockSpec((1, D_MODEL), lambda i: (0, 0))
    return pl.pallas_call(
        functools.partial(_outproj_kernel, dils=dils, tm=tm),
        grid=(n // tm,),
        in_specs=grp + grp + [pl.BlockSpec((tm, CONV_CH), lambda i: (i, 0)),
                              pl.BlockSpec((tm, D_MODEL), lambda i: (i, 0)),
                              pl.BlockSpec((D_MODEL, D_MODEL), lambda i: (0, 0)), vec, vec, vec],
        out_specs=[pl.BlockSpec((tm, D_MODEL), lambda i: (i, 0)),
                   pl.BlockSpec((tm * PACK_SLABS, LANES), lambda i: (i, 0))],
        out_shape=[jax.ShapeDtypeStruct((n, D_MODEL), F32),
                   jax.ShapeDtypeStruct((n * PACK_SLABS, LANES), U32)],
        scratch_shapes=[pltpu.VMEM((GROUP_W // LANES, tm, LANES), F32) for _ in range(2 * N_GROUPS)],
        compiler_params=_params("arbitrary"),
        name="outproj",
    )(*os_, *lses, conv, x, w_o, b_o, ln_g, ln_b)


ROUTER_TM = 256


def _first_index(hit, idx, limit, axis):
    return jnp.min(jnp.where(hit, idx, limit), axis=axis, keepdims=True)


def _router_kernel(ha_ref, hb_ref, w_ref, rb_ref, tri_ref, idx_ref, wt_ref, pos_ref, cnt_ref, run, *, tiles_a):
    i = pl.program_id(0)
    tm = ROUTER_TM

    @pl.when(i == 0)
    def _():
        run[...] = jnp.zeros_like(run)

    h = jnp.where(i < tiles_a, ha_ref[...], hb_ref[...])
    logits = _dot_nt(w_ref[...], h.astype(BF16))
    scores = jax.nn.sigmoid(logits)
    biased = scores + rb_ref[...]
    groups = [biased[g * EXPERTS_PER_GROUP:(g + 1) * EXPERTS_PER_GROUP] for g in range(N_EXPERT_GROUPS)]
    ei = lax.broadcasted_iota(I32, (EXPERTS_PER_GROUP, tm), 0).astype(F32)
    gs = []
    for bg in groups:
        m1 = jnp.max(bg, axis=0, keepdims=True)
        f1 = _first_index(bg == m1, ei, float(EXPERTS_PER_GROUP), 0)
        m2 = jnp.max(jnp.where(ei == f1, -jnp.inf, bg), axis=0, keepdims=True)
        gs.append(m1 + m2)
    gs = jnp.concatenate(gs, axis=0)
    gi = lax.broadcasted_iota(I32, gs.shape, 0).astype(F32)
    keep = jnp.zeros(gs.shape, F32)
    cur = gs
    for _ in range(TOPK_GROUPS):
        m = jnp.max(cur, axis=0, keepdims=True)
        f = _first_index(cur == m, gi, float(N_EXPERT_GROUPS), 0)
        hit = gi == f
        keep = jnp.where(hit, 1.0, keep)
        cur = jnp.where(hit, -jnp.inf, cur)
    masked = jnp.concatenate([jnp.where(keep[g:g + 1] > 0.0, bg, -jnp.inf) for g, bg in enumerate(groups)], axis=0)
    xi = lax.broadcasted_iota(I32, (N_EXPERTS, tm), 0).astype(F32)
    cur = masked
    sel = jnp.zeros((N_EXPERTS, tm), F32)
    picks = []
    for _ in range(TOP_K):
        m = jnp.max(cur, axis=0, keepdims=True)
        f = _first_index(cur == m, xi, float(N_EXPERTS), 0)
        hit = xi == f
        picks.append((f, hit))
        sel = jnp.where(hit, 1.0, sel)
        cur = jnp.where(hit, -jnp.inf, cur)
    before = _dot(sel.astype(BF16), tri_ref[...]) + run[...]
    run[...] = run[...] + jnp.sum(sel, axis=1, keepdims=True)
    ws = [jnp.sum(jnp.where(hit, scores, 0.0), axis=0, keepdims=True) for _, hit in picks]
    wsum = ws[0]
    for w in ws[1:]:
        wsum = wsum + w
    for k, (f, hit) in enumerate(picks):
        idx_ref[k:k + 1, :] = f.astype(I32)
        wt_ref[k:k + 1, :] = ws[k] / wsum * ROUTED_SCALE
        pos_ref[k:k + 1, :] = jnp.sum(jnp.where(hit, before, 0.0), axis=0, keepdims=True).astype(I32)
    cnt_ref[...] = jnp.broadcast_to(run[...], cnt_ref.shape).astype(I32)


def _route(ha, hb, w_router_t, router_bias):
    tm = ROUTER_TM
    tiles_a, tiles_b = ha.shape[0] // tm, hb.shape[0] // tm
    spec_a = pl.BlockSpec((tm, D_MODEL), lambda i: (jnp.minimum(i, tiles_a - 1), 0))
    spec_b = pl.BlockSpec((tm, D_MODEL), lambda i: (jnp.maximum(i - tiles_a, 0), 0))
    t = (tiles_a + tiles_b) * tm
    tri = (jnp.arange(tm)[:, None] < jnp.arange(tm)[None, :]).astype(BF16)
    slot = pl.BlockSpec((TOP_K, tm), lambda i: (0, i))
    idx, wts, pos, cnt = pl.pallas_call(
        functools.partial(_router_kernel, tiles_a=tiles_a),
        grid=(t // tm,),
        in_specs=[spec_a, spec_b,
                  pl.BlockSpec((N_EXPERTS, D_MODEL), lambda i: (0, 0)),
                  pl.BlockSpec((N_EXPERTS, 1), lambda i: (0, 0)),
                  pl.BlockSpec((tm, tm), lambda i: (0, 0))],
        out_specs=[slot, slot, slot, pl.BlockSpec((N_EXPERTS, 128), lambda i: (0, 0))],
        out_shape=[jax.ShapeDtypeStruct((TOP_K, t), I32), jax.ShapeDtypeStruct((TOP_K, t), F32),
                   jax.ShapeDtypeStruct((TOP_K, t), I32), jax.ShapeDtypeStruct((N_EXPERTS, 128), I32)],
        scratch_shapes=[pltpu.VMEM((N_EXPERTS, 1), F32)],
        compiler_params=_params("arbitrary"),
        name="router",
    )(ha, hb, w_router_t, router_bias.reshape(N_EXPERTS, 1), tri)
    return idx, wts, pos, cnt[:, 0]


DEST_TM = 1280


def _dest_kernel(idx_ref, pos_ref, starts_ref, dest_ref):
    tm = idx_ref.shape[1]
    ei = lax.broadcasted_iota(I32, (N_EXPERTS, tm), 0)
    starts = starts_ref[...]
    for k in range(TOP_K):
        first = jnp.sum(jnp.where(ei == idx_ref[k:k + 1, :], starts, 0.0), axis=0, keepdims=True)
        dest_ref[k:k + 1, :] = first.astype(I32) + pos_ref[k:k + 1, :]


def _dest_rows(idx, pos, starts):
    t = idx.shape[1]
    tm = DEST_TM if t % DEST_TM == 0 else ROUTER_TM
    assert t % tm == 0
    slot = pl.BlockSpec((TOP_K, tm), lambda i: (0, i))
    return pl.pallas_call(
        _dest_kernel,
        grid=(t // tm,),
        in_specs=[slot, slot, pl.BlockSpec((N_EXPERTS, 1), lambda i: (0, 0))],
        out_specs=slot,
        out_shape=jax.ShapeDtypeStruct((TOP_K, t), I32),
        compiler_params=_params("arbitrary"),
        name="dest_rows",
    )(idx, pos, starts.astype(F32).reshape(N_EXPERTS, 1))


SC_CORES = 2
SC_SUBCORES = 16
SC_WORKERS = SC_CORES * SC_SUBCORES
SC_WINDOW = 64
SC_WINDOW_SMALL = 32


def _sc_worker():
    return lax.axis_index("s") * SC_CORES + lax.axis_index("c")


def _sc_mesh():
    return plsc.VectorSubcoreMesh(core_axis_name="c", subcore_axis_name="s")


def _sc_scatter_rows(ha, hb, dest_w, zeros, n_out):
    w = dest_w.shape[2]
    na, nb = ha.shape[0], hb.shape[0]
    spare = zeros.shape[0]
    wa, wb = na // w, nb // w
    assert na % (SC_WORKERS * 2 * w) == 0 and nb % w == 0 and wb <= SC_WORKERS
    per_w = wa // SC_WORKERS
    rows_t = pltpu.VMEM((w,) + ha.shape[1:], ha.dtype)
    idx_t = pltpu.VMEM((TOP_K, w), I32)

    @functools.partial(
        pl.kernel, mesh=_sc_mesh(),
        out_type=jax.ShapeDtypeStruct((n_out + spare,) + ha.shape[1:], ha.dtype),
        scratch_types=[rows_t, rows_t, idx_t, idx_t, pltpu.SemaphoreType.DMA, pltpu.SemaphoreType.DMA],
    )
    def scatter(ha_hbm, hb_hbm, dest_hbm, zeros_hbm, out_hbm, rows0, rows1, idx0, idx1, sem0, sem1):
        wid = _sc_worker()
        bufs = ((rows0, idx0, sem0), (rows1, idx1, sem1))

        @pl.when(wid == SC_WORKERS - 1)
        def _():
            pltpu.sync_copy(zeros_hbm, out_hbm.at[pl.ds(n_out, spare)])

        def load(src_hbm, row0, win, b):
            pltpu.sync_copy(src_hbm.at[pl.ds(pl.multiple_of(row0, 8), w)], bufs[b][0])
            pltpu.sync_copy(dest_hbm.at[win], bufs[b][1])

        def copies(b):
            return [pltpu.make_async_copy(bufs[b][0], out_hbm.at[bufs[b][1].at[k]], bufs[b][2])
                    for k in range(TOP_K)]

        win0 = wid * per_w
        load(ha_hbm, win0 * w, win0, 0)

        @pl.loop(0, per_w, step=2)
        def _(i0):
            for b in range(2):
                i = i0 + b
                for c in copies(b):
                    c.start()

                @pl.when(i + 1 < per_w)
                def _():
                    load(ha_hbm, (win0 + i + 1) * w, win0 + i + 1, 1 - b)

                for c in copies(b):
                    c.wait()

        @pl.when(wid < wb)
        def _():
            load(hb_hbm, wid * w, wa + wid, 0)
            for c in copies(0):
                c.start()
            for c in copies(0):
                c.wait()

    return scatter(ha, hb, dest_w, zeros)


def _sc_gather_rows(table, idx):
    n = idx.shape[0]
    w = SC_WINDOW if n % (SC_WORKERS * 2 * SC_WINDOW) == 0 else SC_WINDOW_SMALL
    assert n % (SC_WORKERS * 2 * w) == 0
    per_w = n // SC_WORKERS
    nwin = per_w // w
    rows_t = pltpu.VMEM((w,) + table.shape[1:], table.dtype)

    @functools.partial(
        pl.kernel, mesh=_sc_mesh(),
        out_type=jax.ShapeDtypeStruct((n,) + table.shape[1:], table.dtype),
        scratch_types=[pltpu.VMEM((per_w,), I32), rows_t, rows_t, pltpu.SemaphoreType.DMA, pltpu.SemaphoreType.DMA],
    )
    def gather(table_hbm, idx_hbm, out_hbm, idx_v, rows0, rows1, sem0, sem1):
        base = pl.multiple_of(_sc_worker() * per_w, 8)
        pltpu.sync_copy(idx_hbm.at[pl.ds(base, per_w)], idx_v)
        bufs = ((rows0, sem0), (rows1, sem1))
        fetch = lambda i, b: pltpu.make_async_copy(
            table_hbm.at[idx_v.at[pl.ds(pl.multiple_of(i * w, 8), w)]], bufs[b][0], bufs[b][1])
        fetch(0, 0).start()

        @pl.loop(0, nwin, step=2)
        def _(i0):
            for b in range(2):
                i = i0 + b
                fetch(i, b).wait()

                @pl.when(i + 1 < nwin)
                def _():
                    fetch(i + 1, 1 - b).start()

                pltpu.sync_copy(bufs[b][0], out_hbm.at[pl.ds(pl.multiple_of(base + i * w, 8), w)])

    return gather(table, idx)


EXPERT_CH = 256
X_SLOTS = 4
W_SLOTS = 3


def _expert_kernel(row0_ref, exp_ref, live_ref, fresh_ref, next1_ref, next2_ref, wslot_ref, ordered_ref, xs_ref,
                   wg_ref, wu_ref, wd_ref, ys_ref, xbuf, ybuf, sg, su, sd, wgb, wub, wdb, xsem, ysem, wsem):
    v = pl.program_id(0)
    nv = pl.num_programs(0)
    slot = lax.rem(v, X_SLOTS)
    yslot = lax.rem(v, 3)
    rows = lambda u: pl.ds(pl.multiple_of(row0_ref[u] * PACK_SLABS, PACK_SLABS), EXPERT_CH * PACK_SLABS)
    x_copy = lambda u: pltpu.make_async_copy(xs_ref.at[rows(u)], xbuf.at[lax.rem(u, X_SLOTS)],
                                             xsem.at[lax.rem(u, X_SLOTS)])
    y_copy = lambda u: pltpu.make_async_copy(ybuf.at[lax.rem(u, 3)], ys_ref.at[rows(u)], ysem.at[lax.rem(u, 3)])
    prev1, prev2 = jnp.maximum(v - 1, 0), jnp.maximum(v - 2, 0)
    w_copies = lambda e, s: [pltpu.make_async_copy(w.at[e], stage.at[s], wsem.at[s])
                             for w, stage in ((wg_ref, sg), (wu_ref, su), (wd_ref, sd))]
    spare = pl.ds(ys_ref.shape[0] - EXPERT_CH * PACK_SLABS, EXPERT_CH * PACK_SLABS)

    @pl.when(v == 0)
    def _():
        for c in w_copies(exp_ref[0], 0):
            c.start(priority=1)

        @pl.when(next1_ref[0] >= 0)
        def _():
            for c in w_copies(next1_ref[0], 1):
                c.start(priority=1)

        x_copy(0).start()
        for u in range(1, X_SLOTS - 1):
            @pl.when((u < nv) & (live_ref[jnp.minimum(u, nv - 1)] == 1))
            def _():
                x_copy(u).start()
        ybuf[2] = jnp.zeros(ybuf.shape[1:], U32)
        zero = pltpu.make_async_copy(ybuf.at[2], ys_ref.at[spare], ysem.at[2])
        zero.start()
        zero.wait()

    ahead = jnp.minimum(v + X_SLOTS - 1, nv - 1)

    @pl.when((v + X_SLOTS - 1 < nv) & (live_ref[ahead] == 1))
    def _():
        x_copy(ahead).start()

    @pl.when(fresh_ref[v] == 1)
    def _():
        p = wslot_ref[v]
        for c in w_copies(exp_ref[v], p):
            c.wait()

        @pl.when(next2_ref[v] >= 0)
        def _():
            for c in w_copies(next2_ref[v], lax.rem(p + 2, W_SLOTS)):
                c.start(priority=1)

        wgb[...] = sg[p].astype(BF16)
        wub[...] = su[p].astype(BF16)
        wdb[...] = sd[p].astype(BF16)

    @pl.when(live_ref[v] == 1)
    def _():
        x_copy(v).wait()
        x = jnp.concatenate(_unpack_rows(xbuf.at[slot]), axis=1).astype(BF16)
        hid = (_silu(_dot(x, wgb[...])) * _dot(x, wub[...])).astype(BF16)
        _pack_rows(ybuf.at[yslot], _dot(hid, wdb[...]))

    @pl.when((v >= 2) & (live_ref[prev2] == 1) & (ordered_ref[prev1] == 0))
    def _():
        y_copy(prev2).wait()

    @pl.when((v >= 1) & (live_ref[prev1] == 1) & (ordered_ref[v] == 1))
    def _():
        y_copy(prev1).wait()

    @pl.when(live_ref[v] == 1)
    def _():
        y_copy(v).start()

    @pl.when(v == nv - 1)
    def _():
        @pl.when((v >= 1) & (live_ref[prev1] == 1) & (ordered_ref[v] == 0))
        def _():
            y_copy(prev1).wait()

        @pl.when(live_ref[v] == 1)
        def _():
            y_copy(v).wait()


def _chunks(counts, n_rows):
    ch = EXPERT_CH
    nv = n_rows // ch + N_EXPERTS
    ends = jnp.cumsum(counts)
    starts = ends - counts
    nch = (counts + ch - 1) // ch
    cend = jnp.cumsum(nch)
    cstart = cend - nch
    v = jnp.arange(nv, dtype=I32)
    live = v < cend[-1]
    vc = jnp.minimum(v, cend[-1] - 1)
    ids = jnp.arange(N_EXPERTS, dtype=I32)
    e = jnp.minimum(jnp.sum(cend[None, :] <= vc[:, None], axis=1), N_EXPERTS - 1).astype(I32)
    of_e = lambda a: jnp.sum(jnp.where(e[:, None] == ids[None, :], a[None, :], 0), axis=1)
    starts_e, ends_e, count_e = of_e(starts), of_e(ends), of_e(counts)
    c = vc - of_e(cstart)
    is_last = c == of_e(nch) - 1
    row0 = jnp.where(is_last & (count_e >= ch), ends_e - ch, starts_e + c * ch).astype(I32)
    overrun = live & is_last & (count_e < ch)
    ordered = jnp.concatenate([jnp.zeros((1,), bool), overrun[:-1]]) & live
    fresh = jnp.concatenate([jnp.ones((1,), I32), (e[1:] != e[:-1]).astype(I32)])
    order = jnp.cumsum(counts > 0) - 1
    order_e = of_e(order)

    def after(k):
        hit = (counts > 0)[None, :] & (order[None, :] == order_e[:, None] + k)
        return (jnp.sum(jnp.where(hit, ids[None, :] + 1, 0), axis=1) - 1).astype(I32)

    return (starts.astype(I32), row0, e, live.astype(I32), jnp.where(live, fresh, 0).astype(I32),
            after(1), after(2), (order_e % W_SLOTS).astype(I32), ordered.astype(I32))


def _experts(chunks, xs, w_gate, w_up, w_down):
    ch = EXPERT_CH
    buf = lambda n: pltpu.VMEM((n, ch * PACK_SLABS, LANES), U32)
    anywhere = pl.BlockSpec(memory_space=pl.ANY)
    return pl.pallas_call(
        _expert_kernel,
        grid_spec=pltpu.PrefetchScalarGridSpec(
            num_scalar_prefetch=len(chunks),
            grid=(chunks[0].shape[0],),
            in_specs=[anywhere] * 4,
            out_specs=anywhere,
            scratch_shapes=[buf(X_SLOTS), buf(3),
                            pltpu.VMEM((W_SLOTS, D_MODEL, EXPERT_FF), F32),
                            pltpu.VMEM((W_SLOTS, D_MODEL, EXPERT_FF), F32),
                            pltpu.VMEM((W_SLOTS, EXPERT_FF, D_MODEL), F32),
                            pltpu.VMEM((D_MODEL, EXPERT_FF), BF16), pltpu.VMEM((D_MODEL, EXPERT_FF), BF16),
                            pltpu.VMEM((EXPERT_FF, D_MODEL), BF16),
                            pltpu.SemaphoreType.DMA((X_SLOTS,)), pltpu.SemaphoreType.DMA((3,)),
                            pltpu.SemaphoreType.DMA((W_SLOTS,))],
        ),
        out_shape=jax.ShapeDtypeStruct(xs.shape, U32),
        compiler_params=_params("arbitrary"),
        name="experts",
    )(*chunks, xs, w_gate, w_up, w_down)


COMBINE_TM = 256


def _combine_kernel(wt_ref, h_ref, sg_ref, su_ref, sd_ref, g_ref, beta_ref, *refs):
    y_refs, o_ref = refs[:TOP_K], refs[TOP_K]
    h = h_ref[...]
    hb = h.astype(BF16)
    hid = (_silu(_dot(hb, sg_ref[...])) * _dot(hb, su_ref[...])).astype(BF16)
    acc = DN_ALPHA * h + _dot(hid, sd_ref[...])
    lo = acc[:, :HALF_D]
    hi = acc[:, HALF_D:]
    for k in range(TOP_K):
        y_lo, y_hi = _unpack_rows(y_refs[k])
        w = wt_ref[:, k:k + 1]
        lo = lo + w * y_lo
        hi = hi + w * y_hi
    o_ref[...] = _layer_norm(jnp.concatenate([lo, hi], axis=1), g_ref[...], beta_ref[...])


def _combine(wts_tk, ws_gate, ws_up, ws_down, ln_g, ln_b, h, h_row0, t0, yg):
    tm = COMBINE_TM
    n = yg.shape[0] // (TOP_K * PACK_SLABS)
    assert n % tm == 0 and h_row0 % tm == 0 and t0 % tm == 0
    b0 = t0 // tm
    per_slot = n // tm
    rows = pl.BlockSpec((tm, D_MODEL), lambda i: (h_row0 // tm + i, 0))
    slot_rows = [pl.BlockSpec((tm * PACK_SLABS, LANES), lambda i, k=k: (k * per_slot + i, 0))
                 for k in range(TOP_K)]
    vec = pl.BlockSpec((1, D_MODEL), lambda i: (0, 0))
    return pl.pallas_call(
        _combine_kernel,
        grid=(n // tm,),
        in_specs=[pl.BlockSpec((tm, TOP_K), lambda i: (b0 + i, 0)), rows,
                  pl.BlockSpec((D_MODEL, EXPERT_FF), lambda i: (0, 0)),
                  pl.BlockSpec((D_MODEL, EXPERT_FF), lambda i: (0, 0)),
                  pl.BlockSpec((EXPERT_FF, D_MODEL), lambda i: (0, 0)),
                  vec, vec] + slot_rows,
        out_specs=pl.BlockSpec((tm, D_MODEL), lambda i: (i, 0)),
        out_shape=jax.ShapeDtypeStruct((n, D_MODEL), F32),
        compiler_params=_params("arbitrary"),
        name="combine",
    )(wts_tk, h, ws_gate, ws_up, ws_down, ln_g, ln_b, *([yg] * TOP_K))


def _kv_cache(kv_tail, g, keep):
    k = kv_tail[-keep:, g * GROUP_W:(g + 1) * GROUP_W]
    v = kv_tail[-keep:, ATTN_W + g * GROUP_W:ATTN_W + (g + 1) * GROUP_W]
    return jnp.stack([k, v], axis=1).reshape(keep, 2, HEADS_PER_GROUP, HEAD_DIM)


def kernel(x_prompt, x_sample, cache_kv_w128, cache_kv_w512, cache_kv_w2048, state_conv, w_in, b_in, w_dw, b_dw,
           conv_ln_g, conv_ln_b, w_o, b_o, ln1_g, ln1_b, w_router, router_bias, w_gate, w_up, w_down, ws_gate,
           ws_up, ws_down, ln2_g, ln2_b):
    assert w_in.shape[0] == DEPTH == 1
    batch, seq, _ = x_prompt.shape
    dec_batch, dec_seq, _ = x_sample.shape
    assert batch == 1
    n_s = dec_batch * dec_seq
    caches = (cache_kv_w128, cache_kv_w512, cache_kv_w2048)
    row = lambda a: a[0].reshape(1, -1)

    w_in_b = w_in[0].astype(BF16)
    b_in_r = row(b_in)
    w_o_b = w_o[0].astype(BF16)
    conv_w = (w_dw[0], row(b_dw), row(conv_ln_g), row(conv_ln_b))
    ln1 = (row(ln1_g), row(ln1_b))

    xp = x_prompt[0]
    keep_p = min(max(w for w, _ in DIL_GROUPS), seq)
    dils = tuple(d for _, d in DIL_GROUPS)
    *qkv_p, u_p, kv_p = _project(xp, w_in_b, b_in_r, 512, keep_p, BF16, dils)
    attn_p = [_attention_prompt(qkv_p[g], g) for g in range(N_GROUPS)]
    conv_p = _conv_prompt(u_p, *conv_w, 512)
    h_p, hpk_p = _outproj([a[0] for a in attn_p], [a[1] for a in attn_p], conv_p, xp, w_o_b, row(b_o), *ln1, 512,
                          dils)

    xs = x_sample.reshape(n_s, D_MODEL)
    ones = (1,) * N_GROUPS
    *qkv_s, u_s, kv_s = _project(xs, w_in_b, b_in_r, n_s, n_s, F32, ones)
    caches_t = [jnp.transpose(c[0].reshape(dec_batch, -1, 2 * GROUP_W), (0, 2, 1)) for c in caches]
    attn_s = [_attention_sample(qkv_s[g], caches_t[g], g, dec_seq) for g in range(N_GROUPS)]
    u_hist = jnp.concatenate([state_conv[0], u_s.reshape(dec_batch, dec_seq, CONV_CH)], axis=1)
    conv_s = _conv_sample(u_hist, *conv_w, dec_seq)
    h_s, hpk_s = _outproj([a[0] for a in attn_s], [a[1] for a in attn_s], conv_s, xs, w_o_b, row(b_o), *ln1, n_s,
                          ones)

    idx, wts, pos, counts = _route(h_p, h_s, w_router[0].T.astype(BF16), router_bias[0])
    n_tok = seq + n_s
    starts, *chunks = _chunks(counts, n_tok * TOP_K)
    dest = _dest_rows(idx, pos, starts)
    dest_w = dest.reshape(TOP_K, n_tok // SC_WINDOW, SC_WINDOW).transpose(1, 0, 2)
    tiles = lambda a: a.reshape(-1, PACK_SLABS, LANES)
    flat = lambda a: a.reshape(-1, LANES)
    x_sorted = _sc_scatter_rows(tiles(hpk_p), tiles(hpk_s), dest_w, jnp.zeros((EXPERT_CH, PACK_SLABS, LANES), U32),
                                n_tok * TOP_K)
    y_sorted = _experts(chunks, flat(x_sorted), w_gate[0], w_up[0], w_down[0])
    shared = (ws_gate[0].astype(BF16), ws_up[0].astype(BF16), ws_down[0].astype(BF16))
    comb = functools.partial(_combine, wts.T, *shared, row(ln2_g), row(ln2_b))
    y_table = tiles(y_sorted)

    def finish(h, t0):
        y_slots = _sc_gather_rows(y_table, dest[:, t0:t0 + h.shape[0]].reshape(-1))
        return comb(h, 0, t0, flat(y_slots))

    y_s = finish(h_s, seq)
    y_p = finish(h_p, 0)

    kv_prompt = [_kv_cache(kv_p, g, min(w, seq))[None, None] for g, (w, _) in enumerate(DIL_GROUPS)]
    assert seq >= CONV_WIDTH - 1
    conv_prompt = u_p[-(CONV_WIDTH - 1):]
    kv_s4 = kv_s.reshape(dec_batch, dec_seq, 2, N_GROUPS, HEADS_PER_GROUP, HEAD_DIM)
    kv_sample = [kv_s4[:, :, :, g][None] for g in range(N_GROUPS)]
    conv_sample = u_hist[:, -(CONV_WIDTH - 1):]
    return (y_p[None], y_s.reshape(dec_batch, dec_seq, D_MODEL), *kv_prompt, conv_prompt[None, None],
            *kv_sample, conv_sample[None])
```

```python
import functools

import jax
import jax.numpy as jnp
from jax import lax
from jax.experimental import pallas as pl
from jax.experimental.pallas import tpu as pltpu
from jax.experimental.pallas import tpu_sc as plsc

F32 = jnp.float32
BF16 = jnp.bfloat16
I32 = jnp.int32

D_MODEL = 1024
HEAD_DIM = 64
HEADS_PER_GROUP = 4
GROUP_W = HEADS_PER_GROUP * HEAD_DIM
DIL_GROUPS = ((128, 1), (512, 4), (2048, 16))
N_GROUPS = len(DIL_GROUPS)
ATTN_W = N_GROUPS * GROUP_W
CONV_CH = D_MODEL - ATTN_W
CONV_WIDTH = 31
IN_W = 3 * ATTN_W + 2 * CONV_CH
BAND = 128
N_EXPERTS = 256
TOP_K = 8
N_EXPERT_GROUPS = 8
EXPERTS_PER_GROUP = N_EXPERTS // N_EXPERT_GROUPS
TOPK_GROUPS = 4
EXPERT_FF = 256
ROUTED_SCALE = 2.5
DEPTH = 1
DN_ALPHA = (2 * DEPTH) ** 0.25
LN_EPS = 1e-5
MASKED = -1e30

VMEM_LIMIT_BYTES = 56 * 1024 * 1024


def _params(*sem):
    return pltpu.CompilerParams(dimension_semantics=sem, vmem_limit_bytes=VMEM_LIMIT_BYTES)


def _dot(a, b):
    return jnp.dot(a, b, preferred_element_type=F32)


def _dot_nt(a, b):
    return lax.dot_general(a, b, (((1,), (1,)), ((), ())), preferred_element_type=F32)


def _layer_norm(x, g, b):
    mu = jnp.mean(x, axis=-1, keepdims=True)
    xc = x - mu
    var = jnp.mean(xc * xc, axis=-1, keepdims=True)
    return xc * lax.rsqrt(var + LN_EPS) * g + b


def _silu(x):
    return x * jax.nn.sigmoid(x)


def _alibi_slopes():
    n = N_GROUPS * HEADS_PER_GROUP
    h = jnp.arange(1, n + 1, dtype=F32)
    return (2.0 ** (-8.0 * h / n)).reshape(N_GROUPS, HEADS_PER_GROUP)


LANES = 128


def _proj_kernel(x_ref, w_ref, b_ref, *refs, dils, tm):
    qkv_refs, (u_ref, kv_ref, zs) = refs[:N_GROUPS], refs[N_GROUPS:]
    x = x_ref[...].astype(BF16)
    for part in range(3):
        for g in range(N_GROUPS):
            c0 = part * ATTN_W + g * GROUP_W
            z = _dot(x, w_ref[:, c0:c0 + GROUP_W]) + b_ref[:, c0:c0 + GROUP_W]
            if part > 0:
                kv_ref[:, c0 - ATTN_W:c0 - ATTN_W + GROUP_W] = z
            out, dil = qkv_refs[g], dils[g]
            if dil == 1:
                out[:, part * GROUP_W:(part + 1) * GROUP_W] = z.astype(out.dtype)
                continue
            for half in range(GROUP_W // LANES):
                zs[half] = z[:, half * LANES:(half + 1) * LANES]
            for r in range(dil):
                for half in range(GROUP_W // LANES):
                    c = r * 3 * GROUP_W + part * GROUP_W + half * LANES
                    out[:, c:c + LANES] = zs[half, pl.ds(r, tm // dil, stride=dil), :].astype(out.dtype)
    c0 = 3 * ATTN_W
    a = _dot(x, w_ref[:, c0:c0 + CONV_CH]) + b_ref[:, c0:c0 + CONV_CH]
    gate = _dot(x, w_ref[:, c0 + CONV_CH:]) + b_ref[:, c0 + CONV_CH:]
    u_ref[...] = a * jax.nn.sigmoid(gate)


def _project(x, w_in, b_in, tm, keep, qkv_dtype, dils):
    n = x.shape[0]
    nt = n // tm
    nk = keep // tm
    return pl.pallas_call(
        functools.partial(_proj_kernel, dils=dils, tm=tm),
        grid=(nt,),
        in_specs=[
            pl.BlockSpec((tm, D_MODEL), lambda i: (i, 0)),
            pl.BlockSpec((D_MODEL, IN_W), lambda i: (0, 0)),
            pl.BlockSpec((1, IN_W), lambda i: (0, 0)),
        ],
        out_specs=[pl.BlockSpec((tm // d, d * 3 * GROUP_W), lambda i: (i, 0)) for d in dils] + [
            pl.BlockSpec((tm, CONV_CH), lambda i: (i, 0)),
            pl.BlockSpec((tm, 2 * ATTN_W), lambda i: (jnp.maximum(i - (nt - nk), 0), 0)),
        ],
        out_shape=[jax.ShapeDtypeStruct((n // d, d * 3 * GROUP_W), qkv_dtype) for d in dils] + [
            jax.ShapeDtypeStruct((n, CONV_CH), F32),
            jax.ShapeDtypeStruct((keep, 2 * ATTN_W), F32),
        ],
        scratch_shapes=[pltpu.VMEM((GROUP_W // LANES, tm, LANES), F32)],
        compiler_params=_params("arbitrary"),
        name="proj",
    )(x, w_in, b_in)


def _head_select(parts, rows):
    col = lax.broadcasted_iota(I32, (rows, GROUP_W), 1) // HEAD_DIM
    out = jnp.broadcast_to(parts[-1], (rows, GROUP_W))
    for h in range(HEADS_PER_GROUP - 2, -1, -1):
        out = jnp.where(col == h, parts[h], out)
    return out


def _head_rows(q, rows):
    col = lax.broadcasted_iota(I32, (rows, GROUP_W), 1) // HEAD_DIM
    return jnp.concatenate([jnp.where(col == h, q, jnp.zeros_like(q)) for h in range(HEADS_PER_GROUP)], axis=0)


def _softmax_pv(s, v, rows):
    m = jnp.max(s, axis=-1, keepdims=True)
    e = jnp.exp(s - m)
    l = jnp.sum(e, axis=-1, keepdims=True)
    pv = _dot(e.astype(BF16), v) / l
    lse = m + jnp.log(l)
    o = _head_select([pv[h * rows:(h + 1) * rows] for h in range(HEADS_PER_GROUP)], rows)
    lse_x = _head_select([lse[h * rows:(h + 1) * rows] for h in range(HEADS_PER_GROUP)], rows)
    return o, lse_x


ATTN_QB = 8


def _attn_kernel(q_ref, kp_ref, kc_ref, vp_ref, vc_ref, bias_ref, o_ref, lse_ref):
    b = pl.program_id(1)
    for j in range(ATTN_QB):
        rows = slice(j * BAND, (j + 1) * BAND)
        qm = _head_rows(q_ref[rows, :], BAND)
        if j == 0:
            k = jnp.concatenate([kp_ref[...], kc_ref[rows, :]], axis=0)
            v = jnp.concatenate([vp_ref[...], vc_ref[rows, :]], axis=0)
            bias = bias_ref[jnp.minimum(b, 1)]
        else:
            k = kc_ref[(j - 1) * BAND:(j + 1) * BAND, :]
            v = vc_ref[(j - 1) * BAND:(j + 1) * BAND, :]
            bias = bias_ref[1]
        s = _dot_nt(qm, k) * HEAD_DIM ** -0.5 + bias
        o, lse_x = _softmax_pv(s, v, BAND)
        o_ref[rows, :] = o.astype(o_ref.dtype)
        lse_ref[rows, :] = lse_x


def _prompt_bias(g, dil):
    slopes = _alibi_slopes()[g]
    qi = jnp.arange(BAND)[:, None]
    kj = jnp.arange(2 * BAND)[None, :]
    steps = qi + BAND - kj
    nk = DIL_GROUPS[g][0] // dil
    valid = (steps >= 0) & (steps <= nk)
    bias = -slopes[:, None, None] * (steps * dil).astype(F32)
    inner = jnp.where(valid[None], bias, MASKED)
    first = jnp.where((valid & (kj >= BAND))[None], bias, MASKED)
    return jnp.stack([first, inner]).reshape(2, HEADS_PER_GROUP * BAND, 2 * BAND)


def _attention_prompt(view, g):
    win, dil = DIL_GROUPS[g]
    n_cls = view.shape[0]
    rows = ATTN_QB * BAND
    assert win // dil <= BAND and n_cls % rows == 0 and view.shape[1] == dil * 3 * GROUP_W
    cur = lambda part: pl.BlockSpec((rows, GROUP_W), lambda r, b: (b, r * 3 + part))
    prev = lambda part: pl.BlockSpec((BAND, GROUP_W), lambda r, b: (jnp.maximum(b * ATTN_QB - 1, 0), r * 3 + part))
    return pl.pallas_call(
        _attn_kernel,
        grid=(dil, n_cls // rows),
        in_specs=[cur(0), prev(1), cur(1), prev(2), cur(2),
                  pl.BlockSpec((2, HEADS_PER_GROUP * BAND, 2 * BAND), lambda r, b: (0, 0, 0))],
        out_specs=[pl.BlockSpec((rows, GROUP_W), lambda r, b: (b, r)),
                   pl.BlockSpec((rows, GROUP_W), lambda r, b: (b, r))],
        out_shape=[jax.ShapeDtypeStruct((n_cls, dil * GROUP_W), BF16),
                   jax.ShapeDtypeStruct((n_cls, dil * GROUP_W), F32)],
        compiler_params=_params("arbitrary", "arbitrary"),
        name=f"attn_g{g}",
    )(view, view, view, view, view, _prompt_bias(g, dil))


def _attn_sample_kernel(q_ref, k_ref, v_ref, cache_ref, bias_ref, o_ref, lse_ref, *, n_buf, t, seqs):
    pad = jnp.zeros((BAND - t, GROUP_W), F32)
    scale = HEAD_DIM ** -0.5
    for b in range(seqs):
        rows = slice(b * t, (b + 1) * t)
        k_new = jnp.concatenate([k_ref[rows, :], pad], axis=0).astype(BF16)
        v_new = jnp.concatenate([v_ref[rows, :], pad], axis=0).astype(BF16)
        qm = _head_rows(q_ref[rows, :], t).astype(BF16)
        s_old = _dot(qm, cache_ref[b, 0:GROUP_W, :].astype(BF16)) * scale + bias_ref[:, 0:n_buf]
        s_new = _dot_nt(qm, k_new) * scale + bias_ref[:, n_buf:]
        m = jnp.maximum(jnp.max(s_old, axis=-1, keepdims=True), jnp.max(s_new, axis=-1, keepdims=True))
        e_old = jnp.exp(s_old - m)
        e_new = jnp.exp(s_new - m)
        l = jnp.sum(e_old, axis=-1, keepdims=True) + jnp.sum(e_new, axis=-1, keepdims=True)
        pv = _dot_nt(e_old.astype(BF16), cache_ref[b, GROUP_W:, :].astype(BF16)) + _dot(e_new.astype(BF16), v_new)
        pv = pv / l
        lse = m + jnp.log(l)
        o_ref[rows, :] = _head_select([pv[h * t:(h + 1) * t] for h in range(HEADS_PER_GROUP)], t)
        lse_ref[rows, :] = _head_select([lse[h * t:(h + 1) * t] for h in range(HEADS_PER_GROUP)], t)


def _sample_bias(g, n_buf, t):
    win, dil = DIL_GROUPS[g]
    slopes = _alibi_slopes()[g]
    tq = jnp.arange(t)[:, None]
    j = jnp.arange(n_buf + BAND)[None, :]
    dist = n_buf + tq - j
    valid = (dist >= 0) & (dist % dil == 0) & (dist <= win) & (j < n_buf + t)
    bias = -slopes[:, None, None] * dist.astype(F32)
    return jnp.where(valid[None], bias, MASKED).reshape(HEADS_PER_GROUP * t, n_buf + BAND)


def _attention_sample(qkv, cache_t, g, t):
    nb, n_buf = cache_t.shape[0], cache_t.shape[2]
    seqs = max(1, min(4, 1024 // n_buf))
    assert nb % seqs == 0
    col = lambda part: pl.BlockSpec((seqs * t, GROUP_W), lambda i: (i, part))
    kern = functools.partial(_attn_sample_kernel, n_buf=n_buf, t=t, seqs=seqs)
    return pl.pallas_call(
        kern,
        grid=(nb // seqs,),
        in_specs=[col(0), col(1), col(2),
                  pl.BlockSpec((seqs, 2 * GROUP_W, n_buf), lambda i: (i, 0, 0)),
                  pl.BlockSpec((HEADS_PER_GROUP * t, n_buf + BAND), lambda i: (0, 0))],
        out_specs=[pl.BlockSpec((seqs * t, GROUP_W), lambda i: (i, 0)),
                   pl.BlockSpec((seqs * t, GROUP_W), lambda i: (i, 0))],
        out_shape=[jax.ShapeDtypeStruct((nb * t, GROUP_W), F32),
                   jax.ShapeDtypeStruct((nb * t, GROUP_W), F32)],
        compiler_params=_params("arbitrary"),
        name=f"attn_sample_g{g}",
    )(qkv, qkv, qkv, cache_t, _sample_bias(g, n_buf, t))


CONV_HALO = 32


def _conv_tail(acc, b_ref, g_ref, beta_ref):
    return _silu(_layer_norm(acc + b_ref[...], g_ref[...], beta_ref[...]))


SUBLANES = 8


def _conv_prompt_kernel(halo_ref, u_ref, w_ref, b_ref, g_ref, beta_ref, o_ref, hist, part, *, tm):
    i = pl.program_id(0)
    hist[0:CONV_HALO, :] = jnp.where(i == 0, 0.0, halo_ref[...])
    hist[CONV_HALO:CONV_HALO + tm, :] = u_ref[...]
    hist[CONV_HALO + tm:, :] = jnp.zeros((SUBLANES, CONV_CH), F32)
    off = CONV_HALO - (CONV_WIDTH - 1)
    acc = None
    for s in range(SUBLANES):
        group = None
        for m in range(s, off + CONV_WIDTH, SUBLANES):
            j = m - off
            if j < 0:
                continue
            term = w_ref[j:j + 1, :] * hist[m - s:m - s + tm + SUBLANES, :]
            group = term if group is None else group + term
        part[...] = group
        shifted = part[s:s + tm, :]
        acc = shifted if acc is None else acc + shifted
    o_ref[...] = _conv_tail(acc, b_ref, g_ref, beta_ref).astype(o_ref.dtype)


def _conv_prompt(u, w_dw, b_dw, ln_g, ln_b, tm):
    n = u.shape[0]
    vec = pl.BlockSpec((1, CONV_CH), lambda i: (0, 0))
    return pl.pallas_call(
        functools.partial(_conv_prompt_kernel, tm=tm),
        grid=(n // tm,),
        in_specs=[pl.BlockSpec((CONV_HALO, CONV_CH), lambda i: (jnp.maximum(i * (tm // CONV_HALO) - 1, 0), 0)),
                  pl.BlockSpec((tm, CONV_CH), lambda i: (i, 0)),
                  pl.BlockSpec((CONV_WIDTH, CONV_CH), lambda i: (0, 0)), vec, vec, vec],
        out_specs=pl.BlockSpec((tm, CONV_CH), lambda i: (i, 0)),
        out_shape=jax.ShapeDtypeStruct((n, CONV_CH), BF16),
        scratch_shapes=[pltpu.VMEM((CONV_HALO + tm + SUBLANES, CONV_CH), F32),
                        pltpu.VMEM((tm + SUBLANES, CONV_CH), F32)],
        compiler_params=_params("arbitrary"),
        name="conv_prompt",
    )(u, u, w_dw, b_dw, ln_g, ln_b)


def _conv_sample_kernel(hist_ref, w_ref, b_ref, g_ref, beta_ref, o_ref, *, t):
    acc = jnp.zeros((t, CONV_CH), F32)
    for j in range(CONV_WIDTH):
        acc = acc + w_ref[j:j + 1, :] * hist_ref[0, j:j + t, :]
    o_ref[...] = _conv_tail(acc, b_ref, g_ref, beta_ref)


def _conv_sample(u_hist, w_dw, b_dw, ln_g, ln_b, t):
    nb, rows = u_hist.shape[0], u_hist.shape[1]
    vec = pl.BlockSpec((1, CONV_CH), lambda i: (0, 0))
    return pl.pallas_call(
        functools.partial(_conv_sample_kernel, t=t),
        grid=(nb,),
        in_specs=[pl.BlockSpec((1, rows, CONV_CH), lambda i: (i, 0, 0)),
                  pl.BlockSpec((CONV_WIDTH, CONV_CH), lambda i: (0, 0)), vec, vec, vec],
        out_specs=pl.BlockSpec((t, CONV_CH), lambda i: (i, 0)),
        out_shape=jax.ShapeDtypeStruct((nb * t, CONV_CH), F32),
        compiler_params=_params("arbitrary"),
        name="conv_sample",
    )(u_hist, w_dw, b_dw, ln_g, ln_b)


U32 = jnp.uint32
HALF_D = D_MODEL // 2
PACK_SLABS = HALF_D // LANES


def _token_order(ref, dil, scr, tm):
    if dil == 1:
        return ref[...].astype(F32)
    for r in range(dil):
        for half in range(GROUP_W // LANES):
            c = r * GROUP_W + half * LANES
            scr[half, pl.ds(r, tm // dil, stride=dil), :] = ref[:, c:c + LANES].astype(F32)
    return jnp.concatenate([scr[half] for half in range(GROUP_W // LANES)], axis=1)


def _pack_rows(ref, val):
    rows = val.shape[0]
    bits = lax.bitcast_convert_type(val.astype(BF16).astype(F32), U32)
    packed = lax.shift_right_logical(bits[:, :HALF_D], jnp.uint32(16)) | bits[:, HALF_D:]
    for c in range(PACK_SLABS):
        ref[pl.ds(c, rows, stride=PACK_SLABS), :] = packed[:, c * LANES:(c + 1) * LANES]


def _unpack_rows(ref):
    rows = ref.shape[0] // PACK_SLABS
    u = jnp.concatenate([ref[pl.ds(c, rows, stride=PACK_SLABS), :] for c in range(PACK_SLABS)], axis=1)
    lo = lax.bitcast_convert_type(lax.shift_left(u, jnp.uint32(16)), F32)
    hi = lax.bitcast_convert_type(u & jnp.uint32(0xFFFF0000), F32)
    return lo, hi


def _outproj_kernel(o0, o1, o2, l0, l1, l2, c_ref, x_ref, w_ref, b_ref, g_ref, beta_ref, h_ref, hp_ref, *scr,
                    dils, tm):
    os_ = [_token_order(r, d, scr[2 * g], tm) for g, (r, d) in enumerate(zip((o0, o1, o2), dils))]
    ls = [_token_order(r, d, scr[2 * g + 1], tm) for g, (r, d) in enumerate(zip((l0, l1, l2), dils))]
    m = jnp.maximum(jnp.maximum(ls[0], ls[1]), ls[2])
    es = [jnp.exp(l - m) for l in ls]
    inv = 1.0 / (es[0] + es[1] + es[2])
    mixed = b_ref[...]
    for g in range(N_GROUPS):
        a = (os_[g] * (es[g] * inv)).astype(BF16)
        mixed = mixed + _dot(a, w_ref[g * GROUP_W:(g + 1) * GROUP_W, :])
    mixed = mixed + _dot(c_ref[...].astype(BF16), w_ref[ATTN_W:, :])
    h = _layer_norm(DN_ALPHA * x_ref[...] + mixed, g_ref[...], beta_ref[...])
    h_ref[...] = h
    _pack_rows(hp_ref, h)


def _outproj(os_, lses, conv, x, w_o, b_o, ln_g, ln_b, tm, dils):
    n = x.shape[0]
    grp = [pl.BlockSpec((tm // d, d * GROUP_W), lambda i: (i, 0)) for d in dils]
    vec = pl.BlockSpec((1, D_MODEL), lambda i: (0, 0))
    return pl.pallas_call(
        functools.partial(_outproj_kernel, dils=dils, tm=tm),
        grid=(n // tm,),
        in_specs=grp + grp + [pl.BlockSpec((tm, CONV_CH), lambda i: (i, 0)),
                              pl.BlockSpec((tm, D_MODEL), lambda i: (i, 0)),
                              pl.BlockSpec((D_MODEL, D_MODEL), lambda i: (0, 0)), vec, vec, vec],
        out_specs=[pl.BlockSpec((tm, D_MODEL), lambda i: (i, 0)),
                   pl.BlockSpec((tm * PACK_SLABS, LANES), lambda i: (i, 0))],
        out_shape=[jax.ShapeDtypeStruct((n, D_MODEL), F32),
                   jax.ShapeDtypeStruct((n * PACK_SLABS, LANES), U32)],
        scratch_shapes=[pltpu.VMEM((GROUP_W // LANES, tm, LANES), F32) for _ in range(2 * N_GROUPS)],
        compiler_params=_params("arbitrary"),
        name="outproj",
    )(*os_, *lses, conv, x, w_o, b_o, ln_g, ln_b)


ROUTER_TM = 256


def _first_index(hit, idx, limit, axis):
    return jnp.min(jnp.where(hit, idx, limit), axis=axis, keepdims=True)


def _router_kernel(ha_ref, hb_ref, w_ref, rb_ref, tri_ref, idx_ref, wt_ref, pos_ref, cnt_ref, run, *, tiles_a):
    i = pl.program_id(0)
    tm = ROUTER_TM

    @pl.when(i == 0)
    def _():
        run[...] = jnp.zeros_like(run)

    h = jnp.where(i < tiles_a, ha_ref[...], hb_ref[...])
    logits = _dot_nt(w_ref[...], h.astype(BF16))
    scores = jax.nn.sigmoid(logits)
    biased = scores + rb_ref[...]
    groups = [biased[g * EXPERTS_PER_GROUP:(g + 1) * EXPERTS_PER_GROUP] for g in range(N_EXPERT_GROUPS)]
    ei = lax.broadcasted_iota(I32, (EXPERTS_PER_GROUP, tm), 0).astype(F32)
    gs = []
    for bg in groups:
        m1 = jnp.max(bg, axis=0, keepdims=True)
        f1 = _first_index(bg == m1, ei, float(EXPERTS_PER_GROUP), 0)
        m2 = jnp.max(jnp.where(ei == f1, -jnp.inf, bg), axis=0, keepdims=True)
        gs.append(m1 + m2)
    gs = jnp.concatenate(gs, axis=0)
    gi = lax.broadcasted_iota(I32, gs.shape, 0).astype(F32)
    keep = jnp.zeros(gs.shape, F32)
    cur = gs
    for _ in range(TOPK_GROUPS):
        m = jnp.max(cur, axis=0, keepdims=True)
        f = _first_index(cur == m, gi, float(N_EXPERT_GROUPS), 0)
        hit = gi == f
        keep = jnp.where(hit, 1.0, keep)
        cur = jnp.where(hit, -jnp.inf, cur)
    masked = jnp.concatenate([jnp.where(keep[g:g + 1] > 0.0, bg, -jnp.inf) for g, bg in enumerate(groups)], axis=0)
    xi = lax.broadcasted_iota(I32, (N_EXPERTS, tm), 0).astype(F32)
    cur = masked
    picks = []
    for _ in range(TOP_K):
        m = jnp.max(cur, axis=0, keepdims=True)
        f = _first_index(cur == m, xi, float(N_EXPERTS), 0)
        hit = xi == f
        picks.append((f, hit))
        cur = jnp.where(hit, -jnp.inf, cur)
    sel = jnp.where(cur != masked, 1.0, 0.0)
    before = _dot(sel.astype(BF16), tri_ref[...]) + run[...]
    run[...] = run[...] + jnp.sum(sel, axis=1, keepdims=True)
    ws = [jnp.sum(jnp.where(hit, scores, 0.0), axis=0, keepdims=True) for _, hit in picks]
    wsum = ws[0]
    for w in ws[1:]:
        wsum = wsum + w
    for k, (f, hit) in enumerate(picks):
        idx_ref[k:k + 1, :] = f.astype(I32)
        wt_ref[k:k + 1, :] = ws[k] / wsum * ROUTED_SCALE
        pos_ref[k:k + 1, :] = jnp.sum(jnp.where(hit, before, 0.0), axis=0, keepdims=True).astype(I32)
    cnt_ref[...] = jnp.broadcast_to(run[...], cnt_ref.shape).astype(I32)


def _route(ha, hb, w_router_t, router_bias):
    tm = ROUTER_TM
    tiles_a, tiles_b = ha.shape[0] // tm, hb.shape[0] // tm
    spec_a = pl.BlockSpec((tm, D_MODEL), lambda i: (jnp.minimum(i, tiles_a - 1), 0))
    spec_b = pl.BlockSpec((tm, D_MODEL), lambda i: (jnp.maximum(i - tiles_a, 0), 0))
    t = (tiles_a + tiles_b) * tm
    tri = (jnp.arange(tm)[:, None] < jnp.arange(tm)[None, :]).astype(BF16)
    slot = pl.BlockSpec((TOP_K, tm), lambda i: (0, i))
    idx, wts, pos, cnt = pl.pallas_call(
        functools.partial(_router_kernel, tiles_a=tiles_a),
        grid=(t // tm,),
        in_specs=[spec_a, spec_b,
                  pl.BlockSpec((N_EXPERTS, D_MODEL), lambda i: (0, 0)),
                  pl.BlockSpec((N_EXPERTS, 1), lambda i: (0, 0)),
                  pl.BlockSpec((tm, tm), lambda i: (0, 0))],
        out_specs=[slot, slot, slot, pl.BlockSpec((N_EXPERTS, 128), lambda i: (0, 0))],
        out_shape=[jax.ShapeDtypeStruct((TOP_K, t), I32), jax.ShapeDtypeStruct((TOP_K, t), F32),
                   jax.ShapeDtypeStruct((TOP_K, t), I32), jax.ShapeDtypeStruct((N_EXPERTS, 128), I32)],
        scratch_shapes=[pltpu.VMEM((N_EXPERTS, 1), F32)],
        compiler_params=_params("arbitrary"),
        name="router",
    )(ha, hb, w_router_t, router_bias.reshape(N_EXPERTS, 1), tri)
    return idx, wts, pos, cnt[:, 0]


DEST_TM = 1280


def _dest_kernel(idx_ref, pos_ref, starts_ref, dest_ref):
    tm = idx_ref.shape[1]
    ei = lax.broadcasted_iota(I32, (N_EXPERTS, tm), 0)
    starts = starts_ref[...]
    for k in range(TOP_K):
        first = jnp.sum(jnp.where(ei == idx_ref[k:k + 1, :], starts, 0.0), axis=0, keepdims=True)
        dest_ref[k:k + 1, :] = first.astype(I32) + pos_ref[k:k + 1, :]


def _dest_rows(idx, pos, starts):
    t = idx.shape[1]
    tm = DEST_TM if t % DEST_TM == 0 else ROUTER_TM
    assert t % tm == 0
    slot = pl.BlockSpec((TOP_K, tm), lambda i: (0, i))
    return pl.pallas_call(
        _dest_kernel,
        grid=(t // tm,),
        in_specs=[slot, slot, pl.BlockSpec((N_EXPERTS, 1), lambda i: (0, 0))],
        out_specs=slot,
        out_shape=jax.ShapeDtypeStruct((TOP_K, t), I32),
        compiler_params=_params("arbitrary"),
        name="dest_rows",
    )(idx, pos, starts.astype(F32).reshape(N_EXPERTS, 1))


SC_CORES = 2
SC_SUBCORES = 16
SC_WORKERS = SC_CORES * SC_SUBCORES
SC_WINDOW = 64
SC_WINDOW_SMALL = 32


def _sc_worker():
    return lax.axis_index("s") * SC_CORES + lax.axis_index("c")


def _sc_mesh():
    return plsc.VectorSubcoreMesh(core_axis_name="c", subcore_axis_name="s")


def _sc_scatter_rows(ha, hb, dest_w, zeros, n_out):
    w = dest_w.shape[2]
    na, nb = ha.shape[0], hb.shape[0]
    spare = zeros.shape[0]
    wa, wb = na // w, nb // w
    assert na % (SC_WORKERS * 2 * w) == 0 and nb % w == 0 and wb <= SC_WORKERS
    per_w = wa // SC_WORKERS
    rows_t = pltpu.VMEM((w,) + ha.shape[1:], ha.dtype)
    idx_t = pltpu.VMEM((TOP_K, w), I32)

    @functools.partial(
        pl.kernel, mesh=_sc_mesh(),
        out_type=jax.ShapeDtypeStruct((n_out + spare,) + ha.shape[1:], ha.dtype),
        scratch_types=[rows_t, rows_t, idx_t, idx_t, pltpu.SemaphoreType.DMA, pltpu.SemaphoreType.DMA],
    )
    def scatter(ha_hbm, hb_hbm, dest_hbm, zeros_hbm, out_hbm, rows0, rows1, idx0, idx1, sem0, sem1):
        wid = _sc_worker()
        bufs = ((rows0, idx0, sem0), (rows1, idx1, sem1))

        @pl.when(wid == SC_WORKERS - 1)
        def _():
            pltpu.sync_copy(zeros_hbm, out_hbm.at[pl.ds(n_out, spare)])

        def load(src_hbm, row0, win, b):
            pltpu.sync_copy(src_hbm.at[pl.ds(pl.multiple_of(row0, 8), w)], bufs[b][0])
            pltpu.sync_copy(dest_hbm.at[win], bufs[b][1])

        def copies(b):
            return [pltpu.make_async_copy(bufs[b][0], out_hbm.at[bufs[b][1].at[k]], bufs[b][2])
                    for k in range(TOP_K)]

        win0 = wid * per_w
        load(ha_hbm, win0 * w, win0, 0)

        @pl.loop(0, per_w, step=2)
        def _(i0):
            for b in range(2):
                i = i0 + b
                for c in copies(b):
                    c.start()

                @pl.when(i + 1 < per_w)
                def _():
                    load(ha_hbm, (win0 + i + 1) * w, win0 + i + 1, 1 - b)

                for c in copies(b):
                    c.wait()

        @pl.when(wid < wb)
        def _():
            load(hb_hbm, wid * w, wa + wid, 0)
            for c in copies(0):
                c.start()
            for c in copies(0):
                c.wait()

    return scatter(ha, hb, dest_w, zeros)


def _sc_gather_rows(table, idx):
    n = idx.shape[0]
    w = SC_WINDOW if n % (SC_WORKERS * 2 * SC_WINDOW) == 0 else SC_WINDOW_SMALL
    assert n % (SC_WORKERS * 2 * w) == 0
    per_w = n // SC_WORKERS
    nwin = per_w // w
    rows_t = pltpu.VMEM((w,) + table.shape[1:], table.dtype)

    @functools.partial(
        pl.kernel, mesh=_sc_mesh(),
        out_type=jax.ShapeDtypeStruct((n,) + table.shape[1:], table.dtype),
        scratch_types=[pltpu.VMEM((per_w,), I32), rows_t, rows_t, pltpu.SemaphoreType.DMA, pltpu.SemaphoreType.DMA],
    )
    def gather(table_hbm, idx_hbm, out_hbm, idx_v, rows0, rows1, sem0, sem1):
        base = pl.multiple_of(_sc_worker() * per_w, 8)
        pltpu.sync_copy(idx_hbm.at[pl.ds(base, per_w)], idx_v)
        bufs = ((rows0, sem0), (rows1, sem1))
        fetch = lambda i, b: pltpu.make_async_copy(
            table_hbm.at[idx_v.at[pl.ds(pl.multiple_of(i * w, 8), w)]], bufs[b][0], bufs[b][1])
        fetch(0, 0).start()

        @pl.loop(0, nwin, step=2)
        def _(i0):
            for b in range(2):
                i = i0 + b
                fetch(i, b).wait()

                @pl.when(i + 1 < nwin)
                def _():
                    fetch(i + 1, 1 - b).start()

                pltpu.sync_copy(bufs[b][0], out_hbm.at[pl.ds(pl.multiple_of(base + i * w, 8), w)])

    return gather(table, idx)


EXPERT_CH = 256
X_SLOTS = 6
W_SLOTS = 3


def _expert_kernel(row0_ref, exp_ref, live_ref, fresh_ref, next1_ref, next2_ref, wslot_ref, ordered_ref, xs_ref,
                   wg_ref, wu_ref, wd_ref, ys_ref, xbuf, ybuf, sg, su, sd, wgb, wub, wdb, xsem, ysem, wsem):
    v = pl.program_id(0)
    nv = pl.num_programs(0)
    slot = lax.rem(v, X_SLOTS)
    yslot = lax.rem(v, 3)
    rows = lambda u: pl.ds(pl.multiple_of(row0_ref[u] * PACK_SLABS, PACK_SLABS), EXPERT_CH * PACK_SLABS)
    x_copy = lambda u: pltpu.make_async_copy(xs_ref.at[rows(u)], xbuf.at[lax.rem(u, X_SLOTS)],
                                             xsem.at[lax.rem(u, X_SLOTS)])
    y_copy = lambda u: pltpu.make_async_copy(ybuf.at[lax.rem(u, 3)], ys_ref.at[rows(u)], ysem.at[lax.rem(u, 3)])
    prev1, prev2 = jnp.maximum(v - 1, 0), jnp.maximum(v - 2, 0)
    w_copies = lambda e, s: [pltpu.make_async_copy(w.at[e], stage.at[s], wsem.at[s])
                             for w, stage in ((wg_ref, sg), (wu_ref, su), (wd_ref, sd))]
    spare = pl.ds(ys_ref.shape[0] - EXPERT_CH * PACK_SLABS, EXPERT_CH * PACK_SLABS)

    @pl.when(v == 0)
    def _():
        for c in w_copies(exp_ref[0], 0):
            c.start(priority=1)

        @pl.when(next1_ref[0] >= 0)
        def _():
            for c in w_copies(next1_ref[0], 1):
                c.start(priority=1)

        x_copy(0).start()
        for u in range(1, X_SLOTS - 1):
            @pl.when((u < nv) & (live_ref[jnp.minimum(u, nv - 1)] == 1))
            def _():
                x_copy(u).start()
        ybuf[2] = jnp.zeros(ybuf.shape[1:], U32)
        zero = pltpu.make_async_copy(ybuf.at[2], ys_ref.at[spare], ysem.at[2])
        zero.start()
        zero.wait()

    ahead = jnp.minimum(v + X_SLOTS - 1, nv - 1)

    @pl.when((v + X_SLOTS - 1 < nv) & (live_ref[ahead] == 1))
    def _():
        x_copy(ahead).start()

    @pl.when(fresh_ref[v] == 1)
    def _():
        p = wslot_ref[v]
        for c in w_copies(exp_ref[v], p):
            c.wait()

        @pl.when(next2_ref[v] >= 0)
        def _():
            for c in w_copies(next2_ref[v], lax.rem(p + 2, W_SLOTS)):
                c.start(priority=1)

        wgb[...] = sg[p].astype(BF16)
        wub[...] = su[p].astype(BF16)
        wdb[...] = sd[p].astype(BF16)

    @pl.when(live_ref[v] == 1)
    def _():
        x_copy(v).wait()
        x = jnp.concatenate(_unpack_rows(xbuf.at[slot]), axis=1).astype(BF16)
        hid = (_silu(_dot(x, wgb[...])) * _dot(x, wub[...])).astype(BF16)
        _pack_rows(ybuf.at[yslot], _dot(hid, wdb[...]))

    @pl.when((v >= 2) & (live_ref[prev2] == 1) & (ordered_ref[prev1] == 0))
    def _():
        y_copy(prev2).wait()

    @pl.when((v >= 1) & (live_ref[prev1] == 1) & (ordered_ref[v] == 1))
    def _():
        y_copy(prev1).wait()

    @pl.when(live_ref[v] == 1)
    def _():
        y_copy(v).start()

    @pl.when(v == nv - 1)
    def _():
        @pl.when((v >= 1) & (live_ref[prev1] == 1) & (ordered_ref[v] == 0))
        def _():
            y_copy(prev1).wait()

        @pl.when(live_ref[v] == 1)
        def _():
            y_copy(v).wait()


def _chunks(counts, n_rows):
    ch = EXPERT_CH
    nv = n_rows // ch + N_EXPERTS
    ends = jnp.cumsum(counts)
    starts = ends - counts
    nch = (counts + ch - 1) // ch
    cend = jnp.cumsum(nch)
    cstart = cend - nch
    v = jnp.arange(nv, dtype=I32)
    live = v < cend[-1]
    vc = jnp.minimum(v, cend[-1] - 1)
    ids = jnp.arange(N_EXPERTS, dtype=I32)
    e = jnp.minimum(jnp.sum(cend[None, :] <= vc[:, None], axis=1), N_EXPERTS - 1).astype(I32)
    of_e = lambda a: jnp.sum(jnp.where(e[:, None] == ids[None, :], a[None, :], 0), axis=1)
    starts_e, ends_e, count_e = of_e(starts), of_e(ends), of_e(counts)
    c = vc - of_e(cstart)
    is_last = c == of_e(nch) - 1
    row0 = jnp.where(is_last & (count_e >= ch), ends_e - ch, starts_e + c * ch).astype(I32)
    overrun = live & is_last & (count_e < ch)
    ordered = jnp.concatenate([jnp.zeros((1,), bool), overrun[:-1]]) & live
    fresh = jnp.concatenate([jnp.ones((1,), I32), (e[1:] != e[:-1]).astype(I32)])
    order = jnp.cumsum(counts > 0) - 1
    order_e = of_e(order)

    def after(k):
        hit = (counts > 0)[None, :] & (order[None, :] == order_e[:, None] + k)
        return (jnp.sum(jnp.where(hit, ids[None, :] + 1, 0), axis=1) - 1).astype(I32)

    return (starts.astype(I32), row0, e, live.astype(I32), jnp.where(live, fresh, 0).astype(I32),
            after(1), after(2), (order_e % W_SLOTS).astype(I32), ordered.astype(I32))


def _experts(chunks, xs, w_gate, w_up, w_down):
    ch = EXPERT_CH
    buf = lambda n: pltpu.VMEM((n, ch * PACK_SLABS, LANES), U32)
    anywhere = pl.BlockSpec(memory_space=pl.ANY)
    return pl.pallas_call(
        _expert_kernel,
        grid_spec=pltpu.PrefetchScalarGridSpec(
            num_scalar_prefetch=len(chunks),
            grid=(chunks[0].shape[0],),
            in_specs=[anywhere] * 4,
            out_specs=anywhere,
            scratch_shapes=[buf(X_SLOTS), buf(3),
                            pltpu.VMEM((W_SLOTS, D_MODEL, EXPERT_FF), F32),
                            pltpu.VMEM((W_SLOTS, D_MODEL, EXPERT_FF), F32),
                            pltpu.VMEM((W_SLOTS, EXPERT_FF, D_MODEL), F32),
                            pltpu.VMEM((D_MODEL, EXPERT_FF), BF16), pltpu.VMEM((D_MODEL, EXPERT_FF), BF16),
                            pltpu.VMEM((EXPERT_FF, D_MODEL), BF16),
                            pltpu.SemaphoreType.DMA((X_SLOTS,)), pltpu.SemaphoreType.DMA((3,)),
                            pltpu.SemaphoreType.DMA((W_SLOTS,))],
        ),
        out_shape=jax.ShapeDtypeStruct(xs.shape, U32),
        compiler_params=_params("arbitrary"),
        name="experts",
    )(*chunks, xs, w_gate, w_up, w_down)


COMBINE_TM = 256


def _combine_kernel(wt_ref, h_ref, sg_ref, su_ref, sd_ref, g_ref, beta_ref, *refs):
    y_refs, o_ref = refs[:TOP_K], refs[TOP_K]
    h = h_ref[...]
    hb = h.astype(BF16)
    hid = (_silu(_dot(hb, sg_ref[...])) * _dot(hb, su_ref[...])).astype(BF16)
    acc = DN_ALPHA * h + _dot(hid, sd_ref[...])
    lo = acc[:, :HALF_D]
    hi = acc[:, HALF_D:]
    for k in range(TOP_K):
        y_lo, y_hi = _unpack_rows(y_refs[k])
        w = wt_ref[:, k:k + 1]
        lo = lo + w * y_lo
        hi = hi + w * y_hi
    o_ref[...] = _layer_norm(jnp.concatenate([lo, hi], axis=1), g_ref[...], beta_ref[...])


def _combine(wts_tk, ws_gate, ws_up, ws_down, ln_g, ln_b, h, h_row0, t0, yg):
    tm = COMBINE_TM
    n = yg.shape[0] // (TOP_K * PACK_SLABS)
    assert n % tm == 0 and h_row0 % tm == 0 and t0 % tm == 0
    b0 = t0 // tm
    per_slot = n // tm
    rows = pl.BlockSpec((tm, D_MODEL), lambda i: (h_row0 // tm + i, 0))
    slot_rows = [pl.BlockSpec((tm * PACK_SLABS, LANES), lambda i, k=k: (k * per_slot + i, 0))
                 for k in range(TOP_K)]
    vec = pl.BlockSpec((1, D_MODEL), lambda i: (0, 0))
    return pl.pallas_call(
        _combine_kernel,
        grid=(n // tm,),
        in_specs=[pl.BlockSpec((tm, TOP_K), lambda i: (b0 + i, 0)), rows,
                  pl.BlockSpec((D_MODEL, EXPERT_FF), lambda i: (0, 0)),
                  pl.BlockSpec((D_MODEL, EXPERT_FF), lambda i: (0, 0)),
                  pl.BlockSpec((EXPERT_FF, D_MODEL), lambda i: (0, 0)),
                  vec, vec] + slot_rows,
        out_specs=pl.BlockSpec((tm, D_MODEL), lambda i: (i, 0)),
        out_shape=jax.ShapeDtypeStruct((n, D_MODEL), F32),
        compiler_params=_params("arbitrary"),
        name="combine",
    )(wts_tk, h, ws_gate, ws_up, ws_down, ln_g, ln_b, *([yg] * TOP_K))


def _kv_cache(kv_tail, g, keep):
    k = kv_tail[-keep:, g * GROUP_W:(g + 1) * GROUP_W]
    v = kv_tail[-keep:, ATTN_W + g * GROUP_W:ATTN_W + (g + 1) * GROUP_W]
    return jnp.stack([k, v], axis=1).reshape(keep, 2, HEADS_PER_GROUP, HEAD_DIM)


def kernel(x_prompt, x_sample, cache_kv_w128, cache_kv_w512, cache_kv_w2048, state_conv, w_in, b_in, w_dw, b_dw,
           conv_ln_g, conv_ln_b, w_o, b_o, ln1_g, ln1_b, w_router, router_bias, w_gate, w_up, w_down, ws_gate,
           ws_up, ws_down, ln2_g, ln2_b):
    assert w_in.shape[0] == DEPTH == 1
    batch, seq, _ = x_prompt.shape
    dec_batch, dec_seq, _ = x_sample.shape
    assert batch == 1
    n_s = dec_batch * dec_seq
    caches = (cache_kv_w128, cache_kv_w512, cache_kv_w2048)
    row = lambda a: a[0].reshape(1, -1)

    w_in_b = w_in[0].astype(BF16)
    b_in_r = row(b_in)
    w_o_b = w_o[0].astype(BF16)
    conv_w = (w_dw[0], row(b_dw), row(conv_ln_g), row(conv_ln_b))
    ln1 = (row(ln1_g), row(ln1_b))

    xp = x_prompt[0]
    keep_p = min(max(w for w, _ in DIL_GROUPS), seq)
    dils = tuple(d for _, d in DIL_GROUPS)
    *qkv_p, u_p, kv_p = _project(xp, w_in_b, b_in_r, 512, keep_p, BF16, dils)
    attn_p = [_attention_prompt(qkv_p[g], g) for g in range(N_GROUPS)]
    conv_p = _conv_prompt(u_p, *conv_w, 512)
    h_p, hpk_p = _outproj([a[0] for a in attn_p], [a[1] for a in attn_p], conv_p, xp, w_o_b, row(b_o), *ln1, 512,
                          dils)

    xs = x_sample.reshape(n_s, D_MODEL)
    ones = (1,) * N_GROUPS
    *qkv_s, u_s, kv_s = _project(xs, w_in_b, b_in_r, n_s, n_s, F32, ones)
    caches_t = [jnp.transpose(c[0].reshape(dec_batch, -1, 2 * GROUP_W), (0, 2, 1)) for c in caches]
    attn_s = [_attention_sample(qkv_s[g], caches_t[g], g, dec_seq) for g in range(N_GROUPS)]
    u_hist = jnp.concatenate([state_conv[0], u_s.reshape(dec_batch, dec_seq, CONV_CH)], axis=1)
    conv_s = _conv_sample(u_hist, *conv_w, dec_seq)
    h_s, hpk_s = _outproj([a[0] for a in attn_s], [a[1] for a in attn_s], conv_s, xs, w_o_b, row(b_o), *ln1, n_s,
                          ones)

    idx, wts, pos, counts = _route(h_p, h_s, w_router[0].T.astype(BF16), router_bias[0])
    n_tok = seq + n_s
    starts, *chunks = _chunks(counts, n_tok * TOP_K)
    dest = _dest_rows(idx, pos, starts)
    dest_w = dest.reshape(TOP_K, n_tok // SC_WINDOW, SC_WINDOW).transpose(1, 0, 2)
    tiles = lambda a: a.reshape(-1, PACK_SLABS, LANES)
    flat = lambda a: a.reshape(-1, LANES)
    x_sorted = _sc_scatter_rows(tiles(hpk_p), tiles(hpk_s), dest_w, jnp.zeros((EXPERT_CH, PACK_SLABS, LANES), U32),
                                n_tok * TOP_K)
    y_sorted = _experts(chunks, flat(x_sorted), w_gate[0], w_up[0], w_down[0])
    shared = (ws_gate[0].astype(BF16), ws_up[0].astype(BF16), ws_down[0].astype(BF16))
    comb = functools.partial(_combine, wts.T, *shared, row(ln2_g), row(ln2_b))
    y_table = tiles(y_sorted)

    def finish(h, t0):
        y_slots = _sc_gather_rows(y_table, dest[:, t0:t0 + h.shape[0]].reshape(-1))
        return comb(h, 0, t0, flat(y_slots))

    y_s = finish(h_s, seq)
    y_p = finish(h_p, 0)

    kv_prompt = [_kv_cache(kv_p, g, min(w, seq))[None, None] for g, (w, _) in enumerate(DIL_GROUPS)]
    assert seq >= CONV_WIDTH - 1
    conv_prompt = u_p[-(CONV_WIDTH - 1):]
    kv_s4 = kv_s.reshape(dec_batch, dec_seq, 2, N_GROUPS, HEADS_PER_GROUP, HEAD_DIM)
    kv_sample = [kv_s4[:, :, :, g][None] for g in range(N_GROUPS)]
    conv_sample = u_hist[:, -(CONV_WIDTH - 1):]
    return (y_p[None], y_s.reshape(dec_batch, dec_seq, D_MODEL), *kv_prompt, conv_prompt[None, None],
            *kv_sample, conv_sample[None])
```

```python
import functools

import jax
import jax.numpy as jnp
from jax import lax
from jax.experimental import pallas as pl
from jax.experimental.pallas import tpu as pltpu
from jax.experimental.pallas import tpu_sc as plsc

F32 = jnp.float32
BF16 = jnp.bfloat16
I32 = jnp.int32

D_MODEL = 1024
HEAD_DIM = 64
HEADS_PER_GROUP = 4
GROUP_W = HEADS_PER_GROUP * HEAD_DIM
DIL_GROUPS = ((128, 1), (512, 4), (2048, 16))
N_GROUPS = len(DIL_GROUPS)
ATTN_W = N_GROUPS * GROUP_W
CONV_CH = D_MODEL - ATTN_W
CONV_WIDTH = 31
IN_W = 3 * ATTN_W + 2 * CONV_CH
BAND = 128
N_EXPERTS = 256
TOP_K = 8
N_EXPERT_GROUPS = 8
EXPERTS_PER_GROUP = N_EXPERTS // N_EXPERT_GROUPS
TOPK_GROUPS = 4
EXPERT_FF = 256
ROUTED_SCALE = 2.5
DEPTH = 1
DN_ALPHA = (2 * DEPTH) ** 0.25
LN_EPS = 1e-5
MASKED = -1e30

VMEM_LIMIT_BYTES = 56 * 1024 * 1024


def _params(*sem):
    return pltpu.CompilerParams(dimension_semantics=sem, vmem_limit_bytes=VMEM_LIMIT_BYTES)


def _dot(a, b):
    return jnp.dot(a, b, preferred_element_type=F32)


def _dot_nt(a, b):
    return lax.dot_general(a, b, (((1,), (1,)), ((), ())), preferred_element_type=F32)


def _layer_norm(x, g, b):
    mu = jnp.mean(x, axis=-1, keepdims=True)
    xc = x - mu
    var = jnp.mean(xc * xc, axis=-1, keepdims=True)
    return xc * lax.rsqrt(var + LN_EPS) * g + b


def _silu(x):
    return x * jax.nn.sigmoid(x)


def _alibi_slopes():
    n = N_GROUPS * HEADS_PER_GROUP
    h = jnp.arange(1, n + 1, dtype=F32)
    return (2.0 ** (-8.0 * h / n)).reshape(N_GROUPS, HEADS_PER_GROUP)


LANES = 128


def _proj_kernel(x_ref, w_ref, b_ref, *refs, dils, tm):
    qkv_refs, (u_ref, kv_ref, zs) = refs[:N_GROUPS], refs[N_GROUPS:]
    x = x_ref[...].astype(BF16)
    for part in range(3):
        for g in range(N_GROUPS):
            c0 = part * ATTN_W + g * GROUP_W
            z = _dot(x, w_ref[:, c0:c0 + GROUP_W]) + b_ref[:, c0:c0 + GROUP_W]
            if part > 0:
                kv_ref[:, c0 - ATTN_W:c0 - ATTN_W + GROUP_W] = z
            out, dil = qkv_refs[g], dils[g]
            if dil == 1:
                out[:, part * GROUP_W:(part + 1) * GROUP_W] = z.astype(out.dtype)
                continue
            for half in range(GROUP_W // LANES):
                zs[half] = z[:, half * LANES:(half + 1) * LANES]
            for r in range(dil):
                for half in range(GROUP_W // LANES):
                    c = r * 3 * GROUP_W + part * GROUP_W + half * LANES
                    out[:, c:c + LANES] = zs[half, pl.ds(r, tm // dil, stride=dil), :].astype(out.dtype)
    c0 = 3 * ATTN_W
    a = _dot(x, w_ref[:, c0:c0 + CONV_CH]) + b_ref[:, c0:c0 + CONV_CH]
    gate = _dot(x, w_ref[:, c0 + CONV_CH:]) + b_ref[:, c0 + CONV_CH:]
    u_ref[...] = a * jax.nn.sigmoid(gate)


def _project(x, w_in, b_in, tm, keep, qkv_dtype, dils):
    n = x.shape[0]
    nt = n // tm
    nk = keep // tm
    return pl.pallas_call(
        functools.partial(_proj_kernel, dils=dils, tm=tm),
        grid=(nt,),
        in_specs=[
            pl.BlockSpec((tm, D_MODEL), lambda i: (i, 0)),
            pl.BlockSpec((D_MODEL, IN_W), lambda i: (0, 0)),
            pl.BlockSpec((1, IN_W), lambda i: (0, 0)),
        ],
        out_specs=[pl.BlockSpec((tm // d, d * 3 * GROUP_W), lambda i: (i, 0)) for d in dils] + [
            pl.BlockSpec((tm, CONV_CH), lambda i: (i, 0)),
            pl.BlockSpec((tm, 2 * ATTN_W), lambda i: (jnp.maximum(i - (nt - nk), 0), 0)),
        ],
        out_shape=[jax.ShapeDtypeStruct((n // d, d * 3 * GROUP_W), qkv_dtype) for d in dils] + [
            jax.ShapeDtypeStruct((n, CONV_CH), F32),
            jax.ShapeDtypeStruct((keep, 2 * ATTN_W), F32),
        ],
        scratch_shapes=[pltpu.VMEM((GROUP_W // LANES, tm, LANES), F32)],
        compiler_params=_params("arbitrary"),
        name="proj",
    )(x, w_in, b_in)


def _head_select(parts, rows):
    col = lax.broadcasted_iota(I32, (rows, GROUP_W), 1) // HEAD_DIM
    out = jnp.broadcast_to(parts[-1], (rows, GROUP_W))
    for h in range(HEADS_PER_GROUP - 2, -1, -1):
        out = jnp.where(col == h, parts[h], out)
    return out


def _head_rows(q, rows):
    col = lax.broadcasted_iota(I32, (rows, GROUP_W), 1) // HEAD_DIM
    return jnp.concatenate([jnp.where(col == h, q, jnp.zeros_like(q)) for h in range(HEADS_PER_GROUP)], axis=0)


def _softmax_pv(s, v, rows):
    m = jnp.max(s, axis=-1, keepdims=True)
    e = jnp.exp(s - m)
    l = jnp.sum(e, axis=-1, keepdims=True)
    pv = _dot(e.astype(BF16), v) / l
    lse = m + jnp.log(l)
    o = _head_select([pv[h * rows:(h + 1) * rows] for h in range(HEADS_PER_GROUP)], rows)
    lse_x = _head_select([lse[h * rows:(h + 1) * rows] for h in range(HEADS_PER_GROUP)], rows)
    return o, lse_x


ATTN_QB = 16


def _attn_kernel(q_ref, kp_ref, kc_ref, vp_ref, vc_ref, bias_ref, o_ref, lse_ref, *, qb):
    b = pl.program_id(1)
    for j in range(qb):
        rows = slice(j * BAND, (j + 1) * BAND)
        qm = _head_rows(q_ref[rows, :] * HEAD_DIM ** -0.5, BAND)
        if j == 0:
            k = jnp.concatenate([kp_ref[...], kc_ref[rows, :]], axis=0)
            v = jnp.concatenate([vp_ref[...], vc_ref[rows, :]], axis=0)
            bias = bias_ref[jnp.minimum(b, 1)]
        else:
            k = kc_ref[(j - 1) * BAND:(j + 1) * BAND, :]
            v = vc_ref[(j - 1) * BAND:(j + 1) * BAND, :]
            bias = bias_ref[1]
        s = _dot_nt(qm, k) + bias
        o, lse_x = _softmax_pv(s, v, BAND)
        o_ref[rows, :] = o.astype(o_ref.dtype)
        lse_ref[rows, :] = lse_x


def _prompt_bias(g, dil):
    slopes = _alibi_slopes()[g]
    qi = jnp.arange(BAND)[:, None]
    kj = jnp.arange(2 * BAND)[None, :]
    steps = qi + BAND - kj
    nk = DIL_GROUPS[g][0] // dil
    valid = (steps >= 0) & (steps <= nk)
    bias = -slopes[:, None, None] * (steps * dil).astype(F32)
    inner = jnp.where(valid[None], bias, MASKED)
    first = jnp.where((valid & (kj >= BAND))[None], bias, MASKED)
    return jnp.stack([first, inner]).reshape(2, HEADS_PER_GROUP * BAND, 2 * BAND)


def _attention_prompt(view, g):
    win, dil = DIL_GROUPS[g]
    n_cls = view.shape[0]
    qb = ATTN_QB
    while n_cls % (qb * BAND):
        qb //= 2
    rows = qb * BAND
    assert win // dil <= BAND and qb >= 1 and view.shape[1] == dil * 3 * GROUP_W
    cur = lambda part: pl.BlockSpec((rows, GROUP_W), lambda r, b: (b, r * 3 + part))
    prev = lambda part: pl.BlockSpec((BAND, GROUP_W), lambda r, b: (jnp.maximum(b * qb - 1, 0), r * 3 + part))
    bias = _prompt_bias(g, dil)
    return pl.pallas_call(
        functools.partial(_attn_kernel, qb=qb),
        grid=(dil, n_cls // rows),
        in_specs=[cur(0), prev(1), cur(1), prev(2), cur(2),
                  pl.BlockSpec((2, HEADS_PER_GROUP * BAND, 2 * BAND), lambda r, b: (0, 0, 0))],
        out_specs=[pl.BlockSpec((rows, GROUP_W), lambda r, b: (b, r)),
                   pl.BlockSpec((rows, GROUP_W), lambda r, b: (b, r))],
        out_shape=[jax.ShapeDtypeStruct((n_cls, dil * GROUP_W), BF16),
                   jax.ShapeDtypeStruct((n_cls, dil * GROUP_W), F32)],
        compiler_params=_params("arbitrary", "arbitrary"),
        name=f"attn_g{g}",
    )(view, view, view, view, view, bias)


def _attn_sample_kernel(q_ref, k_ref, v_ref, cache_ref, bias_ref, o_ref, lse_ref, *, n_buf, t, seqs):
    pad = jnp.zeros((BAND - t, GROUP_W), F32)
    scale = HEAD_DIM ** -0.5
    for b in range(seqs):
        rows = slice(b * t, (b + 1) * t)
        k_new = jnp.concatenate([k_ref[rows, :], pad], axis=0).astype(BF16)
        v_new = jnp.concatenate([v_ref[rows, :], pad], axis=0).astype(BF16)
        qm = _head_rows(q_ref[rows, :], t).astype(BF16)
        s_old = _dot(qm, cache_ref[b, 0:GROUP_W, :].astype(BF16)) * scale + bias_ref[:, 0:n_buf]
        s_new = _dot_nt(qm, k_new) * scale + bias_ref[:, n_buf:]
        m = jnp.maximum(jnp.max(s_old, axis=-1, keepdims=True), jnp.max(s_new, axis=-1, keepdims=True))
        e_old = jnp.exp(s_old - m)
        e_new = jnp.exp(s_new - m)
        l = jnp.sum(e_old, axis=-1, keepdims=True) + jnp.sum(e_new, axis=-1, keepdims=True)
        pv = _dot_nt(e_old.astype(BF16), cache_ref[b, GROUP_W:, :].astype(BF16)) + _dot(e_new.astype(BF16), v_new)
        pv = pv / l
        lse = m + jnp.log(l)
        o_ref[rows, :] = _head_select([pv[h * t:(h + 1) * t] for h in range(HEADS_PER_GROUP)], t)
        lse_ref[rows, :] = _head_select([lse[h * t:(h + 1) * t] for h in range(HEADS_PER_GROUP)], t)


def _sample_bias(g, n_buf, t):
    win, dil = DIL_GROUPS[g]
    slopes = _alibi_slopes()[g]
    tq = jnp.arange(t)[:, None]
    j = jnp.arange(n_buf + BAND)[None, :]
    dist = n_buf + tq - j
    valid = (dist >= 0) & (dist % dil == 0) & (dist <= win) & (j < n_buf + t)
    bias = -slopes[:, None, None] * dist.astype(F32)
    return jnp.where(valid[None], bias, MASKED).reshape(HEADS_PER_GROUP * t, n_buf + BAND)


def _attention_sample(qkv, cache_t, g, t):
    nb, n_buf = cache_t.shape[0], cache_t.shape[2]
    seqs = max(1, min(4, 1024 // n_buf))
    assert nb % seqs == 0
    col = lambda part: pl.BlockSpec((seqs * t, GROUP_W), lambda i: (i, part))
    kern = functools.partial(_attn_sample_kernel, n_buf=n_buf, t=t, seqs=seqs)
    bias = _sample_bias(g, n_buf, t)
    return pl.pallas_call(
        kern,
        grid=(nb // seqs,),
        in_specs=[col(0), col(1), col(2),
                  pl.BlockSpec((seqs, 2 * GROUP_W, n_buf), lambda i: (i, 0, 0)),
                  pl.BlockSpec((HEADS_PER_GROUP * t, n_buf + BAND), lambda i: (0, 0))],
        out_specs=[pl.BlockSpec((seqs * t, GROUP_W), lambda i: (i, 0)),
                   pl.BlockSpec((seqs * t, GROUP_W), lambda i: (i, 0))],
        out_shape=[jax.ShapeDtypeStruct((nb * t, GROUP_W), F32),
                   jax.ShapeDtypeStruct((nb * t, GROUP_W), F32)],
        compiler_params=_params("arbitrary"),
        name=f"attn_sample_g{g}",
    )(qkv, qkv, qkv, cache_t, bias)


CONV_HALO = 32


def _conv_tail(acc, b_ref, g_ref, beta_ref):
    return _silu(_layer_norm(acc + b_ref[...], g_ref[...], beta_ref[...]))


SUBLANES = 8


def _conv_prompt_kernel(halo_ref, u_ref, w_ref, b_ref, g_ref, beta_ref, o_ref, hist, part, *, tm):
    i = pl.program_id(0)
    hist[0:CONV_HALO, :] = jnp.where(i == 0, 0.0, halo_ref[...])
    hist[CONV_HALO:CONV_HALO + tm, :] = u_ref[...]
    hist[CONV_HALO + tm:, :] = jnp.zeros((SUBLANES, CONV_CH), F32)
    off = CONV_HALO - (CONV_WIDTH - 1)
    acc = None
    for s in range(SUBLANES):
        group = None
        for m in range(s, off + CONV_WIDTH, SUBLANES):
            j = m - off
            if j < 0:
                continue
            term = w_ref[j:j + 1, :] * hist[m - s:m - s + tm + SUBLANES, :]
            group = term if group is None else group + term
        part[...] = group
        shifted = part[s:s + tm, :]
        acc = shifted if acc is None else acc + shifted
    o_ref[...] = _conv_tail(acc, b_ref, g_ref, beta_ref).astype(o_ref.dtype)


def _conv_prompt(u, w_dw, b_dw, ln_g, ln_b, tm):
    n = u.shape[0]
    vec = pl.BlockSpec((1, CONV_CH), lambda i: (0, 0))
    return pl.pallas_call(
        functools.partial(_conv_prompt_kernel, tm=tm),
        grid=(n // tm,),
        in_specs=[pl.BlockSpec((CONV_HALO, CONV_CH), lambda i: (jnp.maximum(i * (tm // CONV_HALO) - 1, 0), 0)),
                  pl.BlockSpec((tm, CONV_CH), lambda i: (i, 0)),
                  pl.BlockSpec((CONV_WIDTH, CONV_CH), lambda i: (0, 0)), vec, vec, vec],
        out_specs=pl.BlockSpec((tm, CONV_CH), lambda i: (i, 0)),
        out_shape=jax.ShapeDtypeStruct((n, CONV_CH), BF16),
        scratch_shapes=[pltpu.VMEM((CONV_HALO + tm + SUBLANES, CONV_CH), F32),
                        pltpu.VMEM((tm + SUBLANES, CONV_CH), F32)],
        compiler_params=_params("arbitrary"),
        name="conv_prompt",
    )(u, u, w_dw, b_dw, ln_g, ln_b)


def _conv_sample_kernel(hist_ref, w_ref, b_ref, g_ref, beta_ref, o_ref, *, t):
    acc = jnp.zeros((t, CONV_CH), F32)
    for j in range(CONV_WIDTH):
        acc = acc + w_ref[j:j + 1, :] * hist_ref[0, j:j + t, :]
    o_ref[...] = _conv_tail(acc, b_ref, g_ref, beta_ref)


def _conv_sample(u_hist, w_dw, b_dw, ln_g, ln_b, t):
    nb, rows = u_hist.shape[0], u_hist.shape[1]
    vec = pl.BlockSpec((1, CONV_CH), lambda i: (0, 0))
    return pl.pallas_call(
        functools.partial(_conv_sample_kernel, t=t),
        grid=(nb,),
        in_specs=[pl.BlockSpec((1, rows, CONV_CH), lambda i: (i, 0, 0)),
                  pl.BlockSpec((CONV_WIDTH, CONV_CH), lambda i: (0, 0)), vec, vec, vec],
        out_specs=pl.BlockSpec((t, CONV_CH), lambda i: (i, 0)),
        out_shape=jax.ShapeDtypeStruct((nb * t, CONV_CH), F32),
        compiler_params=_params("arbitrary"),
        name="conv_sample",
    )(u_hist, w_dw, b_dw, ln_g, ln_b)


U32 = jnp.uint32
HALF_D = D_MODEL // 2
PACK_SLABS = HALF_D // LANES


def _token_order(ref, dil, scr, tm):
    if dil == 1:
        return ref[...].astype(F32)
    for r in range(dil):
        for half in range(GROUP_W // LANES):
            c = r * GROUP_W + half * LANES
            scr[half, pl.ds(r, tm // dil, stride=dil), :] = ref[:, c:c + LANES].astype(F32)
    return jnp.concatenate([scr[half] for half in range(GROUP_W // LANES)], axis=1)


def _pack_rows(ref, val):
    rows = val.shape[0]
    bits = lax.bitcast_convert_type(val.astype(BF16).astype(F32), U32)
    packed = lax.shift_right_logical(bits[:, :HALF_D], jnp.uint32(16)) | bits[:, HALF_D:]
    for c in range(PACK_SLABS):
        ref[pl.ds(c, rows, stride=PACK_SLABS), :] = packed[:, c * LANES:(c + 1) * LANES]


def _unpack_rows(ref):
    rows = ref.shape[0] // PACK_SLABS
    u = jnp.concatenate([ref[pl.ds(c, rows, stride=PACK_SLABS), :] for c in range(PACK_SLABS)], axis=1)
    lo = lax.bitcast_convert_type(lax.shift_left(u, jnp.uint32(16)), F32)
    hi = lax.bitcast_convert_type(u & jnp.uint32(0xFFFF0000), F32)
    return lo, hi


def _outproj_kernel(o0, o1, o2, l0, l1, l2, c_ref, x_ref, w_ref, b_ref, g_ref, beta_ref, h_ref, hp_ref, *scr,
                    dils, tm):
    os_ = [_token_order(r, d, scr[2 * g], tm) for g, (r, d) in enumerate(zip((o0, o1, o2), dils))]
    ls = [_token_order(r, d, scr[2 * g + 1], tm) for g, (r, d) in enumerate(zip((l0, l1, l2), dils))]
    m = jnp.maximum(jnp.maximum(ls[0], ls[1]), ls[2])
    es = [jnp.exp(l - m) for l in ls]
    inv = 1.0 / (es[0] + es[1] + es[2])
    mixed = b_ref[...]
    for g in range(N_GROUPS):
        a = (os_[g] * (es[g] * inv)).astype(BF16)
        mixed = mixed + _dot(a, w_ref[g * GROUP_W:(g + 1) * GROUP_W, :])
    mixed = mixed + _dot(c_ref[...].astype(BF16), w_ref[ATTN_W:, :])
    h = _layer_norm(DN_ALPHA * x_ref[...] + mixed, g_ref[...], beta_ref[...])
    h_ref[...] = h
    _pack_rows(hp_ref, h)


def _outproj(os_, lses, conv, x, w_o, b_o, ln_g, ln_b, tm, dils):
    n = x.shape[0]
    grp = [pl.BlockSpec((tm // d, d * GROUP_W), lambda i: (i, 0)) for d in dils]
    vec = pl.BlockSpec((1, D_MODEL), lambda i: (0, 0))
    return pl.pallas_call(
        functools.partial(_outproj_kernel, dils=dils, tm=tm),
        grid=(n // tm,),
        in_specs=grp + grp + [pl.BlockSpec((tm, CONV_CH), lambda i: (i, 0)),
                              pl.BlockSpec((tm, D_MODEL), lambda i: (i, 0)),
                              pl.BlockSpec((D_MODEL, D_MODEL), lambda i: (0, 0)), vec, vec, vec],
        out_specs=[pl.BlockSpec((tm, D_MODEL), lambda i: (i, 0)),
                   pl.BlockSpec((tm * PACK_SLABS, LANES), lambda i: (i, 0))],
        out_shape=[jax.ShapeDtypeStruct((n, D_MODEL), F32),
                   jax.ShapeDtypeStruct((n * PACK_SLABS, LANES), U32)],
        scratch_shapes=[pltpu.VMEM((GROUP_W // LANES, tm, LANES), F32) for _ in range(2 * N_GROUPS)],
        compiler_params=_params("arbitrary"),
        name="outproj",
    )(*os_, *lses, conv, x, w_o, b_o, ln_g, ln_b)


ROUTER_TM = 256


def _first_index(hit, idx, limit, axis):
    return jnp.min(jnp.where(hit, idx, limit), axis=axis, keepdims=True)


def _router_kernel(ha_ref, hb_ref, w_ref, rb_ref, tri_ref, idx_ref, wt_ref, pos_ref, cnt_ref, run, *, tiles_a):
    i = pl.program_id(0)
    tm = ROUTER_TM

    @pl.when(i == 0)
    def _():
        run[...] = jnp.zeros_like(run)

    h = jnp.where(i < tiles_a, ha_ref[...], hb_ref[...])
    logits = _dot_nt(w_ref[...], h.astype(BF16))
    scores = jax.nn.sigmoid(logits)
    biased = scores + rb_ref[...]
    groups = [biased[g * EXPERTS_PER_GROUP:(g + 1) * EXPERTS_PER_GROUP] for g in range(N_EXPERT_GROUPS)]
    ei = lax.broadcasted_iota(I32, (EXPERTS_PER_GROUP, tm), 0).astype(F32)
    gs = []
    for bg in groups:
        m1 = jnp.max(bg, axis=0, keepdims=True)
        f1 = _first_index(bg == m1, ei, float(EXPERTS_PER_GROUP), 0)
        m2 = jnp.max(jnp.where(ei == f1, -jnp.inf, bg), axis=0, keepdims=True)
        gs.append(m1 + m2)
    gs = jnp.concatenate(gs, axis=0)
    gi = lax.broadcasted_iota(I32, gs.shape, 0).astype(F32)
    keep = jnp.zeros(gs.shape, F32)
    cur = gs
    for _ in range(TOPK_GROUPS):
        m = jnp.max(cur, axis=0, keepdims=True)
        f = _first_index(cur == m, gi, float(N_EXPERT_GROUPS), 0)
        hit = gi == f
        keep = jnp.where(hit, 1.0, keep)
        cur = jnp.where(hit, -jnp.inf, cur)
    masked = jnp.concatenate([jnp.where(keep[g:g + 1] > 0.0, bg, -jnp.inf) for g, bg in enumerate(groups)], axis=0)
    xi = lax.broadcasted_iota(I32, (N_EXPERTS, tm), 0).astype(F32)
    cur = masked
    picks = []
    for _ in range(TOP_K):
        m = jnp.max(cur, axis=0, keepdims=True)
        f = _first_index(cur == m, xi, float(N_EXPERTS), 0)
        hit = xi == f
        picks.append((f, hit))
        cur = jnp.where(hit, -jnp.inf, cur)
    sel = jnp.where(cur != masked, 1.0, 0.0)
    before = _dot(sel.astype(BF16), tri_ref[...]) + run[...]
    run[...] = run[...] + jnp.sum(sel, axis=1, keepdims=True)
    ws = [jnp.sum(jnp.where(hit, scores, 0.0), axis=0, keepdims=True) for _, hit in picks]
    wsum = ws[0]
    for w in ws[1:]:
        wsum = wsum + w
    for k, (f, hit) in enumerate(picks):
        idx_ref[k:k + 1, :] = f.astype(I32)
        wt_ref[k:k + 1, :] = ws[k] / wsum * ROUTED_SCALE
        pos_ref[k:k + 1, :] = jnp.sum(jnp.where(hit, before, 0.0), axis=0, keepdims=True).astype(I32)
    cnt_ref[...] = jnp.broadcast_to(run[...], cnt_ref.shape).astype(I32)


def _route(ha, hb, w_router_t, router_bias):
    tm = ROUTER_TM
    tiles_a, tiles_b = ha.shape[0] // tm, hb.shape[0] // tm
    spec_a = pl.BlockSpec((tm, D_MODEL), lambda i: (jnp.minimum(i, tiles_a - 1), 0))
    spec_b = pl.BlockSpec((tm, D_MODEL), lambda i: (jnp.maximum(i - tiles_a, 0), 0))
    t = (tiles_a + tiles_b) * tm
    tri = (jnp.arange(tm)[:, None] < jnp.arange(tm)[None, :]).astype(BF16)
    slot = pl.BlockSpec((TOP_K, tm), lambda i: (0, i))
    idx, wts, pos, cnt = pl.pallas_call(
        functools.partial(_router_kernel, tiles_a=tiles_a),
        grid=(t // tm,),
        in_specs=[spec_a, spec_b,
                  pl.BlockSpec((N_EXPERTS, D_MODEL), lambda i: (0, 0)),
                  pl.BlockSpec((N_EXPERTS, 1), lambda i: (0, 0)),
                  pl.BlockSpec((tm, tm), lambda i: (0, 0))],
        out_specs=[slot, slot, slot, pl.BlockSpec((N_EXPERTS, 128), lambda i: (0, 0))],
        out_shape=[jax.ShapeDtypeStruct((TOP_K, t), I32), jax.ShapeDtypeStruct((TOP_K, t), F32),
                   jax.ShapeDtypeStruct((TOP_K, t), I32), jax.ShapeDtypeStruct((N_EXPERTS, 128), I32)],
        scratch_shapes=[pltpu.VMEM((N_EXPERTS, 1), F32)],
        compiler_params=_params("arbitrary"),
        name="router",
    )(ha, hb, w_router_t, router_bias.reshape(N_EXPERTS, 1), tri)
    return idx, wts, pos, cnt[:, 0]


DEST_TM = 1280


def _dest_kernel(idx_ref, pos_ref, starts_ref, dest_ref):
    tm = idx_ref.shape[1]
    ei = lax.broadcasted_iota(I32, (N_EXPERTS, tm), 0)
    starts = starts_ref[...]
    for k in range(TOP_K):
        first = jnp.sum(jnp.where(ei == idx_ref[k:k + 1, :], starts, 0.0), axis=0, keepdims=True)
        dest_ref[k:k + 1, :] = first.astype(I32) + pos_ref[k:k + 1, :]


def _dest_rows(idx, pos, starts):
    t = idx.shape[1]
    tm = DEST_TM if t % DEST_TM == 0 else ROUTER_TM
    assert t % tm == 0
    slot = pl.BlockSpec((TOP_K, tm), lambda i: (0, i))
    return pl.pallas_call(
        _dest_kernel,
        grid=(t // tm,),
        in_specs=[slot, slot, pl.BlockSpec((N_EXPERTS, 1), lambda i: (0, 0))],
        out_specs=slot,
        out_shape=jax.ShapeDtypeStruct((TOP_K, t), I32),
        compiler_params=_params("arbitrary"),
        name="dest_rows",
    )(idx, pos, starts.astype(F32).reshape(N_EXPERTS, 1))


SC_CORES = 2
SC_SUBCORES = 16
SC_WORKERS = SC_CORES * SC_SUBCORES
SC_WINDOW = 64
SC_WINDOW_SMALL = 32


def _sc_worker():
    return lax.axis_index("s") * SC_CORES + lax.axis_index("c")


def _sc_mesh():
    return plsc.VectorSubcoreMesh(core_axis_name="c", subcore_axis_name="s")


def _sc_scatter_rows(ha, hb, dest_w, zeros, n_out):
    w = dest_w.shape[2]
    na, nb = ha.shape[0], hb.shape[0]
    spare = zeros.shape[0]
    wa, wb = na // w, nb // w
    assert na % (SC_WORKERS * 2 * w) == 0 and nb % w == 0 and wb <= SC_WORKERS
    per_w = wa // SC_WORKERS
    rows_t = pltpu.VMEM((w,) + ha.shape[1:], ha.dtype)
    idx_t = pltpu.VMEM((TOP_K, w), I32)

    @functools.partial(
        pl.kernel, mesh=_sc_mesh(),
        out_type=jax.ShapeDtypeStruct((n_out + spare,) + ha.shape[1:], ha.dtype),
        scratch_types=[rows_t, rows_t, idx_t, idx_t, pltpu.SemaphoreType.DMA, pltpu.SemaphoreType.DMA],
    )
    def scatter(ha_hbm, hb_hbm, dest_hbm, zeros_hbm, out_hbm, rows0, rows1, idx0, idx1, sem0, sem1):
        wid = _sc_worker()
        bufs = ((rows0, idx0, sem0), (rows1, idx1, sem1))

        @pl.when(wid == SC_WORKERS - 1)
        def _():
            pltpu.sync_copy(zeros_hbm, out_hbm.at[pl.ds(n_out, spare)])

        def load(src_hbm, row0, win, b):
            pltpu.sync_copy(src_hbm.at[pl.ds(pl.multiple_of(row0, 8), w)], bufs[b][0])
            pltpu.sync_copy(dest_hbm.at[win], bufs[b][1])

        def copies(b):
            return [pltpu.make_async_copy(bufs[b][0], out_hbm.at[bufs[b][1].at[k]], bufs[b][2])
                    for k in range(TOP_K)]

        win0 = wid * per_w
        load(ha_hbm, win0 * w, win0, 0)

        @pl.loop(0, per_w, step=2)
        def _(i0):
            for b in range(2):
                i = i0 + b
                for c in copies(b):
                    c.start()

                @pl.when(i + 1 < per_w)
                def _():
                    load(ha_hbm, (win0 + i + 1) * w, win0 + i + 1, 1 - b)

                for c in copies(b):
                    c.wait()

        @pl.when(wid < wb)
        def _():
            load(hb_hbm, wid * w, wa + wid, 0)
            for c in copies(0):
                c.start()
            for c in copies(0):
                c.wait()

    return scatter(ha, hb, dest_w, zeros)


def _sc_gather_rows(table, idx):
    n = idx.shape[0]
    w = SC_WINDOW if n % (SC_WORKERS * 2 * SC_WINDOW) == 0 else SC_WINDOW_SMALL
    assert n % (SC_WORKERS * 2 * w) == 0
    per_w = n // SC_WORKERS
    nwin = per_w // w
    rows_t = pltpu.VMEM((w,) + table.shape[1:], table.dtype)

    @functools.partial(
        pl.kernel, mesh=_sc_mesh(),
        out_type=jax.ShapeDtypeStruct((n,) + table.shape[1:], table.dtype),
        scratch_types=[pltpu.VMEM((per_w,), I32), rows_t, rows_t, pltpu.SemaphoreType.DMA, pltpu.SemaphoreType.DMA],
    )
    def gather(table_hbm, idx_hbm, out_hbm, idx_v, rows0, rows1, sem0, sem1):
        base = pl.multiple_of(_sc_worker() * per_w, 8)
        pltpu.sync_copy(idx_hbm.at[pl.ds(base, per_w)], idx_v)
        bufs = ((rows0, sem0), (rows1, sem1))
        fetch = lambda i, b: pltpu.make_async_copy(
            table_hbm.at[idx_v.at[pl.ds(pl.multiple_of(i * w, 8), w)]], bufs[b][0], bufs[b][1])
        fetch(0, 0).start()

        @pl.loop(0, nwin, step=2)
        def _(i0):
            for b in range(2):
                i = i0 + b
                fetch(i, b).wait()

                @pl.when(i + 1 < nwin)
                def _():
                    fetch(i + 1, 1 - b).start()

                pltpu.sync_copy(bufs[b][0], out_hbm.at[pl.ds(pl.multiple_of(base + i * w, 8), w)])

    return gather(table, idx)


EXPERT_CH = 256
X_SLOTS = 6
W_SLOTS = 3


def _expert_kernel(row0_ref, exp_ref, live_ref, fresh_ref, next1_ref, next2_ref, wslot_ref, ordered_ref, xs_ref,
                   wg_ref, wu_ref, wd_ref, ys_ref, xbuf, ybuf, sg, su, sd, wgb, wub, wdb, xsem, ysem, wsem):
    v = pl.program_id(0)
    nv = pl.num_programs(0)
    slot = lax.rem(v, X_SLOTS)
    yslot = lax.rem(v, 3)
    rows = lambda u: pl.ds(pl.multiple_of(row0_ref[u] * PACK_SLABS, PACK_SLABS), EXPERT_CH * PACK_SLABS)
    x_copy = lambda u: pltpu.make_async_copy(xs_ref.at[rows(u)], xbuf.at[lax.rem(u, X_SLOTS)],
                                             xsem.at[lax.rem(u, X_SLOTS)])
    y_copy = lambda u: pltpu.make_async_copy(ybuf.at[lax.rem(u, 3)], ys_ref.at[rows(u)], ysem.at[lax.rem(u, 3)])
    prev1, prev2 = jnp.maximum(v - 1, 0), jnp.maximum(v - 2, 0)
    w_copies = lambda e, s: [pltpu.make_async_copy(w.at[e], stage.at[s], wsem.at[s])
                             for w, stage in ((wg_ref, sg), (wu_ref, su), (wd_ref, sd))]
    spare = pl.ds(ys_ref.shape[0] - EXPERT_CH * PACK_SLABS, EXPERT_CH * PACK_SLABS)

    @pl.when(v == 0)
    def _():
        for c in w_copies(exp_ref[0], 0):
            c.start(priority=1)

        @pl.when(next1_ref[0] >= 0)
        def _():
            for c in w_copies(next1_ref[0], 1):
                c.start(priority=1)

        x_copy(0).start()
        for u in range(1, X_SLOTS - 1):
            @pl.when((u < nv) & (live_ref[jnp.minimum(u, nv - 1)] == 1))
            def _():
                x_copy(u).start()
        ybuf[2] = jnp.zeros(ybuf.shape[1:], U32)
        zero = pltpu.make_async_copy(ybuf.at[2], ys_ref.at[spare], ysem.at[2])
        zero.start()
        zero.wait()

    ahead = jnp.minimum(v + X_SLOTS - 1, nv - 1)

    @pl.when((v + X_SLOTS - 1 < nv) & (live_ref[ahead] == 1))
    def _():
        x_copy(ahead).start()

    @pl.when(fresh_ref[v] == 1)
    def _():
        p = wslot_ref[v]
        for c in w_copies(exp_ref[v], p):
            c.wait()

        @pl.when(next2_ref[v] >= 0)
        def _():
            for c in w_copies(next2_ref[v], lax.rem(p + 2, W_SLOTS)):
                c.start(priority=1)

        wgb[...] = sg[p].astype(BF16)
        wub[...] = su[p].astype(BF16)
        wdb[...] = sd[p].astype(BF16)

    @pl.when(live_ref[v] == 1)
    def _():
        x_copy(v).wait()
        x = jnp.concatenate(_unpack_rows(xbuf.at[slot]), axis=1).astype(BF16)
        hid = (_silu(_dot(x, wgb[...])) * _dot(x, wub[...])).astype(BF16)
        _pack_rows(ybuf.at[yslot], _dot(hid, wdb[...]))

    @pl.when((v >= 2) & (live_ref[prev2] == 1) & (ordered_ref[prev1] == 0))
    def _():
        y_copy(prev2).wait()

    @pl.when((v >= 1) & (live_ref[prev1] == 1) & (ordered_ref[v] == 1))
    def _():
        y_copy(prev1).wait()

    @pl.when(live_ref[v] == 1)
    def _():
        y_copy(v).start()

    @pl.when(v == nv - 1)
    def _():
        @pl.when((v >= 1) & (live_ref[prev1] == 1) & (ordered_ref[v] == 0))
        def _():
            y_copy(prev1).wait()

        @pl.when(live_ref[v] == 1)
        def _():
            y_copy(v).wait()


def _chunks(counts, n_rows):
    ch = EXPERT_CH
    nv = n_rows // ch + N_EXPERTS
    ends = jnp.cumsum(counts)
    starts = ends - counts
    nch = (counts + ch - 1) // ch
    cend = jnp.cumsum(nch)
    cstart = cend - nch
    v = jnp.arange(nv, dtype=I32)
    live = v < cend[-1]
    vc = jnp.minimum(v, cend[-1] - 1)
    ids = jnp.arange(N_EXPERTS, dtype=I32)
    e = jnp.minimum(jnp.sum(cend[None, :] <= vc[:, None], axis=1), N_EXPERTS - 1).astype(I32)
    of_e = lambda a: jnp.sum(jnp.where(e[:, None] == ids[None, :], a[None, :], 0), axis=1)
    starts_e, ends_e, count_e = of_e(starts), of_e(ends), of_e(counts)
    c = vc - of_e(cstart)
    is_last = c == of_e(nch) - 1
    row0 = jnp.where(is_last & (count_e >= ch), ends_e - ch, starts_e + c * ch).astype(I32)
    overrun = live & is_last & (count_e < ch)
    ordered = jnp.concatenate([jnp.zeros((1,), bool), overrun[:-1]]) & live
    fresh = jnp.concatenate([jnp.ones((1,), I32), (e[1:] != e[:-1]).astype(I32)])
    order = jnp.cumsum(counts > 0) - 1
    order_e = of_e(order)

    def after(k):
        hit = (counts > 0)[None, :] & (order[None, :] == order_e[:, None] + k)
        return (jnp.sum(jnp.where(hit, ids[None, :] + 1, 0), axis=1) - 1).astype(I32)

    return (starts.astype(I32), row0, e, live.astype(I32), jnp.where(live, fresh, 0).astype(I32),
            after(1), after(2), (order_e % W_SLOTS).astype(I32), ordered.astype(I32))


def _experts(chunks, xs, w_gate, w_up, w_down):
    ch = EXPERT_CH
    buf = lambda n: pltpu.VMEM((n, ch * PACK_SLABS, LANES), U32)
    anywhere = pl.BlockSpec(memory_space=pl.ANY)
    return pl.pallas_call(
        _expert_kernel,
        grid_spec=pltpu.PrefetchScalarGridSpec(
            num_scalar_prefetch=len(chunks),
            grid=(chunks[0].shape[0],),
            in_specs=[anywhere] * 4,
            out_specs=anywhere,
            scratch_shapes=[buf(X_SLOTS), buf(3),
                            pltpu.VMEM((W_SLOTS, D_MODEL, EXPERT_FF), F32),
                            pltpu.VMEM((W_SLOTS, D_MODEL, EXPERT_FF), F32),
                            pltpu.VMEM((W_SLOTS, EXPERT_FF, D_MODEL), F32),
                            pltpu.VMEM((D_MODEL, EXPERT_FF), BF16), pltpu.VMEM((D_MODEL, EXPERT_FF), BF16),
                            pltpu.VMEM((EXPERT_FF, D_MODEL), BF16),
                            pltpu.SemaphoreType.DMA((X_SLOTS,)), pltpu.SemaphoreType.DMA((3,)),
                            pltpu.SemaphoreType.DMA((W_SLOTS,))],
        ),
        out_shape=jax.ShapeDtypeStruct(xs.shape, U32),
        compiler_params=_params("arbitrary"),
        name="experts",
    )(*chunks, xs, w_gate, w_up, w_down)


COMBINE_TM = 256


def _combine_kernel(wt_ref, h_ref, sg_ref, su_ref, sd_ref, g_ref, beta_ref, *refs):
    y_refs, o_ref = refs[:TOP_K], refs[TOP_K]
    h = h_ref[...]
    hb = h.astype(BF16)
    hid = (_silu(_dot(hb, sg_ref[...])) * _dot(hb, su_ref[...])).astype(BF16)
    acc = DN_ALPHA * h + _dot(hid, sd_ref[...])
    lo = acc[:, :HALF_D]
    hi = acc[:, HALF_D:]
    for k in range(TOP_K):
        y_lo, y_hi = _unpack_rows(y_refs[k])
        w = wt_ref[:, k:k + 1]
        lo = lo + w * y_lo
        hi = hi + w * y_hi
    o_ref[...] = _layer_norm(jnp.concatenate([lo, hi], axis=1), g_ref[...], beta_ref[...])


def _combine(wts_tk, ws_gate, ws_up, ws_down, ln_g, ln_b, h, h_row0, t0, yg):
    tm = COMBINE_TM
    n = yg.shape[0] // (TOP_K * PACK_SLABS)
    assert n % tm == 0 and h_row0 % tm == 0 and t0 % tm == 0
    b0 = t0 // tm
    per_slot = n // tm
    rows = pl.BlockSpec((tm, D_MODEL), lambda i: (h_row0 // tm + i, 0))
    slot_rows = [pl.BlockSpec((tm * PACK_SLABS, LANES), lambda i, k=k: (k * per_slot + i, 0))
                 for k in range(TOP_K)]
    vec = pl.BlockSpec((1, D_MODEL), lambda i: (0, 0))
    return pl.pallas_call(
        _combine_kernel,
        grid=(n // tm,),
        in_specs=[pl.BlockSpec((tm, TOP_K), lambda i: (b0 + i, 0)), rows,
                  pl.BlockSpec((D_MODEL, EXPERT_FF), lambda i: (0, 0)),
                  pl.BlockSpec((D_MODEL, EXPERT_FF), lambda i: (0, 0)),
                  pl.BlockSpec((EXPERT_FF, D_MODEL), lambda i: (0, 0)),
                  vec, vec] + slot_rows,
        out_specs=pl.BlockSpec((tm, D_MODEL), lambda i: (i, 0)),
        out_shape=jax.ShapeDtypeStruct((n, D_MODEL), F32),
        compiler_params=_params("arbitrary"),
        name="combine",
    )(wts_tk, h, ws_gate, ws_up, ws_down, ln_g, ln_b, *([yg] * TOP_K))


def _kv_cache(kv_tail, g, keep):
    k = kv_tail[-keep:, g * GROUP_W:(g + 1) * GROUP_W]
    v = kv_tail[-keep:, ATTN_W + g * GROUP_W:ATTN_W + (g + 1) * GROUP_W]
    return jnp.stack([k, v], axis=1).reshape(keep, 2, HEADS_PER_GROUP, HEAD_DIM)


def kernel(x_prompt, x_sample, cache_kv_w128, cache_kv_w512, cache_kv_w2048, state_conv, w_in, b_in, w_dw, b_dw,
           conv_ln_g, conv_ln_b, w_o, b_o, ln1_g, ln1_b, w_router, router_bias, w_gate, w_up, w_down, ws_gate,
           ws_up, ws_down, ln2_g, ln2_b):
    assert w_in.shape[0] == DEPTH == 1
    batch, seq, _ = x_prompt.shape
    dec_batch, dec_seq, _ = x_sample.shape
    assert batch == 1
    n_s = dec_batch * dec_seq
    caches = (cache_kv_w128, cache_kv_w512, cache_kv_w2048)
    row = lambda a: a[0].reshape(1, -1)

    w_in_b = w_in[0].astype(BF16)
    b_in_r = row(b_in)
    w_o_b = w_o[0].astype(BF16)
    conv_w = (w_dw[0], row(b_dw), row(conv_ln_g), row(conv_ln_b))
    ln1 = (row(ln1_g), row(ln1_b))

    xp = x_prompt[0]
    keep_p = min(max(w for w, _ in DIL_GROUPS), seq)
    dils = tuple(d for _, d in DIL_GROUPS)
    *qkv_p, u_p, kv_p = _project(xp, w_in_b, b_in_r, 512, keep_p, BF16, dils)
    attn_p = [_attention_prompt(qkv_p[g], g) for g in range(N_GROUPS)]
    conv_p = _conv_prompt(u_p, *conv_w, 512)
    h_p, hpk_p = _outproj([a[0] for a in attn_p], [a[1] for a in attn_p], conv_p, xp, w_o_b, row(b_o), *ln1, 512,
                          dils)

    xs = x_sample.reshape(n_s, D_MODEL)
    ones = (1,) * N_GROUPS
    *qkv_s, u_s, kv_s = _project(xs, w_in_b, b_in_r, n_s, n_s, F32, ones)
    caches_t = [jnp.transpose(c[0].reshape(dec_batch, -1, 2 * GROUP_W), (0, 2, 1)) for c in caches]
    attn_s = [_attention_sample(qkv_s[g], caches_t[g], g, dec_seq) for g in range(N_GROUPS)]
    u_hist = jnp.concatenate([state_conv[0], u_s.reshape(dec_batch, dec_seq, CONV_CH)], axis=1)
    conv_s = _conv_sample(u_hist, *conv_w, dec_seq)
    h_s, hpk_s = _outproj([a[0] for a in attn_s], [a[1] for a in attn_s], conv_s, xs, w_o_b, row(b_o), *ln1, n_s,
                          ones)

    idx, wts, pos, counts = _route(h_p, h_s, w_router[0].T.astype(BF16), router_bias[0])
    n_tok = seq + n_s
    starts, *chunks = _chunks(counts, n_tok * TOP_K)
    dest = _dest_rows(idx, pos, starts)
    dest_w = dest.reshape(TOP_K, n_tok // SC_WINDOW, SC_WINDOW).transpose(1, 0, 2)
    tiles = lambda a: a.reshape(-1, PACK_SLABS, LANES)
    flat = lambda a: a.reshape(-1, LANES)
    x_sorted = _sc_scatter_rows(tiles(hpk_p), tiles(hpk_s), dest_w, jnp.zeros((EXPERT_CH, PACK_SLABS, LANES), U32),
                                n_tok * TOP_K)
    y_sorted = _experts(chunks, flat(x_sorted), w_gate[0], w_up[0], w_down[0])
    shared = (ws_gate[0].astype(BF16), ws_up[0].astype(BF16), ws_down[0].astype(BF16))
    comb = functools.partial(_combine, wts.T, *shared, row(ln2_g), row(ln2_b))
    y_table = tiles(y_sorted)

    def finish(h, t0):
        y_slots = _sc_gather_rows(y_table, dest[:, t0:t0 + h.shape[0]].reshape(-1))
        return comb(h, 0, t0, flat(y_slots))

    y_s = finish(h_s, seq)
    y_p = finish(h_p, 0)

    kv_prompt = [_kv_cache(kv_p, g, min(w, seq))[None, None] for g, (w, _) in enumerate(DIL_GROUPS)]
    assert seq >= CONV_WIDTH - 1
    conv_prompt = u_p[-(CONV_WIDTH - 1):]
    kv_s4 = kv_s.reshape(dec_batch, dec_seq, 2, N_GROUPS, HEADS_PER_GROUP, HEAD_DIM)
    kv_sample = [kv_s4[:, :, :, g][None] for g in range(N_GROUPS)]
    conv_sample = u_hist[:, -(CONV_WIDTH - 1):]
    return (y_p[None], y_s.reshape(dec_batch, dec_seq, D_MODEL), *kv_prompt, conv_prompt[None, None],
            *kv_sample, conv_sample[None])
```

```python
import functools

import jax
import jax.numpy as jnp
from jax import lax
from jax.experimental import pallas as pl
from jax.experimental.pallas import tpu as pltpu
from jax.experimental.pallas import tpu_sc as plsc

F32 = jnp.float32
BF16 = jnp.bfloat16
I32 = jnp.int32

D_MODEL = 1024
HEAD_DIM = 64
HEADS_PER_GROUP = 4
GROUP_W = HEADS_PER_GROUP * HEAD_DIM
DIL_GROUPS = ((128, 1), (512, 4), (2048, 16))
N_GROUPS = len(DIL_GROUPS)
ATTN_W = N_GROUPS * GROUP_W
CONV_CH = D_MODEL - ATTN_W
CONV_WIDTH = 31
IN_W = 3 * ATTN_W + 2 * CONV_CH
BAND = 128
N_EXPERTS = 256
TOP_K = 8
N_EXPERT_GROUPS = 8
EXPERTS_PER_GROUP = N_EXPERTS // N_EXPERT_GROUPS
TOPK_GROUPS = 4
EXPERT_FF = 256
ROUTED_SCALE = 2.5
DEPTH = 1
DN_ALPHA = (2 * DEPTH) ** 0.25
LN_EPS = 1e-5
MASKED = -1e30

VMEM_LIMIT_BYTES = 56 * 1024 * 1024


def _params(*sem):
    return pltpu.CompilerParams(dimension_semantics=sem, vmem_limit_bytes=VMEM_LIMIT_BYTES)


def _dot(a, b):
    return jnp.dot(a, b, preferred_element_type=F32)


def _dot_nt(a, b):
    return lax.dot_general(a, b, (((1,), (1,)), ((), ())), preferred_element_type=F32)


def _layer_norm(x, g, b):
    mu = jnp.mean(x, axis=-1, keepdims=True)
    xc = x - mu
    var = jnp.mean(xc * xc, axis=-1, keepdims=True)
    return xc * lax.rsqrt(var + LN_EPS) * g + b


def _silu(x):
    return x * jax.nn.sigmoid(x)


def _alibi_slopes():
    n = N_GROUPS * HEADS_PER_GROUP
    h = jnp.arange(1, n + 1, dtype=F32)
    return (2.0 ** (-8.0 * h / n)).reshape(N_GROUPS, HEADS_PER_GROUP)


LANES = 128


def _proj_kernel(x_ref, w_ref, b_ref, *refs, dils, tm):
    qkv_refs, (u_ref, kv_ref, zs) = refs[:N_GROUPS], refs[N_GROUPS:]
    x = x_ref[...].astype(BF16)
    for part in range(3):
        for g in range(N_GROUPS):
            c0 = part * ATTN_W + g * GROUP_W
            z = _dot(x, w_ref[:, c0:c0 + GROUP_W]) + b_ref[:, c0:c0 + GROUP_W]
            if part > 0:
                kv_ref[:, c0 - ATTN_W:c0 - ATTN_W + GROUP_W] = z
            out, dil = qkv_refs[g], dils[g]
            if dil == 1:
                out[:, part * GROUP_W:(part + 1) * GROUP_W] = z.astype(out.dtype)
                continue
            for half in range(GROUP_W // LANES):
                zs[half] = z[:, half * LANES:(half + 1) * LANES]
            for r in range(dil):
                for half in range(GROUP_W // LANES):
                    c = r * 3 * GROUP_W + part * GROUP_W + half * LANES
                    out[:, c:c + LANES] = zs[half, pl.ds(r, tm // dil, stride=dil), :].astype(out.dtype)
    c0 = 3 * ATTN_W
    a = _dot(x, w_ref[:, c0:c0 + CONV_CH]) + b_ref[:, c0:c0 + CONV_CH]
    gate = _dot(x, w_ref[:, c0 + CONV_CH:]) + b_ref[:, c0 + CONV_CH:]
    u_ref[...] = a * jax.nn.sigmoid(gate)


def _project(x, w_in, b_in, tm, keep, qkv_dtype, dils):
    n = x.shape[0]
    nt = n // tm
    nk = keep // tm
    return pl.pallas_call(
        functools.partial(_proj_kernel, dils=dils, tm=tm),
        grid=(nt,),
        in_specs=[
            pl.BlockSpec((tm, D_MODEL), lambda i: (i, 0)),
            pl.BlockSpec((D_MODEL, IN_W), lambda i: (0, 0)),
            pl.BlockSpec((1, IN_W), lambda i: (0, 0)),
        ],
        out_specs=[pl.BlockSpec((tm // d, d * 3 * GROUP_W), lambda i: (i, 0)) for d in dils] + [
            pl.BlockSpec((tm, CONV_CH), lambda i: (i, 0)),
            pl.BlockSpec((tm, 2 * ATTN_W), lambda i: (jnp.maximum(i - (nt - nk), 0), 0)),
        ],
        out_shape=[jax.ShapeDtypeStruct((n // d, d * 3 * GROUP_W), qkv_dtype) for d in dils] + [
            jax.ShapeDtypeStruct((n, CONV_CH), F32),
            jax.ShapeDtypeStruct((keep, 2 * ATTN_W), F32),
        ],
        scratch_shapes=[pltpu.VMEM((GROUP_W // LANES, tm, LANES), F32)],
        compiler_params=_params("arbitrary"),
        name="proj",
    )(x, w_in, b_in)


def _head_select(parts, rows):
    col = lax.broadcasted_iota(I32, (rows, GROUP_W), 1) // HEAD_DIM
    out = jnp.broadcast_to(parts[-1], (rows, GROUP_W))
    for h in range(HEADS_PER_GROUP - 2, -1, -1):
        out = jnp.where(col == h, parts[h], out)
    return out


def _head_rows(q, rows):
    col = lax.broadcasted_iota(I32, (rows, GROUP_W), 1) // HEAD_DIM
    return jnp.concatenate([jnp.where(col == h, q, jnp.zeros_like(q)) for h in range(HEADS_PER_GROUP)], axis=0)


def _softmax_pv(s, v, rows):
    m = jnp.max(s, axis=-1, keepdims=True)
    e = jnp.exp(s - m)
    l = jnp.sum(e, axis=-1, keepdims=True)
    pv = _dot(e.astype(BF16), v) / l
    lse = m + jnp.log(l)
    o = _head_select([pv[h * rows:(h + 1) * rows] for h in range(HEADS_PER_GROUP)], rows)
    lse_x = _head_select([lse[h * rows:(h + 1) * rows] for h in range(HEADS_PER_GROUP)], rows)
    return o, lse_x


ATTN_QB = 16


def _attn_kernel(q_ref, kp_ref, kc_ref, vp_ref, vc_ref, bias_ref, o_ref, lse_ref, *, qb):
    b = pl.program_id(1)
    for j in range(qb):
        rows = slice(j * BAND, (j + 1) * BAND)
        qm = _head_rows(q_ref[rows, :] * HEAD_DIM ** -0.5, BAND)
        if j == 0:
            k = jnp.concatenate([kp_ref[...], kc_ref[rows, :]], axis=0)
            v = jnp.concatenate([vp_ref[...], vc_ref[rows, :]], axis=0)
            bias = bias_ref[jnp.minimum(b, 1)]
        else:
            k = kc_ref[(j - 1) * BAND:(j + 1) * BAND, :]
            v = vc_ref[(j - 1) * BAND:(j + 1) * BAND, :]
            bias = bias_ref[1]
        s = _dot_nt(qm, k) + bias
        o, lse_x = _softmax_pv(s, v, BAND)
        o_ref[rows, :] = o.astype(o_ref.dtype)
        lse_ref[rows, :] = lse_x


def _prompt_bias(g, dil):
    slopes = _alibi_slopes()[g]
    qi = jnp.arange(BAND)[:, None]
    kj = jnp.arange(2 * BAND)[None, :]
    steps = qi + BAND - kj
    nk = DIL_GROUPS[g][0] // dil
    valid = (steps >= 0) & (steps <= nk)
    bias = -slopes[:, None, None] * (steps * dil).astype(F32)
    inner = jnp.where(valid[None], bias, MASKED)
    first = jnp.where((valid & (kj >= BAND))[None], bias, MASKED)
    return jnp.stack([first, inner]).reshape(2, HEADS_PER_GROUP * BAND, 2 * BAND)


def _attention_prompt(view, g):
    win, dil = DIL_GROUPS[g]
    n_cls = view.shape[0]
    qb = ATTN_QB
    while n_cls % (qb * BAND):
        qb //= 2
    rows = qb * BAND
    assert win // dil <= BAND and qb >= 1 and view.shape[1] == dil * 3 * GROUP_W
    cur = lambda part: pl.BlockSpec((rows, GROUP_W), lambda r, b: (b, r * 3 + part))
    prev = lambda part: pl.BlockSpec((BAND, GROUP_W), lambda r, b: (jnp.maximum(b * qb - 1, 0), r * 3 + part))
    bias = _prompt_bias(g, dil)
    return pl.pallas_call(
        functools.partial(_attn_kernel, qb=qb),
        grid=(dil, n_cls // rows),
        in_specs=[cur(0), prev(1), cur(1), prev(2), cur(2),
                  pl.BlockSpec((2, HEADS_PER_GROUP * BAND, 2 * BAND), lambda r, b: (0, 0, 0))],
        out_specs=[pl.BlockSpec((rows, GROUP_W), lambda r, b: (b, r)),
                   pl.BlockSpec((rows, GROUP_W), lambda r, b: (b, r))],
        out_shape=[jax.ShapeDtypeStruct((n_cls, dil * GROUP_W), BF16),
                   jax.ShapeDtypeStruct((n_cls, dil * GROUP_W), F32)],
        compiler_params=_params("arbitrary", "arbitrary"),
        name=f"attn_g{g}",
    )(view, view, view, view, view, bias)


def _attn_sample_kernel(q_ref, k_ref, v_ref, cache_ref, bias_ref, o_ref, lse_ref, *, n_buf, t, seqs):
    pad = jnp.zeros((BAND - t, GROUP_W), F32)
    scale = HEAD_DIM ** -0.5
    for b in range(seqs):
        rows = slice(b * t, (b + 1) * t)
        k_new = jnp.concatenate([k_ref[rows, :], pad], axis=0).astype(BF16)
        v_new = jnp.concatenate([v_ref[rows, :], pad], axis=0).astype(BF16)
        qm = _head_rows(q_ref[rows, :], t).astype(BF16)
        s_old = _dot(qm, cache_ref[b, 0:GROUP_W, :].astype(BF16)) * scale + bias_ref[:, 0:n_buf]
        s_new = _dot_nt(qm, k_new) * scale + bias_ref[:, n_buf:]
        m = jnp.maximum(jnp.max(s_old, axis=-1, keepdims=True), jnp.max(s_new, axis=-1, keepdims=True))
        e_old = jnp.exp(s_old - m)
        e_new = jnp.exp(s_new - m)
        l = jnp.sum(e_old, axis=-1, keepdims=True) + jnp.sum(e_new, axis=-1, keepdims=True)
        pv = _dot_nt(e_old.astype(BF16), cache_ref[b, GROUP_W:, :].astype(BF16)) + _dot(e_new.astype(BF16), v_new)
        pv = pv / l
        lse = m + jnp.log(l)
        o_ref[rows, :] = _head_select([pv[h * t:(h + 1) * t] for h in range(HEADS_PER_GROUP)], t)
        lse_ref[rows, :] = _head_select([lse[h * t:(h + 1) * t] for h in range(HEADS_PER_GROUP)], t)


def _sample_bias(g, n_buf, t):
    win, dil = DIL_GROUPS[g]
    slopes = _alibi_slopes()[g]
    tq = jnp.arange(t)[:, None]
    j = jnp.arange(n_buf + BAND)[None, :]
    dist = n_buf + tq - j
    valid = (dist >= 0) & (dist % dil == 0) & (dist <= win) & (j < n_buf + t)
    bias = -slopes[:, None, None] * dist.astype(F32)
    return jnp.where(valid[None], bias, MASKED).reshape(HEADS_PER_GROUP * t, n_buf + BAND)


def _attention_sample(qkv, cache_t, g, t):
    nb, n_buf = cache_t.shape[0], cache_t.shape[2]
    seqs = max(1, min(4, 1024 // n_buf))
    assert nb % seqs == 0
    col = lambda part: pl.BlockSpec((seqs * t, GROUP_W), lambda i: (i, part))
    kern = functools.partial(_attn_sample_kernel, n_buf=n_buf, t=t, seqs=seqs)
    bias = _sample_bias(g, n_buf, t)
    return pl.pallas_call(
        kern,
        grid=(nb // seqs,),
        in_specs=[col(0), col(1), col(2),
                  pl.BlockSpec((seqs, 2 * GROUP_W, n_buf), lambda i: (i, 0, 0)),
                  pl.BlockSpec((HEADS_PER_GROUP * t, n_buf + BAND), lambda i: (0, 0))],
        out_specs=[pl.BlockSpec((seqs * t, GROUP_W), lambda i: (i, 0)),
                   pl.BlockSpec((seqs * t, GROUP_W), lambda i: (i, 0))],
        out_shape=[jax.ShapeDtypeStruct((nb * t, GROUP_W), F32),
                   jax.ShapeDtypeStruct((nb * t, GROUP_W), F32)],
        compiler_params=_params("arbitrary"),
        name=f"attn_sample_g{g}",
    )(qkv, qkv, qkv, cache_t, bias)


CONV_HALO = 32


def _conv_tail(acc, b_ref, g_ref, beta_ref):
    return _silu(_layer_norm(acc + b_ref[...], g_ref[...], beta_ref[...]))


SUBLANES = 8


def _conv_prompt_kernel(halo_ref, u_ref, w_ref, b_ref, g_ref, beta_ref, o_ref, hist, part, *, tm):
    i = pl.program_id(0)
    hist[0:CONV_HALO, :] = jnp.where(i == 0, 0.0, halo_ref[...])
    hist[CONV_HALO:CONV_HALO + tm, :] = u_ref[...]
    hist[CONV_HALO + tm:, :] = jnp.zeros((SUBLANES, CONV_CH), F32)
    off = CONV_HALO - (CONV_WIDTH - 1)
    acc = None
    for s in range(SUBLANES):
        group = None
        for m in range(s, off + CONV_WIDTH, SUBLANES):
            j = m - off
            if j < 0:
                continue
            term = w_ref[j:j + 1, :] * hist[m - s:m - s + tm + SUBLANES, :]
            group = term if group is None else group + term
        part[...] = group
        shifted = part[s:s + tm, :]
        acc = shifted if acc is None else acc + shifted
    o_ref[...] = _conv_tail(acc, b_ref, g_ref, beta_ref).astype(o_ref.dtype)


def _conv_prompt(u, w_dw, b_dw, ln_g, ln_b, tm):
    n = u.shape[0]
    vec = pl.BlockSpec((1, CONV_CH), lambda i: (0, 0))
    return pl.pallas_call(
        functools.partial(_conv_prompt_kernel, tm=tm),
        grid=(n // tm,),
        in_specs=[pl.BlockSpec((CONV_HALO, CONV_CH), lambda i: (jnp.maximum(i * (tm // CONV_HALO) - 1, 0), 0)),
                  pl.BlockSpec((tm, CONV_CH), lambda i: (i, 0)),
                  pl.BlockSpec((CONV_WIDTH, CONV_CH), lambda i: (0, 0)), vec, vec, vec],
        out_specs=pl.BlockSpec((tm, CONV_CH), lambda i: (i, 0)),
        out_shape=jax.ShapeDtypeStruct((n, CONV_CH), BF16),
        scratch_shapes=[pltpu.VMEM((CONV_HALO + tm + SUBLANES, CONV_CH), F32),
                        pltpu.VMEM((tm + SUBLANES, CONV_CH), F32)],
        compiler_params=_params("arbitrary"),
        name="conv_prompt",
    )(u, u, w_dw, b_dw, ln_g, ln_b)


CONV_SAMPLE_SEQS = 8


def _conv_sample_kernel(hist_ref, w_ref, b_ref, g_ref, beta_ref, o_ref, *, t, seqs):
    for b in range(seqs):
        acc = jnp.zeros((t, CONV_CH), F32)
        for j in range(CONV_WIDTH):
            acc = acc + w_ref[j:j + 1, :] * hist_ref[b, j:j + t, :]
        o_ref[b * t:(b + 1) * t, :] = _conv_tail(acc, b_ref, g_ref, beta_ref)


def _conv_sample(u_hist, w_dw, b_dw, ln_g, ln_b, t):
    nb, rows = u_hist.shape[0], u_hist.shape[1]
    seqs = CONV_SAMPLE_SEQS if nb % CONV_SAMPLE_SEQS == 0 else 1
    vec = pl.BlockSpec((1, CONV_CH), lambda i: (0, 0))
    return pl.pallas_call(
        functools.partial(_conv_sample_kernel, t=t, seqs=seqs),
        grid=(nb // seqs,),
        in_specs=[pl.BlockSpec((seqs, rows, CONV_CH), lambda i: (i, 0, 0)),
                  pl.BlockSpec((CONV_WIDTH, CONV_CH), lambda i: (0, 0)), vec, vec, vec],
        out_specs=pl.BlockSpec((seqs * t, CONV_CH), lambda i: (i, 0)),
        out_shape=jax.ShapeDtypeStruct((nb * t, CONV_CH), F32),
        compiler_params=_params("arbitrary"),
        name="conv_sample",
    )(u_hist, w_dw, b_dw, ln_g, ln_b)


U32 = jnp.uint32
HALF_D = D_MODEL // 2
PACK_SLABS = HALF_D // LANES


def _token_order(ref, dil, scr, tm):
    if dil == 1:
        return ref[...].astype(F32)
    for r in range(dil):
        for half in range(GROUP_W // LANES):
            c = r * GROUP_W + half * LANES
            scr[half, pl.ds(r, tm // dil, stride=dil), :] = ref[:, c:c + LANES].astype(F32)
    return jnp.concatenate([scr[half] for half in range(GROUP_W // LANES)], axis=1)


def _pack_rows(ref, val):
    rows = val.shape[0]
    bits = lax.bitcast_convert_type(val.astype(BF16).astype(F32), U32)
    packed = lax.shift_right_logical(bits[:, :HALF_D], jnp.uint32(16)) | bits[:, HALF_D:]
    for c in range(PACK_SLABS):
        ref[pl.ds(c, rows, stride=PACK_SLABS), :] = packed[:, c * LANES:(c + 1) * LANES]


def _unpack_rows(ref):
    rows = ref.shape[0] // PACK_SLABS
    u = jnp.concatenate([ref[pl.ds(c, rows, stride=PACK_SLABS), :] for c in range(PACK_SLABS)], axis=1)
    lo = lax.bitcast_convert_type(lax.shift_left(u, jnp.uint32(16)), F32)
    hi = lax.bitcast_convert_type(u & jnp.uint32(0xFFFF0000), F32)
    return lo, hi


def _outproj_kernel(o0, o1, o2, l0, l1, l2, c_ref, x_ref, w_ref, b_ref, g_ref, beta_ref, h_ref, hp_ref, *scr,
                    dils, tm):
    os_ = [_token_order(r, d, scr[2 * g], tm) for g, (r, d) in enumerate(zip((o0, o1, o2), dils))]
    ls = [_token_order(r, d, scr[2 * g + 1], tm) for g, (r, d) in enumerate(zip((l0, l1, l2), dils))]
    m = jnp.maximum(jnp.maximum(ls[0], ls[1]), ls[2])
    es = [jnp.exp(l - m) for l in ls]
    inv = 1.0 / (es[0] + es[1] + es[2])
    mixed = b_ref[...]
    for g in range(N_GROUPS):
        a = (os_[g] * (es[g] * inv)).astype(BF16)
        mixed = mixed + _dot(a, w_ref[g * GROUP_W:(g + 1) * GROUP_W, :])
    mixed = mixed + _dot(c_ref[...].astype(BF16), w_ref[ATTN_W:, :])
    h = _layer_norm(DN_ALPHA * x_ref[...] + mixed, g_ref[...], beta_ref[...])
    h_ref[...] = h
    _pack_rows(hp_ref, h)


def _outproj(os_, lses, conv, x, w_o, b_o, ln_g, ln_b, tm, dils):
    n = x.shape[0]
    grp = [pl.BlockSpec((tm // d, d * GROUP_W), lambda i: (i, 0)) for d in dils]
    vec = pl.BlockSpec((1, D_MODEL), lambda i: (0, 0))
    return pl.pallas_call(
        functools.partial(_outproj_kernel, dils=dils, tm=tm),
        grid=(n // tm,),
        in_specs=grp + grp + [pl.BlockSpec((tm, CONV_CH), lambda i: (i, 0)),
                              pl.BlockSpec((tm, D_MODEL), lambda i: (i, 0)),
                              pl.BlockSpec((D_MODEL, D_MODEL), lambda i: (0, 0)), vec, vec, vec],
        out_specs=[pl.BlockSpec((tm, D_MODEL), lambda i: (i, 0)),
                   pl.BlockSpec((tm * PACK_SLABS, LANES), lambda i: (i, 0))],
        out_shape=[jax.ShapeDtypeStruct((n, D_MODEL), F32),
                   jax.ShapeDtypeStruct((n * PACK_SLABS, LANES), U32)],
        scratch_shapes=[pltpu.VMEM((GROUP_W // LANES, tm, LANES), F32) for _ in range(2 * N_GROUPS)],
        compiler_params=_params("arbitrary"),
        name="outproj",
    )(*os_, *lses, conv, x, w_o, b_o, ln_g, ln_b)


ROUTER_TM = 256


def _first_index(hit, idx, limit, axis):
    return jnp.min(jnp.where(hit, idx, limit), axis=axis, keepdims=True)


def _router_kernel(ha_ref, hb_ref, w_ref, rb_ref, tri_ref, idx_ref, wt_ref, pos_ref, cnt_ref, run, *, tiles_a):
    i = pl.program_id(0)
    tm = ROUTER_TM

    @pl.when(i == 0)
    def _():
        run[...] = jnp.zeros_like(run)

    h = jnp.where(i < tiles_a, ha_ref[...], hb_ref[...])
    logits = _dot_nt(w_ref[...], h.astype(BF16))
    scores = jax.nn.sigmoid(logits)
    biased = scores + rb_ref[...]
    groups = [biased[g * EXPERTS_PER_GROUP:(g + 1) * EXPERTS_PER_GROUP] for g in range(N_EXPERT_GROUPS)]
    ei = lax.broadcasted_iota(I32, (EXPERTS_PER_GROUP, tm), 0).astype(F32)
    gs = []
    for bg in groups:
        m1 = jnp.max(bg, axis=0, keepdims=True)
        f1 = _first_index(bg == m1, ei, float(EXPERTS_PER_GROUP), 0)
        m2 = jnp.max(jnp.where(ei == f1, -jnp.inf, bg), axis=0, keepdims=True)
        gs.append(m1 + m2)
    gs = jnp.concatenate(gs, axis=0)
    gi = lax.broadcasted_iota(I32, gs.shape, 0).astype(F32)
    keep = jnp.zeros(gs.shape, F32)
    cur = gs
    for _ in range(TOPK_GROUPS):
        m = jnp.max(cur, axis=0, keepdims=True)
        f = _first_index(cur == m, gi, float(N_EXPERT_GROUPS), 0)
        hit = gi == f
        keep = jnp.where(hit, 1.0, keep)
        cur = jnp.where(hit, -jnp.inf, cur)
    masked = jnp.concatenate([jnp.where(keep[g:g + 1] > 0.0, bg, -jnp.inf) for g, bg in enumerate(groups)], axis=0)
    xi = lax.broadcasted_iota(I32, (N_EXPERTS, tm), 0).astype(F32)
    cur = masked
    picks = []
    for _ in range(TOP_K):
        m = jnp.max(cur, axis=0, keepdims=True)
        f = _first_index(cur == m, xi, float(N_EXPERTS), 0)
        hit = xi == f
        picks.append((f, hit))
        cur = jnp.where(hit, -jnp.inf, cur)
    sel = jnp.where(cur != masked, 1.0, 0.0)
    before = _dot(sel.astype(BF16), tri_ref[...]) + run[...]
    run[...] = run[...] + jnp.sum(sel, axis=1, keepdims=True)
    ws = [jnp.sum(jnp.where(hit, scores, 0.0), axis=0, keepdims=True) for _, hit in picks]
    wsum = ws[0]
    for w in ws[1:]:
        wsum = wsum + w
    for k, (f, hit) in enumerate(picks):
        idx_ref[k:k + 1, :] = f.astype(I32)
        wt_ref[k:k + 1, :] = ws[k] / wsum * ROUTED_SCALE
        pos_ref[k:k + 1, :] = jnp.sum(jnp.where(hit, before, 0.0), axis=0, keepdims=True).astype(I32)
    cnt_ref[...] = jnp.broadcast_to(run[...], cnt_ref.shape).astype(I32)


def _route(ha, hb, w_router_t, router_bias):
    tm = ROUTER_TM
    tiles_a, tiles_b = ha.shape[0] // tm, hb.shape[0] // tm
    spec_a = pl.BlockSpec((tm, D_MODEL), lambda i: (jnp.minimum(i, tiles_a - 1), 0))
    spec_b = pl.BlockSpec((tm, D_MODEL), lambda i: (jnp.maximum(i - tiles_a, 0), 0))
    t = (tiles_a + tiles_b) * tm
    tri = (jnp.arange(tm)[:, None] < jnp.arange(tm)[None, :]).astype(BF16)
    slot = pl.BlockSpec((TOP_K, tm), lambda i: (0, i))
    idx, wts, pos, cnt = pl.pallas_call(
        functools.partial(_router_kernel, tiles_a=tiles_a),
        grid=(t // tm,),
        in_specs=[spec_a, spec_b,
                  pl.BlockSpec((N_EXPERTS, D_MODEL), lambda i: (0, 0)),
                  pl.BlockSpec((N_EXPERTS, 1), lambda i: (0, 0)),
                  pl.BlockSpec((tm, tm), lambda i: (0, 0))],
        out_specs=[slot, slot, slot, pl.BlockSpec((N_EXPERTS, 128), lambda i: (0, 0))],
        out_shape=[jax.ShapeDtypeStruct((TOP_K, t), I32), jax.ShapeDtypeStruct((TOP_K, t), F32),
                   jax.ShapeDtypeStruct((TOP_K, t), I32), jax.ShapeDtypeStruct((N_EXPERTS, 128), I32)],
        scratch_shapes=[pltpu.VMEM((N_EXPERTS, 1), F32)],
        compiler_params=_params("arbitrary"),
        name="router",
    )(ha, hb, w_router_t, router_bias.reshape(N_EXPERTS, 1), tri)
    return idx, wts, pos, cnt[:, 0]


DEST_TM = 1280


def _dest_kernel(idx_ref, pos_ref, starts_ref, dest_ref):
    tm = idx_ref.shape[1]
    ei = lax.broadcasted_iota(I32, (N_EXPERTS, tm), 0)
    starts = starts_ref[...]
    for k in range(TOP_K):
        first = jnp.sum(jnp.where(ei == idx_ref[k:k + 1, :], starts, 0.0), axis=0, keepdims=True)
        dest_ref[k:k + 1, :] = first.astype(I32) + pos_ref[k:k + 1, :]


def _dest_rows(idx, pos, starts):
    t = idx.shape[1]
    tm = DEST_TM if t % DEST_TM == 0 else ROUTER_TM
    assert t % tm == 0
    slot = pl.BlockSpec((TOP_K, tm), lambda i: (0, i))
    return pl.pallas_call(
        _dest_kernel,
        grid=(t // tm,),
        in_specs=[slot, slot, pl.BlockSpec((N_EXPERTS, 1), lambda i: (0, 0))],
        out_specs=slot,
        out_shape=jax.ShapeDtypeStruct((TOP_K, t), I32),
        compiler_params=_params("arbitrary"),
        name="dest_rows",
    )(idx, pos, starts.astype(F32).reshape(N_EXPERTS, 1))


SC_CORES = 2
SC_SUBCORES = 16
SC_WORKERS = SC_CORES * SC_SUBCORES
SC_WINDOW = 64
SC_WINDOW_SMALL = 32


def _sc_worker():
    return lax.axis_index("s") * SC_CORES + lax.axis_index("c")


def _sc_mesh():
    return plsc.VectorSubcoreMesh(core_axis_name="c", subcore_axis_name="s")


def _sc_scatter_rows(ha, hb, dest_w, zeros, n_out):
    w = dest_w.shape[2]
    na, nb = ha.shape[0], hb.shape[0]
    spare = zeros.shape[0]
    wa, wb = na // w, nb // w
    assert na % (SC_WORKERS * 2 * w) == 0 and nb % w == 0 and wb <= SC_WORKERS
    per_w = wa // SC_WORKERS
    rows_t = pltpu.VMEM((w,) + ha.shape[1:], ha.dtype)
    idx_t = pltpu.VMEM((TOP_K, w), I32)

    @functools.partial(
        pl.kernel, mesh=_sc_mesh(),
        out_type=jax.ShapeDtypeStruct((n_out + spare,) + ha.shape[1:], ha.dtype),
        scratch_types=[rows_t, rows_t, idx_t, idx_t, pltpu.SemaphoreType.DMA, pltpu.SemaphoreType.DMA],
    )
    def scatter(ha_hbm, hb_hbm, dest_hbm, zeros_hbm, out_hbm, rows0, rows1, idx0, idx1, sem0, sem1):
        wid = _sc_worker()
        bufs = ((rows0, idx0, sem0), (rows1, idx1, sem1))

        @pl.when(wid == SC_WORKERS - 1)
        def _():
            pltpu.sync_copy(zeros_hbm, out_hbm.at[pl.ds(n_out, spare)])

        def load(src_hbm, row0, win, b):
            pltpu.sync_copy(src_hbm.at[pl.ds(pl.multiple_of(row0, 8), w)], bufs[b][0])
            pltpu.sync_copy(dest_hbm.at[win], bufs[b][1])

        def copies(b):
            return [pltpu.make_async_copy(bufs[b][0], out_hbm.at[bufs[b][1].at[k]], bufs[b][2])
                    for k in range(TOP_K)]

        win0 = wid * per_w
        load(ha_hbm, win0 * w, win0, 0)

        @pl.loop(0, per_w, step=2)
        def _(i0):
            for b in range(2):
                i = i0 + b
                for c in copies(b):
                    c.start()

                @pl.when(i + 1 < per_w)
                def _():
                    load(ha_hbm, (win0 + i + 1) * w, win0 + i + 1, 1 - b)

                for c in copies(b):
                    c.wait()

        @pl.when(wid < wb)
        def _():
            load(hb_hbm, wid * w, wa + wid, 0)
            for c in copies(0):
                c.start()
            for c in copies(0):
                c.wait()

    return scatter(ha, hb, dest_w, zeros)


def _sc_gather_rows(table, idx):
    n = idx.shape[0]
    w = SC_WINDOW if n % (SC_WORKERS * 2 * SC_WINDOW) == 0 else SC_WINDOW_SMALL
    assert n % (SC_WORKERS * 2 * w) == 0
    per_w = n // SC_WORKERS
    nwin = per_w // w
    rows_t = pltpu.VMEM((w,) + table.shape[1:], table.dtype)

    @functools.partial(
        pl.kernel, mesh=_sc_mesh(),
        out_type=jax.ShapeDtypeStruct((n,) + table.shape[1:], table.dtype),
        scratch_types=[pltpu.VMEM((per_w,), I32), rows_t, rows_t, pltpu.SemaphoreType.DMA, pltpu.SemaphoreType.DMA],
    )
    def gather(table_hbm, idx_hbm, out_hbm, idx_v, rows0, rows1, sem0, sem1):
        base = pl.multiple_of(_sc_worker() * per_w, 8)
        pltpu.sync_copy(idx_hbm.at[pl.ds(base, per_w)], idx_v)
        bufs = ((rows0, sem0), (rows1, sem1))
        fetch = lambda i, b: pltpu.make_async_copy(
            table_hbm.at[idx_v.at[pl.ds(pl.multiple_of(i * w, 8), w)]], bufs[b][0], bufs[b][1])
        fetch(0, 0).start()

        @pl.loop(0, nwin, step=2)
        def _(i0):
            for b in range(2):
                i = i0 + b
                fetch(i, b).wait()

                @pl.when(i + 1 < nwin)
                def _():
                    fetch(i + 1, 1 - b).start()

                pltpu.sync_copy(bufs[b][0], out_hbm.at[pl.ds(pl.multiple_of(base + i * w, 8), w)])

    return gather(table, idx)


EXPERT_CH = 256
X_SLOTS = 6
W_SLOTS = 3


def _expert_kernel(row0_ref, exp_ref, live_ref, fresh_ref, next1_ref, next2_ref, wslot_ref, ordered_ref, xs_ref,
                   wg_ref, wu_ref, wd_ref, ys_ref, xbuf, ybuf, sg, su, sd, wgb, wub, wdb, xsem, ysem, wsem):
    v = pl.program_id(0)
    nv = pl.num_programs(0)
    slot = lax.rem(v, X_SLOTS)
    yslot = lax.rem(v, 3)
    rows = lambda u: pl.ds(pl.multiple_of(row0_ref[u] * PACK_SLABS, PACK_SLABS), EXPERT_CH * PACK_SLABS)
    x_copy = lambda u: pltpu.make_async_copy(xs_ref.at[rows(u)], xbuf.at[lax.rem(u, X_SLOTS)],
                                             xsem.at[lax.rem(u, X_SLOTS)])
    y_copy = lambda u: pltpu.make_async_copy(ybuf.at[lax.rem(u, 3)], ys_ref.at[rows(u)], ysem.at[lax.rem(u, 3)])
    prev1, prev2 = jnp.maximum(v - 1, 0), jnp.maximum(v - 2, 0)
    w_copies = lambda e, s: [pltpu.make_async_copy(w.at[e], stage.at[s], wsem.at[s])
                             for w, stage in ((wg_ref, sg), (wu_ref, su), (wd_ref, sd))]
    spare = pl.ds(ys_ref.shape[0] - EXPERT_CH * PACK_SLABS, EXPERT_CH * PACK_SLABS)

    @pl.when(v == 0)
    def _():
        for c in w_copies(exp_ref[0], 0):
            c.start(priority=1)

        @pl.when(next1_ref[0] >= 0)
        def _():
            for c in w_copies(next1_ref[0], 1):
                c.start(priority=1)

        x_copy(0).start()
        for u in range(1, X_SLOTS - 1):
            @pl.when((u < nv) & (live_ref[jnp.minimum(u, nv - 1)] == 1))
            def _():
                x_copy(u).start()
        ybuf[2] = jnp.zeros(ybuf.shape[1:], U32)
        zero = pltpu.make_async_copy(ybuf.at[2], ys_ref.at[spare], ysem.at[2])
        zero.start()
        zero.wait()

    ahead = jnp.minimum(v + X_SLOTS - 1, nv - 1)

    @pl.when((v + X_SLOTS - 1 < nv) & (live_ref[ahead] == 1))
    def _():
        x_copy(ahead).start()

    @pl.when(fresh_ref[v] == 1)
    def _():
        p = wslot_ref[v]
        for c in w_copies(exp_ref[v], p):
            c.wait()

        @pl.when(next2_ref[v] >= 0)
        def _():
            for c in w_copies(next2_ref[v], lax.rem(p + 2, W_SLOTS)):
                c.start(priority=1)

        wgb[...] = sg[p].astype(BF16)
        wub[...] = su[p].astype(BF16)
        wdb[...] = sd[p].astype(BF16)

    @pl.when(live_ref[v] == 1)
    def _():
        x_copy(v).wait()
        x = jnp.concatenate(_unpack_rows(xbuf.at[slot]), axis=1).astype(BF16)
        hid = (_silu(_dot(x, wgb[...])) * _dot(x, wub[...])).astype(BF16)
        _pack_rows(ybuf.at[yslot], _dot(hid, wdb[...]))

    @pl.when((v >= 2) & (live_ref[prev2] == 1) & (ordered_ref[prev1] == 0))
    def _():
        y_copy(prev2).wait()

    @pl.when((v >= 1) & (live_ref[prev1] == 1) & (ordered_ref[v] == 1))
    def _():
        y_copy(prev1).wait()

    @pl.when(live_ref[v] == 1)
    def _():
        y_copy(v).start()

    @pl.when(v == nv - 1)
    def _():
        @pl.when((v >= 1) & (live_ref[prev1] == 1) & (ordered_ref[v] == 0))
        def _():
            y_copy(prev1).wait()

        @pl.when(live_ref[v] == 1)
        def _():
            y_copy(v).wait()


def _chunks(counts, n_rows):
    ch = EXPERT_CH
    nv = n_rows // ch + N_EXPERTS
    ends = jnp.cumsum(counts)
    starts = ends - counts
    nch = (counts + ch - 1) // ch
    cend = jnp.cumsum(nch)
    cstart = cend - nch
    v = jnp.arange(nv, dtype=I32)
    live = v < cend[-1]
    vc = jnp.minimum(v, cend[-1] - 1)
    ids = jnp.arange(N_EXPERTS, dtype=I32)
    e = jnp.minimum(jnp.sum(cend[None, :] <= vc[:, None], axis=1), N_EXPERTS - 1).astype(I32)
    of_e = lambda a: jnp.sum(jnp.where(e[:, None] == ids[None, :], a[None, :], 0), axis=1)
    starts_e, ends_e, count_e = of_e(starts), of_e(ends), of_e(counts)
    c = vc - of_e(cstart)
    is_last = c == of_e(nch) - 1
    row0 = jnp.where(is_last & (count_e >= ch), ends_e - ch, starts_e + c * ch).astype(I32)
    overrun = live & is_last & (count_e < ch)
    ordered = jnp.concatenate([jnp.zeros((1,), bool), overrun[:-1]]) & live
    fresh = jnp.concatenate([jnp.ones((1,), I32), (e[1:] != e[:-1]).astype(I32)])
    order = jnp.cumsum(counts > 0) - 1
    order_e = of_e(order)

    def after(k):
        hit = (counts > 0)[None, :] & (order[None, :] == order_e[:, None] + k)
        return (jnp.sum(jnp.where(hit, ids[None, :] + 1, 0), axis=1) - 1).astype(I32)

    return (starts.astype(I32), row0, e, live.astype(I32), jnp.where(live, fresh, 0).astype(I32),
            after(1), after(2), (order_e % W_SLOTS).astype(I32), ordered.astype(I32))


def _experts(chunks, xs, w_gate, w_up, w_down):
    ch = EXPERT_CH
    buf = lambda n: pltpu.VMEM((n, ch * PACK_SLABS, LANES), U32)
    anywhere = pl.BlockSpec(memory_space=pl.ANY)
    return pl.pallas_call(
        _expert_kernel,
        grid_spec=pltpu.PrefetchScalarGridSpec(
            num_scalar_prefetch=len(chunks),
            grid=(chunks[0].shape[0],),
            in_specs=[anywhere] * 4,
            out_specs=anywhere,
            scratch_shapes=[buf(X_SLOTS), buf(3),
                            pltpu.VMEM((W_SLOTS, D_MODEL, EXPERT_FF), F32),
                            pltpu.VMEM((W_SLOTS, D_MODEL, EXPERT_FF), F32),
                            pltpu.VMEM((W_SLOTS, EXPERT_FF, D_MODEL), F32),
                            pltpu.VMEM((D_MODEL, EXPERT_FF), BF16), pltpu.VMEM((D_MODEL, EXPERT_FF), BF16),
                            pltpu.VMEM((EXPERT_FF, D_MODEL), BF16),
                            pltpu.SemaphoreType.DMA((X_SLOTS,)), pltpu.SemaphoreType.DMA((3,)),
                            pltpu.SemaphoreType.DMA((W_SLOTS,))],
        ),
        out_shape=jax.ShapeDtypeStruct(xs.shape, U32),
        compiler_params=_params("arbitrary"),
        name="experts",
    )(*chunks, xs, w_gate, w_up, w_down)


COMBINE_TM = 512


def _combine_kernel(wt_ref, h_ref, sg_ref, su_ref, sd_ref, g_ref, beta_ref, *refs):
    y_refs, o_ref = refs[:TOP_K], refs[TOP_K]
    h = h_ref[...]
    hb = h.astype(BF16)
    hid = (_silu(_dot(hb, sg_ref[...])) * _dot(hb, su_ref[...])).astype(BF16)
    acc = DN_ALPHA * h + _dot(hid, sd_ref[...])
    lo = acc[:, :HALF_D]
    hi = acc[:, HALF_D:]
    for k in range(TOP_K):
        y_lo, y_hi = _unpack_rows(y_refs[k])
        w = wt_ref[:, k:k + 1]
        lo = lo + w * y_lo
        hi = hi + w * y_hi
    o_ref[...] = _layer_norm(jnp.concatenate([lo, hi], axis=1), g_ref[...], beta_ref[...])


def _combine(wts_tk, ws_gate, ws_up, ws_down, ln_g, ln_b, h, h_row0, t0, yg):
    n = yg.shape[0] // (TOP_K * PACK_SLABS)
    tm = COMBINE_TM if n % COMBINE_TM == 0 else ROUTER_TM
    assert n % tm == 0 and h_row0 % tm == 0 and t0 % tm == 0
    b0 = t0 // tm
    per_slot = n // tm
    rows = pl.BlockSpec((tm, D_MODEL), lambda i: (h_row0 // tm + i, 0))
    slot_rows = [pl.BlockSpec((tm * PACK_SLABS, LANES), lambda i, k=k: (k * per_slot + i, 0))
                 for k in range(TOP_K)]
    vec = pl.BlockSpec((1, D_MODEL), lambda i: (0, 0))
    return pl.pallas_call(
        _combine_kernel,
        grid=(n // tm,),
        in_specs=[pl.BlockSpec((tm, TOP_K), lambda i: (b0 + i, 0)), rows,
                  pl.BlockSpec((D_MODEL, EXPERT_FF), lambda i: (0, 0)),
                  pl.BlockSpec((D_MODEL, EXPERT_FF), lambda i: (0, 0)),
                  pl.BlockSpec((EXPERT_FF, D_MODEL), lambda i: (0, 0)),
                  vec, vec] + slot_rows,
        out_specs=pl.BlockSpec((tm, D_MODEL), lambda i: (i, 0)),
        out_shape=jax.ShapeDtypeStruct((n, D_MODEL), F32),
        compiler_params=_params("arbitrary"),
        name="combine",
    )(wts_tk, h, ws_gate, ws_up, ws_down, ln_g, ln_b, *([yg] * TOP_K))


def _kv_cache(kv_tail, g, keep):
    k = kv_tail[-keep:, g * GROUP_W:(g + 1) * GROUP_W]
    v = kv_tail[-keep:, ATTN_W + g * GROUP_W:ATTN_W + (g + 1) * GROUP_W]
    return jnp.stack([k, v], axis=1).reshape(keep, 2, HEADS_PER_GROUP, HEAD_DIM)


def kernel(x_prompt, x_sample, cache_kv_w128, cache_kv_w512, cache_kv_w2048, state_conv, w_in, b_in, w_dw, b_dw,
           conv_ln_g, conv_ln_b, w_o, b_o, ln1_g, ln1_b, w_router, router_bias, w_gate, w_up, w_down, ws_gate,
           ws_up, ws_down, ln2_g, ln2_b):
    assert w_in.shape[0] == DEPTH == 1
    batch, seq, _ = x_prompt.shape
    dec_batch, dec_seq, _ = x_sample.shape
    assert batch == 1
    n_s = dec_batch * dec_seq
    caches = (cache_kv_w128, cache_kv_w512, cache_kv_w2048)
    row = lambda a: a[0].reshape(1, -1)

    w_in_b = w_in[0].astype(BF16)
    b_in_r = row(b_in)
    w_o_b = w_o[0].astype(BF16)
    conv_w = (w_dw[0], row(b_dw), row(conv_ln_g), row(conv_ln_b))
    ln1 = (row(ln1_g), row(ln1_b))

    xp = x_prompt[0]
    keep_p = min(max(w for w, _ in DIL_GROUPS), seq)
    dils = tuple(d for _, d in DIL_GROUPS)
    *qkv_p, u_p, kv_p = _project(xp, w_in_b, b_in_r, 512, keep_p, BF16, dils)
    attn_p = [_attention_prompt(qkv_p[g], g) for g in range(N_GROUPS)]
    conv_p = _conv_prompt(u_p, *conv_w, 512)
    h_p, hpk_p = _outproj([a[0] for a in attn_p], [a[1] for a in attn_p], conv_p, xp, w_o_b, row(b_o), *ln1, 512,
                          dils)

    xs = x_sample.reshape(n_s, D_MODEL)
    ones = (1,) * N_GROUPS
    *qkv_s, u_s, kv_s = _project(xs, w_in_b, b_in_r, n_s, n_s, F32, ones)
    caches_t = [jnp.transpose(c[0].reshape(dec_batch, -1, 2 * GROUP_W), (0, 2, 1)) for c in caches]
    attn_s = [_attention_sample(qkv_s[g], caches_t[g], g, dec_seq) for g in range(N_GROUPS)]
    u_hist = jnp.concatenate([state_conv[0], u_s.reshape(dec_batch, dec_seq, CONV_CH)], axis=1)
    conv_s = _conv_sample(u_hist, *conv_w, dec_seq)
    h_s, hpk_s = _outproj([a[0] for a in attn_s], [a[1] for a in attn_s], conv_s, xs, w_o_b, row(b_o), *ln1, n_s,
                          ones)

    idx, wts, pos, counts = _route(h_p, h_s, w_router[0].T.astype(BF16), router_bias[0])
    n_tok = seq + n_s
    starts, *chunks = _chunks(counts, n_tok * TOP_K)
    dest = _dest_rows(idx, pos, starts)
    dest_w = dest.reshape(TOP_K, n_tok // SC_WINDOW, SC_WINDOW).transpose(1, 0, 2)
    tiles = lambda a: a.reshape(-1, PACK_SLABS, LANES)
    flat = lambda a: a.reshape(-1, LANES)
    x_sorted = _sc_scatter_rows(tiles(hpk_p), tiles(hpk_s), dest_w, jnp.zeros((EXPERT_CH, PACK_SLABS, LANES), U32),
                                n_tok * TOP_K)
    y_sorted = _experts(chunks, flat(x_sorted), w_gate[0], w_up[0], w_down[0])
    shared = (ws_gate[0].astype(BF16), ws_up[0].astype(BF16), ws_down[0].astype(BF16))
    comb = functools.partial(_combine, wts.T, *shared, row(ln2_g), row(ln2_b))
    y_table = tiles(y_sorted)

    def finish(h, t0):
        y_slots = _sc_gather_rows(y_table, dest[:, t0:t0 + h.shape[0]].reshape(-1))
        return comb(h, 0, t0, flat(y_slots))

    y_s = finish(h_s, seq)
    y_p = finish(h_p, 0)

    kv_prompt = [_kv_cache(kv_p, g, min(w, seq))[None, None] for g, (w, _) in enumerate(DIL_GROUPS)]
    assert seq >= CONV_WIDTH - 1
    conv_prompt = u_p[-(CONV_WIDTH - 1):]
    kv_s4 = kv_s.reshape(dec_batch, dec_seq, 2, N_GROUPS, HEADS_PER_GROUP, HEAD_DIM)
    kv_sample = [kv_s4[:, :, :, g][None] for g in range(N_GROUPS)]
    conv_sample = u_hist[:, -(CONV_WIDTH - 1):]
    return (y_p[None], y_s.reshape(dec_batch, dec_seq, D_MODEL), *kv_prompt, conv_prompt[None, None],
            *kv_sample, conv_sample[None])
```

```python
import functools

import jax
import jax.numpy as jnp
from jax import lax
from jax.experimental import pallas as pl
from jax.experimental.pallas import tpu as pltpu
from jax.experimental.pallas import tpu_sc as plsc

F32 = jnp.float32
BF16 = jnp.bfloat16
I32 = jnp.int32

D_MODEL = 1024
HEAD_DIM = 64
HEADS_PER_GROUP = 4
GROUP_W = HEADS_PER_GROUP * HEAD_DIM
DIL_GROUPS = ((128, 1), (512, 4), (2048, 16))
N_GROUPS = len(DIL_GROUPS)
ATTN_W = N_GROUPS * GROUP_W
CONV_CH = D_MODEL - ATTN_W
CONV_WIDTH = 31
IN_W = 3 * ATTN_W + 2 * CONV_CH
BAND = 128
N_EXPERTS = 256
TOP_K = 8
N_EXPERT_GROUPS = 8
EXPERTS_PER_GROUP = N_EXPERTS // N_EXPERT_GROUPS
TOPK_GROUPS = 4
EXPERT_FF = 256
ROUTED_SCALE = 2.5
DEPTH = 1
DN_ALPHA = (2 * DEPTH) ** 0.25
LN_EPS = 1e-5
MASKED = -1e30

VMEM_LIMIT_BYTES = 56 * 1024 * 1024


def _params(*sem):
    return pltpu.CompilerParams(dimension_semantics=sem, vmem_limit_bytes=VMEM_LIMIT_BYTES)


def _dot(a, b):
    return jnp.dot(a, b, preferred_element_type=F32)


def _dot_nt(a, b):
    return lax.dot_general(a, b, (((1,), (1,)), ((), ())), preferred_element_type=F32)


def _layer_norm(x, g, b):
    mu = jnp.mean(x, axis=-1, keepdims=True)
    xc = x - mu
    var = jnp.mean(xc * xc, axis=-1, keepdims=True)
    return xc * lax.rsqrt(var + LN_EPS) * g + b


def _silu(x):
    return x * jax.nn.sigmoid(x)


def _alibi_slopes():
    n = N_GROUPS * HEADS_PER_GROUP
    h = jnp.arange(1, n + 1, dtype=F32)
    return (2.0 ** (-8.0 * h / n)).reshape(N_GROUPS, HEADS_PER_GROUP)


LANES = 128


def _proj_kernel(x_ref, w_ref, b_ref, *refs, dils, tm):
    qkv_refs, (u_ref, kv_ref, zs) = refs[:N_GROUPS], refs[N_GROUPS:]
    x = x_ref[...].astype(BF16)
    for part in range(3):
        for g in range(N_GROUPS):
            c0 = part * ATTN_W + g * GROUP_W
            z = _dot(x, w_ref[:, c0:c0 + GROUP_W]) + b_ref[:, c0:c0 + GROUP_W]
            if part > 0:
                kv_ref[:, c0 - ATTN_W:c0 - ATTN_W + GROUP_W] = z
            out, dil = qkv_refs[g], dils[g]
            if dil == 1:
                out[:, part * GROUP_W:(part + 1) * GROUP_W] = z.astype(out.dtype)
                continue
            for half in range(GROUP_W // LANES):
                zs[half] = z[:, half * LANES:(half + 1) * LANES]
            for r in range(dil):
                for half in range(GROUP_W // LANES):
                    c = r * 3 * GROUP_W + part * GROUP_W + half * LANES
                    out[:, c:c + LANES] = zs[half, pl.ds(r, tm // dil, stride=dil), :].astype(out.dtype)
    c0 = 3 * ATTN_W
    a = _dot(x, w_ref[:, c0:c0 + CONV_CH]) + b_ref[:, c0:c0 + CONV_CH]
    gate = _dot(x, w_ref[:, c0 + CONV_CH:]) + b_ref[:, c0 + CONV_CH:]
    u_ref[...] = a * jax.nn.sigmoid(gate)


def _project(x, w_in, b_in, tm, keep, qkv_dtype, dils):
    n = x.shape[0]
    nt = n // tm
    nk = keep // tm
    return pl.pallas_call(
        functools.partial(_proj_kernel, dils=dils, tm=tm),
        grid=(nt,),
        in_specs=[
            pl.BlockSpec((tm, D_MODEL), lambda i: (i, 0)),
            pl.BlockSpec((D_MODEL, IN_W), lambda i: (0, 0)),
            pl.BlockSpec((1, IN_W), lambda i: (0, 0)),
        ],
        out_specs=[pl.BlockSpec((tm // d, d * 3 * GROUP_W), lambda i: (i, 0)) for d in dils] + [
            pl.BlockSpec((tm, CONV_CH), lambda i: (i, 0)),
            pl.BlockSpec((tm, 2 * ATTN_W), lambda i: (jnp.maximum(i - (nt - nk), 0), 0)),
        ],
        out_shape=[jax.ShapeDtypeStruct((n // d, d * 3 * GROUP_W), qkv_dtype) for d in dils] + [
            jax.ShapeDtypeStruct((n, CONV_CH), F32),
            jax.ShapeDtypeStruct((keep, 2 * ATTN_W), F32),
        ],
        scratch_shapes=[pltpu.VMEM((GROUP_W // LANES, tm, LANES), F32)],
        compiler_params=_params("arbitrary"),
        name="proj",
    )(x, w_in, b_in)


def _head_select(parts, rows):
    col = lax.broadcasted_iota(I32, (rows, GROUP_W), 1) // HEAD_DIM
    out = jnp.broadcast_to(parts[-1], (rows, GROUP_W))
    for h in range(HEADS_PER_GROUP - 2, -1, -1):
        out = jnp.where(col == h, parts[h], out)
    return out


def _head_rows(q, rows):
    col = lax.broadcasted_iota(I32, (rows, GROUP_W), 1) // HEAD_DIM
    return jnp.concatenate([jnp.where(col == h, q, jnp.zeros_like(q)) for h in range(HEADS_PER_GROUP)], axis=0)


def _softmax_pv(s, v, rows):
    m = jnp.max(s, axis=-1, keepdims=True)
    e = jnp.exp(s - m)
    l = jnp.sum(e, axis=-1, keepdims=True)
    pv = _dot(e.astype(BF16), v) / l
    lse = m + jnp.log(l)
    o = _head_select([pv[h * rows:(h + 1) * rows] for h in range(HEADS_PER_GROUP)], rows)
    lse_x = _head_select([lse[h * rows:(h + 1) * rows] for h in range(HEADS_PER_GROUP)], rows)
    return o, lse_x


ATTN_QB = 16


def _attn_kernel(q_ref, kp_ref, kc_ref, vp_ref, vc_ref, bias_ref, o_ref, lse_ref, *, qb):
    b = pl.program_id(1)
    for j in range(qb):
        rows = slice(j * BAND, (j + 1) * BAND)
        qm = _head_rows(q_ref[rows, :] * HEAD_DIM ** -0.5, BAND)
        if j == 0:
            k = jnp.concatenate([kp_ref[...], kc_ref[rows, :]], axis=0)
            v = jnp.concatenate([vp_ref[...], vc_ref[rows, :]], axis=0)
            bias = bias_ref[jnp.minimum(b, 1)]
        else:
            k = kc_ref[(j - 1) * BAND:(j + 1) * BAND, :]
            v = vc_ref[(j - 1) * BAND:(j + 1) * BAND, :]
            bias = bias_ref[1]
        s = _dot_nt(qm, k) + bias
        o, lse_x = _softmax_pv(s, v, BAND)
        o_ref[rows, :] = o.astype(o_ref.dtype)
        lse_ref[rows, :] = lse_x


def _prompt_bias(g, dil):
    slopes = _alibi_slopes()[g]
    qi = jnp.arange(BAND)[:, None]
    kj = jnp.arange(2 * BAND)[None, :]
    steps = qi + BAND - kj
    nk = DIL_GROUPS[g][0] // dil
    valid = (steps >= 0) & (steps <= nk)
    bias = -slopes[:, None, None] * (steps * dil).astype(F32)
    inner = jnp.where(valid[None], bias, MASKED)
    first = jnp.where((valid & (kj >= BAND))[None], bias, MASKED)
    return jnp.stack([first, inner]).reshape(2, HEADS_PER_GROUP * BAND, 2 * BAND)


def _attention_prompt(view, g):
    win, dil = DIL_GROUPS[g]
    n_cls = view.shape[0]
    qb = ATTN_QB
    while n_cls % (qb * BAND):
        qb //= 2
    rows = qb * BAND
    assert win // dil <= BAND and qb >= 1 and view.shape[1] == dil * 3 * GROUP_W
    cur = lambda part: pl.BlockSpec((rows, GROUP_W), lambda r, b: (b, r * 3 + part))
    prev = lambda part: pl.BlockSpec((BAND, GROUP_W), lambda r, b: (jnp.maximum(b * qb - 1, 0), r * 3 + part))
    bias = _prompt_bias(g, dil)
    return pl.pallas_call(
        functools.partial(_attn_kernel, qb=qb),
        grid=(dil, n_cls // rows),
        in_specs=[cur(0), prev(1), cur(1), prev(2), cur(2),
                  pl.BlockSpec((2, HEADS_PER_GROUP * BAND, 2 * BAND), lambda r, b: (0, 0, 0))],
        out_specs=[pl.BlockSpec((rows, GROUP_W), lambda r, b: (b, r)),
                   pl.BlockSpec((rows, GROUP_W), lambda r, b: (b, r))],
        out_shape=[jax.ShapeDtypeStruct((n_cls, dil * GROUP_W), BF16),
                   jax.ShapeDtypeStruct((n_cls, dil * GROUP_W), F32)],
        compiler_params=_params("arbitrary", "arbitrary"),
        name=f"attn_g{g}",
    )(view, view, view, view, view, bias)


def _attn_sample_kernel(q_ref, k_ref, v_ref, cache_ref, bias_ref, o_ref, lse_ref, *, n_buf, t, seqs):
    pad = jnp.zeros((BAND - t, GROUP_W), F32)
    scale = HEAD_DIM ** -0.5
    for b in range(seqs):
        rows = slice(b * t, (b + 1) * t)
        k_new = jnp.concatenate([k_ref[rows, :], pad], axis=0).astype(BF16)
        v_new = jnp.concatenate([v_ref[rows, :], pad], axis=0).astype(BF16)
        qm = _head_rows(q_ref[rows, :], t).astype(BF16)
        s_old = _dot(qm, cache_ref[b, 0:GROUP_W, :].astype(BF16)) * scale + bias_ref[:, 0:n_buf]
        s_new = _dot_nt(qm, k_new) * scale + bias_ref[:, n_buf:]
        m = jnp.maximum(jnp.max(s_old, axis=-1, keepdims=True), jnp.max(s_new, axis=-1, keepdims=True))
        e_old = jnp.exp(s_old - m)
        e_new = jnp.exp(s_new - m)
        l = jnp.sum(e_old, axis=-1, keepdims=True) + jnp.sum(e_new, axis=-1, keepdims=True)
        pv = _dot_nt(e_old.astype(BF16), cache_ref[b, GROUP_W:, :].astype(BF16)) + _dot(e_new.astype(BF16), v_new)
        pv = pv / l
        lse = m + jnp.log(l)
        o_ref[rows, :] = _head_select([pv[h * t:(h + 1) * t] for h in range(HEADS_PER_GROUP)], t)
        lse_ref[rows, :] = _head_select([lse[h * t:(h + 1) * t] for h in range(HEADS_PER_GROUP)], t)


def _sample_bias(g, n_buf, t):
    win, dil = DIL_GROUPS[g]
    slopes = _alibi_slopes()[g]
    tq = jnp.arange(t)[:, None]
    j = jnp.arange(n_buf + BAND)[None, :]
    dist = n_buf + tq - j
    valid = (dist >= 0) & (dist % dil == 0) & (dist <= win) & (j < n_buf + t)
    bias = -slopes[:, None, None] * dist.astype(F32)
    return jnp.where(valid[None], bias, MASKED).reshape(HEADS_PER_GROUP * t, n_buf + BAND)


def _attention_sample(qkv, cache_t, g, t):
    nb, n_buf = cache_t.shape[0], cache_t.shape[2]
    seqs = max(1, min(4, 1024 // n_buf))
    assert nb % seqs == 0
    col = lambda part: pl.BlockSpec((seqs * t, GROUP_W), lambda i: (i, part))
    kern = functools.partial(_attn_sample_kernel, n_buf=n_buf, t=t, seqs=seqs)
    bias = _sample_bias(g, n_buf, t)
    return pl.pallas_call(
        kern,
        grid=(nb // seqs,),
        in_specs=[col(0), col(1), col(2),
                  pl.BlockSpec((seqs, 2 * GROUP_W, n_buf), lambda i: (i, 0, 0)),
                  pl.BlockSpec((HEADS_PER_GROUP * t, n_buf + BAND), lambda i: (0, 0))],
        out_specs=[pl.BlockSpec((seqs * t, GROUP_W), lambda i: (i, 0)),
                   pl.BlockSpec((seqs * t, GROUP_W), lambda i: (i, 0))],
        out_shape=[jax.ShapeDtypeStruct((nb * t, GROUP_W), F32),
                   jax.ShapeDtypeStruct((nb * t, GROUP_W), F32)],
        compiler_params=_params("arbitrary"),
        name=f"attn_sample_g{g}",
    )(qkv, qkv, qkv, cache_t, bias)


CONV_HALO = 32


def _conv_tail(acc, b_ref, g_ref, beta_ref):
    return _silu(_layer_norm(acc + b_ref[...], g_ref[...], beta_ref[...]))


SUBLANES = 8


def _conv_prompt_kernel(halo_ref, u_ref, w_ref, b_ref, g_ref, beta_ref, o_ref, hist, part, *, tm):
    i = pl.program_id(0)
    hist[0:CONV_HALO, :] = jnp.where(i == 0, 0.0, halo_ref[...])
    hist[CONV_HALO:CONV_HALO + tm, :] = u_ref[...]
    hist[CONV_HALO + tm:, :] = jnp.zeros((SUBLANES, CONV_CH), F32)
    off = CONV_HALO - (CONV_WIDTH - 1)
    acc = None
    for s in range(SUBLANES):
        group = None
        for m in range(s, off + CONV_WIDTH, SUBLANES):
            j = m - off
            if j < 0:
                continue
            term = w_ref[j:j + 1, :] * hist[m - s:m - s + tm + SUBLANES, :]
            group = term if group is None else group + term
        part[...] = group
        shifted = part[s:s + tm, :]
        acc = shifted if acc is None else acc + shifted
    o_ref[...] = _conv_tail(acc, b_ref, g_ref, beta_ref).astype(o_ref.dtype)


def _conv_prompt(u, w_dw, b_dw, ln_g, ln_b, tm):
    n = u.shape[0]
    vec = pl.BlockSpec((1, CONV_CH), lambda i: (0, 0))
    return pl.pallas_call(
        functools.partial(_conv_prompt_kernel, tm=tm),
        grid=(n // tm,),
        in_specs=[pl.BlockSpec((CONV_HALO, CONV_CH), lambda i: (jnp.maximum(i * (tm // CONV_HALO) - 1, 0), 0)),
                  pl.BlockSpec((tm, CONV_CH), lambda i: (i, 0)),
                  pl.BlockSpec((CONV_WIDTH, CONV_CH), lambda i: (0, 0)), vec, vec, vec],
        out_specs=pl.BlockSpec((tm, CONV_CH), lambda i: (i, 0)),
        out_shape=jax.ShapeDtypeStruct((n, CONV_CH), BF16),
        scratch_shapes=[pltpu.VMEM((CONV_HALO + tm + SUBLANES, CONV_CH), F32),
                        pltpu.VMEM((tm + SUBLANES, CONV_CH), F32)],
        compiler_params=_params("arbitrary"),
        name="conv_prompt",
    )(u, u, w_dw, b_dw, ln_g, ln_b)


CONV_SAMPLE_SEQS = 8


def _conv_sample_kernel(hist_ref, w_ref, b_ref, g_ref, beta_ref, o_ref, *, t, seqs):
    for b in range(seqs):
        acc = jnp.zeros((t, CONV_CH), F32)
        for j in range(CONV_WIDTH):
            acc = acc + w_ref[j:j + 1, :] * hist_ref[b, j:j + t, :]
        o_ref[b * t:(b + 1) * t, :] = _conv_tail(acc, b_ref, g_ref, beta_ref)


def _conv_sample(u_hist, w_dw, b_dw, ln_g, ln_b, t):
    nb, rows = u_hist.shape[0], u_hist.shape[1]
    seqs = CONV_SAMPLE_SEQS if nb % CONV_SAMPLE_SEQS == 0 else 1
    vec = pl.BlockSpec((1, CONV_CH), lambda i: (0, 0))
    return pl.pallas_call(
        functools.partial(_conv_sample_kernel, t=t, seqs=seqs),
        grid=(nb // seqs,),
        in_specs=[pl.BlockSpec((seqs, rows, CONV_CH), lambda i: (i, 0, 0)),
                  pl.BlockSpec((CONV_WIDTH, CONV_CH), lambda i: (0, 0)), vec, vec, vec],
        out_specs=pl.BlockSpec((seqs * t, CONV_CH), lambda i: (i, 0)),
        out_shape=jax.ShapeDtypeStruct((nb * t, CONV_CH), F32),
        compiler_params=_params("arbitrary"),
        name="conv_sample",
    )(u_hist, w_dw, b_dw, ln_g, ln_b)


U32 = jnp.uint32
HALF_D = D_MODEL // 2
PACK_SLABS = HALF_D // LANES


def _token_order(ref, dil, scr, tm):
    if dil == 1:
        return ref[...].astype(F32)
    for r in range(dil):
        for half in range(GROUP_W // LANES):
            c = r * GROUP_W + half * LANES
            scr[half, pl.ds(r, tm // dil, stride=dil), :] = ref[:, c:c + LANES].astype(F32)
    return jnp.concatenate([scr[half] for half in range(GROUP_W // LANES)], axis=1)


def _pack_rows(ref, val):
    rows = val.shape[0]
    bits = lax.bitcast_convert_type(val.astype(BF16).astype(F32), U32)
    packed = lax.shift_right_logical(bits[:, :HALF_D], jnp.uint32(16)) | bits[:, HALF_D:]
    for c in range(PACK_SLABS):
        ref[pl.ds(c, rows, stride=PACK_SLABS), :] = packed[:, c * LANES:(c + 1) * LANES]


def _unpack_rows(ref):
    rows = ref.shape[0] // PACK_SLABS
    u = jnp.concatenate([ref[pl.ds(c, rows, stride=PACK_SLABS), :] for c in range(PACK_SLABS)], axis=1)
    lo = lax.bitcast_convert_type(lax.shift_left(u, jnp.uint32(16)), F32)
    hi = lax.bitcast_convert_type(u & jnp.uint32(0xFFFF0000), F32)
    return lo, hi


def _outproj_kernel(o0, o1, o2, l0, l1, l2, c_ref, x_ref, w_ref, b_ref, g_ref, beta_ref, h_ref, hp_ref, *scr,
                    dils, tm):
    os_ = [_token_order(r, d, scr[2 * g], tm) for g, (r, d) in enumerate(zip((o0, o1, o2), dils))]
    ls = [_token_order(r, d, scr[2 * g + 1], tm) for g, (r, d) in enumerate(zip((l0, l1, l2), dils))]
    m = jnp.maximum(jnp.maximum(ls[0], ls[1]), ls[2])
    es = [jnp.exp(l - m) for l in ls]
    inv = 1.0 / (es[0] + es[1] + es[2])
    mixed = b_ref[...]
    for g in range(N_GROUPS):
        a = (os_[g] * (es[g] * inv)).astype(BF16)
        mixed = mixed + _dot(a, w_ref[g * GROUP_W:(g + 1) * GROUP_W, :])
    mixed = mixed + _dot(c_ref[...].astype(BF16), w_ref[ATTN_W:, :])
    h = _layer_norm(DN_ALPHA * x_ref[...] + mixed, g_ref[...], beta_ref[...])
    h_ref[...] = h
    _pack_rows(hp_ref, h)


def _outproj(os_, lses, conv, x, w_o, b_o, ln_g, ln_b, tm, dils):
    n = x.shape[0]
    grp = [pl.BlockSpec((tm // d, d * GROUP_W), lambda i: (i, 0)) for d in dils]
    vec = pl.BlockSpec((1, D_MODEL), lambda i: (0, 0))
    return pl.pallas_call(
        functools.partial(_outproj_kernel, dils=dils, tm=tm),
        grid=(n // tm,),
        in_specs=grp + grp + [pl.BlockSpec((tm, CONV_CH), lambda i: (i, 0)),
                              pl.BlockSpec((tm, D_MODEL), lambda i: (i, 0)),
                              pl.BlockSpec((D_MODEL, D_MODEL), lambda i: (0, 0)), vec, vec, vec],
        out_specs=[pl.BlockSpec((tm, D_MODEL), lambda i: (i, 0)),
                   pl.BlockSpec((tm * PACK_SLABS, LANES), lambda i: (i, 0))],
        out_shape=[jax.ShapeDtypeStruct((n, D_MODEL), F32),
                   jax.ShapeDtypeStruct((n * PACK_SLABS, LANES), U32)],
        scratch_shapes=[pltpu.VMEM((GROUP_W // LANES, tm, LANES), F32) for _ in range(2 * N_GROUPS)],
        compiler_params=_params("arbitrary"),
        name="outproj",
    )(*os_, *lses, conv, x, w_o, b_o, ln_g, ln_b)


ROUTER_TM = 256


def _first_index(hit, idx, limit, axis):
    return jnp.min(jnp.where(hit, idx, limit), axis=axis, keepdims=True)


def _router_kernel(ha_ref, hb_ref, w_ref, rb_ref, tri_ref, idx_ref, wt_ref, pos_ref, cnt_ref, run, *, tiles_a):
    i = pl.program_id(0)
    tm = ROUTER_TM

    @pl.when(i == 0)
    def _():
        run[...] = jnp.zeros_like(run)

    h = jnp.where(i < tiles_a, ha_ref[...], hb_ref[...])
    logits = _dot_nt(w_ref[...], h.astype(BF16))
    scores = jax.nn.sigmoid(logits)
    biased = scores + rb_ref[...]
    groups = [biased[g * EXPERTS_PER_GROUP:(g + 1) * EXPERTS_PER_GROUP] for g in range(N_EXPERT_GROUPS)]
    ei = lax.broadcasted_iota(I32, (EXPERTS_PER_GROUP, tm), 0).astype(F32)
    gs = []
    for bg in groups:
        m1 = jnp.max(bg, axis=0, keepdims=True)
        f1 = _first_index(bg == m1, ei, float(EXPERTS_PER_GROUP), 0)
        m2 = jnp.max(jnp.where(ei == f1, -jnp.inf, bg), axis=0, keepdims=True)
        gs.append(m1 + m2)
    gs = jnp.concatenate(gs, axis=0)
    gi = lax.broadcasted_iota(I32, gs.shape, 0).astype(F32)
    keep = jnp.zeros(gs.shape, F32)
    cur = gs
    for _ in range(TOPK_GROUPS):
        m = jnp.max(cur, axis=0, keepdims=True)
        f = _first_index(cur == m, gi, float(N_EXPERT_GROUPS), 0)
        hit = gi == f
        keep = jnp.where(hit, 1.0, keep)
        cur = jnp.where(hit, -jnp.inf, cur)
    masked = jnp.concatenate([jnp.where(keep[g:g + 1] > 0.0, bg, -jnp.inf) for g, bg in enumerate(groups)], axis=0)
    xi = lax.broadcasted_iota(I32, (N_EXPERTS, tm), 0).astype(F32)
    cur = masked
    picks = []
    for _ in range(TOP_K):
        m = jnp.max(cur, axis=0, keepdims=True)
        f = _first_index(cur == m, xi, float(N_EXPERTS), 0)
        hit = xi == f
        picks.append((f, hit))
        cur = jnp.where(hit, -jnp.inf, cur)
    sel = jnp.where(cur != masked, 1.0, 0.0)
    before = _dot(sel.astype(BF16), tri_ref[...]) + run[...]
    run[...] = run[...] + jnp.sum(sel, axis=1, keepdims=True)
    ws = [jnp.sum(jnp.where(hit, scores, 0.0), axis=0, keepdims=True) for _, hit in picks]
    wsum = ws[0]
    for w in ws[1:]:
        wsum = wsum + w
    for k, (f, hit) in enumerate(picks):
        idx_ref[k:k + 1, :] = f.astype(I32)
        wt_ref[k:k + 1, :] = ws[k] / wsum * ROUTED_SCALE
        pos_ref[k:k + 1, :] = jnp.sum(jnp.where(hit, before, 0.0), axis=0, keepdims=True).astype(I32)
    cnt_ref[...] = jnp.broadcast_to(run[...], cnt_ref.shape).astype(I32)


def _route(ha, hb, w_router_t, router_bias):
    tm = ROUTER_TM
    tiles_a, tiles_b = ha.shape[0] // tm, hb.shape[0] // tm
    spec_a = pl.BlockSpec((tm, D_MODEL), lambda i: (jnp.minimum(i, tiles_a - 1), 0))
    spec_b = pl.BlockSpec((tm, D_MODEL), lambda i: (jnp.maximum(i - tiles_a, 0), 0))
    t = (tiles_a + tiles_b) * tm
    tri = (jnp.arange(tm)[:, None] < jnp.arange(tm)[None, :]).astype(BF16)
    slot = pl.BlockSpec((TOP_K, tm), lambda i: (0, i))
    idx, wts, pos, cnt = pl.pallas_call(
        functools.partial(_router_kernel, tiles_a=tiles_a),
        grid=(t // tm,),
        in_specs=[spec_a, spec_b,
                  pl.BlockSpec((N_EXPERTS, D_MODEL), lambda i: (0, 0)),
                  pl.BlockSpec((N_EXPERTS, 1), lambda i: (0, 0)),
                  pl.BlockSpec((tm, tm), lambda i: (0, 0))],
        out_specs=[slot, slot, slot, pl.BlockSpec((N_EXPERTS, 128), lambda i: (0, 0))],
        out_shape=[jax.ShapeDtypeStruct((TOP_K, t), I32), jax.ShapeDtypeStruct((TOP_K, t), F32),
                   jax.ShapeDtypeStruct((TOP_K, t), I32), jax.ShapeDtypeStruct((N_EXPERTS, 128), I32)],
        scratch_shapes=[pltpu.VMEM((N_EXPERTS, 1), F32)],
        compiler_params=_params("arbitrary"),
        name="router",
    )(ha, hb, w_router_t, router_bias.reshape(N_EXPERTS, 1), tri)
    return idx, wts, pos, cnt[:, 0]


DEST_TM = 1280


def _dest_kernel(idx_ref, pos_ref, starts_ref, dest_ref):
    tm = idx_ref.shape[1]
    ei = lax.broadcasted_iota(I32, (N_EXPERTS, tm), 0)
    starts = starts_ref[...]
    for k in range(TOP_K):
        first = jnp.sum(jnp.where(ei == idx_ref[k:k + 1, :], starts, 0.0), axis=0, keepdims=True)
        dest_ref[k:k + 1, :] = first.astype(I32) + pos_ref[k:k + 1, :]


def _dest_rows(idx, pos, starts):
    t = idx.shape[1]
    tm = DEST_TM if t % DEST_TM == 0 else ROUTER_TM
    assert t % tm == 0
    slot = pl.BlockSpec((TOP_K, tm), lambda i: (0, i))
    return pl.pallas_call(
        _dest_kernel,
        grid=(t // tm,),
        in_specs=[slot, slot, pl.BlockSpec((N_EXPERTS, 1), lambda i: (0, 0))],
        out_specs=slot,
        out_shape=jax.ShapeDtypeStruct((TOP_K, t), I32),
        compiler_params=_params("arbitrary"),
        name="dest_rows",
    )(idx, pos, starts.astype(F32).reshape(N_EXPERTS, 1))


SC_CORES = 2
SC_SUBCORES = 16
SC_WORKERS = SC_CORES * SC_SUBCORES
SC_WINDOW = 64
SC_WINDOW_SMALL = 32


def _sc_worker():
    return lax.axis_index("s") * SC_CORES + lax.axis_index("c")


def _sc_mesh():
    return plsc.VectorSubcoreMesh(core_axis_name="c", subcore_axis_name="s")


def _sc_scatter_rows(ha, hb, dest_w, zeros, n_out):
    w = dest_w.shape[2]
    na, nb = ha.shape[0], hb.shape[0]
    spare = zeros.shape[0]
    wa, wb = na // w, nb // w
    assert na % (SC_WORKERS * 2 * w) == 0 and nb % w == 0 and wb <= SC_WORKERS
    per_w = wa // SC_WORKERS
    rows_t = pltpu.VMEM((w,) + ha.shape[1:], ha.dtype)
    idx_t = pltpu.VMEM((TOP_K, w), I32)

    @functools.partial(
        pl.kernel, mesh=_sc_mesh(),
        out_type=jax.ShapeDtypeStruct((n_out + spare,) + ha.shape[1:], ha.dtype),
        scratch_types=[rows_t, rows_t, idx_t, idx_t, pltpu.SemaphoreType.DMA, pltpu.SemaphoreType.DMA],
    )
    def scatter(ha_hbm, hb_hbm, dest_hbm, zeros_hbm, out_hbm, rows0, rows1, idx0, idx1, sem0, sem1):
        wid = _sc_worker()
        bufs = ((rows0, idx0, sem0), (rows1, idx1, sem1))

        @pl.when(wid == SC_WORKERS - 1)
        def _():
            pltpu.sync_copy(zeros_hbm, out_hbm.at[pl.ds(n_out, spare)])

        def load(src_hbm, row0, win, b):
            pltpu.sync_copy(src_hbm.at[pl.ds(pl.multiple_of(row0, 8), w)], bufs[b][0])
            pltpu.sync_copy(dest_hbm.at[win], bufs[b][1])

        def copies(b):
            return [pltpu.make_async_copy(bufs[b][0], out_hbm.at[bufs[b][1].at[k]], bufs[b][2])
                    for k in range(TOP_K)]

        win0 = wid * per_w
        load(ha_hbm, win0 * w, win0, 0)

        @pl.loop(0, per_w, step=2)
        def _(i0):
            for b in range(2):
                i = i0 + b
                for c in copies(b):
                    c.start()

                @pl.when(i + 1 < per_w)
                def _():
                    load(ha_hbm, (win0 + i + 1) * w, win0 + i + 1, 1 - b)

                for c in copies(b):
                    c.wait()

        @pl.when(wid < wb)
        def _():
            load(hb_hbm, wid * w, wa + wid, 0)
            for c in copies(0):
                c.start()
            for c in copies(0):
                c.wait()

    return scatter(ha, hb, dest_w, zeros)


def _sc_gather_rows(table, idx):
    n = idx.shape[0]
    w = SC_WINDOW if n % (SC_WORKERS * 2 * SC_WINDOW) == 0 else SC_WINDOW_SMALL
    assert n % (SC_WORKERS * 2 * w) == 0
    per_w = n // SC_WORKERS
    nwin = per_w // w
    rows_t = pltpu.VMEM((w,) + table.shape[1:], table.dtype)

    @functools.partial(
        pl.kernel, mesh=_sc_mesh(),
        out_type=jax.ShapeDtypeStruct((n,) + table.shape[1:], table.dtype),
        scratch_types=[pltpu.VMEM((per_w,), I32), rows_t, rows_t, pltpu.SemaphoreType.DMA, pltpu.SemaphoreType.DMA],
    )
    def gather(table_hbm, idx_hbm, out_hbm, idx_v, rows0, rows1, sem0, sem1):
        base = pl.multiple_of(_sc_worker() * per_w, 8)
        pltpu.sync_copy(idx_hbm.at[pl.ds(base, per_w)], idx_v)
        bufs = ((rows0, sem0), (rows1, sem1))
        fetch = lambda i, b: pltpu.make_async_copy(
            table_hbm.at[idx_v.at[pl.ds(pl.multiple_of(i * w, 8), w)]], bufs[b][0], bufs[b][1])
        fetch(0, 0).start()

        @pl.loop(0, nwin, step=2)
        def _(i0):
            for b in range(2):
                i = i0 + b
                fetch(i, b).wait()

                @pl.when(i + 1 < nwin)
                def _():
                    fetch(i + 1, 1 - b).start()

                pltpu.sync_copy(bufs[b][0], out_hbm.at[pl.ds(pl.multiple_of(base + i * w, 8), w)])

    return gather(table, idx)


EXPERT_CH = 256
X_SLOTS = 8
W_SLOTS = 5


def _expert_kernel(row0_ref, exp_ref, live_ref, fresh_ref, lead_ref, ahead_ref, wslot_ref, ordered_ref, xs_ref,
                   wg_ref, wu_ref, wd_ref, ys_ref, xbuf, ybuf, sg, su, sd, wgb, wub, wdb, xsem, ysem, wsem):
    v = pl.program_id(0)
    nv = pl.num_programs(0)
    slot = lax.rem(v, X_SLOTS)
    yslot = lax.rem(v, 3)
    rows = lambda u: pl.ds(pl.multiple_of(row0_ref[u] * PACK_SLABS, PACK_SLABS), EXPERT_CH * PACK_SLABS)
    x_copy = lambda u: pltpu.make_async_copy(xs_ref.at[rows(u)], xbuf.at[lax.rem(u, X_SLOTS)],
                                             xsem.at[lax.rem(u, X_SLOTS)])
    y_copy = lambda u: pltpu.make_async_copy(ybuf.at[lax.rem(u, 3)], ys_ref.at[rows(u)], ysem.at[lax.rem(u, 3)])
    prev1, prev2 = jnp.maximum(v - 1, 0), jnp.maximum(v - 2, 0)
    w_copies = lambda e, s: [pltpu.make_async_copy(w.at[e], stage.at[s], wsem.at[s])
                             for w, stage in ((wg_ref, sg), (wu_ref, su), (wd_ref, sd))]
    spare = pl.ds(ys_ref.shape[0] - EXPERT_CH * PACK_SLABS, EXPERT_CH * PACK_SLABS)

    @pl.when(v == 0)
    def _():
        for c in w_copies(exp_ref[0], 0):
            c.start(priority=1)

        for k in range(1, W_SLOTS - 1):
            @pl.when(lead_ref[k - 1] >= 0)
            def _(k=k):
                for c in w_copies(lead_ref[k - 1], k):
                    c.start(priority=1)

        x_copy(0).start()
        for u in range(1, X_SLOTS - 1):
            @pl.when((u < nv) & (live_ref[jnp.minimum(u, nv - 1)] == 1))
            def _():
                x_copy(u).start()
        ybuf[2] = jnp.zeros(ybuf.shape[1:], U32)
        zero = pltpu.make_async_copy(ybuf.at[2], ys_ref.at[spare], ysem.at[2])
        zero.start()
        zero.wait()

    ahead = jnp.minimum(v + X_SLOTS - 1, nv - 1)

    @pl.when((v + X_SLOTS - 1 < nv) & (live_ref[ahead] == 1))
    def _():
        x_copy(ahead).start()

    @pl.when(fresh_ref[v] == 1)
    def _():
        p = wslot_ref[v]
        for c in w_copies(exp_ref[v], p):
            c.wait()

        @pl.when(ahead_ref[v] >= 0)
        def _():
            for c in w_copies(ahead_ref[v], lax.rem(p + W_SLOTS - 1, W_SLOTS)):
                c.start(priority=1)

        wgb[...] = sg[p].astype(BF16)
        wub[...] = su[p].astype(BF16)
        wdb[...] = sd[p].astype(BF16)

    @pl.when(live_ref[v] == 1)
    def _():
        x_copy(v).wait()
        x = jnp.concatenate(_unpack_rows(xbuf.at[slot]), axis=1).astype(BF16)
        hid = (_silu(_dot(x, wgb[...])) * _dot(x, wub[...])).astype(BF16)
        _pack_rows(ybuf.at[yslot], _dot(hid, wdb[...]))

    @pl.when((v >= 2) & (live_ref[prev2] == 1) & (ordered_ref[prev1] == 0))
    def _():
        y_copy(prev2).wait()

    @pl.when((v >= 1) & (live_ref[prev1] == 1) & (ordered_ref[v] == 1))
    def _():
        y_copy(prev1).wait()

    @pl.when(live_ref[v] == 1)
    def _():
        y_copy(v).start()

    @pl.when(v == nv - 1)
    def _():
        @pl.when((v >= 1) & (live_ref[prev1] == 1) & (ordered_ref[v] == 0))
        def _():
            y_copy(prev1).wait()

        @pl.when(live_ref[v] == 1)
        def _():
            y_copy(v).wait()


def _chunks(counts, n_rows):
    ch = EXPERT_CH
    nv = n_rows // ch + N_EXPERTS
    ends = jnp.cumsum(counts)
    starts = ends - counts
    nch = (counts + ch - 1) // ch
    cend = jnp.cumsum(nch)
    cstart = cend - nch
    v = jnp.arange(nv, dtype=I32)
    live = v < cend[-1]
    vc = jnp.minimum(v, cend[-1] - 1)
    ids = jnp.arange(N_EXPERTS, dtype=I32)
    e = jnp.minimum(jnp.sum(cend[None, :] <= vc[:, None], axis=1), N_EXPERTS - 1).astype(I32)
    of_e = lambda a: jnp.sum(jnp.where(e[:, None] == ids[None, :], a[None, :], 0), axis=1)
    starts_e, ends_e, count_e = of_e(starts), of_e(ends), of_e(counts)
    c = vc - of_e(cstart)
    is_last = c == of_e(nch) - 1
    row0 = jnp.where(is_last & (count_e >= ch), ends_e - ch, starts_e + c * ch).astype(I32)
    overrun = live & is_last & (count_e < ch)
    ordered = jnp.concatenate([jnp.zeros((1,), bool), overrun[:-1]]) & live
    fresh = jnp.concatenate([jnp.ones((1,), I32), (e[1:] != e[:-1]).astype(I32)])
    order = jnp.cumsum(counts > 0) - 1
    order_e = of_e(order)

    def after(k):
        hit = (counts > 0)[None, :] & (order[None, :] == order_e[:, None] + k)
        return (jnp.sum(jnp.where(hit, ids[None, :] + 1, 0), axis=1) - 1).astype(I32)

    lead = jnp.stack([after(k)[0] for k in range(1, W_SLOTS - 1)])
    return (starts.astype(I32), row0, e, live.astype(I32), jnp.where(live, fresh, 0).astype(I32),
            lead, after(W_SLOTS - 1), (order_e % W_SLOTS).astype(I32), ordered.astype(I32))


def _experts(chunks, xs, w_gate, w_up, w_down):
    ch = EXPERT_CH
    buf = lambda n: pltpu.VMEM((n, ch * PACK_SLABS, LANES), U32)
    anywhere = pl.BlockSpec(memory_space=pl.ANY)
    return pl.pallas_call(
        _expert_kernel,
        grid_spec=pltpu.PrefetchScalarGridSpec(
            num_scalar_prefetch=len(chunks),
            grid=(chunks[0].shape[0],),
            in_specs=[anywhere] * 4,
            out_specs=anywhere,
            scratch_shapes=[buf(X_SLOTS), buf(3),
                            pltpu.VMEM((W_SLOTS, D_MODEL, EXPERT_FF), F32),
                            pltpu.VMEM((W_SLOTS, D_MODEL, EXPERT_FF), F32),
                            pltpu.VMEM((W_SLOTS, EXPERT_FF, D_MODEL), F32),
                            pltpu.VMEM((D_MODEL, EXPERT_FF), BF16), pltpu.VMEM((D_MODEL, EXPERT_FF), BF16),
                            pltpu.VMEM((EXPERT_FF, D_MODEL), BF16),
                            pltpu.SemaphoreType.DMA((X_SLOTS,)), pltpu.SemaphoreType.DMA((3,)),
                            pltpu.SemaphoreType.DMA((W_SLOTS,))],
        ),
        out_shape=jax.ShapeDtypeStruct(xs.shape, U32),
        compiler_params=_params("arbitrary"),
        name="experts",
    )(*chunks, xs, w_gate, w_up, w_down)


COMBINE_TM = 512


def _combine_kernel(wt_ref, h_ref, sg_ref, su_ref, sd_ref, g_ref, beta_ref, *refs):
    y_refs, o_ref = refs[:TOP_K], refs[TOP_K]
    h = h_ref[...]
    hb = h.astype(BF16)
    hid = (_silu(_dot(hb, sg_ref[...])) * _dot(hb, su_ref[...])).astype(BF16)
    acc = DN_ALPHA * h + _dot(hid, sd_ref[...])
    lo = acc[:, :HALF_D]
    hi = acc[:, HALF_D:]
    for k in range(TOP_K):
        y_lo, y_hi = _unpack_rows(y_refs[k])
        w = wt_ref[:, k:k + 1]
        lo = lo + w * y_lo
        hi = hi + w * y_hi
    o_ref[...] = _layer_norm(jnp.concatenate([lo, hi], axis=1), g_ref[...], beta_ref[...])


def _combine(wts_tk, ws_gate, ws_up, ws_down, ln_g, ln_b, h, h_row0, t0, yg):
    n = yg.shape[0] // (TOP_K * PACK_SLABS)
    tm = COMBINE_TM if n % COMBINE_TM == 0 else ROUTER_TM
    assert n % tm == 0 and h_row0 % tm == 0 and t0 % tm == 0
    b0 = t0 // tm
    per_slot = n // tm
    rows = pl.BlockSpec((tm, D_MODEL), lambda i: (h_row0 // tm + i, 0))
    slot_rows = [pl.BlockSpec((tm * PACK_SLABS, LANES), lambda i, k=k: (k * per_slot + i, 0))
                 for k in range(TOP_K)]
    vec = pl.BlockSpec((1, D_MODEL), lambda i: (0, 0))
    return pl.pallas_call(
        _combine_kernel,
        grid=(n // tm,),
        in_specs=[pl.BlockSpec((tm, TOP_K), lambda i: (b0 + i, 0)), rows,
                  pl.BlockSpec((D_MODEL, EXPERT_FF), lambda i: (0, 0)),
                  pl.BlockSpec((D_MODEL, EXPERT_FF), lambda i: (0, 0)),
                  pl.BlockSpec((EXPERT_FF, D_MODEL), lambda i: (0, 0)),
                  vec, vec] + slot_rows,
        out_specs=pl.BlockSpec((tm, D_MODEL), lambda i: (i, 0)),
        out_shape=jax.ShapeDtypeStruct((n, D_MODEL), F32),
        compiler_params=_params("arbitrary"),
        name="combine",
    )(wts_tk, h, ws_gate, ws_up, ws_down, ln_g, ln_b, *([yg] * TOP_K))


def _kv_cache(kv_tail, g, keep):
    k = kv_tail[-keep:, g * GROUP_W:(g + 1) * GROUP_W]
    v = kv_tail[-keep:, ATTN_W + g * GROUP_W:ATTN_W + (g + 1) * GROUP_W]
    return jnp.stack([k, v], axis=1).reshape(keep, 2, HEADS_PER_GROUP, HEAD_DIM)


def kernel(x_prompt, x_sample, cache_kv_w128, cache_kv_w512, cache_kv_w2048, state_conv, w_in, b_in, w_dw, b_dw,
           conv_ln_g, conv_ln_b, w_o, b_o, ln1_g, ln1_b, w_router, router_bias, w_gate, w_up, w_down, ws_gate,
           ws_up, ws_down, ln2_g, ln2_b):
    assert w_in.shape[0] == DEPTH == 1
    batch, seq, _ = x_prompt.shape
    dec_batch, dec_seq, _ = x_sample.shape
    assert batch == 1
    n_s = dec_batch * dec_seq
    caches = (cache_kv_w128, cache_kv_w512, cache_kv_w2048)
    row = lambda a: a[0].reshape(1, -1)

    w_in_b = w_in[0].astype(BF16)
    b_in_r = row(b_in)
    w_o_b = w_o[0].astype(BF16)
    conv_w = (w_dw[0], row(b_dw), row(conv_ln_g), row(conv_ln_b))
    ln1 = (row(ln1_g), row(ln1_b))

    xp = x_prompt[0]
    keep_p = min(max(w for w, _ in DIL_GROUPS), seq)
    dils = tuple(d for _, d in DIL_GROUPS)
    *qkv_p, u_p, kv_p = _project(xp, w_in_b, b_in_r, 512, keep_p, BF16, dils)
    attn_p = [_attention_prompt(qkv_p[g], g) for g in range(N_GROUPS)]
    conv_p = _conv_prompt(u_p, *conv_w, 512)
    h_p, hpk_p = _outproj([a[0] for a in attn_p], [a[1] for a in attn_p], conv_p, xp, w_o_b, row(b_o), *ln1, 512,
                          dils)

    xs = x_sample.reshape(n_s, D_MODEL)
    ones = (1,) * N_GROUPS
    *qkv_s, u_s, kv_s = _project(xs, w_in_b, b_in_r, n_s, n_s, F32, ones)
    caches_t = [jnp.transpose(c[0].reshape(dec_batch, -1, 2 * GROUP_W), (0, 2, 1)) for c in caches]
    attn_s = [_attention_sample(qkv_s[g], caches_t[g], g, dec_seq) for g in range(N_GROUPS)]
    u_hist = jnp.concatenate([state_conv[0], u_s.reshape(dec_batch, dec_seq, CONV_CH)], axis=1)
    conv_s = _conv_sample(u_hist, *conv_w, dec_seq)
    h_s, hpk_s = _outproj([a[0] for a in attn_s], [a[1] for a in attn_s], conv_s, xs, w_o_b, row(b_o), *ln1, n_s,
                          ones)

    idx, wts, pos, counts = _route(h_p, h_s, w_router[0].T.astype(BF16), router_bias[0])
    n_tok = seq + n_s
    starts, *chunks = _chunks(counts, n_tok * TOP_K)
    dest = _dest_rows(idx, pos, starts)
    dest_w = dest.reshape(TOP_K, n_tok // SC_WINDOW, SC_WINDOW).transpose(1, 0, 2)
    tiles = lambda a: a.reshape(-1, PACK_SLABS, LANES)
    flat = lambda a: a.reshape(-1, LANES)
    x_sorted = _sc_scatter_rows(tiles(hpk_p), tiles(hpk_s), dest_w, jnp.zeros((EXPERT_CH, PACK_SLABS, LANES), U32),
                                n_tok * TOP_K)
    y_sorted = _experts(chunks, flat(x_sorted), w_gate[0], w_up[0], w_down[0])
    shared = (ws_gate[0].astype(BF16), ws_up[0].astype(BF16), ws_down[0].astype(BF16))
    comb = functools.partial(_combine, wts.T, *shared, row(ln2_g), row(ln2_b))
    y_table = tiles(y_sorted)

    def finish(h, t0):
        y_slots = _sc_gather_rows(y_table, dest[:, t0:t0 + h.shape[0]].reshape(-1))
        return comb(h, 0, t0, flat(y_slots))

    y_s = finish(h_s, seq)
    y_p = finish(h_p, 0)

    kv_prompt = [_kv_cache(kv_p, g, min(w, seq))[None, None] for g, (w, _) in enumerate(DIL_GROUPS)]
    assert seq >= CONV_WIDTH - 1
    conv_prompt = u_p[-(CONV_WIDTH - 1):]
    kv_s4 = kv_s.reshape(dec_batch, dec_seq, 2, N_GROUPS, HEADS_PER_GROUP, HEAD_DIM)
    kv_sample = [kv_s4[:, :, :, g][None] for g in range(N_GROUPS)]
    conv_sample = u_hist[:, -(CONV_WIDTH - 1):]
    return (y_p[None], y_s.reshape(dec_batch, dec_seq, D_MODEL), *kv_prompt, conv_prompt[None, None],
            *kv_sample, conv_sample[None])
```

```python
import functools

import jax
import jax.numpy as jnp
from jax import lax
from jax.experimental import pallas as pl
from jax.experimental.pallas import tpu as pltpu
from jax.experimental.pallas import tpu_sc as plsc

F32 = jnp.float32
BF16 = jnp.bfloat16
I32 = jnp.int32

D_MODEL = 1024
HEAD_DIM = 64
HEADS_PER_GROUP = 4
GROUP_W = HEADS_PER_GROUP * HEAD_DIM
DIL_GROUPS = ((128, 1), (512, 4), (2048, 16))
N_GROUPS = len(DIL_GROUPS)
ATTN_W = N_GROUPS * GROUP_W
CONV_CH = D_MODEL - ATTN_W
CONV_WIDTH = 31
IN_W = 3 * ATTN_W + 2 * CONV_CH
BAND = 128
N_EXPERTS = 256
TOP_K = 8
N_EXPERT_GROUPS = 8
EXPERTS_PER_GROUP = N_EXPERTS // N_EXPERT_GROUPS
TOPK_GROUPS = 4
EXPERT_FF = 256
ROUTED_SCALE = 2.5
DEPTH = 1
DN_ALPHA = (2 * DEPTH) ** 0.25
LN_EPS = 1e-5
MASKED = -1e30

VMEM_LIMIT_BYTES = 56 * 1024 * 1024


def _params(*sem):
    return pltpu.CompilerParams(dimension_semantics=sem, vmem_limit_bytes=VMEM_LIMIT_BYTES)


def _dot(a, b):
    return jnp.dot(a, b, preferred_element_type=F32)


def _dot_nt(a, b):
    return lax.dot_general(a, b, (((1,), (1,)), ((), ())), preferred_element_type=F32)


def _layer_norm(x, g, b):
    mu = jnp.mean(x, axis=-1, keepdims=True)
    xc = x - mu
    var = jnp.mean(xc * xc, axis=-1, keepdims=True)
    return xc * lax.rsqrt(var + LN_EPS) * g + b


def _silu(x):
    return x * jax.nn.sigmoid(x)


def _alibi_slopes():
    n = N_GROUPS * HEADS_PER_GROUP
    h = jnp.arange(1, n + 1, dtype=F32)
    return (2.0 ** (-8.0 * h / n)).reshape(N_GROUPS, HEADS_PER_GROUP)


LANES = 128


def _proj_kernel(x_ref, w_ref, b_ref, *refs, dils, tm):
    qkv_refs, (u_ref, kv_ref, zs) = refs[:N_GROUPS], refs[N_GROUPS:]
    x = x_ref[...].astype(BF16)
    for part in range(3):
        for g in range(N_GROUPS):
            c0 = part * ATTN_W + g * GROUP_W
            z = _dot(x, w_ref[:, c0:c0 + GROUP_W]) + b_ref[:, c0:c0 + GROUP_W]
            if part > 0:
                kv_ref[:, c0 - ATTN_W:c0 - ATTN_W + GROUP_W] = z
            out, dil = qkv_refs[g], dils[g]
            if dil == 1:
                out[:, part * GROUP_W:(part + 1) * GROUP_W] = z.astype(out.dtype)
                continue
            for half in range(GROUP_W // LANES):
                zs[half] = z[:, half * LANES:(half + 1) * LANES]
            for r in range(dil):
                for half in range(GROUP_W // LANES):
                    c = r * 3 * GROUP_W + part * GROUP_W + half * LANES
                    out[:, c:c + LANES] = zs[half, pl.ds(r, tm // dil, stride=dil), :].astype(out.dtype)
    c0 = 3 * ATTN_W
    a = _dot(x, w_ref[:, c0:c0 + CONV_CH]) + b_ref[:, c0:c0 + CONV_CH]
    gate = _dot(x, w_ref[:, c0 + CONV_CH:]) + b_ref[:, c0 + CONV_CH:]
    u_ref[...] = a * jax.nn.sigmoid(gate)


def _project(x, w_in, b_in, tm, keep, qkv_dtype, dils):
    n = x.shape[0]
    nt = n // tm
    nk = keep // tm
    return pl.pallas_call(
        functools.partial(_proj_kernel, dils=dils, tm=tm),
        grid=(nt,),
        in_specs=[
            pl.BlockSpec((tm, D_MODEL), lambda i: (i, 0)),
            pl.BlockSpec((D_MODEL, IN_W), lambda i: (0, 0)),
            pl.BlockSpec((1, IN_W), lambda i: (0, 0)),
        ],
        out_specs=[pl.BlockSpec((tm // d, d * 3 * GROUP_W), lambda i: (i, 0)) for d in dils] + [
            pl.BlockSpec((tm, CONV_CH), lambda i: (i, 0)),
            pl.BlockSpec((tm, 2 * ATTN_W), lambda i: (jnp.maximum(i - (nt - nk), 0), 0)),
        ],
        out_shape=[jax.ShapeDtypeStruct((n // d, d * 3 * GROUP_W), qkv_dtype) for d in dils] + [
            jax.ShapeDtypeStruct((n, CONV_CH), F32),
            jax.ShapeDtypeStruct((keep, 2 * ATTN_W), F32),
        ],
        scratch_shapes=[pltpu.VMEM((GROUP_W // LANES, tm, LANES), F32)],
        compiler_params=_params("arbitrary"),
        name="proj",
    )(x, w_in, b_in)


def _head_select(parts, rows):
    col = lax.broadcasted_iota(I32, (rows, GROUP_W), 1) // HEAD_DIM
    out = jnp.broadcast_to(parts[-1], (rows, GROUP_W))
    for h in range(HEADS_PER_GROUP - 2, -1, -1):
        out = jnp.where(col == h, parts[h], out)
    return out


def _head_rows(q, rows):
    col = lax.broadcasted_iota(I32, (rows, GROUP_W), 1) // HEAD_DIM
    return jnp.concatenate([jnp.where(col == h, q, jnp.zeros_like(q)) for h in range(HEADS_PER_GROUP)], axis=0)


def _softmax_pv(s, v, rows):
    m = jnp.max(s, axis=-1, keepdims=True)
    e = jnp.exp(s - m)
    l = jnp.sum(e, axis=-1, keepdims=True)
    pv = _dot(e.astype(BF16), v) / l
    lse = m + jnp.log(l)
    o = _head_select([pv[h * rows:(h + 1) * rows] for h in range(HEADS_PER_GROUP)], rows)
    lse_x = _head_select([lse[h * rows:(h + 1) * rows] for h in range(HEADS_PER_GROUP)], rows)
    return o, lse_x


ATTN_QB = 16


def _attn_kernel(q_ref, kp_ref, kc_ref, vp_ref, vc_ref, bias_ref, o_ref, lse_ref, *, qb):
    b = pl.program_id(1)
    for j in range(qb):
        rows = slice(j * BAND, (j + 1) * BAND)
        qm = _head_rows(q_ref[rows, :] * HEAD_DIM ** -0.5, BAND)
        if j == 0:
            k = jnp.concatenate([kp_ref[...], kc_ref[rows, :]], axis=0)
            v = jnp.concatenate([vp_ref[...], vc_ref[rows, :]], axis=0)
            bias = bias_ref[jnp.minimum(b, 1)]
        else:
            k = kc_ref[(j - 1) * BAND:(j + 1) * BAND, :]
            v = vc_ref[(j - 1) * BAND:(j + 1) * BAND, :]
            bias = bias_ref[1]
        s = _dot_nt(qm, k) + bias
        o, lse_x = _softmax_pv(s, v, BAND)
        o_ref[rows, :] = o.astype(o_ref.dtype)
        lse_ref[rows, :] = lse_x


def _prompt_bias(g, dil):
    slopes = _alibi_slopes()[g]
    qi = jnp.arange(BAND)[:, None]
    kj = jnp.arange(2 * BAND)[None, :]
    steps = qi + BAND - kj
    nk = DIL_GROUPS[g][0] // dil
    valid = (steps >= 0) & (steps <= nk)
    bias = -slopes[:, None, None] * (steps * dil).astype(F32)
    inner = jnp.where(valid[None], bias, MASKED)
    first = jnp.where((valid & (kj >= BAND))[None], bias, MASKED)
    return jnp.stack([first, inner]).reshape(2, HEADS_PER_GROUP * BAND, 2 * BAND)


def _attention_prompt(view, g):
    win, dil = DIL_GROUPS[g]
    n_cls = view.shape[0]
    qb = ATTN_QB
    while n_cls % (qb * BAND):
        qb //= 2
    rows = qb * BAND
    assert win // dil <= BAND and qb >= 1 and view.shape[1] == dil * 3 * GROUP_W
    cur = lambda part: pl.BlockSpec((rows, GROUP_W), lambda r, b: (b, r * 3 + part))
    prev = lambda part: pl.BlockSpec((BAND, GROUP_W), lambda r, b: (jnp.maximum(b * qb - 1, 0), r * 3 + part))
    bias = _prompt_bias(g, dil)
    return pl.pallas_call(
        functools.partial(_attn_kernel, qb=qb),
        grid=(dil, n_cls // rows),
        in_specs=[cur(0), prev(1), cur(1), prev(2), cur(2),
                  pl.BlockSpec((2, HEADS_PER_GROUP * BAND, 2 * BAND), lambda r, b: (0, 0, 0))],
        out_specs=[pl.BlockSpec((rows, GROUP_W), lambda r, b: (b, r)),
                   pl.BlockSpec((rows, GROUP_W), lambda r, b: (b, r))],
        out_shape=[jax.ShapeDtypeStruct((n_cls, dil * GROUP_W), BF16),
                   jax.ShapeDtypeStruct((n_cls, dil * GROUP_W), F32)],
        compiler_params=_params("arbitrary", "arbitrary"),
        name=f"attn_g{g}",
    )(view, view, view, view, view, bias)


def _attn_sample_kernel(q_ref, k_ref, v_ref, cache_ref, bias_ref, o_ref, lse_ref, *, n_buf, t, seqs):
    pad = jnp.zeros((BAND - t, GROUP_W), F32)
    scale = HEAD_DIM ** -0.5
    for b in range(seqs):
        rows = slice(b * t, (b + 1) * t)
        k_new = jnp.concatenate([k_ref[rows, :], pad], axis=0).astype(BF16)
        v_new = jnp.concatenate([v_ref[rows, :], pad], axis=0).astype(BF16)
        qm = _head_rows(q_ref[rows, :], t).astype(BF16)
        s_old = _dot(qm, cache_ref[b, 0:GROUP_W, :].astype(BF16)) * scale + bias_ref[:, 0:n_buf]
        s_new = _dot_nt(qm, k_new) * scale + bias_ref[:, n_buf:]
        m = jnp.maximum(jnp.max(s_old, axis=-1, keepdims=True), jnp.max(s_new, axis=-1, keepdims=True))
        e_old = jnp.exp(s_old - m)
        e_new = jnp.exp(s_new - m)
        l = jnp.sum(e_old, axis=-1, keepdims=True) + jnp.sum(e_new, axis=-1, keepdims=True)
        pv = _dot_nt(e_old.astype(BF16), cache_ref[b, GROUP_W:, :].astype(BF16)) + _dot(e_new.astype(BF16), v_new)
        pv = pv / l
        lse = m + jnp.log(l)
        o_ref[rows, :] = _head_select([pv[h * t:(h + 1) * t] for h in range(HEADS_PER_GROUP)], t)
        lse_ref[rows, :] = _head_select([lse[h * t:(h + 1) * t] for h in range(HEADS_PER_GROUP)], t)


def _sample_bias(g, n_buf, t):
    win, dil = DIL_GROUPS[g]
    slopes = _alibi_slopes()[g]
    tq = jnp.arange(t)[:, None]
    j = jnp.arange(n_buf + BAND)[None, :]
    dist = n_buf + tq - j
    valid = (dist >= 0) & (dist % dil == 0) & (dist <= win) & (j < n_buf + t)
    bias = -slopes[:, None, None] * dist.astype(F32)
    return jnp.where(valid[None], bias, MASKED).reshape(HEADS_PER_GROUP * t, n_buf + BAND)


def _attention_sample(qkv, cache_t, g, t):
    nb, n_buf = cache_t.shape[0], cache_t.shape[2]
    seqs = max(1, min(4, 1024 // n_buf))
    assert nb % seqs == 0
    col = lambda part: pl.BlockSpec((seqs * t, GROUP_W), lambda i: (i, part))
    kern = functools.partial(_attn_sample_kernel, n_buf=n_buf, t=t, seqs=seqs)
    bias = _sample_bias(g, n_buf, t)
    return pl.pallas_call(
        kern,
        grid=(nb // seqs,),
        in_specs=[col(0), col(1), col(2),
                  pl.BlockSpec((seqs, 2 * GROUP_W, n_buf), lambda i: (i, 0, 0)),
                  pl.BlockSpec((HEADS_PER_GROUP * t, n_buf + BAND), lambda i: (0, 0))],
        out_specs=[pl.BlockSpec((seqs * t, GROUP_W), lambda i: (i, 0)),
                   pl.BlockSpec((seqs * t, GROUP_W), lambda i: (i, 0))],
        out_shape=[jax.ShapeDtypeStruct((nb * t, GROUP_W), F32),
                   jax.ShapeDtypeStruct((nb * t, GROUP_W), F32)],
        compiler_params=_params("arbitrary"),
        name=f"attn_sample_g{g}",
    )(qkv, qkv, qkv, cache_t, bias)


CONV_HALO = 32


def _conv_tail(acc, b_ref, g_ref, beta_ref):
    return _silu(_layer_norm(acc + b_ref[...], g_ref[...], beta_ref[...]))


SUBLANES = 8


def _conv_prompt_kernel(halo_ref, u_ref, w_ref, b_ref, g_ref, beta_ref, o_ref, hist, part, *, tm):
    i = pl.program_id(0)
    hist[0:CONV_HALO, :] = jnp.where(i == 0, 0.0, halo_ref[...])
    hist[CONV_HALO:CONV_HALO + tm, :] = u_ref[...]
    hist[CONV_HALO + tm:, :] = jnp.zeros((SUBLANES, CONV_CH), F32)
    off = CONV_HALO - (CONV_WIDTH - 1)
    acc = None
    for s in range(SUBLANES):
        group = None
        for m in range(s, off + CONV_WIDTH, SUBLANES):
            j = m - off
            if j < 0:
                continue
            term = w_ref[j:j + 1, :] * hist[m - s:m - s + tm + SUBLANES, :]
            group = term if group is None else group + term
        part[...] = group
        shifted = part[s:s + tm, :]
        acc = shifted if acc is None else acc + shifted
    o_ref[...] = _conv_tail(acc, b_ref, g_ref, beta_ref).astype(o_ref.dtype)


def _conv_prompt(u, w_dw, b_dw, ln_g, ln_b, tm):
    n = u.shape[0]
    vec = pl.BlockSpec((1, CONV_CH), lambda i: (0, 0))
    return pl.pallas_call(
        functools.partial(_conv_prompt_kernel, tm=tm),
        grid=(n // tm,),
        in_specs=[pl.BlockSpec((CONV_HALO, CONV_CH), lambda i: (jnp.maximum(i * (tm // CONV_HALO) - 1, 0), 0)),
                  pl.BlockSpec((tm, CONV_CH), lambda i: (i, 0)),
                  pl.BlockSpec((CONV_WIDTH, CONV_CH), lambda i: (0, 0)), vec, vec, vec],
        out_specs=pl.BlockSpec((tm, CONV_CH), lambda i: (i, 0)),
        out_shape=jax.ShapeDtypeStruct((n, CONV_CH), BF16),
        scratch_shapes=[pltpu.VMEM((CONV_HALO + tm + SUBLANES, CONV_CH), F32),
                        pltpu.VMEM((tm + SUBLANES, CONV_CH), F32)],
        compiler_params=_params("arbitrary"),
        name="conv_prompt",
    )(u, u, w_dw, b_dw, ln_g, ln_b)


CONV_SAMPLE_SEQS = 8


def _conv_sample_kernel(hist_ref, w_ref, b_ref, g_ref, beta_ref, o_ref, *, t, seqs):
    for b in range(seqs):
        acc = jnp.zeros((t, CONV_CH), F32)
        for j in range(CONV_WIDTH):
            acc = acc + w_ref[j:j + 1, :] * hist_ref[b, j:j + t, :]
        o_ref[b * t:(b + 1) * t, :] = _conv_tail(acc, b_ref, g_ref, beta_ref)


def _conv_sample(u_hist, w_dw, b_dw, ln_g, ln_b, t):
    nb, rows = u_hist.shape[0], u_hist.shape[1]
    seqs = CONV_SAMPLE_SEQS if nb % CONV_SAMPLE_SEQS == 0 else 1
    vec = pl.BlockSpec((1, CONV_CH), lambda i: (0, 0))
    return pl.pallas_call(
        functools.partial(_conv_sample_kernel, t=t, seqs=seqs),
        grid=(nb // seqs,),
        in_specs=[pl.BlockSpec((seqs, rows, CONV_CH), lambda i: (i, 0, 0)),
                  pl.BlockSpec((CONV_WIDTH, CONV_CH), lambda i: (0, 0)), vec, vec, vec],
        out_specs=pl.BlockSpec((seqs * t, CONV_CH), lambda i: (i, 0)),
        out_shape=jax.ShapeDtypeStruct((nb * t, CONV_CH), F32),
        compiler_params=_params("arbitrary"),
        name="conv_sample",
    )(u_hist, w_dw, b_dw, ln_g, ln_b)


U32 = jnp.uint32
HALF_D = D_MODEL // 2
PACK_SLABS = HALF_D // LANES


def _token_order(ref, dil, scr, tm):
    if dil == 1:
        return ref[...].astype(F32)
    for r in range(dil):
        for half in range(GROUP_W // LANES):
            c = r * GROUP_W + half * LANES
            scr[half, pl.ds(r, tm // dil, stride=dil), :] = ref[:, c:c + LANES].astype(F32)
    return jnp.concatenate([scr[half] for half in range(GROUP_W // LANES)], axis=1)


def _pack_rows(ref, val):
    rows = val.shape[0]
    bits = lax.bitcast_convert_type(val.astype(BF16).astype(F32), U32)
    packed = lax.shift_right_logical(bits[:, :HALF_D], jnp.uint32(16)) | bits[:, HALF_D:]
    for c in range(PACK_SLABS):
        ref[pl.ds(c, rows, stride=PACK_SLABS), :] = packed[:, c * LANES:(c + 1) * LANES]


def _unpack_rows(ref):
    rows = ref.shape[0] // PACK_SLABS
    u = jnp.concatenate([ref[pl.ds(c, rows, stride=PACK_SLABS), :] for c in range(PACK_SLABS)], axis=1)
    lo = lax.bitcast_convert_type(lax.shift_left(u, jnp.uint32(16)), F32)
    hi = lax.bitcast_convert_type(u & jnp.uint32(0xFFFF0000), F32)
    return lo, hi


def _outproj_kernel(o0, o1, o2, l0, l1, l2, c_ref, x_ref, w_ref, b_ref, g_ref, beta_ref, h_ref, hp_ref, *scr,
                    dils, tm):
    os_ = [_token_order(r, d, scr[2 * g], tm) for g, (r, d) in enumerate(zip((o0, o1, o2), dils))]
    ls = [_token_order(r, d, scr[2 * g + 1], tm) for g, (r, d) in enumerate(zip((l0, l1, l2), dils))]
    m = jnp.maximum(jnp.maximum(ls[0], ls[1]), ls[2])
    es = [jnp.exp(l - m) for l in ls]
    inv = 1.0 / (es[0] + es[1] + es[2])
    mixed = b_ref[...]
    for g in range(N_GROUPS):
        a = (os_[g] * (es[g] * inv)).astype(BF16)
        mixed = mixed + _dot(a, w_ref[g * GROUP_W:(g + 1) * GROUP_W, :])
    mixed = mixed + _dot(c_ref[...].astype(BF16), w_ref[ATTN_W:, :])
    h = _layer_norm(DN_ALPHA * x_ref[...] + mixed, g_ref[...], beta_ref[...])
    h_ref[...] = h
    _pack_rows(hp_ref, h)


def _outproj(os_, lses, conv, x, w_o, b_o, ln_g, ln_b, tm, dils):
    n = x.shape[0]
    grp = [pl.BlockSpec((tm // d, d * GROUP_W), lambda i: (i, 0)) for d in dils]
    vec = pl.BlockSpec((1, D_MODEL), lambda i: (0, 0))
    return pl.pallas_call(
        functools.partial(_outproj_kernel, dils=dils, tm=tm),
        grid=(n // tm,),
        in_specs=grp + grp + [pl.BlockSpec((tm, CONV_CH), lambda i: (i, 0)),
                              pl.BlockSpec((tm, D_MODEL), lambda i: (i, 0)),
                              pl.BlockSpec((D_MODEL, D_MODEL), lambda i: (0, 0)), vec, vec, vec],
        out_specs=[pl.BlockSpec((tm, D_MODEL), lambda i: (i, 0)),
                   pl.BlockSpec((tm * PACK_SLABS, LANES), lambda i: (i, 0))],
        out_shape=[jax.ShapeDtypeStruct((n, D_MODEL), F32),
                   jax.ShapeDtypeStruct((n * PACK_SLABS, LANES), U32)],
        scratch_shapes=[pltpu.VMEM((GROUP_W // LANES, tm, LANES), F32) for _ in range(2 * N_GROUPS)],
        compiler_params=_params("arbitrary"),
        name="outproj",
    )(*os_, *lses, conv, x, w_o, b_o, ln_g, ln_b)


ROUTER_TM = 256


def _first_index(hit, idx, limit, axis):
    return jnp.min(jnp.where(hit, idx, limit), axis=axis, keepdims=True)


def _router_kernel(ha_ref, hb_ref, w_ref, rb_ref, tri_ref, idx_ref, wt_ref, pos_ref, cnt_ref, run, *, tiles_a):
    i = pl.program_id(0)
    tm = ROUTER_TM

    @pl.when(i == 0)
    def _():
        run[...] = jnp.zeros_like(run)

    h = jnp.where(i < tiles_a, ha_ref[...], hb_ref[...])
    logits = _dot_nt(w_ref[...], h.astype(BF16))
    scores = jax.nn.sigmoid(logits)
    biased = scores + rb_ref[...]
    groups = [biased[g * EXPERTS_PER_GROUP:(g + 1) * EXPERTS_PER_GROUP] for g in range(N_EXPERT_GROUPS)]
    ei = lax.broadcasted_iota(I32, (EXPERTS_PER_GROUP, tm), 0).astype(F32)
    gs = []
    for bg in groups:
        m1 = jnp.max(bg, axis=0, keepdims=True)
        f1 = _first_index(bg == m1, ei, float(EXPERTS_PER_GROUP), 0)
        m2 = jnp.max(jnp.where(ei == f1, -jnp.inf, bg), axis=0, keepdims=True)
        gs.append(m1 + m2)
    gs = jnp.concatenate(gs, axis=0)
    gi = lax.broadcasted_iota(I32, gs.shape, 0).astype(F32)
    keep = jnp.zeros(gs.shape, F32)
    cur = gs
    for _ in range(TOPK_GROUPS):
        m = jnp.max(cur, axis=0, keepdims=True)
        f = _first_index(cur == m, gi, float(N_EXPERT_GROUPS), 0)
        hit = gi == f
        keep = jnp.where(hit, 1.0, keep)
        cur = jnp.where(hit, -jnp.inf, cur)
    masked = jnp.concatenate([jnp.where(keep[g:g + 1] > 0.0, bg, -jnp.inf) for g, bg in enumerate(groups)], axis=0)
    xi = lax.broadcasted_iota(I32, (N_EXPERTS, tm), 0).astype(F32)
    cur = masked
    picks = []
    for _ in range(TOP_K):
        m = jnp.max(cur, axis=0, keepdims=True)
        f = _first_index(cur == m, xi, float(N_EXPERTS), 0)
        hit = xi == f
        picks.append((f, hit))
        cur = jnp.where(hit, -jnp.inf, cur)
    sel = jnp.where(cur != masked, 1.0, 0.0)
    before = _dot(sel.astype(BF16), tri_ref[...]) + run[...]
    run[...] = run[...] + jnp.sum(sel, axis=1, keepdims=True)
    ws = [jnp.sum(jnp.where(hit, scores, 0.0), axis=0, keepdims=True) for _, hit in picks]
    wsum = ws[0]
    for w in ws[1:]:
        wsum = wsum + w
    for k, (f, hit) in enumerate(picks):
        idx_ref[k:k + 1, :] = f.astype(I32)
        wt_ref[k:k + 1, :] = ws[k] / wsum * ROUTED_SCALE
        pos_ref[k:k + 1, :] = jnp.sum(jnp.where(hit, before, 0.0), axis=0, keepdims=True).astype(I32)
    cnt_ref[...] = jnp.broadcast_to(run[...], cnt_ref.shape).astype(I32)


def _route(ha, hb, w_router_t, router_bias):
    tm = ROUTER_TM
    tiles_a, tiles_b = ha.shape[0] // tm, hb.shape[0] // tm
    spec_a = pl.BlockSpec((tm, D_MODEL), lambda i: (jnp.minimum(i, tiles_a - 1), 0))
    spec_b = pl.BlockSpec((tm, D_MODEL), lambda i: (jnp.maximum(i - tiles_a, 0), 0))
    t = (tiles_a + tiles_b) * tm
    tri = (jnp.arange(tm)[:, None] < jnp.arange(tm)[None, :]).astype(BF16)
    slot = pl.BlockSpec((TOP_K, tm), lambda i: (0, i))
    idx, wts, pos, cnt = pl.pallas_call(
        functools.partial(_router_kernel, tiles_a=tiles_a),
        grid=(t // tm,),
        in_specs=[spec_a, spec_b,
                  pl.BlockSpec((N_EXPERTS, D_MODEL), lambda i: (0, 0)),
                  pl.BlockSpec((N_EXPERTS, 1), lambda i: (0, 0)),
                  pl.BlockSpec((tm, tm), lambda i: (0, 0))],
        out_specs=[slot, slot, slot, pl.BlockSpec((N_EXPERTS, 128), lambda i: (0, 0))],
        out_shape=[jax.ShapeDtypeStruct((TOP_K, t), I32), jax.ShapeDtypeStruct((TOP_K, t), F32),
                   jax.ShapeDtypeStruct((TOP_K, t), I32), jax.ShapeDtypeStruct((N_EXPERTS, 128), I32)],
        scratch_shapes=[pltpu.VMEM((N_EXPERTS, 1), F32)],
        compiler_params=_params("arbitrary"),
        name="router",
    )(ha, hb, w_router_t, router_bias.reshape(N_EXPERTS, 1), tri)
    return idx, wts, pos, cnt[:, 0]


DEST_TM = 1280


def _dest_kernel(idx_ref, pos_ref, starts_ref, dest_ref):
    tm = idx_ref.shape[1]
    ei = lax.broadcasted_iota(I32, (N_EXPERTS, tm), 0)
    starts = starts_ref[...]
    for k in range(TOP_K):
        first = jnp.sum(jnp.where(ei == idx_ref[k:k + 1, :], starts, 0.0), axis=0, keepdims=True)
        dest_ref[k:k + 1, :] = first.astype(I32) + pos_ref[k:k + 1, :]


def _dest_rows(idx, pos, starts):
    t = idx.shape[1]
    tm = DEST_TM if t % DEST_TM == 0 else ROUTER_TM
    assert t % tm == 0
    slot = pl.BlockSpec((TOP_K, tm), lambda i: (0, i))
    return pl.pallas_call(
        _dest_kernel,
        grid=(t // tm,),
        in_specs=[slot, slot, pl.BlockSpec((N_EXPERTS, 1), lambda i: (0, 0))],
        out_specs=slot,
        out_shape=jax.ShapeDtypeStruct((TOP_K, t), I32),
        compiler_params=_params("arbitrary"),
        name="dest_rows",
    )(idx, pos, starts.astype(F32).reshape(N_EXPERTS, 1))


SC_CORES = 2
SC_SUBCORES = 16
SC_WORKERS = SC_CORES * SC_SUBCORES
SC_WINDOW = 64
SC_WINDOW_SMALL = 32


def _sc_worker():
    return lax.axis_index("s") * SC_CORES + lax.axis_index("c")


def _sc_mesh():
    return plsc.VectorSubcoreMesh(core_axis_name="c", subcore_axis_name="s")


def _sc_scatter_rows(ha, hb, dest_w, zeros, n_out):
    w = dest_w.shape[2]
    na, nb = ha.shape[0], hb.shape[0]
    spare = zeros.shape[0]
    wa, wb = na // w, nb // w
    assert na % (SC_WORKERS * 2 * w) == 0 and nb % w == 0 and wb <= SC_WORKERS
    per_w = wa // SC_WORKERS
    rows_t = pltpu.VMEM((w,) + ha.shape[1:], ha.dtype)
    idx_t = pltpu.VMEM((TOP_K, w), I32)

    @functools.partial(
        pl.kernel, mesh=_sc_mesh(),
        out_type=jax.ShapeDtypeStruct((n_out + spare,) + ha.shape[1:], ha.dtype),
        scratch_types=[rows_t, rows_t, idx_t, idx_t, pltpu.SemaphoreType.DMA, pltpu.SemaphoreType.DMA],
    )
    def scatter(ha_hbm, hb_hbm, dest_hbm, zeros_hbm, out_hbm, rows0, rows1, idx0, idx1, sem0, sem1):
        wid = _sc_worker()
        bufs = ((rows0, idx0, sem0), (rows1, idx1, sem1))

        @pl.when(wid == SC_WORKERS - 1)
        def _():
            pltpu.sync_copy(zeros_hbm, out_hbm.at[pl.ds(n_out, spare)])

        def load(src_hbm, row0, win, b):
            pltpu.sync_copy(src_hbm.at[pl.ds(pl.multiple_of(row0, 8), w)], bufs[b][0])
            pltpu.sync_copy(dest_hbm.at[win], bufs[b][1])

        def copies(b):
            return [pltpu.make_async_copy(bufs[b][0], out_hbm.at[bufs[b][1].at[k]], bufs[b][2])
                    for k in range(TOP_K)]

        win0 = wid * per_w
        load(ha_hbm, win0 * w, win0, 0)

        @pl.loop(0, per_w, step=2)
        def _(i0):
            for b in range(2):
                i = i0 + b
                for c in copies(b):
                    c.start()

                @pl.when(i + 1 < per_w)
                def _():
                    load(ha_hbm, (win0 + i + 1) * w, win0 + i + 1, 1 - b)

                for c in copies(b):
                    c.wait()

        @pl.when(wid < wb)
        def _():
            load(hb_hbm, wid * w, wa + wid, 0)
            for c in copies(0):
                c.start()
            for c in copies(0):
                c.wait()

    return scatter(ha, hb, dest_w, zeros)


def _sc_gather_rows(table, idx):
    n = idx.shape[0]
    w = SC_WINDOW if n % (SC_WORKERS * 2 * SC_WINDOW) == 0 else SC_WINDOW_SMALL
    assert n % (SC_WORKERS * 2 * w) == 0
    per_w = n // SC_WORKERS
    nwin = per_w // w
    rows_t = pltpu.VMEM((w,) + table.shape[1:], table.dtype)

    @functools.partial(
        pl.kernel, mesh=_sc_mesh(),
        out_type=jax.ShapeDtypeStruct((n,) + table.shape[1:], table.dtype),
        scratch_types=[pltpu.VMEM((per_w,), I32), rows_t, rows_t, pltpu.SemaphoreType.DMA, pltpu.SemaphoreType.DMA],
    )
    def gather(table_hbm, idx_hbm, out_hbm, idx_v, rows0, rows1, sem0, sem1):
        base = pl.multiple_of(_sc_worker() * per_w, 8)
        pltpu.sync_copy(idx_hbm.at[pl.ds(base, per_w)], idx_v)
        bufs = ((rows0, sem0), (rows1, sem1))
        fetch = lambda i, b: pltpu.make_async_copy(
            table_hbm.at[idx_v.at[pl.ds(pl.multiple_of(i * w, 8), w)]], bufs[b][0], bufs[b][1])
        fetch(0, 0).start()

        @pl.loop(0, nwin, step=2)
        def _(i0):
            for b in range(2):
                i = i0 + b
                fetch(i, b).wait()

                @pl.when(i + 1 < nwin)
                def _():
                    fetch(i + 1, 1 - b).start()

                pltpu.sync_copy(bufs[b][0], out_hbm.at[pl.ds(pl.multiple_of(base + i * w, 8), w)])

    return gather(table, idx)


EXPERT_CH = 256
X_SLOTS = 6
W_SLOTS = 3


def _expert_kernel(row0_ref, exp_ref, live_ref, fresh_ref, next1_ref, next2_ref, wslot_ref, ordered_ref, xs_ref,
                   wg_ref, wu_ref, wd_ref, ys_ref, xbuf, ybuf, sg, su, sd, wgb, wub, wdb, xsem, ysem, wsem):
    v = pl.program_id(0)
    nv = pl.num_programs(0)
    slot = lax.rem(v, X_SLOTS)
    yslot = lax.rem(v, 3)
    rows = lambda u: pl.ds(pl.multiple_of(row0_ref[u] * PACK_SLABS, PACK_SLABS), EXPERT_CH * PACK_SLABS)
    x_copy = lambda u: pltpu.make_async_copy(xs_ref.at[rows(u)], xbuf.at[lax.rem(u, X_SLOTS)],
                                             xsem.at[lax.rem(u, X_SLOTS)])
    y_copy = lambda u: pltpu.make_async_copy(ybuf.at[lax.rem(u, 3)], ys_ref.at[rows(u)], ysem.at[lax.rem(u, 3)])
    prev1, prev2 = jnp.maximum(v - 1, 0), jnp.maximum(v - 2, 0)
    w_copies = lambda e, s: [pltpu.make_async_copy(w.at[e], stage.at[s], wsem.at[s])
                             for w, stage in ((wg_ref, sg), (wu_ref, su), (wd_ref, sd))]
    spare = pl.ds(ys_ref.shape[0] - EXPERT_CH * PACK_SLABS, EXPERT_CH * PACK_SLABS)

    @pl.when(v == 0)
    def _():
        for c in w_copies(exp_ref[0], 0):
            c.start(priority=1)

        @pl.when(next1_ref[0] >= 0)
        def _():
            for c in w_copies(next1_ref[0], 1):
                c.start(priority=1)

        x_copy(0).start()
        for u in range(1, X_SLOTS - 1):
            @pl.when((u < nv) & (live_ref[jnp.minimum(u, nv - 1)] == 1))
            def _():
                x_copy(u).start()
        ybuf[2] = jnp.zeros(ybuf.shape[1:], U32)
        zero = pltpu.make_async_copy(ybuf.at[2], ys_ref.at[spare], ysem.at[2])
        zero.start()
        zero.wait()

    ahead = jnp.minimum(v + X_SLOTS - 1, nv - 1)

    @pl.when((v + X_SLOTS - 1 < nv) & (live_ref[ahead] == 1))
    def _():
        x_copy(ahead).start()

    @pl.when(fresh_ref[v] == 1)
    def _():
        p = wslot_ref[v]
        for c in w_copies(exp_ref[v], p):
            c.wait()

        @pl.when(next2_ref[v] >= 0)
        def _():
            for c in w_copies(next2_ref[v], lax.rem(p + 2, W_SLOTS)):
                c.start(priority=1)

        wgb[...] = sg[p].astype(BF16)
        wub[...] = su[p].astype(BF16)
        wdb[...] = sd[p].astype(BF16)

    @pl.when(live_ref[v] == 1)
    def _():
        x_copy(v).wait()
        x = jnp.concatenate(_unpack_rows(xbuf.at[slot]), axis=1).astype(BF16)
        hid = (_silu(_dot(x, wgb[...])) * _dot(x, wub[...])).astype(BF16)
        _pack_rows(ybuf.at[yslot], _dot(hid, wdb[...]))

    @pl.when((v >= 2) & (live_ref[prev2] == 1) & (ordered_ref[prev1] == 0))
    def _():
        y_copy(prev2).wait()

    @pl.when((v >= 1) & (live_ref[prev1] == 1) & (ordered_ref[v] == 1))
    def _():
        y_copy(prev1).wait()

    @pl.when(live_ref[v] == 1)
    def _():
        y_copy(v).start()

    @pl.when(v == nv - 1)
    def _():
        @pl.when((v >= 1) & (live_ref[prev1] == 1) & (ordered_ref[v] == 0))
        def _():
            y_copy(prev1).wait()

        @pl.when(live_ref[v] == 1)
        def _():
            y_copy(v).wait()


def _chunks(counts, n_rows):
    ch = EXPERT_CH
    nv = n_rows // ch + N_EXPERTS
    ends = jnp.cumsum(counts)
    starts = ends - counts
    nch = (counts + ch - 1) // ch
    cend = jnp.cumsum(nch)
    cstart = cend - nch
    v = jnp.arange(nv, dtype=I32)
    live = v < cend[-1]
    vc = jnp.minimum(v, cend[-1] - 1)
    ids = jnp.arange(N_EXPERTS, dtype=I32)
    e = jnp.minimum(jnp.sum(cend[None, :] <= vc[:, None], axis=1), N_EXPERTS - 1).astype(I32)
    of_e = lambda a: jnp.sum(jnp.where(e[:, None] == ids[None, :], a[None, :], 0), axis=1)
    starts_e, ends_e, count_e = of_e(starts), of_e(ends), of_e(counts)
    c = vc - of_e(cstart)
    is_last = c == of_e(nch) - 1
    row0 = jnp.where(is_last & (count_e >= ch), ends_e - ch, starts_e + c * ch).astype(I32)
    overrun = live & is_last & (count_e < ch)
    ordered = jnp.concatenate([jnp.zeros((1,), bool), overrun[:-1]]) & live
    fresh = jnp.concatenate([jnp.ones((1,), I32), (e[1:] != e[:-1]).astype(I32)])
    order = jnp.cumsum(counts > 0) - 1
    order_e = of_e(order)

    def after(k):
        hit = (counts > 0)[None, :] & (order[None, :] == order_e[:, None] + k)
        return (jnp.sum(jnp.where(hit, ids[None, :] + 1, 0), axis=1) - 1).astype(I32)

    return (starts.astype(I32), row0, e, live.astype(I32), jnp.where(live, fresh, 0).astype(I32),
            after(1), after(2), (order_e % W_SLOTS).astype(I32), ordered.astype(I32))


def _experts(chunks, xs, w_gate, w_up, w_down):
    ch = EXPERT_CH
    buf = lambda n: pltpu.VMEM((n, ch * PACK_SLABS, LANES), U32)
    anywhere = pl.BlockSpec(memory_space=pl.ANY)
    return pl.pallas_call(
        _expert_kernel,
        grid_spec=pltpu.PrefetchScalarGridSpec(
            num_scalar_prefetch=len(chunks),
            grid=(chunks[0].shape[0],),
            in_specs=[anywhere] * 4,
            out_specs=anywhere,
            scratch_shapes=[buf(X_SLOTS), buf(3),
                            pltpu.VMEM((W_SLOTS, D_MODEL, EXPERT_FF), F32),
                            pltpu.VMEM((W_SLOTS, D_MODEL, EXPERT_FF), F32),
                            pltpu.VMEM((W_SLOTS, EXPERT_FF, D_MODEL), F32),
                            pltpu.VMEM((D_MODEL, EXPERT_FF), BF16), pltpu.VMEM((D_MODEL, EXPERT_FF), BF16),
                            pltpu.VMEM((EXPERT_FF, D_MODEL), BF16),
                            pltpu.SemaphoreType.DMA((X_SLOTS,)), pltpu.SemaphoreType.DMA((3,)),
                            pltpu.SemaphoreType.DMA((W_SLOTS,))],
        ),
        out_shape=jax.ShapeDtypeStruct(xs.shape, U32),
        compiler_params=_params("arbitrary"),
        name="experts",
    )(*chunks, xs, w_gate, w_up, w_down)


COMBINE_TM = 512


def _combine_kernel(wt_ref, h_ref, sg_ref, su_ref, sd_ref, g_ref, beta_ref, *refs):
    y_refs, o_ref = refs[:TOP_K], refs[TOP_K]
    h = h_ref[...]
    hb = h.astype(BF16)
    hid = (_silu(_dot(hb, sg_ref[...])) * _dot(hb, su_ref[...])).astype(BF16)
    acc = DN_ALPHA * h + _dot(hid, sd_ref[...])
    lo = acc[:, :HALF_D]
    hi = acc[:, HALF_D:]
    for k in range(TOP_K):
        y_lo, y_hi = _unpack_rows(y_refs[k])
        w = wt_ref[:, k:k + 1]
        lo = lo + w * y_lo
        hi = hi + w * y_hi
    o_ref[...] = _layer_norm(jnp.concatenate([lo, hi], axis=1), g_ref[...], beta_ref[...])


def _combine(wts_tk, ws_gate, ws_up, ws_down, ln_g, ln_b, h, h_row0, t0, yg):
    n = yg.shape[0] // (TOP_K * PACK_SLABS)
    tm = COMBINE_TM if n % COMBINE_TM == 0 else ROUTER_TM
    assert n % tm == 0 and h_row0 % tm == 0 and t0 % tm == 0
    b0 = t0 // tm
    per_slot = n // tm
    rows = pl.BlockSpec((tm, D_MODEL), lambda i: (h_row0 // tm + i, 0))
    slot_rows = [pl.BlockSpec((tm * PACK_SLABS, LANES), lambda i, k=k: (k * per_slot + i, 0))
                 for k in range(TOP_K)]
    vec = pl.BlockSpec((1, D_MODEL), lambda i: (0, 0))
    return pl.pallas_call(
        _combine_kernel,
        grid=(n // tm,),
        in_specs=[pl.BlockSpec((tm, TOP_K), lambda i: (b0 + i, 0)), rows,
                  pl.BlockSpec((D_MODEL, EXPERT_FF), lambda i: (0, 0)),
                  pl.BlockSpec((D_MODEL, EXPERT_FF), lambda i: (0, 0)),
                  pl.BlockSpec((EXPERT_FF, D_MODEL), lambda i: (0, 0)),
                  vec, vec] + slot_rows,
        out_specs=pl.BlockSpec((tm, D_MODEL), lambda i: (i, 0)),
        out_shape=jax.ShapeDtypeStruct((n, D_MODEL), F32),
        compiler_params=_params("arbitrary"),
        name="combine",
    )(wts_tk, h, ws_gate, ws_up, ws_down, ln_g, ln_b, *([yg] * TOP_K))


def _kv_cache(kv_tail, g, keep):
    k = kv_tail[-keep:, g * GROUP_W:(g + 1) * GROUP_W]
    v = kv_tail[-keep:, ATTN_W + g * GROUP_W:ATTN_W + (g + 1) * GROUP_W]
    return jnp.stack([k, v], axis=1).reshape(keep, 2, HEADS_PER_GROUP, HEAD_DIM)


def kernel(x_prompt, x_sample, cache_kv_w128, cache_kv_w512, cache_kv_w2048, state_conv, w_in, b_in, w_dw, b_dw,
           conv_ln_g, conv_ln_b, w_o, b_o, ln1_g, ln1_b, w_router, router_bias, w_gate, w_up, w_down, ws_gate,
           ws_up, ws_down, ln2_g, ln2_b):
    assert w_in.shape[0] == DEPTH == 1
    batch, seq, _ = x_prompt.shape
    dec_batch, dec_seq, _ = x_sample.shape
    assert batch == 1
    n_s = dec_batch * dec_seq
    caches = (cache_kv_w128, cache_kv_w512, cache_kv_w2048)
    row = lambda a: a[0].reshape(1, -1)

    w_in_b = w_in[0].astype(BF16)
    b_in_r = row(b_in)
    w_o_b = w_o[0].astype(BF16)
    conv_w = (w_dw[0], row(b_dw), row(conv_ln_g), row(conv_ln_b))
    ln1 = (row(ln1_g), row(ln1_b))

    xp = x_prompt[0]
    keep_p = min(max(w for w, _ in DIL_GROUPS), seq)
    dils = tuple(d for _, d in DIL_GROUPS)
    *qkv_p, u_p, kv_p = _project(xp, w_in_b, b_in_r, 512, keep_p, BF16, dils)
    attn_p = [_attention_prompt(qkv_p[g], g) for g in range(N_GROUPS)]
    conv_p = _conv_prompt(u_p, *conv_w, 512)
    h_p, hpk_p = _outproj([a[0] for a in attn_p], [a[1] for a in attn_p], conv_p, xp, w_o_b, row(b_o), *ln1, 1024,
                          dils)

    xs = x_sample.reshape(n_s, D_MODEL)
    ones = (1,) * N_GROUPS
    *qkv_s, u_s, kv_s = _project(xs, w_in_b, b_in_r, n_s, n_s, F32, ones)
    caches_t = [jnp.transpose(c[0].reshape(dec_batch, -1, 2 * GROUP_W), (0, 2, 1)) for c in caches]
    attn_s = [_attention_sample(qkv_s[g], caches_t[g], g, dec_seq) for g in range(N_GROUPS)]
    u_hist = jnp.concatenate([state_conv[0], u_s.reshape(dec_batch, dec_seq, CONV_CH)], axis=1)
    conv_s = _conv_sample(u_hist, *conv_w, dec_seq)
    h_s, hpk_s = _outproj([a[0] for a in attn_s], [a[1] for a in attn_s], conv_s, xs, w_o_b, row(b_o), *ln1, n_s,
                          ones)

    idx, wts, pos, counts = _route(h_p, h_s, w_router[0].T.astype(BF16), router_bias[0])
    n_tok = seq + n_s
    starts, *chunks = _chunks(counts, n_tok * TOP_K)
    dest = _dest_rows(idx, pos, starts)
    dest_w = dest.reshape(TOP_K, n_tok // SC_WINDOW, SC_WINDOW).transpose(1, 0, 2)
    tiles = lambda a: a.reshape(-1, PACK_SLABS, LANES)
    flat = lambda a: a.reshape(-1, LANES)
    x_sorted = _sc_scatter_rows(tiles(hpk_p), tiles(hpk_s), dest_w, jnp.zeros((EXPERT_CH, PACK_SLABS, LANES), U32),
                                n_tok * TOP_K)
    y_sorted = _experts(chunks, flat(x_sorted), w_gate[0], w_up[0], w_down[0])
    shared = (ws_gate[0].astype(BF16), ws_up[0].astype(BF16), ws_down[0].astype(BF16))
    comb = functools.partial(_combine, wts.T, *shared, row(ln2_g), row(ln2_b))
    y_table = tiles(y_sorted)

    def finish(h, t0):
        y_slots = _sc_gather_rows(y_table, dest[:, t0:t0 + h.shape[0]].reshape(-1))
        return comb(h, 0, t0, flat(y_slots))

    y_s = finish(h_s, seq)
    y_p = finish(h_p, 0)

    kv_prompt = [_kv_cache(kv_p, g, min(w, seq))[None, None] for g, (w, _) in enumerate(DIL_GROUPS)]
    assert seq >= CONV_WIDTH - 1
    conv_prompt = u_p[-(CONV_WIDTH - 1):]
    kv_s4 = kv_s.reshape(dec_batch, dec_seq, 2, N_GROUPS, HEADS_PER_GROUP, HEAD_DIM)
    kv_sample = [kv_s4[:, :, :, g][None] for g in range(N_GROUPS)]
    conv_sample = u_hist[:, -(CONV_WIDTH - 1):]
    return (y_p[None], y_s.reshape(dec_batch, dec_seq, D_MODEL), *kv_prompt, conv_prompt[None, None],
            *kv_sample, conv_sample[None])
```

```python
import functools

import jax
import jax.numpy as jnp
from jax import lax
from jax.experimental import pallas as pl
from jax.experimental.pallas import tpu as pltpu
from jax.experimental.pallas import tpu_sc as plsc

F32 = jnp.float32
BF16 = jnp.bfloat16
I32 = jnp.int32

D_MODEL = 1024
HEAD_DIM = 64
HEADS_PER_GROUP = 4
GROUP_W = HEADS_PER_GROUP * HEAD_DIM
DIL_GROUPS = ((128, 1), (512, 4), (2048, 16))
N_GROUPS = len(DIL_GROUPS)
ATTN_W = N_GROUPS * GROUP_W
CONV_CH = D_MODEL - ATTN_W
CONV_WIDTH = 31
IN_W = 3 * ATTN_W + 2 * CONV_CH
BAND = 128
N_EXPERTS = 256
TOP_K = 8
N_EXPERT_GROUPS = 8
EXPERTS_PER_GROUP = N_EXPERTS // N_EXPERT_GROUPS
TOPK_GROUPS = 4
EXPERT_FF = 256
ROUTED_SCALE = 2.5
DEPTH = 1
DN_ALPHA = (2 * DEPTH) ** 0.25
LN_EPS = 1e-5
MASKED = -1e30

VMEM_LIMIT_BYTES = 56 * 1024 * 1024


def _params(*sem):
    return pltpu.CompilerParams(dimension_semantics=sem, vmem_limit_bytes=VMEM_LIMIT_BYTES)


def _dot(a, b):
    return jnp.dot(a, b, preferred_element_type=F32)


def _dot_nt(a, b):
    return lax.dot_general(a, b, (((1,), (1,)), ((), ())), preferred_element_type=F32)


def _layer_norm(x, g, b):
    mu = jnp.mean(x, axis=-1, keepdims=True)
    xc = x - mu
    var = jnp.mean(xc * xc, axis=-1, keepdims=True)
    return xc * lax.rsqrt(var + LN_EPS) * g + b


def _silu(x):
    return x * jax.nn.sigmoid(x)


def _alibi_slopes():
    n = N_GROUPS * HEADS_PER_GROUP
    h = jnp.arange(1, n + 1, dtype=F32)
    return (2.0 ** (-8.0 * h / n)).reshape(N_GROUPS, HEADS_PER_GROUP)


LANES = 128


def _proj_kernel(x_ref, w_ref, b_ref, *refs, dils, tm):
    qkv_refs, (u_ref, kv_ref, zs) = refs[:N_GROUPS], refs[N_GROUPS:]
    x = x_ref[...].astype(BF16)
    for part in range(3):
        for g in range(N_GROUPS):
            c0 = part * ATTN_W + g * GROUP_W
            z = _dot(x, w_ref[:, c0:c0 + GROUP_W]) + b_ref[:, c0:c0 + GROUP_W]
            if part > 0:
                kv_ref[:, c0 - ATTN_W:c0 - ATTN_W + GROUP_W] = z
            out, dil = qkv_refs[g], dils[g]
            if dil == 1:
                out[:, part * GROUP_W:(part + 1) * GROUP_W] = z.astype(out.dtype)
                continue
            for half in range(GROUP_W // LANES):
                zs[half] = z[:, half * LANES:(half + 1) * LANES]
            for r in range(dil):
                for half in range(GROUP_W // LANES):
                    c = r * 3 * GROUP_W + part * GROUP_W + half * LANES
                    out[:, c:c + LANES] = zs[half, pl.ds(r, tm // dil, stride=dil), :].astype(out.dtype)
    c0 = 3 * ATTN_W
    a = _dot(x, w_ref[:, c0:c0 + CONV_CH]) + b_ref[:, c0:c0 + CONV_CH]
    gate = _dot(x, w_ref[:, c0 + CONV_CH:]) + b_ref[:, c0 + CONV_CH:]
    u_ref[...] = a * jax.nn.sigmoid(gate)


def _project(x, w_in, b_in, tm, keep, qkv_dtype, dils):
    n = x.shape[0]
    nt = n // tm
    nk = keep // tm
    return pl.pallas_call(
        functools.partial(_proj_kernel, dils=dils, tm=tm),
        grid=(nt,),
        in_specs=[
            pl.BlockSpec((tm, D_MODEL), lambda i: (i, 0)),
            pl.BlockSpec((D_MODEL, IN_W), lambda i: (0, 0)),
            pl.BlockSpec((1, IN_W), lambda i: (0, 0)),
        ],
        out_specs=[pl.BlockSpec((tm // d, d * 3 * GROUP_W), lambda i: (i, 0)) for d in dils] + [
            pl.BlockSpec((tm, CONV_CH), lambda i: (i, 0)),
            pl.BlockSpec((tm, 2 * ATTN_W), lambda i: (jnp.maximum(i - (nt - nk), 0), 0)),
        ],
        out_shape=[jax.ShapeDtypeStruct((n // d, d * 3 * GROUP_W), qkv_dtype) for d in dils] + [
            jax.ShapeDtypeStruct((n, CONV_CH), F32),
            jax.ShapeDtypeStruct((keep, 2 * ATTN_W), F32),
        ],
        scratch_shapes=[pltpu.VMEM((GROUP_W // LANES, tm, LANES), F32)],
        compiler_params=_params("arbitrary"),
        name="proj",
    )(x, w_in, b_in)


def _head_select(parts, rows):
    col = lax.broadcasted_iota(I32, (rows, GROUP_W), 1) // HEAD_DIM
    out = jnp.broadcast_to(parts[-1], (rows, GROUP_W))
    for h in range(HEADS_PER_GROUP - 2, -1, -1):
        out = jnp.where(col == h, parts[h], out)
    return out


def _head_rows(q, rows):
    col = lax.broadcasted_iota(I32, (rows, GROUP_W), 1) // HEAD_DIM
    return jnp.concatenate([jnp.where(col == h, q, jnp.zeros_like(q)) for h in range(HEADS_PER_GROUP)], axis=0)


def _softmax_pv(s, v, rows):
    m = jnp.max(s, axis=-1, keepdims=True)
    e = jnp.exp(s - m)
    l = jnp.sum(e, axis=-1, keepdims=True)
    pv = _dot(e.astype(BF16), v) / l
    lse = m + jnp.log(l)
    o = _head_select([pv[h * rows:(h + 1) * rows] for h in range(HEADS_PER_GROUP)], rows)
    lse_x = _head_select([lse[h * rows:(h + 1) * rows] for h in range(HEADS_PER_GROUP)], rows)
    return o, lse_x


ATTN_QB = 16


def _attn_kernel(q_ref, kp_ref, kc_ref, vp_ref, vc_ref, bias_ref, o_ref, lse_ref, *, qb):
    b = pl.program_id(1)
    for j in range(qb):
        rows = slice(j * BAND, (j + 1) * BAND)
        qm = _head_rows(q_ref[rows, :] * HEAD_DIM ** -0.5, BAND)
        if j == 0:
            k = jnp.concatenate([kp_ref[...], kc_ref[rows, :]], axis=0)
            v = jnp.concatenate([vp_ref[...], vc_ref[rows, :]], axis=0)
            bias = bias_ref[jnp.minimum(b, 1)]
        else:
            k = kc_ref[(j - 1) * BAND:(j + 1) * BAND, :]
            v = vc_ref[(j - 1) * BAND:(j + 1) * BAND, :]
            bias = bias_ref[1]
        s = _dot_nt(qm, k) + bias
        o, lse_x = _softmax_pv(s, v, BAND)
        o_ref[rows, :] = o.astype(o_ref.dtype)
        lse_ref[rows, :] = lse_x


def _prompt_bias(g, dil):
    slopes = _alibi_slopes()[g]
    qi = jnp.arange(BAND)[:, None]
    kj = jnp.arange(2 * BAND)[None, :]
    steps = qi + BAND - kj
    nk = DIL_GROUPS[g][0] // dil
    valid = (steps >= 0) & (steps <= nk)
    bias = -slopes[:, None, None] * (steps * dil).astype(F32)
    inner = jnp.where(valid[None], bias, MASKED)
    first = jnp.where((valid & (kj >= BAND))[None], bias, MASKED)
    return jnp.stack([first, inner]).reshape(2, HEADS_PER_GROUP * BAND, 2 * BAND)


def _attention_prompt(view, g):
    win, dil = DIL_GROUPS[g]
    n_cls = view.shape[0]
    qb = ATTN_QB
    while n_cls % (qb * BAND):
        qb //= 2
    rows = qb * BAND
    assert win // dil <= BAND and qb >= 1 and view.shape[1] == dil * 3 * GROUP_W
    cur = lambda part: pl.BlockSpec((rows, GROUP_W), lambda r, b: (b, r * 3 + part))
    prev = lambda part: pl.BlockSpec((BAND, GROUP_W), lambda r, b: (jnp.maximum(b * qb - 1, 0), r * 3 + part))
    bias = _prompt_bias(g, dil)
    return pl.pallas_call(
        functools.partial(_attn_kernel, qb=qb),
        grid=(dil, n_cls // rows),
        in_specs=[cur(0), prev(1), cur(1), prev(2), cur(2),
                  pl.BlockSpec((2, HEADS_PER_GROUP * BAND, 2 * BAND), lambda r, b: (0, 0, 0))],
        out_specs=[pl.BlockSpec((rows, GROUP_W), lambda r, b: (b, r)),
                   pl.BlockSpec((rows, GROUP_W), lambda r, b: (b, r))],
        out_shape=[jax.ShapeDtypeStruct((n_cls, dil * GROUP_W), BF16),
                   jax.ShapeDtypeStruct((n_cls, dil * GROUP_W), F32)],
        compiler_params=_params("arbitrary", "arbitrary"),
        name=f"attn_g{g}",
    )(view, view, view, view, view, bias)


def _attn_sample_kernel(q_ref, k_ref, v_ref, cache_ref, bias_ref, o_ref, lse_ref, *, n_buf, t, seqs):
    pad = jnp.zeros((BAND - t, GROUP_W), F32)
    scale = HEAD_DIM ** -0.5
    for b in range(seqs):
        rows = slice(b * t, (b + 1) * t)
        k_new = jnp.concatenate([k_ref[rows, :], pad], axis=0).astype(BF16)
        v_new = jnp.concatenate([v_ref[rows, :], pad], axis=0).astype(BF16)
        qm = _head_rows(q_ref[rows, :], t).astype(BF16)
        s_old = _dot(qm, cache_ref[b, 0:GROUP_W, :].astype(BF16)) * scale + bias_ref[:, 0:n_buf]
        s_new = _dot_nt(qm, k_new) * scale + bias_ref[:, n_buf:]
        m = jnp.maximum(jnp.max(s_old, axis=-1, keepdims=True), jnp.max(s_new, axis=-1, keepdims=True))
        e_old = jnp.exp(s_old - m)
        e_new = jnp.exp(s_new - m)
        l = jnp.sum(e_old, axis=-1, keepdims=True) + jnp.sum(e_new, axis=-1, keepdims=True)
        pv = _dot_nt(e_old.astype(BF16), cache_ref[b, GROUP_W:, :].astype(BF16)) + _dot(e_new.astype(BF16), v_new)
        pv = pv / l
        lse = m + jnp.log(l)
        o_ref[rows, :] = _head_select([pv[h * t:(h + 1) * t] for h in range(HEADS_PER_GROUP)], t)
        lse_ref[rows, :] = _head_select([lse[h * t:(h + 1) * t] for h in range(HEADS_PER_GROUP)], t)


def _sample_bias(g, n_buf, t):
    win, dil = DIL_GROUPS[g]
    slopes = _alibi_slopes()[g]
    tq = jnp.arange(t)[:, None]
    j = jnp.arange(n_buf + BAND)[None, :]
    dist = n_buf + tq - j
    valid = (dist >= 0) & (dist % dil == 0) & (dist <= win) & (j < n_buf + t)
    bias = -slopes[:, None, None] * dist.astype(F32)
    return jnp.where(valid[None], bias, MASKED).reshape(HEADS_PER_GROUP * t, n_buf + BAND)


def _attention_sample(qkv, cache_t, g, t):
    nb, n_buf = cache_t.shape[0], cache_t.shape[2]
    seqs = max(1, min(4, 1024 // n_buf))
    assert nb % seqs == 0
    col = lambda part: pl.BlockSpec((seqs * t, GROUP_W), lambda i: (i, part))
    kern = functools.partial(_attn_sample_kernel, n_buf=n_buf, t=t, seqs=seqs)
    bias = _sample_bias(g, n_buf, t)
    return pl.pallas_call(
        kern,
        grid=(nb // seqs,),
        in_specs=[col(0), col(1), col(2),
                  pl.BlockSpec((seqs, 2 * GROUP_W, n_buf), lambda i: (i, 0, 0)),
                  pl.BlockSpec((HEADS_PER_GROUP * t, n_buf + BAND), lambda i: (0, 0))],
        out_specs=[pl.BlockSpec((seqs * t, GROUP_W), lambda i: (i, 0)),
                   pl.BlockSpec((seqs * t, GROUP_W), lambda i: (i, 0))],
        out_shape=[jax.ShapeDtypeStruct((nb * t, GROUP_W), F32),
                   jax.ShapeDtypeStruct((nb * t, GROUP_W), F32)],
        compiler_params=_params("arbitrary"),
        name=f"attn_sample_g{g}",
    )(qkv, qkv, qkv, cache_t, bias)


CONV_HALO = 32


def _conv_tail(acc, b_ref, g_ref, beta_ref):
    return _silu(_layer_norm(acc + b_ref[...], g_ref[...], beta_ref[...]))


SUBLANES = 8


def _conv_prompt_kernel(halo_ref, u_ref, w_ref, b_ref, g_ref, beta_ref, o_ref, hist, part, *, tm):
    i = pl.program_id(0)
    hist[0:CONV_HALO, :] = jnp.where(i == 0, 0.0, halo_ref[...])
    hist[CONV_HALO:CONV_HALO + tm, :] = u_ref[...]
    hist[CONV_HALO + tm:, :] = jnp.zeros((SUBLANES, CONV_CH), F32)
    off = CONV_HALO - (CONV_WIDTH - 1)
    acc = None
    for s in range(SUBLANES):
        group = None
        for m in range(s, off + CONV_WIDTH, SUBLANES):
            j = m - off
            if j < 0:
                continue
            term = w_ref[j:j + 1, :] * hist[m - s:m - s + tm + SUBLANES, :]
            group = term if group is None else group + term
        part[...] = group
        shifted = part[s:s + tm, :]
        acc = shifted if acc is None else acc + shifted
    o_ref[...] = _conv_tail(acc, b_ref, g_ref, beta_ref).astype(o_ref.dtype)


def _conv_prompt(u, w_dw, b_dw, ln_g, ln_b, tm):
    n = u.shape[0]
    vec = pl.BlockSpec((1, CONV_CH), lambda i: (0, 0))
    return pl.pallas_call(
        functools.partial(_conv_prompt_kernel, tm=tm),
        grid=(n // tm,),
        in_specs=[pl.BlockSpec((CONV_HALO, CONV_CH), lambda i: (jnp.maximum(i * (tm // CONV_HALO) - 1, 0), 0)),
                  pl.BlockSpec((tm, CONV_CH), lambda i: (i, 0)),
                  pl.BlockSpec((CONV_WIDTH, CONV_CH), lambda i: (0, 0)), vec, vec, vec],
        out_specs=pl.BlockSpec((tm, CONV_CH), lambda i: (i, 0)),
        out_shape=jax.ShapeDtypeStruct((n, CONV_CH), BF16),
        scratch_shapes=[pltpu.VMEM((CONV_HALO + tm + SUBLANES, CONV_CH), F32),
                        pltpu.VMEM((tm + SUBLANES, CONV_CH), F32)],
        compiler_params=_params("arbitrary"),
        name="conv_prompt",
    )(u, u, w_dw, b_dw, ln_g, ln_b)


CONV_SAMPLE_SEQS = 8


def _conv_sample_kernel(hist_ref, w_ref, b_ref, g_ref, beta_ref, o_ref, *, t, seqs):
    for b in range(seqs):
        acc = jnp.zeros((t, CONV_CH), F32)
        for j in range(CONV_WIDTH):
            acc = acc + w_ref[j:j + 1, :] * hist_ref[b, j:j + t, :]
        o_ref[b * t:(b + 1) * t, :] = _conv_tail(acc, b_ref, g_ref, beta_ref)


def _conv_sample(u_hist, w_dw, b_dw, ln_g, ln_b, t):
    nb, rows = u_hist.shape[0], u_hist.shape[1]
    seqs = CONV_SAMPLE_SEQS if nb % CONV_SAMPLE_SEQS == 0 else 1
    vec = pl.BlockSpec((1, CONV_CH), lambda i: (0, 0))
    return pl.pallas_call(
        functools.partial(_conv_sample_kernel, t=t, seqs=seqs),
        grid=(nb // seqs,),
        in_specs=[pl.BlockSpec((seqs, rows, CONV_CH), lambda i: (i, 0, 0)),
                  pl.BlockSpec((CONV_WIDTH, CONV_CH), lambda i: (0, 0)), vec, vec, vec],
        out_specs=pl.BlockSpec((seqs * t, CONV_CH), lambda i: (i, 0)),
        out_shape=jax.ShapeDtypeStruct((nb * t, CONV_CH), F32),
        compiler_params=_params("arbitrary"),
        name="conv_sample",
    )(u_hist, w_dw, b_dw, ln_g, ln_b)


U32 = jnp.uint32
HALF_D = D_MODEL // 2
PACK_SLABS = HALF_D // LANES


def _token_order(ref, dil, scr, tm):
    if dil == 1:
        return ref[...].astype(F32)
    for r in range(dil):
        for half in range(GROUP_W // LANES):
            c = r * GROUP_W + half * LANES
            scr[half, pl.ds(r, tm // dil, stride=dil), :] = ref[:, c:c + LANES].astype(F32)
    return jnp.concatenate([scr[half] for half in range(GROUP_W // LANES)], axis=1)


def _pack_rows(ref, val):
    rows = val.shape[0]
    bits = lax.bitcast_convert_type(val.astype(BF16).astype(F32), U32)
    packed = lax.shift_right_logical(bits[:, :HALF_D], jnp.uint32(16)) | bits[:, HALF_D:]
    for c in range(PACK_SLABS):
        ref[pl.ds(c, rows, stride=PACK_SLABS), :] = packed[:, c * LANES:(c + 1) * LANES]


def _unpack_rows(ref):
    rows = ref.shape[0] // PACK_SLABS
    u = jnp.concatenate([ref[pl.ds(c, rows, stride=PACK_SLABS), :] for c in range(PACK_SLABS)], axis=1)
    lo = lax.bitcast_convert_type(lax.shift_left(u, jnp.uint32(16)), F32)
    hi = lax.bitcast_convert_type(u & jnp.uint32(0xFFFF0000), F32)
    return lo, hi


def _outproj_kernel(o0, o1, o2, l0, l1, l2, c_ref, x_ref, w_ref, b_ref, g_ref, beta_ref, h_ref, hp_ref, *scr,
                    dils, tm):
    os_ = [_token_order(r, d, scr[2 * g], tm) for g, (r, d) in enumerate(zip((o0, o1, o2), dils))]
    ls = [_token_order(r, d, scr[2 * g + 1], tm) for g, (r, d) in enumerate(zip((l0, l1, l2), dils))]
    m = jnp.maximum(jnp.maximum(ls[0], ls[1]), ls[2])
    es = [jnp.exp(l - m) for l in ls]
    inv = 1.0 / (es[0] + es[1] + es[2])
    mixed = b_ref[...]
    for g in range(N_GROUPS):
        a = (os_[g] * (es[g] * inv)).astype(BF16)
        mixed = mixed + _dot(a, w_ref[g * GROUP_W:(g + 1) * GROUP_W, :])
    mixed = mixed + _dot(c_ref[...].astype(BF16), w_ref[ATTN_W:, :])
    h = _layer_norm(DN_ALPHA * x_ref[...] + mixed, g_ref[...], beta_ref[...])
    h_ref[...] = h
    _pack_rows(hp_ref, h)


def _outproj(os_, lses, conv, x, w_o, b_o, ln_g, ln_b, tm, dils):
    n = x.shape[0]
    grp = [pl.BlockSpec((tm // d, d * GROUP_W), lambda i: (i, 0)) for d in dils]
    vec = pl.BlockSpec((1, D_MODEL), lambda i: (0, 0))
    return pl.pallas_call(
        functools.partial(_outproj_kernel, dils=dils, tm=tm),
        grid=(n // tm,),
        in_specs=grp + grp + [pl.BlockSpec((tm, CONV_CH), lambda i: (i, 0)),
                              pl.BlockSpec((tm, D_MODEL), lambda i: (i, 0)),
                              pl.BlockSpec((D_MODEL, D_MODEL), lambda i: (0, 0)), vec, vec, vec],
        out_specs=[pl.BlockSpec((tm, D_MODEL), lambda i: (i, 0)),
                   pl.BlockSpec((tm * PACK_SLABS, LANES), lambda i: (i, 0))],
        out_shape=[jax.ShapeDtypeStruct((n, D_MODEL), F32),
                   jax.ShapeDtypeStruct((n * PACK_SLABS, LANES), U32)],
        scratch_shapes=[pltpu.VMEM((GROUP_W // LANES, tm, LANES), F32) for _ in range(2 * N_GROUPS)],
        compiler_params=_params("arbitrary"),
        name="outproj",
    )(*os_, *lses, conv, x, w_o, b_o, ln_g, ln_b)


ROUTER_TM = 256


def _first_index(hit, idx, limit, axis):
    return jnp.min(jnp.where(hit, idx, limit), axis=axis, keepdims=True)


def _router_kernel(ha_ref, hb_ref, w_ref, rb_ref, tri_ref, idx_ref, wt_ref, pos_ref, cnt_ref, run, *, tiles_a):
    i = pl.program_id(0)
    tm = ROUTER_TM

    @pl.when(i == 0)
    def _():
        run[...] = jnp.zeros_like(run)

    h = jnp.where(i < tiles_a, ha_ref[...], hb_ref[...])
    logits = _dot_nt(w_ref[...], h.astype(BF16))
    scores = jax.nn.sigmoid(logits)
    biased = scores + rb_ref[...]
    groups = [biased[g * EXPERTS_PER_GROUP:(g + 1) * EXPERTS_PER_GROUP] for g in range(N_EXPERT_GROUPS)]
    ei = lax.broadcasted_iota(I32, (EXPERTS_PER_GROUP, tm), 0).astype(F32)
    gs = []
    for bg in groups:
        m1 = jnp.max(bg, axis=0, keepdims=True)
        f1 = _first_index(bg == m1, ei, float(EXPERTS_PER_GROUP), 0)
        m2 = jnp.max(jnp.where(ei == f1, -jnp.inf, bg), axis=0, keepdims=True)
        gs.append(m1 + m2)
    gs = jnp.concatenate(gs, axis=0)
    gi = lax.broadcasted_iota(I32, gs.shape, 0).astype(F32)
    keep = jnp.zeros(gs.shape, F32)
    cur = gs
    for _ in range(TOPK_GROUPS):
        m = jnp.max(cur, axis=0, keepdims=True)
        f = _first_index(cur == m, gi, float(N_EXPERT_GROUPS), 0)
        hit = gi == f
        keep = jnp.where(hit, 1.0, keep)
        cur = jnp.where(hit, -jnp.inf, cur)
    masked = jnp.concatenate([jnp.where(keep[g:g + 1] > 0.0, bg, -jnp.inf) for g, bg in enumerate(groups)], axis=0)
    xi = lax.broadcasted_iota(I32, (N_EXPERTS, tm), 0).astype(F32)
    cur = masked
    picks = []
    for _ in range(TOP_K):
        m = jnp.max(cur, axis=0, keepdims=True)
        f = _first_index(cur == m, xi, float(N_EXPERTS), 0)
        hit = xi == f
        picks.append((f, hit))
        cur = jnp.where(hit, -jnp.inf, cur)
    sel = jnp.where(cur != masked, 1.0, 0.0)
    before = _dot(sel.astype(BF16), tri_ref[...]) + run[...]
    run[...] = run[...] + jnp.sum(sel, axis=1, keepdims=True)
    ws = [jnp.sum(jnp.where(hit, scores, 0.0), axis=0, keepdims=True) for _, hit in picks]
    wsum = ws[0]
    for w in ws[1:]:
        wsum = wsum + w
    for k, (f, hit) in enumerate(picks):
        idx_ref[k:k + 1, :] = f.astype(I32)
        wt_ref[k:k + 1, :] = ws[k] / wsum * ROUTED_SCALE
        pos_ref[k:k + 1, :] = jnp.sum(jnp.where(hit, before, 0.0), axis=0, keepdims=True).astype(I32)
    cnt_ref[...] = jnp.broadcast_to(run[...], cnt_ref.shape).astype(I32)


def _route(ha, hb, w_router_t, router_bias):
    tm = ROUTER_TM
    tiles_a, tiles_b = ha.shape[0] // tm, hb.shape[0] // tm
    spec_a = pl.BlockSpec((tm, D_MODEL), lambda i: (jnp.minimum(i, tiles_a - 1), 0))
    spec_b = pl.BlockSpec((tm, D_MODEL), lambda i: (jnp.maximum(i - tiles_a, 0), 0))
    t = (tiles_a + tiles_b) * tm
    tri = (jnp.arange(tm)[:, None] < jnp.arange(tm)[None, :]).astype(BF16)
    slot = pl.BlockSpec((TOP_K, tm), lambda i: (0, i))
    idx, wts, pos, cnt = pl.pallas_call(
        functools.partial(_router_kernel, tiles_a=tiles_a),
        grid=(t // tm,),
        in_specs=[spec_a, spec_b,
                  pl.BlockSpec((N_EXPERTS, D_MODEL), lambda i: (0, 0)),
                  pl.BlockSpec((N_EXPERTS, 1), lambda i: (0, 0)),
                  pl.BlockSpec((tm, tm), lambda i: (0, 0))],
        out_specs=[slot, slot, slot, pl.BlockSpec((N_EXPERTS, 128), lambda i: (0, 0))],
        out_shape=[jax.ShapeDtypeStruct((TOP_K, t), I32), jax.ShapeDtypeStruct((TOP_K, t), F32),
                   jax.ShapeDtypeStruct((TOP_K, t), I32), jax.ShapeDtypeStruct((N_EXPERTS, 128), I32)],
        scratch_shapes=[pltpu.VMEM((N_EXPERTS, 1), F32)],
        compiler_params=_params("arbitrary"),
        name="router",
    )(ha, hb, w_router_t, router_bias.reshape(N_EXPERTS, 1), tri)
    return idx, wts, pos, cnt[:, 0]


DEST_TM = 1280


def _dest_kernel(idx_ref, pos_ref, starts_ref, dest_ref):
    tm = idx_ref.shape[1]
    ei = lax.broadcasted_iota(I32, (N_EXPERTS, tm), 0)
    starts = starts_ref[...]
    for k in range(TOP_K):
        first = jnp.sum(jnp.where(ei == idx_ref[k:k + 1, :], starts, 0.0), axis=0, keepdims=True)
        dest_ref[k:k + 1, :] = first.astype(I32) + pos_ref[k:k + 1, :]


def _dest_rows(idx, pos, starts):
    t = idx.shape[1]
    tm = DEST_TM if t % DEST_TM == 0 else ROUTER_TM
    assert t % tm == 0
    slot = pl.BlockSpec((TOP_K, tm), lambda i: (0, i))
    return pl.pallas_call(
        _dest_kernel,
        grid=(t // tm,),
        in_specs=[slot, slot, pl.BlockSpec((N_EXPERTS, 1), lambda i: (0, 0))],
        out_specs=slot,
        out_shape=jax.ShapeDtypeStruct((TOP_K, t), I32),
        compiler_params=_params("arbitrary"),
        name="dest_rows",
    )(idx, pos, starts.astype(F32).reshape(N_EXPERTS, 1))


SC_CORES = 2
SC_SUBCORES = 16
SC_WORKERS = SC_CORES * SC_SUBCORES
SC_WINDOW = 64
SC_WINDOW_SMALL = 32


def _sc_worker():
    return lax.axis_index("s") * SC_CORES + lax.axis_index("c")


def _sc_mesh():
    return plsc.VectorSubcoreMesh(core_axis_name="c", subcore_axis_name="s")


def _sc_scatter_rows(ha, hb, dest_w, zeros, n_out):
    w = dest_w.shape[2]
    na, nb = ha.shape[0], hb.shape[0]
    spare = zeros.shape[0]
    wa, wb = na // w, nb // w
    assert na % (SC_WORKERS * 2 * w) == 0 and nb % w == 0 and wb <= SC_WORKERS
    per_w = wa // SC_WORKERS
    rows_t = pltpu.VMEM((w,) + ha.shape[1:], ha.dtype)
    idx_t = pltpu.VMEM((TOP_K, w), I32)

    @functools.partial(
        pl.kernel, mesh=_sc_mesh(),
        out_type=jax.ShapeDtypeStruct((n_out + spare,) + ha.shape[1:], ha.dtype),
        scratch_types=[rows_t, rows_t, idx_t, idx_t, pltpu.SemaphoreType.DMA, pltpu.SemaphoreType.DMA],
    )
    def scatter(ha_hbm, hb_hbm, dest_hbm, zeros_hbm, out_hbm, rows0, rows1, idx0, idx1, sem0, sem1):
        wid = _sc_worker()
        bufs = ((rows0, idx0, sem0), (rows1, idx1, sem1))

        @pl.when(wid == SC_WORKERS - 1)
        def _():
            pltpu.sync_copy(zeros_hbm, out_hbm.at[pl.ds(n_out, spare)])

        def load(src_hbm, row0, win, b):
            pltpu.sync_copy(src_hbm.at[pl.ds(pl.multiple_of(row0, 8), w)], bufs[b][0])
            pltpu.sync_copy(dest_hbm.at[win], bufs[b][1])

        def copies(b):
            return [pltpu.make_async_copy(bufs[b][0], out_hbm.at[bufs[b][1].at[k]], bufs[b][2])
                    for k in range(TOP_K)]

        win0 = wid * per_w
        load(ha_hbm, win0 * w, win0, 0)

        @pl.loop(0, per_w, step=2)
        def _(i0):
            for b in range(2):
                i = i0 + b
                for c in copies(b):
                    c.start()

                @pl.when(i + 1 < per_w)
                def _():
                    load(ha_hbm, (win0 + i + 1) * w, win0 + i + 1, 1 - b)

                for c in copies(b):
                    c.wait()

        @pl.when(wid < wb)
        def _():
            load(hb_hbm, wid * w, wa + wid, 0)
            for c in copies(0):
                c.start()
            for c in copies(0):
                c.wait()

    return scatter(ha, hb, dest_w, zeros)


def _sc_gather_rows(table, idx):
    n = idx.shape[0]
    w = SC_WINDOW if n % (SC_WORKERS * 2 * SC_WINDOW) == 0 else SC_WINDOW_SMALL
    assert n % (SC_WORKERS * 2 * w) == 0
    per_w = n // SC_WORKERS
    nwin = per_w // w
    rows_t = pltpu.VMEM((w,) + table.shape[1:], table.dtype)

    @functools.partial(
        pl.kernel, mesh=_sc_mesh(),
        out_type=jax.ShapeDtypeStruct((n,) + table.shape[1:], table.dtype),
        scratch_types=[pltpu.VMEM((per_w,), I32), rows_t, rows_t, pltpu.SemaphoreType.DMA, pltpu.SemaphoreType.DMA],
    )
    def gather(table_hbm, idx_hbm, out_hbm, idx_v, rows0, rows1, sem0, sem1):
        base = pl.multiple_of(_sc_worker() * per_w, 8)
        pltpu.sync_copy(idx_hbm.at[pl.ds(base, per_w)], idx_v)
        bufs = ((rows0, sem0), (rows1, sem1))
        fetch = lambda i, b: pltpu.make_async_copy(
            table_hbm.at[idx_v.at[pl.ds(pl.multiple_of(i * w, 8), w)]], bufs[b][0], bufs[b][1])
        fetch(0, 0).start()

        @pl.loop(0, nwin, step=2)
        def _(i0):
            for b in range(2):
                i = i0 + b
                fetch(i, b).wait()

                @pl.when(i + 1 < nwin)
                def _():
                    fetch(i + 1, 1 - b).start()

                pltpu.sync_copy(bufs[b][0], out_hbm.at[pl.ds(pl.multiple_of(base + i * w, 8), w)])

    return gather(table, idx)


EXPERT_CH = 256
X_SLOTS = 6
W_SLOTS = 3


def _expert_kernel(row0_ref, exp_ref, live_ref, fresh_ref, next1_ref, next2_ref, wslot_ref, ordered_ref, xs_ref,
                   wg_ref, wu_ref, wd_ref, ys_ref, xbuf, ybuf, sg, su, sd, wgb, wub, wdb, xsem, ysem, wsem):
    v = pl.program_id(0)
    nv = pl.num_programs(0)
    slot = lax.rem(v, X_SLOTS)
    yslot = lax.rem(v, 3)
    rows = lambda u: pl.ds(pl.multiple_of(row0_ref[u] * PACK_SLABS, PACK_SLABS), EXPERT_CH * PACK_SLABS)
    x_copy = lambda u: pltpu.make_async_copy(xs_ref.at[rows(u)], xbuf.at[lax.rem(u, X_SLOTS)],
                                             xsem.at[lax.rem(u, X_SLOTS)])
    y_copy = lambda u: pltpu.make_async_copy(ybuf.at[lax.rem(u, 3)], ys_ref.at[rows(u)], ysem.at[lax.rem(u, 3)])
    prev1, prev2 = jnp.maximum(v - 1, 0), jnp.maximum(v - 2, 0)
    w_copies = lambda e, s: [pltpu.make_async_copy(w.at[e], stage.at[s], wsem.at[s])
                             for w, stage in ((wg_ref, sg), (wu_ref, su), (wd_ref, sd))]
    spare = pl.ds(ys_ref.shape[0] - EXPERT_CH * PACK_SLABS, EXPERT_CH * PACK_SLABS)

    @pl.when(v == 0)
    def _():
        for c in w_copies(exp_ref[0], 0):
            c.start(priority=1)

        @pl.when(next1_ref[0] >= 0)
        def _():
            for c in w_copies(next1_ref[0], 1):
                c.start(priority=1)

        x_copy(0).start()
        for u in range(1, X_SLOTS - 1):
            @pl.when((u < nv) & (live_ref[jnp.minimum(u, nv - 1)] == 1))
            def _():
                x_copy(u).start()
        ybuf[2] = jnp.zeros(ybuf.shape[1:], U32)
        zero = pltpu.make_async_copy(ybuf.at[2], ys_ref.at[spare], ysem.at[2])
        zero.start()
        zero.wait()

    ahead = jnp.minimum(v + X_SLOTS - 1, nv - 1)

    @pl.when((v + X_SLOTS - 1 < nv) & (live_ref[ahead] == 1))
    def _():
        x_copy(ahead).start()

    @pl.when(fresh_ref[v] == 1)
    def _():
        p = wslot_ref[v]
        for c in w_copies(exp_ref[v], p):
            c.wait()

        @pl.when(next2_ref[v] >= 0)
        def _():
            for c in w_copies(next2_ref[v], lax.rem(p + 2, W_SLOTS)):
                c.start(priority=1)

        wgb[...] = sg[p].astype(BF16)
        wub[...] = su[p].astype(BF16)
        wdb[...] = sd[p].astype(BF16)

    @pl.when(live_ref[v] == 1)
    def _():
        x_copy(v).wait()
        x = jnp.concatenate(_unpack_rows(xbuf.at[slot]), axis=1).astype(BF16)
        hid = (_silu(_dot(x, wgb[...])) * _dot(x, wub[...])).astype(BF16)
        _pack_rows(ybuf.at[yslot], _dot(hid, wdb[...]))

    @pl.when((v >= 2) & (live_ref[prev2] == 1) & (ordered_ref[prev1] == 0))
    def _():
        y_copy(prev2).wait()

    @pl.when((v >= 1) & (live_ref[prev1] == 1) & (ordered_ref[v] == 1))
    def _():
        y_copy(prev1).wait()

    @pl.when(live_ref[v] == 1)
    def _():
        y_copy(v).start()

    @pl.when(v == nv - 1)
    def _():
        @pl.when((v >= 1) & (live_ref[prev1] == 1) & (ordered_ref[v] == 0))
        def _():
            y_copy(prev1).wait()

        @pl.when(live_ref[v] == 1)
        def _():
            y_copy(v).wait()


def _chunks(counts, n_rows):
    ch = EXPERT_CH
    nv = n_rows // ch + N_EXPERTS
    ends = jnp.cumsum(counts)
    starts = ends - counts
    nch = (counts + ch - 1) // ch
    cend = jnp.cumsum(nch)
    cstart = cend - nch
    v = jnp.arange(nv, dtype=I32)
    live = v < cend[-1]
    vc = jnp.minimum(v, cend[-1] - 1)
    ids = jnp.arange(N_EXPERTS, dtype=I32)
    e = jnp.minimum(jnp.sum(cend[None, :] <= vc[:, None], axis=1), N_EXPERTS - 1).astype(I32)
    of_e = lambda a: jnp.sum(jnp.where(e[:, None] == ids[None, :], a[None, :], 0), axis=1)
    starts_e, ends_e, count_e = of_e(starts), of_e(ends), of_e(counts)
    c = vc - of_e(cstart)
    is_last = c == of_e(nch) - 1
    row0 = jnp.where(is_last & (count_e >= ch), ends_e - ch, starts_e + c * ch).astype(I32)
    overrun = live & is_last & (count_e < ch)
    ordered = jnp.concatenate([jnp.zeros((1,), bool), overrun[:-1]]) & live
    fresh = jnp.concatenate([jnp.ones((1,), I32), (e[1:] != e[:-1]).astype(I32)])
    order = jnp.cumsum(counts > 0) - 1
    order_e = of_e(order)

    def after(k):
        hit = (counts > 0)[None, :] & (order[None, :] == order_e[:, None] + k)
        return (jnp.sum(jnp.where(hit, ids[None, :] + 1, 0), axis=1) - 1).astype(I32)

    return (starts.astype(I32), row0, e, live.astype(I32), jnp.where(live, fresh, 0).astype(I32),
            after(1), after(2), (order_e % W_SLOTS).astype(I32), ordered.astype(I32))


def _experts(chunks, xs, w_gate, w_up, w_down):
    ch = EXPERT_CH
    buf = lambda n: pltpu.VMEM((n, ch * PACK_SLABS, LANES), U32)
    anywhere = pl.BlockSpec(memory_space=pl.ANY)
    return pl.pallas_call(
        _expert_kernel,
        grid_spec=pltpu.PrefetchScalarGridSpec(
            num_scalar_prefetch=len(chunks),
            grid=(chunks[0].shape[0],),
            in_specs=[anywhere] * 4,
            out_specs=anywhere,
            scratch_shapes=[buf(X_SLOTS), buf(3),
                            pltpu.VMEM((W_SLOTS, D_MODEL, EXPERT_FF), F32),
                            pltpu.VMEM((W_SLOTS, D_MODEL, EXPERT_FF), F32),
                            pltpu.VMEM((W_SLOTS, EXPERT_FF, D_MODEL), F32),
                            pltpu.VMEM((D_MODEL, EXPERT_FF), BF16), pltpu.VMEM((D_MODEL, EXPERT_FF), BF16),
                            pltpu.VMEM((EXPERT_FF, D_MODEL), BF16),
                            pltpu.SemaphoreType.DMA((X_SLOTS,)), pltpu.SemaphoreType.DMA((3,)),
                            pltpu.SemaphoreType.DMA((W_SLOTS,))],
        ),
        out_shape=jax.ShapeDtypeStruct(xs.shape, U32),
        compiler_params=_params("arbitrary"),
        name="experts",
    )(*chunks, xs, w_gate, w_up, w_down)


COMBINE_TM = 512


def _combine_kernel(wt_ref, h_ref, sg_ref, su_ref, sd_ref, g_ref, beta_ref, *refs):
    y_refs, o_ref = refs[:TOP_K], refs[TOP_K]
    h = h_ref[...]
    hb = h.astype(BF16)
    hid = (_silu(_dot(hb, sg_ref[...])) * _dot(hb, su_ref[...])).astype(BF16)
    acc = DN_ALPHA * h + _dot(hid, sd_ref[...])
    lo = acc[:, :HALF_D]
    hi = acc[:, HALF_D:]
    for k in range(TOP_K):
        y_lo, y_hi = _unpack_rows(y_refs[k])
        w = wt_ref[:, k:k + 1]
        lo = lo + w * y_lo
        hi = hi + w * y_hi
    o_ref[...] = _layer_norm(jnp.concatenate([lo, hi], axis=1), g_ref[...], beta_ref[...])


def _combine(wts_tk, ws_gate, ws_up, ws_down, ln_g, ln_b, h, h_row0, t0, yg):
    n = yg.shape[0] // (TOP_K * PACK_SLABS)
    tm = COMBINE_TM if n % COMBINE_TM == 0 else ROUTER_TM
    assert n % tm == 0 and h_row0 % tm == 0 and t0 % tm == 0
    b0 = t0 // tm
    per_slot = n // tm
    rows = pl.BlockSpec((tm, D_MODEL), lambda i: (h_row0 // tm + i, 0))
    slot_rows = [pl.BlockSpec((tm * PACK_SLABS, LANES), lambda i, k=k: (k * per_slot + i, 0))
                 for k in range(TOP_K)]
    vec = pl.BlockSpec((1, D_MODEL), lambda i: (0, 0))
    return pl.pallas_call(
        _combine_kernel,
        grid=(n // tm,),
        in_specs=[pl.BlockSpec((tm, TOP_K), lambda i: (b0 + i, 0)), rows,
                  pl.BlockSpec((D_MODEL, EXPERT_FF), lambda i: (0, 0)),
                  pl.BlockSpec((D_MODEL, EXPERT_FF), lambda i: (0, 0)),
                  pl.BlockSpec((EXPERT_FF, D_MODEL), lambda i: (0, 0)),
                  vec, vec] + slot_rows,
        out_specs=pl.BlockSpec((tm, D_MODEL), lambda i: (i, 0)),
        out_shape=jax.ShapeDtypeStruct((n, D_MODEL), F32),
        compiler_params=_params("arbitrary"),
        name="combine",
    )(wts_tk, h, ws_gate, ws_up, ws_down, ln_g, ln_b, *([yg] * TOP_K))


def _kv_cache(kv_tail, g, keep):
    k = kv_tail[-keep:, g * GROUP_W:(g + 1) * GROUP_W]
    v = kv_tail[-keep:, ATTN_W + g * GROUP_W:ATTN_W + (g + 1) * GROUP_W]
    return jnp.stack([k, v], axis=1).reshape(keep, 2, HEADS_PER_GROUP, HEAD_DIM)


def kernel(x_prompt, x_sample, cache_kv_w128, cache_kv_w512, cache_kv_w2048, state_conv, w_in, b_in, w_dw, b_dw,
           conv_ln_g, conv_ln_b, w_o, b_o, ln1_g, ln1_b, w_router, router_bias, w_gate, w_up, w_down, ws_gate,
           ws_up, ws_down, ln2_g, ln2_b):
    assert w_in.shape[0] == DEPTH == 1
    batch, seq, _ = x_prompt.shape
    dec_batch, dec_seq, _ = x_sample.shape
    assert batch == 1
    n_s = dec_batch * dec_seq
    caches = (cache_kv_w128, cache_kv_w512, cache_kv_w2048)
    row = lambda a: a[0].reshape(1, -1)

    w_in_b = w_in[0].astype(BF16)
    b_in_r = row(b_in)
    w_o_b = w_o[0].astype(BF16)
    conv_w = (w_dw[0], row(b_dw), row(conv_ln_g), row(conv_ln_b))
    ln1 = (row(ln1_g), row(ln1_b))

    xp = x_prompt[0]
    keep_p = min(max(w for w, _ in DIL_GROUPS), seq)
    dils = tuple(d for _, d in DIL_GROUPS)
    *qkv_p, u_p, kv_p = _project(xp, w_in_b, b_in_r, 1024, keep_p, BF16, dils)
    attn_p = [_attention_prompt(qkv_p[g], g) for g in range(N_GROUPS)]
    conv_p = _conv_prompt(u_p, *conv_w, 512)
    h_p, hpk_p = _outproj([a[0] for a in attn_p], [a[1] for a in attn_p], conv_p, xp, w_o_b, row(b_o), *ln1, 1024,
                          dils)

    xs = x_sample.reshape(n_s, D_MODEL)
    ones = (1,) * N_GROUPS
    *qkv_s, u_s, kv_s = _project(xs, w_in_b, b_in_r, n_s, n_s, F32, ones)
    caches_t = [jnp.transpose(c[0].reshape(dec_batch, -1, 2 * GROUP_W), (0, 2, 1)) for c in caches]
    attn_s = [_attention_sample(qkv_s[g], caches_t[g], g, dec_seq) for g in range(N_GROUPS)]
    u_hist = jnp.concatenate([state_conv[0], u_s.reshape(dec_batch, dec_seq, CONV_CH)], axis=1)
    conv_s = _conv_sample(u_hist, *conv_w, dec_seq)
    h_s, hpk_s = _outproj([a[0] for a in attn_s], [a[1] for a in attn_s], conv_s, xs, w_o_b, row(b_o), *ln1, n_s,
                          ones)

    idx, wts, pos, counts = _route(h_p, h_s, w_router[0].T.astype(BF16), router_bias[0])
    n_tok = seq + n_s
    starts, *chunks = _chunks(counts, n_tok * TOP_K)
    dest = _dest_rows(idx, pos, starts)
    dest_w = dest.reshape(TOP_K, n_tok // SC_WINDOW, SC_WINDOW).transpose(1, 0, 2)
    tiles = lambda a: a.reshape(-1, PACK_SLABS, LANES)
    flat = lambda a: a.reshape(-1, LANES)
    x_sorted = _sc_scatter_rows(tiles(hpk_p), tiles(hpk_s), dest_w, jnp.zeros((EXPERT_CH, PACK_SLABS, LANES), U32),
                                n_tok * TOP_K)
    y_sorted = _experts(chunks, flat(x_sorted), w_gate[0], w_up[0], w_down[0])
    shared = (ws_gate[0].astype(BF16), ws_up[0].astype(BF16), ws_down[0].astype(BF16))
    comb = functools.partial(_combine, wts.T, *shared, row(ln2_g), row(ln2_b))
    y_table = tiles(y_sorted)

    def finish(h, t0):
        y_slots = _sc_gather_rows(y_table, dest[:, t0:t0 + h.shape[0]].reshape(-1))
        return comb(h, 0, t0, flat(y_slots))

    y_s = finish(h_s, seq)
    y_p = finish(h_p, 0)

    kv_prompt = [_kv_cache(kv_p, g, min(w, seq))[None, None] for g, (w, _) in enumerate(DIL_GROUPS)]
    assert seq >= CONV_WIDTH - 1
    conv_prompt = u_p[-(CONV_WIDTH - 1):]
    kv_s4 = kv_s.reshape(dec_batch, dec_seq, 2, N_GROUPS, HEADS_PER_GROUP, HEAD_DIM)
    kv_sample = [kv_s4[:, :, :, g][None] for g in range(N_GROUPS)]
    conv_sample = u_hist[:, -(CONV_WIDTH - 1):]
    return (y_p[None], y_s.reshape(dec_batch, dec_seq, D_MODEL), *kv_prompt, conv_prompt[None, None],
            *kv_sample, conv_sample[None])
```

```python
import functools

import jax
import jax.numpy as jnp
from jax import lax
from jax.experimental import pallas as pl
from jax.experimental.pallas import tpu as pltpu
from jax.experimental.pallas import tpu_sc as plsc

F32 = jnp.float32
BF16 = jnp.bfloat16
I32 = jnp.int32

D_MODEL = 1024
HEAD_DIM = 64
HEADS_PER_GROUP = 4
GROUP_W = HEADS_PER_GROUP * HEAD_DIM
DIL_GROUPS = ((128, 1), (512, 4), (2048, 16))
N_GROUPS = len(DIL_GROUPS)
ATTN_W = N_GROUPS * GROUP_W
CONV_CH = D_MODEL - ATTN_W
CONV_WIDTH = 31
IN_W = 3 * ATTN_W + 2 * CONV_CH
BAND = 128
N_EXPERTS = 256
TOP_K = 8
N_EXPERT_GROUPS = 8
EXPERTS_PER_GROUP = N_EXPERTS // N_EXPERT_GROUPS
TOPK_GROUPS = 4
EXPERT_FF = 256
ROUTED_SCALE = 2.5
DEPTH = 1
DN_ALPHA = (2 * DEPTH) ** 0.25
LN_EPS = 1e-5
MASKED = -1e30

VMEM_LIMIT_BYTES = 56 * 1024 * 1024


def _params(*sem):
    return pltpu.CompilerParams(dimension_semantics=sem, vmem_limit_bytes=VMEM_LIMIT_BYTES)


def _dot(a, b):
    return jnp.dot(a, b, preferred_element_type=F32)


def _dot_nt(a, b):
    return lax.dot_general(a, b, (((1,), (1,)), ((), ())), preferred_element_type=F32)


def _layer_norm(x, g, b):
    mu = jnp.mean(x, axis=-1, keepdims=True)
    xc = x - mu
    var = jnp.mean(xc * xc, axis=-1, keepdims=True)
    return xc * lax.rsqrt(var + LN_EPS) * g + b


def _silu(x):
    return x * jax.nn.sigmoid(x)


def _alibi_slopes():
    n = N_GROUPS * HEADS_PER_GROUP
    h = jnp.arange(1, n + 1, dtype=F32)
    return (2.0 ** (-8.0 * h / n)).reshape(N_GROUPS, HEADS_PER_GROUP)


LANES = 128


def _proj_kernel(x_ref, w_ref, b_ref, *refs, dils, tm):
    qkv_refs, (u_ref, kv_ref, zs) = refs[:N_GROUPS], refs[N_GROUPS:]
    x = x_ref[...].astype(BF16)
    for part in range(3):
        for g in range(N_GROUPS):
            c0 = part * ATTN_W + g * GROUP_W
            z = _dot(x, w_ref[:, c0:c0 + GROUP_W]) + b_ref[:, c0:c0 + GROUP_W]
            if part > 0:
                kv_ref[:, c0 - ATTN_W:c0 - ATTN_W + GROUP_W] = z
            out, dil = qkv_refs[g], dils[g]
            if dil == 1:
                out[:, part * GROUP_W:(part + 1) * GROUP_W] = z.astype(out.dtype)
                continue
            for half in range(GROUP_W // LANES):
                zs[half] = z[:, half * LANES:(half + 1) * LANES]
            for r in range(dil):
                for half in range(GROUP_W // LANES):
                    c = r * 3 * GROUP_W + part * GROUP_W + half * LANES
                    out[:, c:c + LANES] = zs[half, pl.ds(r, tm // dil, stride=dil), :].astype(out.dtype)
    c0 = 3 * ATTN_W
    a = _dot(x, w_ref[:, c0:c0 + CONV_CH]) + b_ref[:, c0:c0 + CONV_CH]
    gate = _dot(x, w_ref[:, c0 + CONV_CH:]) + b_ref[:, c0 + CONV_CH:]
    u_ref[...] = a * jax.nn.sigmoid(gate)


def _project(x, w_in, b_in, tm, keep, qkv_dtype, dils):
    n = x.shape[0]
    nt = n // tm
    nk = keep // tm
    return pl.pallas_call(
        functools.partial(_proj_kernel, dils=dils, tm=tm),
        grid=(nt,),
        in_specs=[
            pl.BlockSpec((tm, D_MODEL), lambda i: (i, 0)),
            pl.BlockSpec((D_MODEL, IN_W), lambda i: (0, 0)),
            pl.BlockSpec((1, IN_W), lambda i: (0, 0)),
        ],
        out_specs=[pl.BlockSpec((tm // d, d * 3 * GROUP_W), lambda i: (i, 0)) for d in dils] + [
            pl.BlockSpec((tm, CONV_CH), lambda i: (i, 0)),
            pl.BlockSpec((tm, 2 * ATTN_W), lambda i: (jnp.maximum(i - (nt - nk), 0), 0)),
        ],
        out_shape=[jax.ShapeDtypeStruct((n // d, d * 3 * GROUP_W), qkv_dtype) for d in dils] + [
            jax.ShapeDtypeStruct((n, CONV_CH), F32),
            jax.ShapeDtypeStruct((keep, 2 * ATTN_W), F32),
        ],
        scratch_shapes=[pltpu.VMEM((GROUP_W // LANES, tm, LANES), F32)],
        compiler_params=_params("arbitrary"),
        name="proj",
    )(x, w_in, b_in)


def _head_select(parts, rows):
    col = lax.broadcasted_iota(I32, (rows, GROUP_W), 1) // HEAD_DIM
    out = jnp.broadcast_to(parts[-1], (rows, GROUP_W))
    for h in range(HEADS_PER_GROUP - 2, -1, -1):
        out = jnp.where(col == h, parts[h], out)
    return out


def _head_rows(q, rows):
    col = lax.broadcasted_iota(I32, (rows, GROUP_W), 1) // HEAD_DIM
    return jnp.concatenate([jnp.where(col == h, q, jnp.zeros_like(q)) for h in range(HEADS_PER_GROUP)], axis=0)


def _softmax_pv(s, v, rows):
    m = jnp.max(s, axis=-1, keepdims=True)
    e = jnp.exp(s - m)
    l = jnp.sum(e, axis=-1, keepdims=True)
    pv = _dot(e.astype(BF16), v) / l
    lse = m + jnp.log(l)
    o = _head_select([pv[h * rows:(h + 1) * rows] for h in range(HEADS_PER_GROUP)], rows)
    lse_x = _head_select([lse[h * rows:(h + 1) * rows] for h in range(HEADS_PER_GROUP)], rows)
    return o, lse_x


ATTN_QB = 16


def _attn_kernel(q_ref, kp_ref, kc_ref, vp_ref, vc_ref, bias_ref, o_ref, lse_ref, *, qb):
    b = pl.program_id(1)
    for j in range(qb):
        rows = slice(j * BAND, (j + 1) * BAND)
        qm = _head_rows(q_ref[rows, :] * HEAD_DIM ** -0.5, BAND)
        if j == 0:
            k = jnp.concatenate([kp_ref[...], kc_ref[rows, :]], axis=0)
            v = jnp.concatenate([vp_ref[...], vc_ref[rows, :]], axis=0)
            bias = bias_ref[jnp.minimum(b, 1)]
        else:
            k = kc_ref[(j - 1) * BAND:(j + 1) * BAND, :]
            v = vc_ref[(j - 1) * BAND:(j + 1) * BAND, :]
            bias = bias_ref[1]
        s = _dot_nt(qm, k) + bias
        o, lse_x = _softmax_pv(s, v, BAND)
        o_ref[rows, :] = o.astype(o_ref.dtype)
        lse_ref[rows, :] = lse_x


def _prompt_bias(g, dil):
    slopes = _alibi_slopes()[g]
    qi = jnp.arange(BAND)[:, None]
    kj = jnp.arange(2 * BAND)[None, :]
    steps = qi + BAND - kj
    nk = DIL_GROUPS[g][0] // dil
    valid = (steps >= 0) & (steps <= nk)
    bias = -slopes[:, None, None] * (steps * dil).astype(F32)
    inner = jnp.where(valid[None], bias, MASKED)
    first = jnp.where((valid & (kj >= BAND))[None], bias, MASKED)
    return jnp.stack([first, inner]).reshape(2, HEADS_PER_GROUP * BAND, 2 * BAND)


def _attention_prompt(view, g):
    win, dil = DIL_GROUPS[g]
    n_cls = view.shape[0]
    qb = ATTN_QB
    while n_cls % (qb * BAND):
        qb //= 2
    rows = qb * BAND
    assert win // dil <= BAND and qb >= 1 and view.shape[1] == dil * 3 * GROUP_W
    cur = lambda part: pl.BlockSpec((rows, GROUP_W), lambda r, b: (b, r * 3 + part))
    prev = lambda part: pl.BlockSpec((BAND, GROUP_W), lambda r, b: (jnp.maximum(b * qb - 1, 0), r * 3 + part))
    bias = _prompt_bias(g, dil)
    return pl.pallas_call(
        functools.partial(_attn_kernel, qb=qb),
        grid=(dil, n_cls // rows),
        in_specs=[cur(0), prev(1), cur(1), prev(2), cur(2),
                  pl.BlockSpec((2, HEADS_PER_GROUP * BAND, 2 * BAND), lambda r, b: (0, 0, 0))],
        out_specs=[pl.BlockSpec((rows, GROUP_W), lambda r, b: (b, r)),
                   pl.BlockSpec((rows, GROUP_W), lambda r, b: (b, r))],
        out_shape=[jax.ShapeDtypeStruct((n_cls, dil * GROUP_W), BF16),
                   jax.ShapeDtypeStruct((n_cls, dil * GROUP_W), F32)],
        compiler_params=_params("arbitrary", "arbitrary"),
        name=f"attn_g{g}",
    )(view, view, view, view, view, bias)


def _attn_sample_kernel(q_ref, k_ref, v_ref, cache_ref, bias_ref, o_ref, lse_ref, *, n_buf, t, seqs):
    pad = jnp.zeros((BAND - t, GROUP_W), F32)
    scale = HEAD_DIM ** -0.5
    for b in range(seqs):
        rows = slice(b * t, (b + 1) * t)
        k_new = jnp.concatenate([k_ref[rows, :], pad], axis=0).astype(BF16)
        v_new = jnp.concatenate([v_ref[rows, :], pad], axis=0).astype(BF16)
        qm = _head_rows(q_ref[rows, :], t).astype(BF16)
        s_old = _dot(qm, cache_ref[b, 0:GROUP_W, :].astype(BF16)) * scale + bias_ref[:, 0:n_buf]
        s_new = _dot_nt(qm, k_new) * scale + bias_ref[:, n_buf:]
        m = jnp.maximum(jnp.max(s_old, axis=-1, keepdims=True), jnp.max(s_new, axis=-1, keepdims=True))
        e_old = jnp.exp(s_old - m)
        e_new = jnp.exp(s_new - m)
        l = jnp.sum(e_old, axis=-1, keepdims=True) + jnp.sum(e_new, axis=-1, keepdims=True)
        pv = _dot_nt(e_old.astype(BF16), cache_ref[b, GROUP_W:, :].astype(BF16)) + _dot(e_new.astype(BF16), v_new)
        pv = pv / l
        lse = m + jnp.log(l)
        o_ref[rows, :] = _head_select([pv[h * t:(h + 1) * t] for h in range(HEADS_PER_GROUP)], t)
        lse_ref[rows, :] = _head_select([lse[h * t:(h + 1) * t] for h in range(HEADS_PER_GROUP)], t)


def _sample_bias(g, n_buf, t):
    win, dil = DIL_GROUPS[g]
    slopes = _alibi_slopes()[g]
    tq = jnp.arange(t)[:, None]
    j = jnp.arange(n_buf + BAND)[None, :]
    dist = n_buf + tq - j
    valid = (dist >= 0) & (dist % dil == 0) & (dist <= win) & (j < n_buf + t)
    bias = -slopes[:, None, None] * dist.astype(F32)
    return jnp.where(valid[None], bias, MASKED).reshape(HEADS_PER_GROUP * t, n_buf + BAND)


def _attention_sample(qkv, cache_t, g, t):
    nb, n_buf = cache_t.shape[0], cache_t.shape[2]
    seqs = max(1, min(4, 1024 // n_buf))
    assert nb % seqs == 0
    col = lambda part: pl.BlockSpec((seqs * t, GROUP_W), lambda i: (i, part))
    kern = functools.partial(_attn_sample_kernel, n_buf=n_buf, t=t, seqs=seqs)
    bias = _sample_bias(g, n_buf, t)
    return pl.pallas_call(
        kern,
        grid=(nb // seqs,),
        in_specs=[col(0), col(1), col(2),
                  pl.BlockSpec((seqs, 2 * GROUP_W, n_buf), lambda i: (i, 0, 0)),
                  pl.BlockSpec((HEADS_PER_GROUP * t, n_buf + BAND), lambda i: (0, 0))],
        out_specs=[pl.BlockSpec((seqs * t, GROUP_W), lambda i: (i, 0)),
                   pl.BlockSpec((seqs * t, GROUP_W), lambda i: (i, 0))],
        out_shape=[jax.ShapeDtypeStruct((nb * t, GROUP_W), F32),
                   jax.ShapeDtypeStruct((nb * t, GROUP_W), F32)],
        compiler_params=_params("arbitrary"),
        name=f"attn_sample_g{g}",
    )(qkv, qkv, qkv, cache_t, bias)


CONV_HALO = 32


def _conv_tail(acc, b_ref, g_ref, beta_ref):
    return _silu(_layer_norm(acc + b_ref[...], g_ref[...], beta_ref[...]))


SUBLANES = 8


def _conv_prompt_kernel(halo_ref, u_ref, w_ref, b_ref, g_ref, beta_ref, o_ref, hist, part, *, tm):
    i = pl.program_id(0)
    hist[0:CONV_HALO, :] = jnp.where(i == 0, 0.0, halo_ref[...])
    hist[CONV_HALO:CONV_HALO + tm, :] = u_ref[...]
    hist[CONV_HALO + tm:, :] = jnp.zeros((SUBLANES, CONV_CH), F32)
    off = CONV_HALO - (CONV_WIDTH - 1)
    acc = None
    for s in range(SUBLANES):
        group = None
        for m in range(s, off + CONV_WIDTH, SUBLANES):
            j = m - off
            if j < 0:
                continue
            term = w_ref[j:j + 1, :] * hist[m - s:m - s + tm + SUBLANES, :]
            group = term if group is None else group + term
        part[...] = group
        shifted = part[s:s + tm, :]
        acc = shifted if acc is None else acc + shifted
    o_ref[...] = _conv_tail(acc, b_ref, g_ref, beta_ref).astype(o_ref.dtype)


def _conv_prompt(u, w_dw, b_dw, ln_g, ln_b, tm):
    n = u.shape[0]
    vec = pl.BlockSpec((1, CONV_CH), lambda i: (0, 0))
    return pl.pallas_call(
        functools.partial(_conv_prompt_kernel, tm=tm),
        grid=(n // tm,),
        in_specs=[pl.BlockSpec((CONV_HALO, CONV_CH), lambda i: (jnp.maximum(i * (tm // CONV_HALO) - 1, 0), 0)),
                  pl.BlockSpec((tm, CONV_CH), lambda i: (i, 0)),
                  pl.BlockSpec((CONV_WIDTH, CONV_CH), lambda i: (0, 0)), vec, vec, vec],
        out_specs=pl.BlockSpec((tm, CONV_CH), lambda i: (i, 0)),
        out_shape=jax.ShapeDtypeStruct((n, CONV_CH), BF16),
        scratch_shapes=[pltpu.VMEM((CONV_HALO + tm + SUBLANES, CONV_CH), F32),
                        pltpu.VMEM((tm + SUBLANES, CONV_CH), F32)],
        compiler_params=_params("arbitrary"),
        name="conv_prompt",
    )(u, u, w_dw, b_dw, ln_g, ln_b)


CONV_SAMPLE_SEQS = 8


def _conv_sample_kernel(hist_ref, w_ref, b_ref, g_ref, beta_ref, o_ref, *, t, seqs):
    for b in range(seqs):
        acc = jnp.zeros((t, CONV_CH), F32)
        for j in range(CONV_WIDTH):
            acc = acc + w_ref[j:j + 1, :] * hist_ref[b, j:j + t, :]
        o_ref[b * t:(b + 1) * t, :] = _conv_tail(acc, b_ref, g_ref, beta_ref)


def _conv_sample(u_hist, w_dw, b_dw, ln_g, ln_b, t):
    nb, rows = u_hist.shape[0], u_hist.shape[1]
    seqs = CONV_SAMPLE_SEQS if nb % CONV_SAMPLE_SEQS == 0 else 1
    vec = pl.BlockSpec((1, CONV_CH), lambda i: (0, 0))
    return pl.pallas_call(
        functools.partial(_conv_sample_kernel, t=t, seqs=seqs),
        grid=(nb // seqs,),
        in_specs=[pl.BlockSpec((seqs, rows, CONV_CH), lambda i: (i, 0, 0)),
                  pl.BlockSpec((CONV_WIDTH, CONV_CH), lambda i: (0, 0)), vec, vec, vec],
        out_specs=pl.BlockSpec((seqs * t, CONV_CH), lambda i: (i, 0)),
        out_shape=jax.ShapeDtypeStruct((nb * t, CONV_CH), F32),
        compiler_params=_params("arbitrary"),
        name="conv_sample",
    )(u_hist, w_dw, b_dw, ln_g, ln_b)


U32 = jnp.uint32
HALF_D = D_MODEL // 2
PACK_SLABS = HALF_D // LANES


def _token_order(ref, dil, scr, tm):
    if dil == 1:
        return ref[...].astype(F32)
    for r in range(dil):
        for half in range(GROUP_W // LANES):
            c = r * GROUP_W + half * LANES
            scr[half, pl.ds(r, tm // dil, stride=dil), :] = ref[:, c:c + LANES].astype(F32)
    return jnp.concatenate([scr[half] for half in range(GROUP_W // LANES)], axis=1)


def _pack_rows(ref, val):
    rows = val.shape[0]
    bits = lax.bitcast_convert_type(val.astype(BF16).astype(F32), U32)
    packed = lax.shift_right_logical(bits[:, :HALF_D], jnp.uint32(16)) | bits[:, HALF_D:]
    for c in range(PACK_SLABS):
        ref[pl.ds(c, rows, stride=PACK_SLABS), :] = packed[:, c * LANES:(c + 1) * LANES]


def _unpack_rows(ref):
    rows = ref.shape[0] // PACK_SLABS
    u = jnp.concatenate([ref[pl.ds(c, rows, stride=PACK_SLABS), :] for c in range(PACK_SLABS)], axis=1)
    lo = lax.bitcast_convert_type(lax.shift_left(u, jnp.uint32(16)), F32)
    hi = lax.bitcast_convert_type(u & jnp.uint32(0xFFFF0000), F32)
    return lo, hi


def _outproj_kernel(o0, o1, o2, l0, l1, l2, c_ref, x_ref, w_ref, b_ref, g_ref, beta_ref, h_ref, hp_ref, *scr,
                    dils, tm):
    os_ = [_token_order(r, d, scr[2 * g], tm) for g, (r, d) in enumerate(zip((o0, o1, o2), dils))]
    ls = [_token_order(r, d, scr[2 * g + 1], tm) for g, (r, d) in enumerate(zip((l0, l1, l2), dils))]
    m = jnp.maximum(jnp.maximum(ls[0], ls[1]), ls[2])
    es = [jnp.exp(l - m) for l in ls]
    inv = 1.0 / (es[0] + es[1] + es[2])
    mixed = b_ref[...]
    for g in range(N_GROUPS):
        a = (os_[g] * (es[g] * inv)).astype(BF16)
        mixed = mixed + _dot(a, w_ref[g * GROUP_W:(g + 1) * GROUP_W, :])
    mixed = mixed + _dot(c_ref[...].astype(BF16), w_ref[ATTN_W:, :])
    h = _layer_norm(DN_ALPHA * x_ref[...] + mixed, g_ref[...], beta_ref[...])
    h_ref[...] = h
    _pack_rows(hp_ref, h)


def _outproj(os_, lses, conv, x, w_o, b_o, ln_g, ln_b, tm, dils):
    n = x.shape[0]
    grp = [pl.BlockSpec((tm // d, d * GROUP_W), lambda i: (i, 0)) for d in dils]
    vec = pl.BlockSpec((1, D_MODEL), lambda i: (0, 0))
    return pl.pallas_call(
        functools.partial(_outproj_kernel, dils=dils, tm=tm),
        grid=(n // tm,),
        in_specs=grp + grp + [pl.BlockSpec((tm, CONV_CH), lambda i: (i, 0)),
                              pl.BlockSpec((tm, D_MODEL), lambda i: (i, 0)),
                              pl.BlockSpec((D_MODEL, D_MODEL), lambda i: (0, 0)), vec, vec, vec],
        out_specs=[pl.BlockSpec((tm, D_MODEL), lambda i: (i, 0)),
                   pl.BlockSpec((tm * PACK_SLABS, LANES), lambda i: (i, 0))],
        out_shape=[jax.ShapeDtypeStruct((n, D_MODEL), F32),
                   jax.ShapeDtypeStruct((n * PACK_SLABS, LANES), U32)],
        scratch_shapes=[pltpu.VMEM((GROUP_W // LANES, tm, LANES), F32) for _ in range(2 * N_GROUPS)],
        compiler_params=_params("arbitrary"),
        name="outproj",
    )(*os_, *lses, conv, x, w_o, b_o, ln_g, ln_b)


ROUTER_TM = 256


def _first_index(hit, idx, limit, axis):
    return jnp.min(jnp.where(hit, idx, limit), axis=axis, keepdims=True)


def _router_kernel(ha_ref, hb_ref, w_ref, rb_ref, tri_ref, idx_ref, wt_ref, pos_ref, cnt_ref, run, *, tiles_a):
    i = pl.program_id(0)
    tm = ROUTER_TM

    @pl.when(i == 0)
    def _():
        run[...] = jnp.zeros_like(run)

    h = jnp.where(i < tiles_a, ha_ref[...], hb_ref[...])
    logits = _dot_nt(w_ref[...], h.astype(BF16))
    scores = jax.nn.sigmoid(logits)
    biased = scores + rb_ref[...]
    groups = [biased[g * EXPERTS_PER_GROUP:(g + 1) * EXPERTS_PER_GROUP] for g in range(N_EXPERT_GROUPS)]
    ei = lax.broadcasted_iota(I32, (EXPERTS_PER_GROUP, tm), 0).astype(F32)
    gs = []
    for bg in groups:
        m1 = jnp.max(bg, axis=0, keepdims=True)
        f1 = _first_index(bg == m1, ei, float(EXPERTS_PER_GROUP), 0)
        m2 = jnp.max(jnp.where(ei == f1, -jnp.inf, bg), axis=0, keepdims=True)
        gs.append(m1 + m2)
    gs = jnp.concatenate(gs, axis=0)
    gi = lax.broadcasted_iota(I32, gs.shape, 0).astype(F32)
    keep = jnp.zeros(gs.shape, F32)
    cur = gs
    for _ in range(TOPK_GROUPS):
        m = jnp.max(cur, axis=0, keepdims=True)
        f = _first_index(cur == m, gi, float(N_EXPERT_GROUPS), 0)
        hit = gi == f
        keep = jnp.where(hit, 1.0, keep)
        cur = jnp.where(hit, -jnp.inf, cur)
    masked = jnp.concatenate([jnp.where(keep[g:g + 1] > 0.0, bg, -jnp.inf) for g, bg in enumerate(groups)], axis=0)
    xi = lax.broadcasted_iota(I32, (N_EXPERTS, tm), 0).astype(F32)
    cur = masked
    picks = []
    for _ in range(TOP_K):
        m = jnp.max(cur, axis=0, keepdims=True)
        f = _first_index(cur == m, xi, float(N_EXPERTS), 0)
        hit = xi == f
        picks.append((f, hit))
        cur = jnp.where(hit, -jnp.inf, cur)
    sel = jnp.where(cur != masked, 1.0, 0.0)
    before = _dot(sel.astype(BF16), tri_ref[...]) + run[...]
    run[...] = run[...] + jnp.sum(sel, axis=1, keepdims=True)
    ws = [jnp.sum(jnp.where(hit, scores, 0.0), axis=0, keepdims=True) for _, hit in picks]
    wsum = ws[0]
    for w in ws[1:]:
        wsum = wsum + w
    for k, (f, hit) in enumerate(picks):
        idx_ref[k:k + 1, :] = f.astype(I32)
        pos_ref[k:k + 1, :] = jnp.sum(jnp.where(hit, before, 0.0), axis=0, keepdims=True).astype(I32)
    wt_ref[...] = jnp.concatenate([w / wsum * ROUTED_SCALE for w in ws], axis=0).T
    cnt_ref[...] = jnp.broadcast_to(run[...], cnt_ref.shape).astype(I32)


def _route(ha, hb, w_router_t, router_bias):
    tm = ROUTER_TM
    tiles_a, tiles_b = ha.shape[0] // tm, hb.shape[0] // tm
    spec_a = pl.BlockSpec((tm, D_MODEL), lambda i: (jnp.minimum(i, tiles_a - 1), 0))
    spec_b = pl.BlockSpec((tm, D_MODEL), lambda i: (jnp.maximum(i - tiles_a, 0), 0))
    t = (tiles_a + tiles_b) * tm
    tri = (jnp.arange(tm)[:, None] < jnp.arange(tm)[None, :]).astype(BF16)
    slot = pl.BlockSpec((TOP_K, tm), lambda i: (0, i))
    idx, wts, pos, cnt = pl.pallas_call(
        functools.partial(_router_kernel, tiles_a=tiles_a),
        grid=(t // tm,),
        in_specs=[spec_a, spec_b,
                  pl.BlockSpec((N_EXPERTS, D_MODEL), lambda i: (0, 0)),
                  pl.BlockSpec((N_EXPERTS, 1), lambda i: (0, 0)),
                  pl.BlockSpec((tm, tm), lambda i: (0, 0))],
        out_specs=[slot, pl.BlockSpec((tm, TOP_K), lambda i: (i, 0)), slot,
                   pl.BlockSpec((N_EXPERTS, 128), lambda i: (0, 0))],
        out_shape=[jax.ShapeDtypeStruct((TOP_K, t), I32), jax.ShapeDtypeStruct((t, TOP_K), F32),
                   jax.ShapeDtypeStruct((TOP_K, t), I32), jax.ShapeDtypeStruct((N_EXPERTS, 128), I32)],
        scratch_shapes=[pltpu.VMEM((N_EXPERTS, 1), F32)],
        compiler_params=_params("arbitrary"),
        name="router",
    )(ha, hb, w_router_t, router_bias.reshape(N_EXPERTS, 1), tri)
    return idx, wts, pos, cnt[:, 0]


DEST_TM = 1280


def _dest_kernel(idx_ref, pos_ref, starts_ref, dest_ref):
    tm = idx_ref.shape[1]
    ei = lax.broadcasted_iota(I32, (N_EXPERTS, tm), 0)
    starts = starts_ref[...]
    for k in range(TOP_K):
        first = jnp.sum(jnp.where(ei == idx_ref[k:k + 1, :], starts, 0.0), axis=0, keepdims=True)
        dest_ref[k:k + 1, :] = first.astype(I32) + pos_ref[k:k + 1, :]


def _dest_rows(idx, pos, starts):
    t = idx.shape[1]
    tm = DEST_TM if t % DEST_TM == 0 else ROUTER_TM
    assert t % tm == 0
    slot = pl.BlockSpec((TOP_K, tm), lambda i: (0, i))
    return pl.pallas_call(
        _dest_kernel,
        grid=(t // tm,),
        in_specs=[slot, slot, pl.BlockSpec((N_EXPERTS, 1), lambda i: (0, 0))],
        out_specs=slot,
        out_shape=jax.ShapeDtypeStruct((TOP_K, t), I32),
        compiler_params=_params("arbitrary"),
        name="dest_rows",
    )(idx, pos, starts.astype(F32).reshape(N_EXPERTS, 1))


SC_CORES = 2
SC_SUBCORES = 16
SC_WORKERS = SC_CORES * SC_SUBCORES
SC_WINDOW = 64
SC_WINDOW_SMALL = 32


def _sc_worker():
    return lax.axis_index("s") * SC_CORES + lax.axis_index("c")


def _sc_mesh():
    return plsc.VectorSubcoreMesh(core_axis_name="c", subcore_axis_name="s")


def _sc_scatter_rows(ha, hb, dest_w, zeros, n_out):
    w = dest_w.shape[2]
    na, nb = ha.shape[0], hb.shape[0]
    spare = zeros.shape[0]
    wa, wb = na // w, nb // w
    assert na % (SC_WORKERS * 2 * w) == 0 and nb % w == 0 and wb <= SC_WORKERS
    per_w = wa // SC_WORKERS
    rows_t = pltpu.VMEM((w,) + ha.shape[1:], ha.dtype)
    idx_t = pltpu.VMEM((TOP_K, w), I32)

    @functools.partial(
        pl.kernel, mesh=_sc_mesh(),
        out_type=jax.ShapeDtypeStruct((n_out + spare,) + ha.shape[1:], ha.dtype),
        scratch_types=[rows_t, rows_t, idx_t, idx_t, pltpu.SemaphoreType.DMA, pltpu.SemaphoreType.DMA],
    )
    def scatter(ha_hbm, hb_hbm, dest_hbm, zeros_hbm, out_hbm, rows0, rows1, idx0, idx1, sem0, sem1):
        wid = _sc_worker()
        bufs = ((rows0, idx0, sem0), (rows1, idx1, sem1))

        @pl.when(wid == SC_WORKERS - 1)
        def _():
            pltpu.sync_copy(zeros_hbm, out_hbm.at[pl.ds(n_out, spare)])

        def load(src_hbm, row0, win, b):
            pltpu.sync_copy(src_hbm.at[pl.ds(pl.multiple_of(row0, 8), w)], bufs[b][0])
            pltpu.sync_copy(dest_hbm.at[win], bufs[b][1])

        def copies(b):
            return [pltpu.make_async_copy(bufs[b][0], out_hbm.at[bufs[b][1].at[k]], bufs[b][2])
                    for k in range(TOP_K)]

        win0 = wid * per_w
        load(ha_hbm, win0 * w, win0, 0)

        @pl.loop(0, per_w, step=2)
        def _(i0):
            for b in range(2):
                i = i0 + b
                for c in copies(b):
                    c.start()

                @pl.when(i + 1 < per_w)
                def _():
                    load(ha_hbm, (win0 + i + 1) * w, win0 + i + 1, 1 - b)

                for c in copies(b):
                    c.wait()

        @pl.when(wid < wb)
        def _():
            load(hb_hbm, wid * w, wa + wid, 0)
            for c in copies(0):
                c.start()
            for c in copies(0):
                c.wait()

    return scatter(ha, hb, dest_w, zeros)


def _sc_gather_rows(table, idx):
    n = idx.shape[0]
    w = SC_WINDOW if n % (SC_WORKERS * 2 * SC_WINDOW) == 0 else SC_WINDOW_SMALL
    assert n % (SC_WORKERS * 2 * w) == 0
    per_w = n // SC_WORKERS
    nwin = per_w // w
    rows_t = pltpu.VMEM((w,) + table.shape[1:], table.dtype)

    @functools.partial(
        pl.kernel, mesh=_sc_mesh(),
        out_type=jax.ShapeDtypeStruct((n,) + table.shape[1:], table.dtype),
        scratch_types=[pltpu.VMEM((per_w,), I32), rows_t, rows_t, pltpu.SemaphoreType.DMA, pltpu.SemaphoreType.DMA],
    )
    def gather(table_hbm, idx_hbm, out_hbm, idx_v, rows0, rows1, sem0, sem1):
        base = pl.multiple_of(_sc_worker() * per_w, 8)
        pltpu.sync_copy(idx_hbm.at[pl.ds(base, per_w)], idx_v)
        bufs = ((rows0, sem0), (rows1, sem1))
        fetch = lambda i, b: pltpu.make_async_copy(
            table_hbm.at[idx_v.at[pl.ds(pl.multiple_of(i * w, 8), w)]], bufs[b][0], bufs[b][1])
        fetch(0, 0).start()

        @pl.loop(0, nwin, step=2)
        def _(i0):
            for b in range(2):
                i = i0 + b
                fetch(i, b).wait()

                @pl.when(i + 1 < nwin)
                def _():
                    fetch(i + 1, 1 - b).start()

                pltpu.sync_copy(bufs[b][0], out_hbm.at[pl.ds(pl.multiple_of(base + i * w, 8), w)])

    return gather(table, idx)


EXPERT_CH = 256
X_SLOTS = 6
W_SLOTS = 3


def _expert_kernel(row0_ref, exp_ref, live_ref, fresh_ref, next1_ref, next2_ref, wslot_ref, ordered_ref, xs_ref,
                   wg_ref, wu_ref, wd_ref, ys_ref, xbuf, ybuf, sg, su, sd, wgb, wub, wdb, xsem, ysem, wsem):
    v = pl.program_id(0)
    nv = pl.num_programs(0)
    slot = lax.rem(v, X_SLOTS)
    yslot = lax.rem(v, 3)
    rows = lambda u: pl.ds(pl.multiple_of(row0_ref[u] * PACK_SLABS, PACK_SLABS), EXPERT_CH * PACK_SLABS)
    x_copy = lambda u: pltpu.make_async_copy(xs_ref.at[rows(u)], xbuf.at[lax.rem(u, X_SLOTS)],
                                             xsem.at[lax.rem(u, X_SLOTS)])
    y_copy = lambda u: pltpu.make_async_copy(ybuf.at[lax.rem(u, 3)], ys_ref.at[rows(u)], ysem.at[lax.rem(u, 3)])
    prev1, prev2 = jnp.maximum(v - 1, 0), jnp.maximum(v - 2, 0)
    w_copies = lambda e, s: [pltpu.make_async_copy(w.at[e], stage.at[s], wsem.at[s])
                             for w, stage in ((wg_ref, sg), (wu_ref, su), (wd_ref, sd))]
    spare = pl.ds(ys_ref.shape[0] - EXPERT_CH * PACK_SLABS, EXPERT_CH * PACK_SLABS)

    @pl.when(v == 0)
    def _():
        for c in w_copies(exp_ref[0], 0):
            c.start(priority=1)

        @pl.when(next1_ref[0] >= 0)
        def _():
            for c in w_copies(next1_ref[0], 1):
                c.start(priority=1)

        x_copy(0).start()
        for u in range(1, X_SLOTS - 1):
            @pl.when((u < nv) & (live_ref[jnp.minimum(u, nv - 1)] == 1))
            def _():
                x_copy(u).start()
        ybuf[2] = jnp.zeros(ybuf.shape[1:], U32)
        zero = pltpu.make_async_copy(ybuf.at[2], ys_ref.at[spare], ysem.at[2])
        zero.start()
        zero.wait()

    ahead = jnp.minimum(v + X_SLOTS - 1, nv - 1)

    @pl.when((v + X_SLOTS - 1 < nv) & (live_ref[ahead] == 1))
    def _():
        x_copy(ahead).start()

    @pl.when(fresh_ref[v] == 1)
    def _():
        p = wslot_ref[v]
        for c in w_copies(exp_ref[v], p):
            c.wait()

        @pl.when(next2_ref[v] >= 0)
        def _():
            for c in w_copies(next2_ref[v], lax.rem(p + 2, W_SLOTS)):
                c.start(priority=1)

        wgb[...] = sg[p].astype(BF16)
        wub[...] = su[p].astype(BF16)
        wdb[...] = sd[p].astype(BF16)

    @pl.when(live_ref[v] == 1)
    def _():
        x_copy(v).wait()
        x = jnp.concatenate(_unpack_rows(xbuf.at[slot]), axis=1).astype(BF16)
        hid = (_silu(_dot(x, wgb[...])) * _dot(x, wub[...])).astype(BF16)
        _pack_rows(ybuf.at[yslot], _dot(hid, wdb[...]))

    @pl.when((v >= 2) & (live_ref[prev2] == 1) & (ordered_ref[prev1] == 0))
    def _():
        y_copy(prev2).wait()

    @pl.when((v >= 1) & (live_ref[prev1] == 1) & (ordered_ref[v] == 1))
    def _():
        y_copy(prev1).wait()

    @pl.when(live_ref[v] == 1)
    def _():
        y_copy(v).start()

    @pl.when(v == nv - 1)
    def _():
        @pl.when((v >= 1) & (live_ref[prev1] == 1) & (ordered_ref[v] == 0))
        def _():
            y_copy(prev1).wait()

        @pl.when(live_ref[v] == 1)
        def _():
            y_copy(v).wait()


def _chunks(counts, n_rows):
    ch = EXPERT_CH
    nv = n_rows // ch + N_EXPERTS
    ends = jnp.cumsum(counts)
    starts = ends - counts
    nch = (counts + ch - 1) // ch
    cend = jnp.cumsum(nch)
    cstart = cend - nch
    v = jnp.arange(nv, dtype=I32)
    live = v < cend[-1]
    vc = jnp.minimum(v, cend[-1] - 1)
    ids = jnp.arange(N_EXPERTS, dtype=I32)
    e = jnp.minimum(jnp.sum(cend[None, :] <= vc[:, None], axis=1), N_EXPERTS - 1).astype(I32)
    of_e = lambda a: jnp.sum(jnp.where(e[:, None] == ids[None, :], a[None, :], 0), axis=1)
    starts_e, ends_e, count_e = of_e(starts), of_e(ends), of_e(counts)
    c = vc - of_e(cstart)
    is_last = c == of_e(nch) - 1
    row0 = jnp.where(is_last & (count_e >= ch), ends_e - ch, starts_e + c * ch).astype(I32)
    overrun = live & is_last & (count_e < ch)
    ordered = jnp.concatenate([jnp.zeros((1,), bool), overrun[:-1]]) & live
    fresh = jnp.concatenate([jnp.ones((1,), I32), (e[1:] != e[:-1]).astype(I32)])
    order = jnp.cumsum(counts > 0) - 1
    order_e = of_e(order)

    def after(k):
        hit = (counts > 0)[None, :] & (order[None, :] == order_e[:, None] + k)
        return (jnp.sum(jnp.where(hit, ids[None, :] + 1, 0), axis=1) - 1).astype(I32)

    return (starts.astype(I32), row0, e, live.astype(I32), jnp.where(live, fresh, 0).astype(I32),
            after(1), after(2), (order_e % W_SLOTS).astype(I32), ordered.astype(I32))


def _experts(chunks, xs, w_gate, w_up, w_down):
    ch = EXPERT_CH
    buf = lambda n: pltpu.VMEM((n, ch * PACK_SLABS, LANES), U32)
    anywhere = pl.BlockSpec(memory_space=pl.ANY)
    return pl.pallas_call(
        _expert_kernel,
        grid_spec=pltpu.PrefetchScalarGridSpec(
            num_scalar_prefetch=len(chunks),
            grid=(chunks[0].shape[0],),
            in_specs=[anywhere] * 4,
            out_specs=anywhere,
            scratch_shapes=[buf(X_SLOTS), buf(3),
                            pltpu.VMEM((W_SLOTS, D_MODEL, EXPERT_FF), F32),
                            pltpu.VMEM((W_SLOTS, D_MODEL, EXPERT_FF), F32),
                            pltpu.VMEM((W_SLOTS, EXPERT_FF, D_MODEL), F32),
                            pltpu.VMEM((D_MODEL, EXPERT_FF), BF16), pltpu.VMEM((D_MODEL, EXPERT_FF), BF16),
                            pltpu.VMEM((EXPERT_FF, D_MODEL), BF16),
                            pltpu.SemaphoreType.DMA((X_SLOTS,)), pltpu.SemaphoreType.DMA((3,)),
                            pltpu.SemaphoreType.DMA((W_SLOTS,))],
        ),
        out_shape=jax.ShapeDtypeStruct(xs.shape, U32),
        compiler_params=_params("arbitrary"),
        name="experts",
    )(*chunks, xs, w_gate, w_up, w_down)


COMBINE_TM = 512


def _combine_kernel(wt_ref, h_ref, sg_ref, su_ref, sd_ref, g_ref, beta_ref, *refs):
    y_refs, o_ref = refs[:TOP_K], refs[TOP_K]
    h = h_ref[...]
    hb = h.astype(BF16)
    hid = (_silu(_dot(hb, sg_ref[...])) * _dot(hb, su_ref[...])).astype(BF16)
    acc = DN_ALPHA * h + _dot(hid, sd_ref[...])
    lo = acc[:, :HALF_D]
    hi = acc[:, HALF_D:]
    for k in range(TOP_K):
        y_lo, y_hi = _unpack_rows(y_refs[k])
        w = wt_ref[:, k:k + 1]
        lo = lo + w * y_lo
        hi = hi + w * y_hi
    o_ref[...] = _layer_norm(jnp.concatenate([lo, hi], axis=1), g_ref[...], beta_ref[...])


def _combine(wts_tk, ws_gate, ws_up, ws_down, ln_g, ln_b, h, h_row0, t0, yg):
    n = yg.shape[0] // (TOP_K * PACK_SLABS)
    tm = COMBINE_TM if n % COMBINE_TM == 0 else ROUTER_TM
    assert n % tm == 0 and h_row0 % tm == 0 and t0 % tm == 0
    b0 = t0 // tm
    per_slot = n // tm
    rows = pl.BlockSpec((tm, D_MODEL), lambda i: (h_row0 // tm + i, 0))
    slot_rows = [pl.BlockSpec((tm * PACK_SLABS, LANES), lambda i, k=k: (k * per_slot + i, 0))
                 for k in range(TOP_K)]
    vec = pl.BlockSpec((1, D_MODEL), lambda i: (0, 0))
    return pl.pallas_call(
        _combine_kernel,
        grid=(n // tm,),
        in_specs=[pl.BlockSpec((tm, TOP_K), lambda i: (b0 + i, 0)), rows,
                  pl.BlockSpec((D_MODEL, EXPERT_FF), lambda i: (0, 0)),
                  pl.BlockSpec((D_MODEL, EXPERT_FF), lambda i: (0, 0)),
                  pl.BlockSpec((EXPERT_FF, D_MODEL), lambda i: (0, 0)),
                  vec, vec] + slot_rows,
        out_specs=pl.BlockSpec((tm, D_MODEL), lambda i: (i, 0)),
        out_shape=jax.ShapeDtypeStruct((n, D_MODEL), F32),
        compiler_params=_params("arbitrary"),
        name="combine",
    )(wts_tk, h, ws_gate, ws_up, ws_down, ln_g, ln_b, *([yg] * TOP_K))


def _kv_cache(kv_tail, g, keep):
    k = kv_tail[-keep:, g * GROUP_W:(g + 1) * GROUP_W]
    v = kv_tail[-keep:, ATTN_W + g * GROUP_W:ATTN_W + (g + 1) * GROUP_W]
    return jnp.stack([k, v], axis=1).reshape(keep, 2, HEADS_PER_GROUP, HEAD_DIM)


def kernel(x_prompt, x_sample, cache_kv_w128, cache_kv_w512, cache_kv_w2048, state_conv, w_in, b_in, w_dw, b_dw,
           conv_ln_g, conv_ln_b, w_o, b_o, ln1_g, ln1_b, w_router, router_bias, w_gate, w_up, w_down, ws_gate,
           ws_up, ws_down, ln2_g, ln2_b):
    assert w_in.shape[0] == DEPTH == 1
    batch, seq, _ = x_prompt.shape
    dec_batch, dec_seq, _ = x_sample.shape
    assert batch == 1
    n_s = dec_batch * dec_seq
    caches = (cache_kv_w128, cache_kv_w512, cache_kv_w2048)
    row = lambda a: a[0].reshape(1, -1)

    w_in_b = w_in[0].astype(BF16)
    b_in_r = row(b_in)
    w_o_b = w_o[0].astype(BF16)
    conv_w = (w_dw[0], row(b_dw), row(conv_ln_g), row(conv_ln_b))
    ln1 = (row(ln1_g), row(ln1_b))

    xp = x_prompt[0]
    keep_p = min(max(w for w, _ in DIL_GROUPS), seq)
    dils = tuple(d for _, d in DIL_GROUPS)
    *qkv_p, u_p, kv_p = _project(xp, w_in_b, b_in_r, 1024, keep_p, BF16, dils)
    attn_p = [_attention_prompt(qkv_p[g], g) for g in range(N_GROUPS)]
    conv_p = _conv_prompt(u_p, *conv_w, 512)
    h_p, hpk_p = _outproj([a[0] for a in attn_p], [a[1] for a in attn_p], conv_p, xp, w_o_b, row(b_o), *ln1, 1024,
                          dils)

    xs = x_sample.reshape(n_s, D_MODEL)
    ones = (1,) * N_GROUPS
    *qkv_s, u_s, kv_s = _project(xs, w_in_b, b_in_r, n_s, n_s, F32, ones)
    caches_t = [jnp.transpose(c[0].reshape(dec_batch, -1, 2 * GROUP_W), (0, 2, 1)) for c in caches]
    attn_s = [_attention_sample(qkv_s[g], caches_t[g], g, dec_seq) for g in range(N_GROUPS)]
    u_hist = jnp.concatenate([state_conv[0], u_s.reshape(dec_batch, dec_seq, CONV_CH)], axis=1)
    conv_s = _conv_sample(u_hist, *conv_w, dec_seq)
    h_s, hpk_s = _outproj([a[0] for a in attn_s], [a[1] for a in attn_s], conv_s, xs, w_o_b, row(b_o), *ln1, n_s,
                          ones)

    idx, wts, pos, counts = _route(h_p, h_s, w_router[0].T.astype(BF16), router_bias[0])
    n_tok = seq + n_s
    starts, *chunks = _chunks(counts, n_tok * TOP_K)
    dest = _dest_rows(idx, pos, starts)
    dest_w = dest.reshape(TOP_K, n_tok // SC_WINDOW, SC_WINDOW).transpose(1, 0, 2)
    tiles = lambda a: a.reshape(-1, PACK_SLABS, LANES)
    flat = lambda a: a.reshape(-1, LANES)
    x_sorted = _sc_scatter_rows(tiles(hpk_p), tiles(hpk_s), dest_w, jnp.zeros((EXPERT_CH, PACK_SLABS, LANES), U32),
                                n_tok * TOP_K)
    y_sorted = _experts(chunks, flat(x_sorted), w_gate[0], w_up[0], w_down[0])
    shared = (ws_gate[0].astype(BF16), ws_up[0].astype(BF16), ws_down[0].astype(BF16))
    comb = functools.partial(_combine, wts, *shared, row(ln2_g), row(ln2_b))
    y_table = tiles(y_sorted)

    def finish(h, t0):
        y_slots = _sc_gather_rows(y_table, dest[:, t0:t0 + h.shape[0]].reshape(-1))
        return comb(h, 0, t0, flat(y_slots))

    y_s = finish(h_s, seq)
    y_p = finish(h_p, 0)

    kv_prompt = [_kv_cache(kv_p, g, min(w, seq))[None, None] for g, (w, _) in enumerate(DIL_GROUPS)]
    assert seq >= CONV_WIDTH - 1
    conv_prompt = u_p[-(CONV_WIDTH - 1):]
    kv_s4 = kv_s.reshape(dec_batch, dec_seq, 2, N_GROUPS, HEADS_PER_GROUP, HEAD_DIM)
    kv_sample = [kv_s4[:, :, :, g][None] for g in range(N_GROUPS)]
    conv_sample = u_hist[:, -(CONV_WIDTH - 1):]
    return (y_p[None], y_s.reshape(dec_batch, dec_seq, D_MODEL), *kv_prompt, conv_prompt[None, None],
            *kv_sample, conv_sample[None])
```
